```python
import jax, jax.numpy as jnp
from jax import lax
import numpy as np

D_MODEL = 1024
BATCH = 8
SEQ = 4096
DEPTH = 2

CTX_LEN = 256
GRID_W = 64
EXPAND = 2
D_INNER = EXPAND * D_MODEL
D_A = D_INNER // 2
D_B = D_INNER - D_A
N_HEADS_A = 4
HEAD_DIM_A = D_A // N_HEADS_A
N_GROUPS_B = 4
GROUP_B = D_B // N_GROUPS_B
N_GROUPS_C = 4
GROUP_C = D_INNER // N_GROUPS_C
POOL_WINDOWS = (2, 4, 8, 16)
CONV_W = 3
CHUNK = 64
N_GATES = 4 * N_HEADS_A
W_IN_EVEN = 5 * D_A + 2 * D_B + N_GATES
W_IN_ODD = 2 * D_INNER
EPS = 1e-6

kernel_name = 'hybrid_mlstm_fourier_pool_dit'

f32 = jnp.float32


def _rmsnorm(a, g):
    af = a.astype(f32)
    af = af * lax.rsqrt(jnp.mean(af * af, axis=-1, keepdims=True) + EPS)
    return (af * g.astype(f32)).astype(a.dtype)


def _short_conv(u, w):
    T = u.shape[1]
    pad = CONV_W // 2
    up = jnp.pad(u, ((0, 0), (pad, CONV_W - 1 - pad), (0, 0)))
    out = w[0] * up[:, 0:T]
    for j in range(1, CONV_W):
        out = out + w[j] * up[:, j:j + T]
    return out


def _mlstm_chunked(q, k, v, ig, lf, state):
    B, H, T, Dh = q.shape
    nc = T // CHUNK

    def to_chunks(a):
        return jnp.moveaxis(a.reshape(a.shape[:2] + (nc, CHUNK) + a.shape[3:]), 2, 0)

    mask = jnp.tril(jnp.ones((CHUNK, CHUNK), dtype=bool))

    def step(carry, inp):
        C, n, m = carry
        qc, kc, vc, ic, fc = inp
        b = jnp.cumsum(fc, axis=-1)
        dm = b[..., :, None] - b[..., None, :] + ic[..., None, :]
        dm = jnp.where(mask, dm, -jnp.inf)
        m_inter = b + m[..., None]
        m_t = jnp.maximum(jnp.max(dm, axis=-1), m_inter)
        s = jnp.einsum('bhtd,bhsd->bhts', qc, kc) * jnp.exp(dm - m_t[..., None])
        inter = jnp.exp(m_inter - m_t)
        num = jnp.einsum('bhts,bhsd->bhtd', s, vc) + inter[..., None] * jnp.einsum('bhtd,bhde->bhte', qc, C)
        den = jnp.sum(s, axis=-1) + inter * jnp.einsum('bhtd,bhd->bht', qc, n)
        h = num / jnp.maximum(jnp.abs(den), jnp.exp(-m_t))[..., None]
        b_end = b[..., -1]
        w_s = b_end[..., None] - b + ic
        m_new = jnp.maximum(b_end + m, jnp.max(w_s, axis=-1))
        decay = jnp.exp(b_end + m - m_new)
        ws = jnp.exp(w_s - m_new[..., None])
        C_new = decay[..., None, None] * C + jnp.einsum('bhs,bhsd,bhse->bhde', ws, kc, vc)
        n_new = decay[..., None] * n + jnp.einsum('bhs,bhsd->bhd', ws, kc)
        return (C_new, n_new, m_new), h

    final, hs = lax.scan(step, state, (to_chunks(q), to_chunks(k), to_chunks(v), to_chunks(ig), to_chunks(lf)))
    h = jnp.moveaxis(hs, 0, 2).reshape(B, H, T, Dh)
    return h, final


def _to_heads(a):
    B, T, _ = a.shape
    return a.reshape(B, T, N_HEADS_A, HEAD_DIM_A).transpose(0, 2, 1, 3).astype(f32)


def _project_even(h, win, gate_b, conv_qk):
    B, T, _ = h.shape
    p = h @ win
    qk = jax.nn.silu(_short_conv(p[..., :2 * D_A], conv_qk))
    q = _to_heads(qk[..., :D_A]) * (HEAD_DIM_A ** -0.5)
    k = _to_heads(qk[..., D_A:])
    v = _to_heads(p[..., 2 * D_A:3 * D_A])
    o = p[..., 3 * D_A:4 * D_A]
    z_a = p[..., 4 * D_A:5 * D_A]
    u_b = p[..., 5 * D_A:5 * D_A + D_B]
    z_b = p[..., 5 * D_A + D_B:5 * D_A + 2 * D_B]
    g = (p[..., 5 * D_A + 2 * D_B:] + gate_b).astype(f32)
    g = g.reshape(B, T, 4, N_HEADS_A).transpose(2, 0, 3, 1)
    gates = (g[0], jax.nn.log_sigmoid(g[1]), g[2], jax.nn.log_sigmoid(g[3]))
    return (q, k, v), gates, (o, z_a, u_b, z_b)


def _mlstm_bidir(qkv_c, gates_c, qkv_x, gates_x):
    B, H, _, Dh = qkv_c[0].shape
    zero = (jnp.zeros((B, H, Dh, Dh), f32), jnp.zeros((B, H, Dh), f32), jnp.zeros((B, H), f32))

    def flip(a):
        return jnp.flip(a, axis=2)

    hc_f, st_f = _mlstm_chunked(*qkv_c, gates_c[0], gates_c[1], zero)
    hx_f, _ = _mlstm_chunked(*qkv_x, gates_x[0], gates_x[1], st_f)
    hc_b, st_b = _mlstm_chunked(*map(flip, qkv_c), flip(gates_c[2]), flip(gates_c[3]), zero)
    hx_b, _ = _mlstm_chunked(*map(flip, qkv_x), flip(gates_x[2]), flip(gates_x[3]), st_b)
    return hc_f + flip(hc_b), hx_f + flip(hx_b)


def _fourier_mix(u, fw):
    B, T, _ = u.shape
    ug = u.astype(f32).reshape(B, T, N_GROUPS_B, GROUP_B)
    y = jnp.fft.fft2(ug, axes=(1, 3), norm='ortho').real
    return jnp.einsum('btgc,gcd->btgd', y, fw.astype(f32)).reshape(B, T, D_B)


def _even_out(h, extra, head_g, fw, wout, dtype):
    o, z_a, u_b, z_b = extra
    B, H, T, Dh = h.shape
    h = jax.nn.sigmoid(o.astype(f32)).reshape(B, T, H, Dh) * h.transpose(0, 2, 1, 3)
    h = h * lax.rsqrt(jnp.mean(h * h, axis=-1, keepdims=True) + EPS)
    y_a = h.reshape(B, T, D_A) * head_g.astype(f32) * jax.nn.silu(z_a.astype(f32))
    y_b = _fourier_mix(u_b, fw) * jax.nn.silu(z_b.astype(f32))
    y = jnp.concatenate([y_a, y_b], axis=-1).astype(dtype)
    return y @ wout


def _box_mean(u, axis, w):
    n = u.shape[axis]
    pads = [(1, 0) if a == axis else (0, 0) for a in range(u.ndim)]
    P = jnp.pad(jnp.cumsum(u, axis=axis), pads)
    idx = jnp.arange(n)
    lo = jnp.clip(idx - w // 2, 0, n)
    hi = jnp.clip(idx + w - w // 2, 0, n)
    s = jnp.take(P, hi, axis=axis) - jnp.take(P, lo, axis=axis)
    cnt = (hi - lo).astype(u.dtype).reshape((n,) + (1,) * (u.ndim - axis - 1))
    return s / cnt


def _pool_mix(u, pool_w, scale, rows):
    B, T, _ = u.shape
    parts = []
    for g, w in enumerate(POOL_WINDOWS):
        ug = u[..., g * GROUP_C:(g + 1) * GROUP_C].astype(f32)
        if rows is None:
            m = _box_mean(ug, 1, w)
        else:
            grid = ug.reshape(B, rows, GRID_W, GROUP_C)
            m = _box_mean(_box_mean(grid, 1, w), 2, w).reshape(B, T, GROUP_C)
        parts.append(m - ug)
    y = jnp.stack(parts, axis=2)
    y = jnp.einsum('btgc,gcd->btgd', y, pool_w.astype(f32)).reshape(B, T, D_INNER)
    return y * scale.astype(f32)


def _odd_branch(h, win, pool_w, scale, wout, rows):
    p = h @ win
    y = _pool_mix(p[..., :D_INNER], pool_w, scale, rows) * jax.nn.silu(p[..., D_INNER:].astype(f32))
    return y.astype(h.dtype) @ wout


def setup_inputs(seed: int = 0) -> dict:
    key = jax.random.key(seed)
    ks = jax.random.split(key, 20)
    n_even = (DEPTH + 1) // 2
    n_odd = DEPTH // 2
    nrm = jax.random.normal
    x = nrm(ks[0], (BATCH, SEQ, D_MODEL), f32)
    c = nrm(ks[1], (BATCH, D_MODEL), f32)
    ctx = nrm(ks[2], (BATCH, CTX_LEN, D_MODEL), f32)
    c_ctx = nrm(ks[3], (D_MODEL,), f32)
    ada_w = nrm(ks[4], (DEPTH, D_MODEL, 3 * D_MODEL), f32) * (0.5 * D_MODEL ** -0.5)
    ada_b = 0.02 * nrm(ks[5], (DEPTH, 3 * D_MODEL), f32)
    norm_g = 1.0 + 0.1 * nrm(ks[6], (DEPTH, D_MODEL), f32)
    win_even = nrm(ks[7], (n_even, D_MODEL, W_IN_EVEN), f32) * (D_MODEL ** -0.5)
    i_bias = 0.1 * nrm(ks[8], (n_even, 2, N_HEADS_A), f32)
    f_bias = jnp.linspace(3.0, 6.0, N_HEADS_A, dtype=f32) + 0.1 * nrm(ks[9], (n_even, 2, N_HEADS_A), f32)
    gate_b_even = jnp.stack([i_bias[:, 0], f_bias[:, 0], i_bias[:, 1], f_bias[:, 1]], axis=1).reshape(n_even, N_GATES)
    conv_qk_even = nrm(ks[10], (n_even, CONV_W, 2 * D_A), f32) * (CONV_W ** -0.5)
    head_norm_even = 1.0 + 0.1 * nrm(ks[11], (n_even, D_A), f32)
    fourier_w_even = nrm(ks[12], (n_even, N_GROUPS_B, GROUP_B, GROUP_B), f32) * (GROUP_B ** -0.5)
    wout_even = nrm(ks[13], (n_even, D_INNER, D_MODEL), f32) * (D_INNER ** -0.5)
    win_odd = nrm(ks[14], (n_odd, D_MODEL, W_IN_ODD), f32) * (D_MODEL ** -0.5)
    pool_w_odd = nrm(ks[15], (n_odd, N_GROUPS_C, GROUP_C, GROUP_C), f32) * (GROUP_C ** -0.5)
    pool_scale_odd = 1.0 + 0.1 * nrm(ks[16], (n_odd, D_INNER), f32)
    wout_odd = nrm(ks[17], (n_odd, D_INNER, D_MODEL), f32) * (D_INNER ** -0.5)
    final_g = 1.0 + 0.1 * nrm(ks[18], (D_MODEL,), f32)
    return {'x': x, 'c': c, 'ctx': ctx, 'c_ctx': c_ctx, 'ada_w': ada_w, 'ada_b': ada_b, 'norm_g': norm_g,
            'win_even': win_even, 'gate_b_even': gate_b_even, 'conv_qk_even': conv_qk_even,
            'head_norm_even': head_norm_even, 'fourier_w_even': fourier_w_even, 'wout_even': wout_even,
            'win_odd': win_odd, 'pool_w_odd': pool_w_odd, 'pool_scale_odd': pool_scale_odd,
            'wout_odd': wout_odd, 'final_g': final_g}


def reference(x, c, ctx, c_ctx, ada_w, ada_b, norm_g, win_even, gate_b_even, conv_qk_even,
              head_norm_even, fourier_w_even, wout_even, win_odd, pool_w_odd, pool_scale_odd,
              wout_odd, final_g):
    rows = x.shape[1] // GRID_W
    for l in range(DEPTH):
        even = (l % 2 == 0)
        ctx_out = l < DEPTH - 1
        mod_x = jax.nn.silu(c) @ ada_w[l] + ada_b[l]
        shift_x, scale_x, gate_x = jnp.split(mod_x[:, None, :], 3, axis=-1)
        hx = _rmsnorm(x, norm_g[l]) * (1 + scale_x) + shift_x
        if even or ctx_out:
            mod_c = jax.nn.silu(c_ctx) @ ada_w[l] + ada_b[l]
            shift_c, scale_c, gate_c = jnp.split(mod_c, 3, axis=-1)
            hc = _rmsnorm(ctx, norm_g[l]) * (1 + scale_c) + shift_c
        if even:
            e = l // 2
            qkv_x, gates_x, extra_x = _project_even(hx, win_even[e], gate_b_even[e], conv_qk_even[e])
            qkv_c, gates_c, extra_c = _project_even(hc, win_even[e], gate_b_even[e], conv_qk_even[e])
            h_c, h_x = _mlstm_bidir(qkv_c, gates_c, qkv_x, gates_x)
            y_x = _even_out(h_x, extra_x, head_norm_even[e], fourier_w_even[e], wout_even[e], x.dtype)
            if ctx_out:
                y_c = _even_out(h_c, extra_c, head_norm_even[e], fourier_w_even[e], wout_even[e], ctx.dtype)
        else:
            o = l // 2
            y_x = _odd_branch(hx, win_odd[o], pool_w_odd[o], pool_scale_odd[o], wout_odd[o], rows)
            if ctx_out:
                y_c = _odd_branch(hc, win_odd[o], pool_w_odd[o], pool_scale_odd[o], wout_odd[o], None)
        x = x + gate_x * y_x
        if ctx_out:
            ctx = ctx + gate_c * y_c
    return _rmsnorm(x, final_g)
```

```python
import functools

import numpy as np
import jax
import jax.numpy as jnp
from jax import lax
from jax.experimental import pallas as pl
from jax.experimental.pallas import tpu as pltpu

D_MODEL = 1024
DEPTH = 2
CTX_LEN = 256
GRID_W = 64
D_INNER = 2 * D_MODEL
D_A = D_INNER // 2
D_B = D_INNER - D_A
N_HEADS_A = 4
HEAD_DIM_A = D_A // N_HEADS_A
N_GROUPS_B = 4
GROUP_B = D_B // N_GROUPS_B
N_GROUPS_C = 4
GROUP_C = D_INNER // N_GROUPS_C
POOL_WINDOWS = (2, 4, 8, 16)
CONV_W = 3
N_GATES = 4 * N_HEADS_A
W_MAIN_EVEN = 5 * D_A + 2 * D_B
EPS = 1e-6

f32 = jnp.float32
bf16 = jnp.bfloat16

V7X_VMEM_BYTES = 64 * 1024 * 1024
V7X_LANES = 128
V7X_SUBLANES = 8

MLSTM_CHUNK = 256
FFT_N1 = 64
FFT_N2 = 64
FFT_PITCH = 72
NEG_BIG = -1e30


def _vmem_limit(nbytes):
    return int(min(max(nbytes * 5 // 4 + (4 << 20), 16 << 20), V7X_VMEM_BYTES - (6 << 20)))


def _silu(v):
    return v * jax.nn.sigmoid(v)


def _log_sigmoid(v):
    return jnp.minimum(v, 0.0) - jnp.log1p(jnp.exp(-jnp.abs(v)))


def _mod_kernel(r_ref, w_ref, b_ref, o_ref):
    s = _silu(r_ref[...])
    o_ref[...] = jnp.dot(s, w_ref[...], preferred_element_type=f32,
                         precision=lax.Precision.HIGHEST) + b_ref[...]


def _modulation(rows, ada_w, ada_b):
    nrow = rows.shape[0]
    tn = 1024
    return pl.pallas_call(
        _mod_kernel,
        grid=(DEPTH, 3 * D_MODEL // tn),
        in_specs=[
            pl.BlockSpec((nrow, D_MODEL), lambda l, j: (0, 0)),
            pl.BlockSpec((None, D_MODEL, tn), lambda l, j: (l, 0, j)),
            pl.BlockSpec((None, 1, tn), lambda l, j: (l, 0, j)),
        ],
        out_specs=pl.BlockSpec((None, nrow, tn), lambda l, j: (l, 0, j)),
        out_shape=jax.ShapeDtypeStruct((DEPTH, nrow, 3 * D_MODEL), f32),
        compiler_params=pltpu.CompilerParams(dimension_semantics=("arbitrary", "arbitrary")),
        name="modulation",
    )(rows, ada_w, ada_b.reshape(DEPTH, 1, 3 * D_MODEL))


def _normed(x, amp, shift):
    ms = jnp.mean(x * x, axis=-1, keepdims=True)
    return (x * lax.rsqrt(ms + EPS)) * amp + shift


def _inproj_even_kernel(x_ref, xp_ref, xn_ref, sh_ref, sc_ref, g_ref, w_ref, wg_ref, wgt_ref, gb_ref, gbt_ref,
                        cw_ref, q_ref, k_ref, v_ref, o_ref, za_ref, ub_ref, zb_ref, gc_ref, gt_ref, *, tm, nt):
    i = pl.program_id(1)
    amp = g_ref[...] * (1.0 + sc_ref[...])
    shift = sh_ref[...]
    hx = _normed(x_ref[...], amp, shift).astype(bf16)
    halo = jnp.concatenate([xp_ref[...], xn_ref[...]], axis=0)
    hh = _normed(halo, amp, shift).astype(bf16)
    has_prev = (i > 0).astype(f32)
    has_next = (i < nt - 1).astype(f32)
    row = lax.broadcasted_iota(jnp.int32, (tm, 1), 0)
    cn = 512
    for j in range(2 * D_A // cn):
        w = w_ref[:, j * cn:(j + 1) * cn]
        p = jnp.dot(hx, w, preferred_element_type=f32)
        ph = jnp.dot(hh, w, preferred_element_type=f32)
        prev = ph[V7X_SUBLANES - 1:V7X_SUBLANES, :] * has_prev
        nxt = ph[V7X_SUBLANES:V7X_SUBLANES + 1, :] * has_next
        up = jnp.where(row == 0, prev, pltpu.roll(p, 1, 0))
        dn = jnp.where(row == tm - 1, nxt, pltpu.roll(p, tm - 1, 0))
        cw = cw_ref[:, j * cn:(j + 1) * cn]
        y = _silu(cw[0:1, :] * up + cw[1:2, :] * p + cw[2:3, :] * dn)
        if j < D_A // cn:
            q_ref[:, j * cn:(j + 1) * cn] = (y * (HEAD_DIM_A ** -0.5)).astype(bf16)
        else:
            jj = j - D_A // cn
            k_ref[:, jj * cn:(jj + 1) * cn] = y.astype(bf16)
    for idx, ref in enumerate((v_ref, o_ref, za_ref, ub_ref, zb_ref)):
        for jj in range(D_A // cn):
            c0 = 2 * D_A + idx * D_A + jj * cn
            ref[:, jj * cn:(jj + 1) * cn] = jnp.dot(hx, w_ref[:, c0:c0 + cn],
                                                    preferred_element_type=f32).astype(bf16)
    gc_ref[...] = jnp.dot(hx, wg_ref[...], preferred_element_type=f32) + gb_ref[...]
    gt_ref[...] = lax.dot_general(wgt_ref[...], hx, (((1,), (1,)), ((), ())),
                                  preferred_element_type=f32) + gbt_ref[...]


def _inproj_even(x, shift, scale, norm_g, w_main, wg, wgt, gb, gbt, conv_w, tm):
    B, T, D = x.shape
    nt = T // tm
    hb = tm // V7X_SUBLANES
    nhb = T // V7X_SUBLANES
    row_spec = pl.BlockSpec((None, tm, D_A), lambda b, i: (b, i, 0))
    vec = lambda n: pl.BlockSpec((1, n), lambda b, i: (0, 0))
    est = (2 * w_main.size * 2 + 2 * tm * D * 4 + 7 * 2 * tm * D_A * 2 + 2 * tm * 256 * 4 + 6 * tm * 512 * 4)
    outs = pl.pallas_call(
        functools.partial(_inproj_even_kernel, tm=tm, nt=nt),
        grid=(B, nt),
        in_specs=[
            pl.BlockSpec((None, tm, D), lambda b, i: (b, i, 0)),
            pl.BlockSpec((None, V7X_SUBLANES, D), lambda b, i: (b, jnp.maximum(i * hb - 1, 0), 0)),
            pl.BlockSpec((None, V7X_SUBLANES, D), lambda b, i: (b, jnp.minimum((i + 1) * hb, nhb - 1), 0)),
            pl.BlockSpec((None, 1, D), lambda b, i: (b, 0, 0)),
            pl.BlockSpec((None, 1, D), lambda b, i: (b, 0, 0)),
            vec(D),
            pl.BlockSpec(w_main.shape, lambda b, i: (0, 0)),
            pl.BlockSpec(wg.shape, lambda b, i: (0, 0)),
            pl.BlockSpec(wgt.shape, lambda b, i: (0, 0)),
            vec(2 * V7X_LANES),
            pl.BlockSpec((N_GATES, 1), lambda b, i: (0, 0)),
            pl.BlockSpec(conv_w.shape, lambda b, i: (0, 0)),
        ],
        out_specs=[row_spec] * 7 + [
            pl.BlockSpec((None, tm, 2 * V7X_LANES), lambda b, i: (b, i, 0)),
            pl.BlockSpec((None, N_GATES, tm), lambda b, i: (b, 0, i)),
        ],
        out_shape=[jax.ShapeDtypeStruct((B, T, D_A), bf16)] * 7 + [
            jax.ShapeDtypeStruct((B, T, 2 * V7X_LANES), f32),
            jax.ShapeDtypeStruct((B, N_GATES, T), f32),
        ],
        compiler_params=pltpu.CompilerParams(dimension_semantics=("parallel", "arbitrary"),
                                             vmem_limit_bytes=_vmem_limit(est)),
        name="inproj_even",
    )(x, x, x, shift, scale, norm_g.reshape(1, D), w_main, wg, wgt, gb, gbt, conv_w)
    return outs


def _split3(v):
    hi = v.astype(bf16)
    r1 = v - hi.astype(f32)
    mid = r1.astype(bf16)
    lo = (r1 - mid.astype(f32)).astype(bf16)
    return hi, mid, lo


def _gate_prep_kernel(gc_ref, gt_ref, bcol_ref, acol_ref, arow_ref, *, L):
    r = lax.broadcasted_iota(jnp.int32, (L, L), 0)
    c = lax.broadcasted_iota(jnp.int32, (L, L), 1)
    tri_l = (c <= r).astype(bf16)
    tri_u = (c >= r).astype(bf16)

    g = gc_ref[...]
    ig = g[:, :V7X_LANES]
    lf = _log_sigmoid(g[:, V7X_LANES:])
    pre = suf = None
    for piece in _split3(lf):
        a = jnp.dot(tri_l, piece, preferred_element_type=f32)
        b = jnp.dot(tri_u, piece, preferred_element_type=f32)
        pre = a if pre is None else pre + a
        suf = b if suf is None else suf + b
    lane = lax.broadcasted_iota(jnp.int32, (L, V7X_LANES), 1)
    bcol = jnp.where(lane < N_HEADS_A, pre, suf)
    bcol_ref[...] = bcol
    acol_ref[...] = ig - bcol

    gt = gt_ref[...]
    igt = gt[:V7X_SUBLANES, :]
    lft = _log_sigmoid(gt[V7X_SUBLANES:, :])
    pre = suf = None
    for piece in _split3(lft):
        a = jnp.dot(piece, tri_u, preferred_element_type=f32)
        b = jnp.dot(piece, tri_l, preferred_element_type=f32)
        pre = a if pre is None else pre + a
        suf = b if suf is None else suf + b
    sub = lax.broadcasted_iota(jnp.int32, (V7X_SUBLANES, L), 0)
    arow_ref[...] = igt - jnp.where(sub < N_HEADS_A, pre, suf)


def _gate_prep(gc, gt, L):
    B, T, _ = gc.shape
    bcol, acol, arow = pl.pallas_call(
        functools.partial(_gate_prep_kernel, L=L),
        grid=(B, T // L),
        in_specs=[
            pl.BlockSpec((None, L, 2 * V7X_LANES), lambda b, i: (b, i, 0)),
            pl.BlockSpec((None, N_GATES, L), lambda b, i: (b, 0, i)),
        ],
        out_specs=[
            pl.BlockSpec((None, L, V7X_LANES), lambda b, i: (b, i, 0)),
            pl.BlockSpec((None, L, V7X_LANES), lambda b, i: (b, i, 0)),
            pl.BlockSpec((None, V7X_SUBLANES, L), lambda b, i: (b, 0, i)),
        ],
        out_shape=[
            jax.ShapeDtypeStruct((B, T, V7X_LANES), f32),
            jax.ShapeDtypeStruct((B, T, V7X_LANES), f32),
            jax.ShapeDtypeStruct((B, V7X_SUBLANES, T), f32),
        ],
        compiler_params=pltpu.CompilerParams(dimension_semantics=("parallel", "parallel")),
        name="gate_prep",
    )(gc, gt)
    H = N_HEADS_A
    cols = jnp.stack([bcol[..., 0:H], acol[..., 0:H], bcol[..., H:2 * H], acol[..., H:2 * H]], axis=-1)
    cols = cols.transpose(0, 2, 1, 3)
    rows = arow.reshape(B, 2, H, T).transpose(0, 2, 1, 3)
    return cols, rows


def _mlstm_kernel(qx_ref, kx_ref, vx_ref, cx_ref, rx_ref, qc_ref, kc_ref, vc_ref, cc_ref, rc_ref,
                  hx_ref, hc_ref, C_ref, n_ref, *, L, nx, nc):
    r_i = lax.broadcasted_iota(jnp.int32, (L, L), 0)
    c_i = lax.broadcasted_iota(jnp.int32, (L, L), 1)

    def chunk(q_ref, k_ref, v_ref, col_ref, row_ref, r0, d, m_prev):
        rows = pl.ds(r0, L)
        q = q_ref[rows, :]
        k = k_ref[rows, :]
        v = v_ref[rows, :]
        cols = col_ref[rows, :]
        b_col = cols[:, 2 * d:2 * d + 1]
        a_col = cols[:, 2 * d + 1:2 * d + 2]
        a_row = row_ref[d:d + 1, rows]
        mask = (c_i <= r_i) if d == 0 else (c_i >= r_i)
        dm = jnp.where(mask, b_col + a_row, NEG_BIG)
        m_inter = b_col + m_prev
        m_t = jnp.maximum(jnp.max(dm, axis=1, keepdims=True), m_inter)
        s = lax.dot_general(q, k, (((1,), (1,)), ((), ())), preferred_element_type=f32) * jnp.exp(dm - m_t)
        inter = jnp.exp(m_inter - m_t)
        qC = jnp.dot(q, C_ref[...].astype(bf16), preferred_element_type=f32)
        qn = jnp.sum(q.astype(f32) * n_ref[...], axis=1, keepdims=True)
        num = jnp.dot(s.astype(bf16), v, preferred_element_type=f32) + inter * qC
        den = jnp.sum(s, axis=1, keepdims=True) + inter * qn
        h = num / jnp.maximum(jnp.abs(den), jnp.exp(-m_t))

        b_end = b_col[L - 1:L, :] if d == 0 else b_col[0:1, :]
        w_col = b_end + a_col
        m_new = jnp.maximum(b_end + m_prev, jnp.max(w_col, axis=0, keepdims=True))
        decay = jnp.exp(b_end + m_prev - m_new)
        kw = k.astype(f32) * jnp.exp(w_col - m_new)
        C_ref[...] = decay * C_ref[...] + lax.dot_general(kw.astype(bf16), v, (((0,), (0,)), ((), ())),
                                                           preferred_element_type=f32)
        n_ref[...] = decay * n_ref[...] + jnp.sum(kw, axis=0, keepdims=True)
        return h, m_new

    def reset():
        C_ref[...] = jnp.zeros_like(C_ref)
        n_ref[...] = jnp.zeros_like(n_ref)
        return jnp.zeros((1, 1), f32)

    ctx = (qc_ref, kc_ref, vc_ref, cc_ref, rc_ref)
    lat = (qx_ref, kx_ref, vx_ref, cx_ref, rx_ref)

    m = reset()
    for ci in range(nc):
        h, m = chunk(*ctx, ci * L, 0, m)
        hc_ref[ci * L:(ci + 1) * L, :] = h

    def fwd_body(ci, m):
        r0 = pl.multiple_of(ci * L, L)
        h, m = chunk(*lat, r0, 0, m)
        hx_ref[pl.ds(r0, L), :] = h
        return m

    lax.fori_loop(0, nx, fwd_body, m)

    m = reset()
    for ci in reversed(range(nc)):
        h, m = chunk(*ctx, ci * L, 1, m)
        hc_ref[ci * L:(ci + 1) * L, :] += h

    def bwd_body(j, m):
        r0 = pl.multiple_of((nx - 1 - j) * L, L)
        h, m = chunk(*lat, r0, 1, m)
        hx_ref[pl.ds(r0, L), :] += h
        return m

    lax.fori_loop(0, nx, bwd_body, m)


def _mlstm(qx, kx, vx, colsx, rowsx, qc, kc, vc, colsc, rowsc, L):
    B, T, _ = qx.shape
    Tc = qc.shape[1]
    H, Dh = N_HEADS_A, HEAD_DIM_A

    def seq_spec(t):
        return pl.BlockSpec((None, t, Dh), lambda b, h: (b, 0, h))

    def col_spec(t):
        return pl.BlockSpec((None, None, t, 4), lambda b, h: (b, h, 0, 0))

    def row_spec(t):
        return pl.BlockSpec((None, None, 2, t), lambda b, h: (b, h, 0, 0))

    est = (2 * (3 * (T + Tc) * Dh * 2 + (T + Tc) * V7X_LANES * 4 + 8 * (T + Tc) * 4 + (T + Tc) * Dh * 4)
           + Dh * Dh * 4 + 16 * L * L * 4)
    return pl.pallas_call(
        functools.partial(_mlstm_kernel, L=L, nx=T // L, nc=Tc // L),
        grid=(B, H),
        in_specs=[seq_spec(T), seq_spec(T), seq_spec(T), col_spec(T), row_spec(T),
                  seq_spec(Tc), seq_spec(Tc), seq_spec(Tc), col_spec(Tc), row_spec(Tc)],
        out_specs=[seq_spec(T), seq_spec(Tc)],
        out_shape=[jax.ShapeDtypeStruct((B, T, D_A), f32), jax.ShapeDtypeStruct((B, Tc, D_A), f32)],
        scratch_shapes=[pltpu.VMEM((Dh, Dh), f32), pltpu.VMEM((1, Dh), f32)],
        compiler_params=pltpu.CompilerParams(dimension_semantics=("parallel", "parallel"),
                                             vmem_limit_bytes=_vmem_limit(est)),
        name="mlstm",
    )(qx, kx, vx, colsx, rowsx, qc, kc, vc, colsc, rowsc)


def _fourier_w_kernel(cs_ref, fw_ref, o_ref, *, scale):
    o_ref[...] = (jnp.dot(cs_ref[...], fw_ref[...], preferred_element_type=f32,
                          precision=lax.Precision.HIGHEST) * scale).astype(bf16)


def _fourier_weights(fw, T):
    n = GROUP_B
    kk = np.outer(np.arange(n), np.arange(n)) % n
    ang = 2.0 * np.pi * kk / n
    cs = jnp.asarray(np.concatenate([np.cos(ang), np.sin(ang)], axis=0), f32)
    return pl.pallas_call(
        functools.partial(_fourier_w_kernel, scale=float(1.0 / np.sqrt(T * n))),
        grid=(N_GROUPS_B,),
        in_specs=[pl.BlockSpec((2 * n, n), lambda g: (0, 0)),
                  pl.BlockSpec((None, n, n), lambda g: (g, 0, 0))],
        out_specs=pl.BlockSpec((None, 2 * n, n), lambda g: (g, 0, 0)),
        out_shape=jax.ShapeDtypeStruct((N_GROUPS_B, 2 * n, n), bf16),
        compiler_params=pltpu.CompilerParams(dimension_semantics=("arbitrary",)),
        name="fourier_weights",
    )(cs, fw)


def _fft_tables():
    n1, n2 = FFT_N1, FFT_N2
    n = n1 * n2
    t1 = np.arange(n1)
    k1 = np.arange(n1)
    t2 = np.arange(n2)
    idx = (k1[None, :, None] * (n2 * t1[None, None, :] + t2[:, None, None])) % n
    ang = 2.0 * np.pi * idx / n
    m1 = np.concatenate([np.cos(ang), -np.sin(ang)], axis=1)
    k2 = np.arange(n2)
    ph = 2.0 * np.pi * (np.outer(k2, t2) % n2) / n2
    c, s = np.cos(ph), np.sin(ph)
    m2 = np.block([[c, s], [-s, c]])
    return jnp.asarray(m1, bf16), jnp.asarray(m2, bf16)


def _fourier_kernel(u_ref, z_ref, m1_ref, m2_ref, w_ref, o_ref, up_ref, yp_ref, xp_ref, x2_ref):
    n1, n2, P = FFT_N1, FFT_N2, FFT_PITCH
    nl = GROUP_B // V7X_LANES

    def fill(t1, carry):
        src = pl.multiple_of(t1 * n2, n2)
        dst = pl.multiple_of(t1 * P, V7X_SUBLANES)
        blk = u_ref[pl.ds(src, n2), :].astype(f32)
        for s in range(nl):
            up_ref[s, pl.ds(dst, n2), :] = blk[:, s * V7X_LANES:(s + 1) * V7X_LANES]
        return carry

    lax.fori_loop(0, n1, fill, 0)

    def stage1(t2, carry):
        rhs = jnp.concatenate([up_ref[s, pl.ds(t2, n1, stride=P), :] for s in range(nl)], axis=1)
        y = jnp.dot(m1_ref[t2], rhs.astype(bf16), preferred_element_type=f32)
        dst = pl.multiple_of(t2 * P, V7X_SUBLANES)
        for ri in range(2):
            for s in range(nl):
                yp_ref[ri * nl + s, pl.ds(dst, n1), :] = y[ri * n1:(ri + 1) * n1,
                                                           s * V7X_LANES:(s + 1) * V7X_LANES]
        return carry

    lax.fori_loop(0, n2, stage1, 0)

    m2 = m2_ref[...]

    def stage2(k1, carry):
        parts = []
        for ri in range(2):
            parts.append(jnp.concatenate(
                [yp_ref[ri * nl + s, pl.ds(k1, n2, stride=P), :] for s in range(nl)], axis=1))
        rhs = jnp.concatenate(parts, axis=0).astype(bf16)
        x = jnp.dot(m2, rhs, preferred_element_type=f32)
        for ri in range(2):
            for s in range(nl):
                xp_ref[ri * nl + s, pl.ds(k1, n2, stride=P), :] = x[ri * n2:(ri + 1) * n2,
                                                                    s * V7X_LANES:(s + 1) * V7X_LANES]
        return carry

    lax.fori_loop(0, n1, stage2, 0)

    def unpitch(k2, carry):
        src = pl.multiple_of(k2 * P, V7X_SUBLANES)
        dst = pl.multiple_of(k2 * n1, n1)
        for ri in range(2):
            for s in range(nl):
                c0 = (ri * nl + s) * V7X_LANES
                x2_ref[pl.ds(dst, n1), c0:c0 + V7X_LANES] = xp_ref[ri * nl + s, pl.ds(src, n1), :].astype(bf16)
        return carry

    lax.fori_loop(0, n2, unpitch, 0)

    tr = 512
    w = w_ref[...]

    def mix(i, carry):
        r0 = pl.multiple_of(i * tr, tr)
        y = jnp.dot(x2_ref[pl.ds(r0, tr), :], w, preferred_element_type=f32)
        o_ref[pl.ds(r0, tr), :] = (y * _silu(z_ref[pl.ds(r0, tr), :].astype(f32))).astype(bf16)
        return carry

    lax.fori_loop(0, (n1 * n2) // tr, mix, 0)


def _fourier_latent(ub, zb, wcs):
    B, T, _ = ub.shape
    assert T == FFT_N1 * FFT_N2
    m1, m2 = _fft_tables()
    G, n = N_GROUPS_B, GROUP_B
    nl = n // V7X_LANES
    rows_p = FFT_N1 * FFT_PITCH
    blk = pl.BlockSpec((None, T, n), lambda b, g: (b, 0, g))
    est = (3 * 2 * T * n * 2 + 5 * nl * rows_p * V7X_LANES * 4 + T * 2 * n * 2 + 2 * m1.size * 2 + 8 * 512 * n * 4)
    return pl.pallas_call(
        _fourier_kernel,
        grid=(B, G),
        in_specs=[blk, blk,
                  pl.BlockSpec(m1.shape, lambda b, g: (0, 0, 0)),
                  pl.BlockSpec(m2.shape, lambda b, g: (0, 0)),
                  pl.BlockSpec((None, 2 * n, n), lambda b, g: (g, 0, 0))],
        out_specs=blk,
        out_shape=jax.ShapeDtypeStruct((B, T, D_B), bf16),
        scratch_shapes=[pltpu.VMEM((nl, rows_p, V7X_LANES), f32),
                        pltpu.VMEM((2 * nl, rows_p, V7X_LANES), f32),
                        pltpu.VMEM((2 * nl, rows_p, V7X_LANES), f32),
                        pltpu.VMEM((T, 2 * n), bf16)],
        compiler_params=pltpu.CompilerParams(dimension_semantics=("parallel", "parallel"),
                                             vmem_limit_bytes=_vmem_limit(est)),
        name="fourier_latent",
    )(ub, zb, m1, m2, wcs)


def _fourier_ctx_kernel(u_ref, z_ref, cs_ref, w_ref, o_ref):
    u = u_ref[...]
    x = jnp.dot(cs_ref[...], u, preferred_element_type=f32).astype(bf16)
    t = u.shape[0]
    w = w_ref[...]
    n = GROUP_B
    y = (jnp.dot(x[:t], w[:n], preferred_element_type=f32) + jnp.dot(x[t:], w[n:], preferred_element_type=f32))
    o_ref[...] = (y * _silu(z_ref[...].astype(f32))).astype(bf16)


def _fourier_ctx(ub, zb, wcs):
    B, T, _ = ub.shape
    ang = 2.0 * np.pi * (np.outer(np.arange(T), np.arange(T)) % T) / T
    cs = jnp.asarray(np.concatenate([np.cos(ang), -np.sin(ang)], axis=0), bf16)
    n = GROUP_B
    blk = pl.BlockSpec((None, T, n), lambda b, g: (b, 0, g))
    return pl.pallas_call(
        _fourier_ctx_kernel,
        grid=(B, N_GROUPS_B),
        in_specs=[blk, blk, pl.BlockSpec(cs.shape, lambda b, g: (0, 0)),
                  pl.BlockSpec((None, 2 * n, n), lambda b, g: (g, 0, 0))],
        out_specs=blk,
        out_shape=jax.ShapeDtypeStruct((B, T, D_B), bf16),
        compiler_params=pltpu.CompilerParams(dimension_semantics=("parallel", "parallel")),
        name="fourier_ctx",
    )(ub, zb, cs, wcs)


def _outproj_even_kernel(x_ref, h_ref, o_ref, za_ref, yb_ref, gate_ref, hg_ref, w_ref, out_ref):
    parts = []
    for hd in range(N_HEADS_A):
        sl = slice(hd * HEAD_DIM_A, (hd + 1) * HEAD_DIM_A)
        hh = jax.nn.sigmoid(o_ref[:, sl].astype(f32)) * h_ref[:, sl]
        hh = hh * lax.rsqrt(jnp.mean(hh * hh, axis=-1, keepdims=True) + EPS)
        parts.append((hh * hg_ref[:, sl] * _silu(za_ref[:, sl].astype(f32))).astype(bf16))
    ya = jnp.concatenate(parts, axis=1)
    acc = jnp.dot(ya, w_ref[:D_A, :], preferred_element_type=f32)
    acc = acc + jnp.dot(yb_ref[...], w_ref[D_A:, :], preferred_element_type=f32)
    out_ref[...] = x_ref[...] + gate_ref[...] * acc


def _outproj_even(x, h, o, za, yb, gate, head_g, wout, tm):
    B, T, D = x.shape
    row = lambda n: pl.BlockSpec((None, tm, n), lambda b, i: (b, i, 0))
    est = 2 * wout.size * 2 + 2 * tm * (2 * D * 4 + D_A * 4 + 3 * D_A * 2) + 8 * tm * D * 4
    return pl.pallas_call(
        _outproj_even_kernel,
        grid=(B, T // tm),
        in_specs=[row(D), row(D_A), row(D_A), row(D_A), row(D_B),
                  pl.BlockSpec((None, 1, D), lambda b, i: (b, 0, 0)),
                  pl.BlockSpec((1, D_A), lambda b, i: (0, 0)),
                  pl.BlockSpec(wout.shape, lambda b, i: (0, 0))],
        out_specs=row(D),
        out_shape=jax.ShapeDtypeStruct((B, T, D), f32),
        compiler_params=pltpu.CompilerParams(dimension_semantics=("parallel", "parallel"),
                                             vmem_limit_bytes=_vmem_limit(est)),
        name="outproj_even",
    )(x, h, o, za, yb, gate, head_g.reshape(1, D_A), wout)


def _inproj_odd_kernel(x_ref, sh_ref, sc_ref, g_ref, w_ref, u_ref, z_ref):
    amp = g_ref[...] * (1.0 + sc_ref[...])
    hx = _normed(x_ref[...], amp, sh_ref[...]).astype(bf16)
    cn = 512
    for j in range(D_INNER // cn):
        u_ref[:, j * cn:(j + 1) * cn] = jnp.dot(hx, w_ref[:, j * cn:(j + 1) * cn],
                                                preferred_element_type=f32).astype(bf16)
        z_ref[:, j * cn:(j + 1) * cn] = jnp.dot(hx, w_ref[:, D_INNER + j * cn:D_INNER + (j + 1) * cn],
                                                preferred_element_type=f32).astype(bf16)


def _inproj_odd(x, shift, scale, norm_g, w, tm):
    B, T, D = x.shape
    row = lambda n: pl.BlockSpec((None, tm, n), lambda b, i: (b, i, 0))
    est = 2 * w.size * 2 + 2 * tm * D * 4 + 4 * tm * D_INNER * 2 + 6 * tm * 512 * 4
    return pl.pallas_call(
        _inproj_odd_kernel,
        grid=(B, T // tm),
        in_specs=[row(D),
                  pl.BlockSpec((None, 1, D), lambda b, i: (b, 0, 0)),
                  pl.BlockSpec((None, 1, D), lambda b, i: (b, 0, 0)),
                  pl.BlockSpec((1, D), lambda b, i: (0, 0)),
                  pl.BlockSpec(w.shape, lambda b, i: (0, 0))],
        out_specs=[row(D_INNER), row(D_INNER)],
        out_shape=[jax.ShapeDtypeStruct((B, T, D_INNER), bf16)] * 2,
        compiler_params=pltpu.CompilerParams(dimension_semantics=("parallel", "parallel"),
                                             vmem_limit_bytes=_vmem_limit(est)),
        name="inproj_odd",
    )(x, shift, scale, norm_g.reshape(1, D), w)


POOL_PAD = max(POOL_WINDOWS) // 2


def _pool_tables():
    w_idx = np.arange(GRID_W)
    band = np.zeros((N_GROUPS_C, GRID_W, GRID_W), np.float32)
    inv_w = np.zeros((N_GROUPS_C, GRID_W, V7X_LANES), np.float32)
    for g, win in enumerate(POOL_WINDOWS):
        lo = np.clip(w_idx - win // 2, 0, GRID_W)
        hi = np.clip(w_idx + win - win // 2, 0, GRID_W)
        band[g] = (w_idx[None, :] >= lo[:, None]) & (w_idx[None, :] < hi[:, None])
        inv_w[g] = (1.0 / (hi - lo))[:, None]
    return jnp.asarray(band, bf16), jnp.asarray(inv_w, f32)


def _pool_kernel(u_ref, z_ref, band_ref, invw_ref, pw_ref, sc_ref, o_ref, sw_ref, yp_ref, *, rows):
    g = pl.program_id(1)
    W = GRID_W
    band = band_ref[...]
    zero_blk = jnp.zeros((W, GROUP_C), f32)
    for pb in range(POOL_PAD):
        sw_ref[pb * W:(pb + 1) * W, :] = zero_blk
        sw_ref[(POOL_PAD + rows + pb) * W:(POOL_PAD + rows + pb + 1) * W, :] = zero_blk

    def width_sum(r, carry):
        src = pl.multiple_of(r * W, W)
        dst = pl.multiple_of((r + POOL_PAD) * W, W)
        sw_ref[pl.ds(dst, W), :] = jnp.dot(band, u_ref[pl.ds(src, W), :], preferred_element_type=f32)
        return carry

    lax.fori_loop(0, rows, width_sum, 0)

    inv_w = invw_ref[:, 0:1]

    def make_rows_pass(win):
        lo_off, hi_off = win // 2, win - win // 2

        def rows_pass(r, carry):
            acc = None
            for off in range(-lo_off, hi_off):
                src = pl.multiple_of((r + POOL_PAD + off) * W, W)
                blk = sw_ref[pl.ds(src, W), :]
                acc = blk if acc is None else acc + blk
            cnt = jnp.minimum(r + hi_off, rows) - jnp.maximum(r - lo_off, 0)
            src = pl.multiple_of(r * W, W)
            ug = u_ref[pl.ds(src, W), :].astype(f32)
            m = acc * (inv_w / cnt.astype(f32))
            yp_ref[pl.ds(src, W), :] = (m - ug).astype(bf16)
            return carry

        return rows_pass

    for gi, win in enumerate(POOL_WINDOWS):
        @pl.when(g == gi)
        def _():
            lax.fori_loop(0, rows, make_rows_pass(win), 0)

    tr = 512
    pw = pw_ref[...]
    scale = sc_ref[...]

    def mix(i, carry):
        r0 = pl.multiple_of(i * tr, tr)
        y = jnp.dot(yp_ref[pl.ds(r0, tr), :], pw, preferred_element_type=f32) * scale
        o_ref[pl.ds(r0, tr), :] = (y * _silu(z_ref[pl.ds(r0, tr), :].astype(f32))).astype(bf16)
        return carry

    lax.fori_loop(0, (rows * W) // tr, mix, 0)


def _pool_mix(u, z, pool_w, scale):
    B, T, _ = u.shape
    rows = T // GRID_W
    band, inv_w = _pool_tables()
    n = GROUP_C
    blk = pl.BlockSpec((None, T, n), lambda b, g: (b, 0, g))
    est = 3 * 2 * T * n * 2 + (rows + 2 * POOL_PAD) * GRID_W * n * 4 + T * n * 2 + 2 * n * n * 2 + 8 * 512 * n * 4
    return pl.pallas_call(
        functools.partial(_pool_kernel, rows=rows),
        grid=(B, N_GROUPS_C),
        in_specs=[blk, blk,
                  pl.BlockSpec((None, GRID_W, GRID_W), lambda b, g: (g, 0, 0)),
                  pl.BlockSpec((None, GRID_W, V7X_LANES), lambda b, g: (g, 0, 0)),
                  pl.BlockSpec((None, n, n), lambda b, g: (g, 0, 0)),
                  pl.BlockSpec((None, 1, n), lambda b, g: (g, 0, 0))],
        out_specs=blk,
        out_shape=jax.ShapeDtypeStruct((B, T, D_INNER), bf16),
        scratch_shapes=[pltpu.VMEM(((rows + 2 * POOL_PAD) * GRID_W, n), f32),
                        pltpu.VMEM((T, n), bf16)],
        compiler_params=pltpu.CompilerParams(dimension_semantics=("parallel", "arbitrary"),
                                             vmem_limit_bytes=_vmem_limit(est)),
        name="pool_mix",
    )(u, z, band, inv_w, pool_w, scale.reshape(N_GROUPS_C, 1, n))


def _outproj_odd_kernel(x_ref, y_ref, gate_ref, fg_ref, w_ref, out_ref):
    acc = jnp.dot(y_ref[...], w_ref[...], preferred_element_type=f32)
    x = x_ref[...] + gate_ref[...] * acc
    out_ref[...] = x * lax.rsqrt(jnp.mean(x * x, axis=-1, keepdims=True) + EPS) * fg_ref[...]


def _outproj_odd(x, y, gate, final_g, wout, tm):
    B, T, D = x.shape
    row = lambda n: pl.BlockSpec((None, tm, n), lambda b, i: (b, i, 0))
    est = 2 * wout.size * 2 + 2 * tm * (2 * D * 4 + D_INNER * 2) + 6 * tm * D * 4
    return pl.pallas_call(
        _outproj_odd_kernel,
        grid=(B, T // tm),
        in_specs=[row(D), row(D_INNER),
                  pl.BlockSpec((None, 1, D), lambda b, i: (b, 0, 0)),
                  pl.BlockSpec((1, D), lambda b, i: (0, 0)),
                  pl.BlockSpec(wout.shape, lambda b, i: (0, 0))],
        out_specs=row(D),
        out_shape=jax.ShapeDtypeStruct((B, T, D), f32),
        compiler_params=pltpu.CompilerParams(dimension_semantics=("parallel", "parallel"),
                                             vmem_limit_bytes=_vmem_limit(est)),
        name="outproj_odd",
    )(x, y, gate, final_g.reshape(1, D), wout)


def kernel(x, c, ctx, c_ctx, ada_w, ada_b, norm_g, win_even, gate_b_even, conv_qk_even, head_norm_even,
           fourier_w_even, wout_even, win_odd, pool_w_odd, pool_scale_odd, wout_odd, final_g):
    B, T, D = x.shape
    Tc = ctx.shape[1]
    H = N_HEADS_A
    L = MLSTM_CHUNK

    nrow = -(-(B + 1) // V7X_SUBLANES) * V7X_SUBLANES
    rows_in = jnp.zeros((nrow, D), f32).at[:B].set(c).at[B].set(c_ctx)
    mod = _modulation(rows_in, ada_w, ada_b)

    def mod_parts(l, r0, r1, n):
        m = mod[l, r0:r1]
        parts = [jnp.broadcast_to(m[:, None, i * D:(i + 1) * D], (n, 1, D)) for i in range(3)]
        return parts

    we = win_even[0]
    w_main = we[:, :W_MAIN_EVEN].astype(bf16)
    gcols = we[:, W_MAIN_EVEN:].reshape(D, 4, H)
    gbias = gate_b_even[0].reshape(4, H)
    pad = jnp.zeros((D, V7X_LANES - 2 * H), f32)
    wg = jnp.concatenate([gcols[:, 0], gcols[:, 2], pad, gcols[:, 1], gcols[:, 3], pad], axis=1).astype(bf16)
    bpad = jnp.zeros((V7X_LANES - 2 * H,), f32)
    gb = jnp.concatenate([gbias[0], gbias[2], bpad, gbias[1], gbias[3], bpad]).reshape(1, 2 * V7X_LANES)
    wgt = jnp.concatenate([gcols[:, 0], gcols[:, 2], gcols[:, 1], gcols[:, 3]], axis=1).T.astype(bf16)
    gbt = jnp.concatenate([gbias[0], gbias[2], gbias[1], gbias[3]]).reshape(N_GATES, 1)
    conv_w = conv_qk_even[0]

    shift_x, scale_x, gate_x = mod_parts(0, 0, B, B)
    shift_c, scale_c, gate_c = mod_parts(0, B, B + 1, B)

    qx, kx, vx, ox, zax, ubx, zbx, gcx, gtx = _inproj_even(
        x, shift_x, scale_x, norm_g[0], w_main, wg, wgt, gb, gbt, conv_w, tm=512)
    qc, kc, vc, oc, zac, ubc, zbc, gcc, gtc = _inproj_even(
        ctx, shift_c, scale_c, norm_g[0], w_main, wg, wgt, gb, gbt, conv_w, tm=Tc)

    colsx, rowsx = _gate_prep(gcx, gtx, L)
    colsc, rowsc = _gate_prep(gcc, gtc, L)
    h_x, h_c = _mlstm(qx, kx, vx, colsx, rowsx, qc, kc, vc, colsc, rowsc, L)

    wcs = _fourier_weights(fourier_w_even[0], T)
    yb_x = _fourier_latent(ubx, zbx, wcs)
    wout_e = wout_even[0].astype(bf16)
    x1 = _outproj_even(x, h_x, ox, zax, yb_x, gate_x, head_norm_even[0], wout_e, tm=512)

    wcs_c = _fourier_weights(fourier_w_even[0], Tc)
    yb_c = _fourier_ctx(ubc, zbc, wcs_c)
    ctx1 = _outproj_even(ctx, h_c, oc, zac, yb_c, gate_c, head_norm_even[0], wout_e, tm=Tc)
    del ctx1

    shift_x, scale_x, gate_x = mod_parts(1, 0, B, B)
    u1, z1 = _inproj_odd(x1, shift_x, scale_x, norm_g[1], win_odd[0].astype(bf16), tm=512)
    y1 = _pool_mix(u1, z1, pool_w_odd[0].astype(bf16), pool_scale_odd[0])
    return _outproj_odd(x1, y1, gate_x, final_g, wout_odd[0].astype(bf16), tm=512)
```

```python
import functools

import numpy as np
import jax
import jax.numpy as jnp
from jax import lax
from jax.experimental import pallas as pl
from jax.experimental.pallas import tpu as pltpu

D_MODEL = 1024
DEPTH = 2
CTX_LEN = 256
GRID_W = 64
D_INNER = 2 * D_MODEL
D_A = D_INNER // 2
D_B = D_INNER - D_A
N_HEADS_A = 4
HEAD_DIM_A = D_A // N_HEADS_A
N_GROUPS_B = 4
GROUP_B = D_B // N_GROUPS_B
N_GROUPS_C = 4
GROUP_C = D_INNER // N_GROUPS_C
POOL_WINDOWS = (2, 4, 8, 16)
CONV_W = 3
N_GATES = 4 * N_HEADS_A
W_MAIN_EVEN = 5 * D_A + 2 * D_B
EPS = 1e-6

f32 = jnp.float32
bf16 = jnp.bfloat16

V7X_VMEM_BYTES = 64 * 1024 * 1024
V7X_LANES = 128
V7X_SUBLANES = 8

MLSTM_CHUNK = 256
FFT_N1 = 64
FFT_N2 = 64
FFT_PITCH = 72
FFT_UNROLL = 8
NEG_BIG = -1e30


def _vmem_limit(nbytes):
    return int(min(max(nbytes * 5 // 4 + (4 << 20), 16 << 20), V7X_VMEM_BYTES - (6 << 20)))


def _silu(v):
    return v * jax.nn.sigmoid(v)


def _log_sigmoid(v):
    return jnp.minimum(v, 0.0) - jnp.log1p(jnp.exp(-jnp.abs(v)))


def _mod_kernel(r_ref, w_ref, b_ref, o_ref):
    s = _silu(r_ref[...])
    o_ref[...] = jnp.dot(s, w_ref[...], preferred_element_type=f32,
                         precision=lax.Precision.HIGHEST) + b_ref[...]


def _modulation(rows, ada_w, ada_b):
    nrow = rows.shape[0]
    tn = 1024
    return pl.pallas_call(
        _mod_kernel,
        grid=(DEPTH, 3 * D_MODEL // tn),
        in_specs=[
            pl.BlockSpec((nrow, D_MODEL), lambda l, j: (0, 0)),
            pl.BlockSpec((None, D_MODEL, tn), lambda l, j: (l, 0, j)),
            pl.BlockSpec((None, 1, tn), lambda l, j: (l, 0, j)),
        ],
        out_specs=pl.BlockSpec((None, nrow, tn), lambda l, j: (l, 0, j)),
        out_shape=jax.ShapeDtypeStruct((DEPTH, nrow, 3 * D_MODEL), f32),
        compiler_params=pltpu.CompilerParams(dimension_semantics=("arbitrary", "arbitrary")),
        name="modulation",
    )(rows, ada_w, ada_b.reshape(DEPTH, 1, 3 * D_MODEL))


def _normed(x, amp, shift):
    ms = jnp.mean(x * x, axis=-1, keepdims=True)
    return (x * lax.rsqrt(ms + EPS)) * amp + shift


def _inproj_even_kernel(x_ref, xp_ref, xn_ref, sh_ref, sc_ref, g_ref, w_ref, wg_ref, wgt_ref, gb_ref, gbt_ref,
                        cw_ref, q_ref, k_ref, v_ref, o_ref, za_ref, ub_ref, zb_ref, gc_ref, gt_ref, *, tm, nt):
    i = pl.program_id(1)
    amp = g_ref[...] * (1.0 + sc_ref[...])
    shift = sh_ref[...]
    hx = _normed(x_ref[...], amp, shift).astype(bf16)
    halo = jnp.concatenate([xp_ref[...], xn_ref[...]], axis=0)
    hh = _normed(halo, amp, shift).astype(bf16)
    has_prev = (i > 0).astype(f32)
    has_next = (i < nt - 1).astype(f32)
    row = lax.broadcasted_iota(jnp.int32, (tm, 1), 0)
    cn = 512
    for j in range(2 * D_A // cn):
        w = w_ref[:, j * cn:(j + 1) * cn]
        p = jnp.dot(hx, w, preferred_element_type=f32)
        ph = jnp.dot(hh, w, preferred_element_type=f32)
        prev = ph[V7X_SUBLANES - 1:V7X_SUBLANES, :] * has_prev
        nxt = ph[V7X_SUBLANES:V7X_SUBLANES + 1, :] * has_next
        up = jnp.where(row == 0, prev, pltpu.roll(p, 1, 0))
        dn = jnp.where(row == tm - 1, nxt, pltpu.roll(p, tm - 1, 0))
        cw = cw_ref[:, j * cn:(j + 1) * cn]
        y = _silu(cw[0:1, :] * up + cw[1:2, :] * p + cw[2:3, :] * dn)
        if j < D_A // cn:
            q_ref[:, j * cn:(j + 1) * cn] = (y * (HEAD_DIM_A ** -0.5)).astype(bf16)
        else:
            jj = j - D_A // cn
            k_ref[:, jj * cn:(jj + 1) * cn] = y.astype(bf16)
    for idx, ref in enumerate((v_ref, o_ref, za_ref, ub_ref, zb_ref)):
        for jj in range(D_A // cn):
            c0 = 2 * D_A + idx * D_A + jj * cn
            ref[:, jj * cn:(jj + 1) * cn] = jnp.dot(hx, w_ref[:, c0:c0 + cn],
                                                    preferred_element_type=f32).astype(bf16)
    gc_ref[...] = jnp.dot(hx, wg_ref[...], preferred_element_type=f32) + gb_ref[...]
    gt_ref[...] = lax.dot_general(wgt_ref[...], hx, (((1,), (1,)), ((), ())),
                                  preferred_element_type=f32) + gbt_ref[...]


def _inproj_even(x, shift, scale, norm_g, w_main, wg, wgt, gb, gbt, conv_w, tm):
    B, T, D = x.shape
    nt = T // tm
    hb = tm // V7X_SUBLANES
    nhb = T // V7X_SUBLANES
    row_spec = pl.BlockSpec((None, tm, D_A), lambda b, i: (b, i, 0))
    vec = lambda n: pl.BlockSpec((1, n), lambda b, i: (0, 0))
    est = (2 * w_main.size * 2 + 2 * tm * D * 4 + 7 * 2 * tm * D_A * 2 + 2 * tm * 256 * 4 + 6 * tm * 512 * 4)
    outs = pl.pallas_call(
        functools.partial(_inproj_even_kernel, tm=tm, nt=nt),
        grid=(B, nt),
        in_specs=[
            pl.BlockSpec((None, tm, D), lambda b, i: (b, i, 0)),
            pl.BlockSpec((None, V7X_SUBLANES, D), lambda b, i: (b, jnp.maximum(i * hb - 1, 0), 0)),
            pl.BlockSpec((None, V7X_SUBLANES, D), lambda b, i: (b, jnp.minimum((i + 1) * hb, nhb - 1), 0)),
            pl.BlockSpec((None, 1, D), lambda b, i: (b, 0, 0)),
            pl.BlockSpec((None, 1, D), lambda b, i: (b, 0, 0)),
            vec(D),
            pl.BlockSpec(w_main.shape, lambda b, i: (0, 0)),
            pl.BlockSpec(wg.shape, lambda b, i: (0, 0)),
            pl.BlockSpec(wgt.shape, lambda b, i: (0, 0)),
            vec(2 * V7X_LANES),
            pl.BlockSpec((N_GATES, 1), lambda b, i: (0, 0)),
            pl.BlockSpec(conv_w.shape, lambda b, i: (0, 0)),
        ],
        out_specs=[row_spec] * 7 + [
            pl.BlockSpec((None, tm, 2 * V7X_LANES), lambda b, i: (b, i, 0)),
            pl.BlockSpec((None, N_GATES, tm), lambda b, i: (b, 0, i)),
        ],
        out_shape=[jax.ShapeDtypeStruct((B, T, D_A), bf16)] * 7 + [
            jax.ShapeDtypeStruct((B, T, 2 * V7X_LANES), f32),
            jax.ShapeDtypeStruct((B, N_GATES, T), f32),
        ],
        compiler_params=pltpu.CompilerParams(dimension_semantics=("parallel", "arbitrary"),
                                             vmem_limit_bytes=_vmem_limit(est)),
        name="inproj_even",
    )(x, x, x, shift, scale, norm_g.reshape(1, D), w_main, wg, wgt, gb, gbt, conv_w)
    return outs


def _split3(v):
    hi = v.astype(bf16)
    r1 = v - hi.astype(f32)
    mid = r1.astype(bf16)
    lo = (r1 - mid.astype(f32)).astype(bf16)
    return hi, mid, lo


def _gate_prep_kernel(gc_ref, gt_ref, bcol_ref, acol_ref, arow_ref, *, L):
    r = lax.broadcasted_iota(jnp.int32, (L, L), 0)
    c = lax.broadcasted_iota(jnp.int32, (L, L), 1)
    tri_l = (c <= r).astype(bf16)
    tri_u = (c >= r).astype(bf16)

    g = gc_ref[...]
    ig = g[:, :V7X_LANES]
    lf = _log_sigmoid(g[:, V7X_LANES:])
    pre = suf = None
    for piece in _split3(lf):
        a = jnp.dot(tri_l, piece, preferred_element_type=f32)
        b = jnp.dot(tri_u, piece, preferred_element_type=f32)
        pre = a if pre is None else pre + a
        suf = b if suf is None else suf + b
    lane = lax.broadcasted_iota(jnp.int32, (L, V7X_LANES), 1)
    bcol = jnp.where(lane < N_HEADS_A, pre, suf)
    bcol_ref[...] = bcol
    acol_ref[...] = ig - bcol

    gt = gt_ref[...]
    igt = gt[:V7X_SUBLANES, :]
    lft = _log_sigmoid(gt[V7X_SUBLANES:, :])
    pre = suf = None
    for piece in _split3(lft):
        a = jnp.dot(piece, tri_u, preferred_element_type=f32)
        b = jnp.dot(piece, tri_l, preferred_element_type=f32)
        pre = a if pre is None else pre + a
        suf = b if suf is None else suf + b
    sub = lax.broadcasted_iota(jnp.int32, (V7X_SUBLANES, L), 0)
    arow_ref[...] = igt - jnp.where(sub < N_HEADS_A, pre, suf)


def _gate_prep(gc, gt, L):
    B, T, _ = gc.shape
    bcol, acol, arow = pl.pallas_call(
        functools.partial(_gate_prep_kernel, L=L),
        grid=(B, T // L),
        in_specs=[
            pl.BlockSpec((None, L, 2 * V7X_LANES), lambda b, i: (b, i, 0)),
            pl.BlockSpec((None, N_GATES, L), lambda b, i: (b, 0, i)),
        ],
        out_specs=[
            pl.BlockSpec((None, L, V7X_LANES), lambda b, i: (b, i, 0)),
            pl.BlockSpec((None, L, V7X_LANES), lambda b, i: (b, i, 0)),
            pl.BlockSpec((None, V7X_SUBLANES, L), lambda b, i: (b, 0, i)),
        ],
        out_shape=[
            jax.ShapeDtypeStruct((B, T, V7X_LANES), f32),
            jax.ShapeDtypeStruct((B, T, V7X_LANES), f32),
            jax.ShapeDtypeStruct((B, V7X_SUBLANES, T), f32),
        ],
        compiler_params=pltpu.CompilerParams(dimension_semantics=("parallel", "parallel")),
        name="gate_prep",
    )(gc, gt)
    H = N_HEADS_A
    cols = jnp.stack([bcol[..., 0:H], acol[..., 0:H], bcol[..., H:2 * H], acol[..., H:2 * H]], axis=-1)
    cols = cols.transpose(0, 2, 1, 3)
    rows = arow.reshape(B, 2, H, T).transpose(0, 2, 1, 3)
    return cols, rows


def _mlstm_kernel(qx_ref, kx_ref, vx_ref, cx_ref, rx_ref, qc_ref, kc_ref, vc_ref, cc_ref, rc_ref,
                  hx_ref, hc_ref, C_ref, n_ref, *, L, nx, nc):
    r_i = lax.broadcasted_iota(jnp.int32, (L, L), 0)
    c_i = lax.broadcasted_iota(jnp.int32, (L, L), 1)

    def chunk(q_ref, k_ref, v_ref, col_ref, row_ref, r0, d, m_prev):
        rows = pl.ds(r0, L)
        q = q_ref[rows, :]
        k = k_ref[rows, :]
        v = v_ref[rows, :]
        cols = col_ref[rows, :]
        b_col = cols[:, 2 * d:2 * d + 1]
        a_col = cols[:, 2 * d + 1:2 * d + 2]
        a_row = row_ref[d:d + 1, rows]
        mask = (c_i <= r_i) if d == 0 else (c_i >= r_i)
        dm = jnp.where(mask, b_col + a_row, NEG_BIG)
        m_inter = b_col + m_prev
        m_t = jnp.maximum(jnp.max(dm, axis=1, keepdims=True), m_inter)
        s = lax.dot_general(q, k, (((1,), (1,)), ((), ())), preferred_element_type=f32) * jnp.exp(dm - m_t)
        inter = jnp.exp(m_inter - m_t)
        qC = jnp.dot(q, C_ref[...].astype(bf16), preferred_element_type=f32)
        qn = jnp.sum(q.astype(f32) * n_ref[...], axis=1, keepdims=True)
        num = jnp.dot(s.astype(bf16), v, preferred_element_type=f32) + inter * qC
        den = jnp.sum(s, axis=1, keepdims=True) + inter * qn
        h = num / jnp.maximum(jnp.abs(den), jnp.exp(-m_t))

        b_end = b_col[L - 1:L, :] if d == 0 else b_col[0:1, :]
        w_col = b_end + a_col
        m_new = jnp.maximum(b_end + m_prev, jnp.max(w_col, axis=0, keepdims=True))
        decay = jnp.exp(b_end + m_prev - m_new)
        kw = k.astype(f32) * jnp.exp(w_col - m_new)
        C_ref[...] = decay * C_ref[...] + lax.dot_general(kw.astype(bf16), v, (((0,), (0,)), ((), ())),
                                                           preferred_element_type=f32)
        n_ref[...] = decay * n_ref[...] + jnp.sum(kw, axis=0, keepdims=True)
        return h, m_new

    def reset():
        C_ref[...] = jnp.zeros_like(C_ref)
        n_ref[...] = jnp.zeros_like(n_ref)
        return jnp.zeros((1, 1), f32)

    ctx = (qc_ref, kc_ref, vc_ref, cc_ref, rc_ref)
    lat = (qx_ref, kx_ref, vx_ref, cx_ref, rx_ref)

    m = reset()
    for ci in range(nc):
        h, m = chunk(*ctx, ci * L, 0, m)
        hc_ref[ci * L:(ci + 1) * L, :] = h

    def fwd_body(ci, m):
        r0 = pl.multiple_of(ci * L, L)
        h, m = chunk(*lat, r0, 0, m)
        hx_ref[pl.ds(r0, L), :] = h
        return m

    lax.fori_loop(0, nx, fwd_body, m)

    m = reset()
    for ci in reversed(range(nc)):
        h, m = chunk(*ctx, ci * L, 1, m)
        hc_ref[ci * L:(ci + 1) * L, :] += h

    def bwd_body(j, m):
        r0 = pl.multiple_of((nx - 1 - j) * L, L)
        h, m = chunk(*lat, r0, 1, m)
        hx_ref[pl.ds(r0, L), :] += h
        return m

    lax.fori_loop(0, nx, bwd_body, m)


def _mlstm(qx, kx, vx, colsx, rowsx, qc, kc, vc, colsc, rowsc, L):
    B, T, _ = qx.shape
    Tc = qc.shape[1]
    H, Dh = N_HEADS_A, HEAD_DIM_A

    def seq_spec(t):
        return pl.BlockSpec((None, t, Dh), lambda b, h: (b, 0, h))

    def col_spec(t):
        return pl.BlockSpec((None, None, t, 4), lambda b, h: (b, h, 0, 0))

    def row_spec(t):
        return pl.BlockSpec((None, None, 2, t), lambda b, h: (b, h, 0, 0))

    est = (2 * (3 * (T + Tc) * Dh * 2 + (T + Tc) * V7X_LANES * 4 + 8 * (T + Tc) * 4 + (T + Tc) * Dh * 4)
           + Dh * Dh * 4 + 16 * L * L * 4)
    return pl.pallas_call(
        functools.partial(_mlstm_kernel, L=L, nx=T // L, nc=Tc // L),
        grid=(B, H),
        in_specs=[seq_spec(T), seq_spec(T), seq_spec(T), col_spec(T), row_spec(T),
                  seq_spec(Tc), seq_spec(Tc), seq_spec(Tc), col_spec(Tc), row_spec(Tc)],
        out_specs=[seq_spec(T), seq_spec(Tc)],
        out_shape=[jax.ShapeDtypeStruct((B, T, D_A), f32), jax.ShapeDtypeStruct((B, Tc, D_A), f32)],
        scratch_shapes=[pltpu.VMEM((Dh, Dh), f32), pltpu.VMEM((1, Dh), f32)],
        compiler_params=pltpu.CompilerParams(dimension_semantics=("parallel", "parallel"),
                                             vmem_limit_bytes=_vmem_limit(est)),
        name="mlstm",
    )(qx, kx, vx, colsx, rowsx, qc, kc, vc, colsc, rowsc)


def _fourier_w_kernel(cs_ref, fw_ref, o_ref, *, scale):
    o_ref[...] = (jnp.dot(cs_ref[...], fw_ref[...], preferred_element_type=f32,
                          precision=lax.Precision.HIGHEST) * scale).astype(bf16)


def _fourier_weights(fw, T):
    n = GROUP_B
    kk = np.outer(np.arange(n), np.arange(n)) % n
    ang = 2.0 * np.pi * kk / n
    cs = jnp.asarray(np.concatenate([np.cos(ang), np.sin(ang)], axis=0), f32)
    return pl.pallas_call(
        functools.partial(_fourier_w_kernel, scale=float(1.0 / np.sqrt(T * n))),
        grid=(N_GROUPS_B,),
        in_specs=[pl.BlockSpec((2 * n, n), lambda g: (0, 0)),
                  pl.BlockSpec((None, n, n), lambda g: (g, 0, 0))],
        out_specs=pl.BlockSpec((None, 2 * n, n), lambda g: (g, 0, 0)),
        out_shape=jax.ShapeDtypeStruct((N_GROUPS_B, 2 * n, n), bf16),
        compiler_params=pltpu.CompilerParams(dimension_semantics=("arbitrary",)),
        name="fourier_weights",
    )(cs, fw)


def _fft_tables():
    n1, n2 = FFT_N1, FFT_N2
    n = n1 * n2
    t1 = np.arange(n1)
    k1 = np.arange(n1)
    t2 = np.arange(n2)
    idx = (k1[None, :, None] * (n2 * t1[None, None, :] + t2[:, None, None])) % n
    ang = 2.0 * np.pi * idx / n
    m1 = np.concatenate([np.cos(ang), -np.sin(ang)], axis=1)
    k2 = np.arange(n2)
    ph = 2.0 * np.pi * (np.outer(k2, t2) % n2) / n2
    c, s = np.cos(ph), np.sin(ph)
    m2 = np.block([[c, s], [-s, c]])
    return jnp.asarray(m1, f32).astype(bf16), jnp.asarray(m2, f32).astype(bf16)


def _fourier_kernel(u_ref, z_ref, m1_ref, m2_ref, w_ref, o_ref, up_ref, yp_ref, xp_ref, x2_ref):
    n1, n2, P = FFT_N1, FFT_N2, FFT_PITCH
    nl = GROUP_B // V7X_LANES

    def fill(t1, carry):
        src = pl.multiple_of(t1 * n2, n2)
        dst = pl.multiple_of(t1 * P, V7X_SUBLANES)
        blk = u_ref[pl.ds(src, n2), :].astype(f32)
        for s in range(nl):
            up_ref[s, pl.ds(dst, n2), :] = blk[:, s * V7X_LANES:(s + 1) * V7X_LANES]
        return carry

    lax.fori_loop(0, n1, fill, 0, unroll=4)

    def stage1(t2, carry):
        rhs = jnp.concatenate([up_ref[s, pl.ds(t2, n1, stride=P), :] for s in range(nl)], axis=1)
        y = jnp.dot(m1_ref[t2], rhs.astype(bf16), preferred_element_type=f32)
        dst = pl.multiple_of(t2 * P, V7X_SUBLANES)
        for ri in range(2):
            for s in range(nl):
                yp_ref[ri * nl + s, pl.ds(dst, n1), :] = y[ri * n1:(ri + 1) * n1,
                                                           s * V7X_LANES:(s + 1) * V7X_LANES]
        return carry

    lax.fori_loop(0, n2, stage1, 0, unroll=FFT_UNROLL)

    m2 = m2_ref[...]

    def stage2(k1, carry):
        parts = []
        for ri in range(2):
            parts.append(jnp.concatenate(
                [yp_ref[ri * nl + s, pl.ds(k1, n2, stride=P), :] for s in range(nl)], axis=1))
        rhs = jnp.concatenate(parts, axis=0).astype(bf16)
        x = jnp.dot(m2, rhs, preferred_element_type=f32)
        for ri in range(2):
            for s in range(nl):
                xp_ref[ri * nl + s, pl.ds(k1, n2, stride=P), :] = x[ri * n2:(ri + 1) * n2,
                                                                    s * V7X_LANES:(s + 1) * V7X_LANES]
        return carry

    lax.fori_loop(0, n1, stage2, 0, unroll=FFT_UNROLL)

    def unpitch(k2, carry):
        src = pl.multiple_of(k2 * P, V7X_SUBLANES)
        dst = pl.multiple_of(k2 * n1, n1)
        for ri in range(2):
            for s in range(nl):
                c0 = (ri * nl + s) * V7X_LANES
                x2_ref[pl.ds(dst, n1), c0:c0 + V7X_LANES] = xp_ref[ri * nl + s, pl.ds(src, n1), :].astype(bf16)
        return carry

    lax.fori_loop(0, n2, unpitch, 0, unroll=4)

    tr = 512
    w = w_ref[...]

    def mix(i, carry):
        r0 = pl.multiple_of(i * tr, tr)
        y = jnp.dot(x2_ref[pl.ds(r0, tr), :], w, preferred_element_type=f32)
        o_ref[pl.ds(r0, tr), :] = (y * _silu(z_ref[pl.ds(r0, tr), :].astype(f32))).astype(bf16)
        return carry

    lax.fori_loop(0, (n1 * n2) // tr, mix, 0)


def _fourier_latent(ub, zb, wcs):
    B, T, _ = ub.shape
    assert T == FFT_N1 * FFT_N2
    m1, m2 = _fft_tables()
    G, n = N_GROUPS_B, GROUP_B
    nl = n // V7X_LANES
    rows_p = FFT_N1 * FFT_PITCH
    blk = pl.BlockSpec((None, T, n), lambda b, g: (b, 0, g))
    est = (3 * 2 * T * n * 2 + 5 * nl * rows_p * V7X_LANES * 4 + T * 2 * n * 2 + 2 * m1.size * 2 + 8 * 512 * n * 4)
    return pl.pallas_call(
        _fourier_kernel,
        grid=(B, G),
        in_specs=[blk, blk,
                  pl.BlockSpec(m1.shape, lambda b, g: (0, 0, 0)),
                  pl.BlockSpec(m2.shape, lambda b, g: (0, 0)),
                  pl.BlockSpec((None, 2 * n, n), lambda b, g: (g, 0, 0))],
        out_specs=blk,
        out_shape=jax.ShapeDtypeStruct((B, T, D_B), bf16),
        scratch_shapes=[pltpu.VMEM((nl, rows_p, V7X_LANES), f32),
                        pltpu.VMEM((2 * nl, rows_p, V7X_LANES), f32),
                        pltpu.VMEM((2 * nl, rows_p, V7X_LANES), f32),
                        pltpu.VMEM((T, 2 * n), bf16)],
        compiler_params=pltpu.CompilerParams(dimension_semantics=("parallel", "parallel"),
                                             vmem_limit_bytes=_vmem_limit(est)),
        name="fourier_latent",
    )(ub, zb, m1, m2, wcs)


def _fourier_ctx_kernel(u_ref, z_ref, cs_ref, w_ref, o_ref):
    u = u_ref[...]
    x = jnp.dot(cs_ref[...], u, preferred_element_type=f32).astype(bf16)
    t = u.shape[0]
    w = w_ref[...]
    n = GROUP_B
    y = (jnp.dot(x[:t], w[:n], preferred_element_type=f32) + jnp.dot(x[t:], w[n:], preferred_element_type=f32))
    o_ref[...] = (y * _silu(z_ref[...].astype(f32))).astype(bf16)


def _fourier_ctx(ub, zb, wcs):
    B, T, _ = ub.shape
    ang = 2.0 * np.pi * (np.outer(np.arange(T), np.arange(T)) % T) / T
    cs = jnp.asarray(np.concatenate([np.cos(ang), -np.sin(ang)], axis=0), f32).astype(bf16)
    n = GROUP_B
    blk = pl.BlockSpec((None, T, n), lambda b, g: (b, 0, g))
    return pl.pallas_call(
        _fourier_ctx_kernel,
        grid=(B, N_GROUPS_B),
        in_specs=[blk, blk, pl.BlockSpec(cs.shape, lambda b, g: (0, 0)),
                  pl.BlockSpec((None, 2 * n, n), lambda b, g: (g, 0, 0))],
        out_specs=blk,
        out_shape=jax.ShapeDtypeStruct((B, T, D_B), bf16),
        compiler_params=pltpu.CompilerParams(dimension_semantics=("parallel", "parallel")),
        name="fourier_ctx",
    )(ub, zb, cs, wcs)


def _outproj_even_kernel(x_ref, h_ref, o_ref, za_ref, yb_ref, gate_ref, hg_ref, w_ref, out_ref):
    parts = []
    for hd in range(N_HEADS_A):
        sl = slice(hd * HEAD_DIM_A, (hd + 1) * HEAD_DIM_A)
        hh = jax.nn.sigmoid(o_ref[:, sl].astype(f32)) * h_ref[:, sl]
        hh = hh * lax.rsqrt(jnp.mean(hh * hh, axis=-1, keepdims=True) + EPS)
        parts.append((hh * hg_ref[:, sl] * _silu(za_ref[:, sl].astype(f32))).astype(bf16))
    ya = jnp.concatenate(parts, axis=1)
    acc = jnp.dot(ya, w_ref[:D_A, :], preferred_element_type=f32)
    acc = acc + jnp.dot(yb_ref[...], w_ref[D_A:, :], preferred_element_type=f32)
    out_ref[...] = x_ref[...] + gate_ref[...] * acc


def _outproj_even(x, h, o, za, yb, gate, head_g, wout, tm):
    B, T, D = x.shape
    row = lambda n: pl.BlockSpec((None, tm, n), lambda b, i: (b, i, 0))
    est = 2 * wout.size * 2 + 2 * tm * (2 * D * 4 + D_A * 4 + 3 * D_A * 2) + 8 * tm * D * 4
    return pl.pallas_call(
        _outproj_even_kernel,
        grid=(B, T // tm),
        in_specs=[row(D), row(D_A), row(D_A), row(D_A), row(D_B),
                  pl.BlockSpec((None, 1, D), lambda b, i: (b, 0, 0)),
                  pl.BlockSpec((1, D_A), lambda b, i: (0, 0)),
                  pl.BlockSpec(wout.shape, lambda b, i: (0, 0))],
        out_specs=row(D),
        out_shape=jax.ShapeDtypeStruct((B, T, D), f32),
        compiler_params=pltpu.CompilerParams(dimension_semantics=("parallel", "parallel"),
                                             vmem_limit_bytes=_vmem_limit(est)),
        name="outproj_even",
    )(x, h, o, za, yb, gate, head_g.reshape(1, D_A), wout)


def _inproj_odd_kernel(x_ref, sh_ref, sc_ref, g_ref, w_ref, u_ref, z_ref):
    amp = g_ref[...] * (1.0 + sc_ref[...])
    hx = _normed(x_ref[...], amp, sh_ref[...]).astype(bf16)
    cn = 512
    for j in range(D_INNER // cn):
        u_ref[:, j * cn:(j + 1) * cn] = jnp.dot(hx, w_ref[:, j * cn:(j + 1) * cn],
                                                preferred_element_type=f32).astype(bf16)
        z_ref[:, j * cn:(j + 1) * cn] = jnp.dot(hx, w_ref[:, D_INNER + j * cn:D_INNER + (j + 1) * cn],
                                                preferred_element_type=f32).astype(bf16)


def _inproj_odd(x, shift, scale, norm_g, w, tm):
    B, T, D = x.shape
    row = lambda n: pl.BlockSpec((None, tm, n), lambda b, i: (b, i, 0))
    est = 2 * w.size * 2 + 2 * tm * D * 4 + 4 * tm * D_INNER * 2 + 6 * tm * 512 * 4
    return pl.pallas_call(
        _inproj_odd_kernel,
        grid=(B, T // tm),
        in_specs=[row(D),
                  pl.BlockSpec((None, 1, D), lambda b, i: (b, 0, 0)),
                  pl.BlockSpec((None, 1, D), lambda b, i: (b, 0, 0)),
                  pl.BlockSpec((1, D), lambda b, i: (0, 0)),
                  pl.BlockSpec(w.shape, lambda b, i: (0, 0))],
        out_specs=[row(D_INNER), row(D_INNER)],
        out_shape=[jax.ShapeDtypeStruct((B, T, D_INNER), bf16)] * 2,
        compiler_params=pltpu.CompilerParams(dimension_semantics=("parallel", "parallel"),
                                             vmem_limit_bytes=_vmem_limit(est)),
        name="inproj_odd",
    )(x, shift, scale, norm_g.reshape(1, D), w)


POOL_UNROLL = 8


def _pool_tables():
    w_idx = np.arange(GRID_W)
    band = np.zeros((N_GROUPS_C, GRID_W, GRID_W), np.float32)
    inv_w = np.zeros((N_GROUPS_C, GRID_W, V7X_LANES), np.float32)
    for g, win in enumerate(POOL_WINDOWS):
        lo = np.clip(w_idx - win // 2, 0, GRID_W)
        hi = np.clip(w_idx + win - win // 2, 0, GRID_W)
        band[g] = (w_idx[None, :] >= lo[:, None]) & (w_idx[None, :] < hi[:, None])
        inv_w[g] = (1.0 / (hi - lo))[:, None]
    return jnp.asarray(band, bf16), jnp.asarray(inv_w, f32)


def _pool_kernel(u_ref, z_ref, band_ref, invw_ref, pw_ref, sc_ref, o_ref, ps_ref, yp_ref, *, rows):
    g = pl.program_id(1)
    W = GRID_W
    band = band_ref[...]
    lo_off = hi_off = 0
    for gi, win in enumerate(POOL_WINDOWS):
        lo_off = jnp.where(g == gi, win // 2, lo_off)
        hi_off = jnp.where(g == gi, win - win // 2, hi_off)

    ps_ref[0:W, :] = jnp.zeros((W, GROUP_C), f32)

    def width_sum(r, carry):
        src = pl.multiple_of(r * W, W)
        dst = pl.multiple_of((r + 1) * W, W)
        s = jnp.dot(band, u_ref[pl.ds(src, W), :], preferred_element_type=f32)
        ps_ref[pl.ds(dst, W), :] = ps_ref[pl.ds(src, W), :] + s
        return carry

    lax.fori_loop(0, rows, width_sum, 0, unroll=POOL_UNROLL)

    inv_w = invw_ref[...]

    def rows_pass(r, carry):
        lo = jnp.maximum(r - lo_off, 0)
        hi = jnp.minimum(r + hi_off, rows)
        acc = (ps_ref[pl.ds(pl.multiple_of(hi * W, W), W), :]
               - ps_ref[pl.ds(pl.multiple_of(lo * W, W), W), :])
        inv = inv_w / (hi - lo).astype(f32)
        inv = jnp.concatenate([inv] * (GROUP_C // V7X_LANES), axis=1)
        src = pl.multiple_of(r * W, W)
        ug = u_ref[pl.ds(src, W), :].astype(f32)
        yp_ref[pl.ds(src, W), :] = (acc * inv - ug).astype(bf16)
        return carry

    lax.fori_loop(0, rows, rows_pass, 0, unroll=2)

    tr = 512
    pw = pw_ref[...]
    scale = sc_ref[...]

    def mix(i, carry):
        r0 = pl.multiple_of(i * tr, tr)
        y = jnp.dot(yp_ref[pl.ds(r0, tr), :], pw, preferred_element_type=f32) * scale
        o_ref[pl.ds(r0, tr), :] = (y * _silu(z_ref[pl.ds(r0, tr), :].astype(f32))).astype(bf16)
        return carry

    lax.fori_loop(0, (rows * W) // tr, mix, 0)


def _pool_mix(u, z, pool_w, scale):
    B, T, _ = u.shape
    rows = T // GRID_W
    band, inv_w = _pool_tables()
    n = GROUP_C
    blk = pl.BlockSpec((None, T, n), lambda b, g: (b, 0, g))
    est = 3 * 2 * T * n * 2 + (rows + 1) * GRID_W * n * 4 + T * n * 2 + 2 * n * n * 2 + 8 * 512 * n * 4
    return pl.pallas_call(
        functools.partial(_pool_kernel, rows=rows),
        grid=(B, N_GROUPS_C),
        in_specs=[blk, blk,
                  pl.BlockSpec((None, GRID_W, GRID_W), lambda b, g: (g, 0, 0)),
                  pl.BlockSpec((None, GRID_W, V7X_LANES), lambda b, g: (g, 0, 0)),
                  pl.BlockSpec((None, n, n), lambda b, g: (g, 0, 0)),
                  pl.BlockSpec((None, 1, n), lambda b, g: (g, 0, 0))],
        out_specs=blk,
        out_shape=jax.ShapeDtypeStruct((B, T, D_INNER), bf16),
        scratch_shapes=[pltpu.VMEM(((rows + 1) * GRID_W, n), f32),
                        pltpu.VMEM((T, n), bf16)],
        compiler_params=pltpu.CompilerParams(dimension_semantics=("parallel", "arbitrary"),
                                             vmem_limit_bytes=_vmem_limit(est)),
        name="pool_mix",
    )(u, z, band, inv_w, pool_w, scale.reshape(N_GROUPS_C, 1, n))


def _outproj_odd_kernel(x_ref, y_ref, gate_ref, fg_ref, w_ref, out_ref):
    acc = jnp.dot(y_ref[...], w_ref[...], preferred_element_type=f32)
    x = x_ref[...] + gate_ref[...] * acc
    out_ref[...] = x * lax.rsqrt(jnp.mean(x * x, axis=-1, keepdims=True) + EPS) * fg_ref[...]


def _outproj_odd(x, y, gate, final_g, wout, tm):
    B, T, D = x.shape
    row = lambda n: pl.BlockSpec((None, tm, n), lambda b, i: (b, i, 0))
    est = 2 * wout.size * 2 + 2 * tm * (2 * D * 4 + D_INNER * 2) + 6 * tm * D * 4
    return pl.pallas_call(
        _outproj_odd_kernel,
        grid=(B, T // tm),
        in_specs=[row(D), row(D_INNER),
                  pl.BlockSpec((None, 1, D), lambda b, i: (b, 0, 0)),
                  pl.BlockSpec((1, D), lambda b, i: (0, 0)),
                  pl.BlockSpec(wout.shape, lambda b, i: (0, 0))],
        out_specs=row(D),
        out_shape=jax.ShapeDtypeStruct((B, T, D), f32),
        compiler_params=pltpu.CompilerParams(dimension_semantics=("parallel", "parallel"),
                                             vmem_limit_bytes=_vmem_limit(est)),
        name="outproj_odd",
    )(x, y, gate, final_g.reshape(1, D), wout)


def kernel(x, c, ctx, c_ctx, ada_w, ada_b, norm_g, win_even, gate_b_even, conv_qk_even, head_norm_even,
           fourier_w_even, wout_even, win_odd, pool_w_odd, pool_scale_odd, wout_odd, final_g):
    B, T, D = x.shape
    Tc = ctx.shape[1]
    H = N_HEADS_A
    L = MLSTM_CHUNK

    nrow = -(-(B + 1) // V7X_SUBLANES) * V7X_SUBLANES
    rows_in = jnp.zeros((nrow, D), f32).at[:B].set(c).at[B].set(c_ctx)
    mod = _modulation(rows_in, ada_w, ada_b)

    def mod_parts(l, r0, r1, n):
        m = mod[l, r0:r1]
        parts = [jnp.broadcast_to(m[:, None, i * D:(i + 1) * D], (n, 1, D)) for i in range(3)]
        return parts

    we = win_even[0]
    w_main = we[:, :W_MAIN_EVEN].astype(bf16)
    gcols = we[:, W_MAIN_EVEN:].reshape(D, 4, H)
    gbias = gate_b_even[0].reshape(4, H)
    pad = jnp.zeros((D, V7X_LANES - 2 * H), f32)
    wg = jnp.concatenate([gcols[:, 0], gcols[:, 2], pad, gcols[:, 1], gcols[:, 3], pad], axis=1).astype(bf16)
    bpad = jnp.zeros((V7X_LANES - 2 * H,), f32)
    gb = jnp.concatenate([gbias[0], gbias[2], bpad, gbias[1], gbias[3], bpad]).reshape(1, 2 * V7X_LANES)
    wgt = jnp.concatenate([gcols[:, 0], gcols[:, 2], gcols[:, 1], gcols[:, 3]], axis=1).T.astype(bf16)
    gbt = jnp.concatenate([gbias[0], gbias[2], gbias[1], gbias[3]]).reshape(N_GATES, 1)
    conv_w = conv_qk_even[0]

    shift_x, scale_x, gate_x = mod_parts(0, 0, B, B)
    shift_c, scale_c, gate_c = mod_parts(0, B, B + 1, B)

    qx, kx, vx, ox, zax, ubx, zbx, gcx, gtx = _inproj_even(
        x, shift_x, scale_x, norm_g[0], w_main, wg, wgt, gb, gbt, conv_w, tm=512)
    qc, kc, vc, oc, zac, ubc, zbc, gcc, gtc = _inproj_even(
        ctx, shift_c, scale_c, norm_g[0], w_main, wg, wgt, gb, gbt, conv_w, tm=Tc)

    colsx, rowsx = _gate_prep(gcx, gtx, L)
    colsc, rowsc = _gate_prep(gcc, gtc, L)
    h_x, h_c = _mlstm(qx, kx, vx, colsx, rowsx, qc, kc, vc, colsc, rowsc, L)

    wcs = _fourier_weights(fourier_w_even[0], T)
    yb_x = _fourier_latent(ubx, zbx, wcs)
    wout_e = wout_even[0].astype(bf16)
    x1 = _outproj_even(x, h_x, ox, zax, yb_x, gate_x, head_norm_even[0], wout_e, tm=512)

    wcs_c = _fourier_weights(fourier_w_even[0], Tc)
    yb_c = _fourier_ctx(ubc, zbc, wcs_c)
    ctx1 = _outproj_even(ctx, h_c, oc, zac, yb_c, gate_c, head_norm_even[0], wout_e, tm=Tc)
    del ctx1

    shift_x, scale_x, gate_x = mod_parts(1, 0, B, B)
    u1, z1 = _inproj_odd(x1, shift_x, scale_x, norm_g[1], win_odd[0].astype(bf16), tm=512)
    y1 = _pool_mix(u1, z1, pool_w_odd[0].astype(bf16), pool_scale_odd[0])
    return _outproj_odd(x1, y1, gate_x, final_g, wout_odd[0].astype(bf16), tm=512)
```

```python
import functools

import numpy as np
import jax
import jax.numpy as jnp
from jax import lax
from jax.experimental import pallas as pl
from jax.experimental.pallas import tpu as pltpu

D_MODEL = 1024
DEPTH = 2
CTX_LEN = 256
GRID_W = 64
D_INNER = 2 * D_MODEL
D_A = D_INNER // 2
D_B = D_INNER - D_A
N_HEADS_A = 4
HEAD_DIM_A = D_A // N_HEADS_A
N_GROUPS_B = 4
GROUP_B = D_B // N_GROUPS_B
N_GROUPS_C = 4
GROUP_C = D_INNER // N_GROUPS_C
POOL_WINDOWS = (2, 4, 8, 16)
CONV_W = 3
N_GATES = 4 * N_HEADS_A
W_MAIN_EVEN = 5 * D_A + 2 * D_B
EPS = 1e-6

f32 = jnp.float32
bf16 = jnp.bfloat16

V7X_VMEM_BYTES = 64 * 1024 * 1024
V7X_LANES = 128
V7X_SUBLANES = 8

MLSTM_CHUNK = 256
FFT_N1 = 64
FFT_N2 = 64
FFT_PITCH = 72
FFT_UNROLL = 8
NEG_BIG = -1e30


def _vmem_limit(nbytes):
    return int(min(max(nbytes * 5 // 4 + (4 << 20), 16 << 20), V7X_VMEM_BYTES - (6 << 20)))


def _silu(v):
    return v * jax.nn.sigmoid(v)


def _log_sigmoid(v):
    return jnp.minimum(v, 0.0) - jnp.log1p(jnp.exp(-jnp.abs(v)))


def _mod_kernel(r_ref, w_ref, b_ref, o_ref):
    s = _silu(r_ref[...])
    o_ref[...] = jnp.dot(s, w_ref[...], preferred_element_type=f32,
                         precision=lax.Precision.HIGHEST) + b_ref[...]


def _modulation(rows, ada_w, ada_b):
    nrow = rows.shape[0]
    tn = 1024
    return pl.pallas_call(
        _mod_kernel,
        grid=(DEPTH, 3 * D_MODEL // tn),
        in_specs=[
            pl.BlockSpec((nrow, D_MODEL), lambda l, j: (0, 0)),
            pl.BlockSpec((None, D_MODEL, tn), lambda l, j: (l, 0, j)),
            pl.BlockSpec((None, 1, tn), lambda l, j: (l, 0, j)),
        ],
        out_specs=pl.BlockSpec((None, nrow, tn), lambda l, j: (l, 0, j)),
        out_shape=jax.ShapeDtypeStruct((DEPTH, nrow, 3 * D_MODEL), f32),
        compiler_params=pltpu.CompilerParams(dimension_semantics=("arbitrary", "arbitrary")),
        name="modulation",
    )(rows, ada_w, ada_b.reshape(DEPTH, 1, 3 * D_MODEL))


def _normed(x, amp, shift):
    ms = jnp.mean(x * x, axis=-1, keepdims=True)
    return (x * lax.rsqrt(ms + EPS)) * amp + shift


def _inproj_even_kernel(x_ref, xp_ref, xn_ref, sh_ref, sc_ref, g_ref, w_ref, wgt_ref, gbt_ref,
                        cw_ref, q_ref, k_ref, v_ref, o_ref, za_ref, ub_ref, zb_ref, gt_ref, *, tm, nt):
    i = pl.program_id(1)
    amp = g_ref[...] * (1.0 + sc_ref[...])
    shift = sh_ref[...]
    hx = _normed(x_ref[...], amp, shift).astype(bf16)
    halo = jnp.concatenate([xp_ref[...], xn_ref[...]], axis=0)
    hh = _normed(halo, amp, shift).astype(bf16)
    has_prev = (i > 0).astype(f32)
    has_next = (i < nt - 1).astype(f32)
    row = lax.broadcasted_iota(jnp.int32, (tm, 1), 0)
    cn = 512
    for j in range(2 * D_A // cn):
        w = w_ref[:, j * cn:(j + 1) * cn]
        p = jnp.dot(hx, w, preferred_element_type=f32)
        ph = jnp.dot(hh, w, preferred_element_type=f32)
        prev = ph[V7X_SUBLANES - 1:V7X_SUBLANES, :] * has_prev
        nxt = ph[V7X_SUBLANES:V7X_SUBLANES + 1, :] * has_next
        up = jnp.where(row == 0, prev, pltpu.roll(p, 1, 0))
        dn = jnp.where(row == tm - 1, nxt, pltpu.roll(p, tm - 1, 0))
        cw = cw_ref[:, j * cn:(j + 1) * cn]
        y = _silu(cw[0:1, :] * up + cw[1:2, :] * p + cw[2:3, :] * dn)
        if j < D_A // cn:
            q_ref[:, j * cn:(j + 1) * cn] = (y * (HEAD_DIM_A ** -0.5)).astype(bf16)
        else:
            jj = j - D_A // cn
            k_ref[:, jj * cn:(jj + 1) * cn] = y.astype(bf16)
    for idx, ref in enumerate((v_ref, o_ref, za_ref, ub_ref, zb_ref)):
        for jj in range(D_A // cn):
            c0 = 2 * D_A + idx * D_A + jj * cn
            ref[:, jj * cn:(jj + 1) * cn] = jnp.dot(hx, w_ref[:, c0:c0 + cn],
                                                    preferred_element_type=f32).astype(bf16)
    gt_ref[...] = lax.dot_general(wgt_ref[...], hx, (((1,), (1,)), ((), ())),
                                  preferred_element_type=f32) + gbt_ref[...]


def _inproj_even(x, shift, scale, norm_g, w_main, wgt, gbt, conv_w, tm):
    B, T, D = x.shape
    nt = T // tm
    hb = tm // V7X_SUBLANES
    nhb = T // V7X_SUBLANES
    row_spec = pl.BlockSpec((None, tm, D_A), lambda b, i: (b, i, 0))
    vec = lambda n: pl.BlockSpec((1, n), lambda b, i: (0, 0))
    est = (2 * w_main.size * 2 + 2 * tm * D * 4 + 7 * 2 * tm * D_A * 2 + 6 * tm * 512 * 4)
    outs = pl.pallas_call(
        functools.partial(_inproj_even_kernel, tm=tm, nt=nt),
        grid=(B, nt),
        in_specs=[
            pl.BlockSpec((None, tm, D), lambda b, i: (b, i, 0)),
            pl.BlockSpec((None, V7X_SUBLANES, D), lambda b, i: (b, jnp.maximum(i * hb - 1, 0), 0)),
            pl.BlockSpec((None, V7X_SUBLANES, D), lambda b, i: (b, jnp.minimum((i + 1) * hb, nhb - 1), 0)),
            pl.BlockSpec((None, 1, D), lambda b, i: (b, 0, 0)),
            pl.BlockSpec((None, 1, D), lambda b, i: (b, 0, 0)),
            vec(D),
            pl.BlockSpec(w_main.shape, lambda b, i: (0, 0)),
            pl.BlockSpec(wgt.shape, lambda b, i: (0, 0)),
            pl.BlockSpec((N_GATES, 1), lambda b, i: (0, 0)),
            pl.BlockSpec(conv_w.shape, lambda b, i: (0, 0)),
        ],
        out_specs=[row_spec] * 7 + [pl.BlockSpec((None, N_GATES, tm), lambda b, i: (b, 0, i))],
        out_shape=[jax.ShapeDtypeStruct((B, T, D_A), bf16)] * 7 + [jax.ShapeDtypeStruct((B, N_GATES, T), f32)],
        compiler_params=pltpu.CompilerParams(dimension_semantics=("parallel", "arbitrary"),
                                             vmem_limit_bytes=_vmem_limit(est)),
        name="inproj_even",
    )(x, x, x, shift, scale, norm_g.reshape(1, D), w_main, wgt, gbt, conv_w)
    return outs


GATE_SLOTS = 2 * N_HEADS_A
GATE_PIECES = 3
GATE_QUANTS = 3
assert GATE_SLOTS & (GATE_SLOTS - 1) == 0 and GATE_QUANTS * GATE_PIECES * GATE_SLOTS <= V7X_LANES


def _scan_max_lanes(x, reverse):
    n = x.shape[1]
    lane = lax.broadcasted_iota(jnp.int32, x.shape, 1)
    s = 1
    while s < n:
        if reverse:
            x = jnp.where(lane < n - s, jnp.maximum(x, pltpu.roll(x, n - s, 1)), x)
        else:
            x = jnp.where(lane >= s, jnp.maximum(x, pltpu.roll(x, s, 1)), x)
        s *= 2
    return x


def _split3(v):
    hi = v.astype(bf16)
    r1 = v - hi.astype(f32)
    mid = r1.astype(bf16)
    lo = (r1 - mid.astype(f32)).astype(bf16)
    return hi, mid, lo


def _gate_prep_kernel(gt_ref, cols_ref, arow_ref, *, L, nchunk):
    r = lax.broadcasted_iota(jnp.int32, (L, L), 0)
    c = lax.broadcasted_iota(jnp.int32, (L, L), 1)
    tri_l = (c <= r).astype(bf16)
    tri_u = (c >= r).astype(bf16)
    S = V7X_SUBLANES
    fwd = (lax.broadcasted_iota(jnp.int32, (S, L), 0) & 1) == 0
    fill = jnp.zeros((V7X_LANES - GATE_QUANTS * GATE_PIECES * S, L), f32)
    for ci in range(nchunk):
        sl = slice(ci * L, (ci + 1) * L)
        ig = gt_ref[:S, sl]
        lf = _log_sigmoid(gt_ref[S:, sl])
        pieces = jnp.concatenate([p.astype(f32) for p in _split3(lf)] + [jnp.zeros((S, L), f32)],
                                 axis=0).astype(bf16)
        pre = jnp.dot(pieces, tri_u, preferred_element_type=f32)
        suf = jnp.dot(pieces, tri_l, preferred_element_type=f32)
        pre = pre[:S] + pre[S:2 * S] + pre[2 * S:3 * S]
        suf = suf[:S] + suf[S:2 * S] + suf[2 * S:3 * S]
        b_row = jnp.where(fwd, pre, suf)
        a_row = ig - b_row
        cmax = jnp.where(fwd, _scan_max_lanes(a_row, False), _scan_max_lanes(a_row, True))
        arow_ref[:, sl] = a_row
        parts = [p.astype(f32) for quant in (b_row, a_row, cmax) for p in _split3(quant)]
        packed = jnp.concatenate(parts + [fill], axis=0)
        cols_ref[sl, :] = packed.T.astype(bf16)


def _gate_prep(gt, L):
    B, _, T = gt.shape
    nchunk = min(4, T // L)
    tb = nchunk * L
    cols, arow = pl.pallas_call(
        functools.partial(_gate_prep_kernel, L=L, nchunk=nchunk),
        grid=(B, T // tb),
        in_specs=[pl.BlockSpec((None, N_GATES, tb), lambda b, i: (b, 0, i))],
        out_specs=[
            pl.BlockSpec((None, tb, V7X_LANES), lambda b, i: (b, i, 0)),
            pl.BlockSpec((None, V7X_SUBLANES, tb), lambda b, i: (b, 0, i)),
        ],
        out_shape=[
            jax.ShapeDtypeStruct((B, T, V7X_LANES), bf16),
            jax.ShapeDtypeStruct((B, V7X_SUBLANES, T), f32),
        ],
        compiler_params=pltpu.CompilerParams(dimension_semantics=("parallel", "parallel")),
        name="gate_prep",
    )(gt)
    return cols, arow.reshape(B, N_HEADS_A, 2, T)


def _mlstm_kernel(qx_ref, kx_ref, vx_ref, cx_ref, rx_ref, qc_ref, kc_ref, vc_ref, cc_ref, rc_ref,
                  hx_ref, hc_ref, cnf_ref, cnb_ref, *, L, nx, nc):
    cn_refs = (cnf_ref, cnb_ref)
    hd = pl.program_id(1)
    Dh, LN = HEAD_DIM_A, V7X_LANES
    r_i = lax.broadcasted_iota(jnp.int32, (L, L), 0)
    c_i = lax.broadcasted_iota(jnp.int32, (L, L), 1)
    masks = (c_i <= r_i, c_i >= r_i)

    sr = lax.broadcasted_iota(jnp.int32, (LN, GATE_QUANTS * LN), 0)
    sc = lax.broadcasted_iota(jnp.int32, (LN, GATE_QUANTS * LN), 1)
    span = GATE_PIECES * GATE_SLOTS
    in_block = None
    for qi in range(GATE_QUANTS):
        blk = (sr >= qi * span) & (sr < (qi + 1) * span) & (sc >= qi * LN) & (sc < (qi + 1) * LN)
        in_block = blk if in_block is None else in_block | blk
    sels = [(in_block & ((sr & (GATE_SLOTS - 1)) == 2 * hd + d)).astype(bf16) for d in range(2)]
    ones = jnp.ones((L, LN), bf16)

    def tile(v, width):
        return jnp.concatenate([v] * (width // LN), axis=1)

    def chunk(q_ref, k_ref, v_ref, col_ref, row_ref, r0, d, m_prev):
        cn_ref = cn_refs[d]
        rows = pl.ds(r0, L)
        q = q_ref[rows, :]
        k = k_ref[rows, :]
        vo = jnp.concatenate([v_ref[rows, :], ones], axis=1)
        rep = jnp.dot(col_ref[rows, :], sels[d], preferred_element_type=f32)
        b_rep = rep[:, :LN]
        a_rep = rep[:, LN:2 * LN]
        g_rep = jnp.maximum(rep[:, 2 * LN:], m_prev)
        a_row = row_ref[d:d + 1, rows]
        p = jnp.exp(jnp.where(masks[d], a_row - tile(g_rep, L), NEG_BIG))
        s = lax.dot_general(q, k, (((1,), (1,)), ((), ())), preferred_element_type=f32) * p
        sv = jnp.dot(s.astype(bf16), vo, preferred_element_type=f32)
        qcn = jnp.dot(q, cn_ref[...].astype(bf16), preferred_element_type=f32)
        inter = jnp.exp(m_prev - g_rep)
        den = sv[:, Dh:] + inter * qcn[:, Dh:]
        rcp = 1.0 / jnp.maximum(jnp.abs(den), jnp.exp(-(b_rep + g_rep)))
        h = (sv[:, :Dh] + tile(inter, Dh) * qcn[:, :Dh]) * tile(rcp, Dh)

        b_end = b_rep[L - 1:L, :] if d == 0 else b_rep[0:1, :]
        w = b_end + a_rep
        m_new = jnp.maximum(b_end + m_prev, jnp.max(w, axis=0, keepdims=True))
        decay = jnp.exp(b_end + m_prev - m_new)
        kw = (k.astype(f32) * tile(jnp.exp(w - m_new), Dh)).astype(bf16)
        upd = lax.dot_general(kw, vo, (((0,), (0,)), ((), ())), preferred_element_type=f32)
        cn_ref[...] = tile(decay, Dh + LN) * cn_ref[...] + upd
        return h, m_new

    ctx = (qc_ref, kc_ref, vc_ref, cc_ref, rc_ref)
    lat = (qx_ref, kx_ref, vx_ref, cx_ref, rx_ref)

    for cn_ref in cn_refs:
        cn_ref[...] = jnp.zeros_like(cn_ref)
    m_f = m_b = jnp.zeros((1, LN), f32)
    written = set()
    for ci in range(nc):
        for d, cj in ((0, ci), (1, nc - 1 - ci)):
            h, m_d = chunk(*ctx, cj * L, d, m_f if d == 0 else m_b)
            if d == 0:
                m_f = m_d
            else:
                m_b = m_d
            if cj in written:
                hc_ref[cj * L:(cj + 1) * L, :] += h
            else:
                hc_ref[cj * L:(cj + 1) * L, :] = h
                written.add(cj)

    def make_body(accumulate):
        def body(i, carry):
            m_f, m_b = carry
            rf = pl.multiple_of(i * L, L)
            rb = pl.multiple_of((nx - 1 - i) * L, L)
            h_f, m_f = chunk(*lat, rf, 0, m_f)
            h_b, m_b = chunk(*lat, rb, 1, m_b)
            if accumulate:
                hx_ref[pl.ds(rf, L), :] += h_f
                hx_ref[pl.ds(rb, L), :] += h_b
            else:
                hx_ref[pl.ds(rf, L), :] = h_f
                hx_ref[pl.ds(rb, L), :] = h_b
            return m_f, m_b
        return body

    carry = lax.fori_loop(0, nx // 2, make_body(False), (m_f, m_b))
    lax.fori_loop(nx // 2, nx, make_body(True), carry)


def _mlstm(qx, kx, vx, colsx, rowsx, qc, kc, vc, colsc, rowsc, L):
    B, T, _ = qx.shape
    Tc = qc.shape[1]
    H, Dh = N_HEADS_A, HEAD_DIM_A
    assert T % (2 * L) == 0 and Tc % L == 0

    def seq_spec(t):
        return pl.BlockSpec((None, t, Dh), lambda b, h: (b, 0, h))

    def col_spec(t):
        return pl.BlockSpec((None, t, V7X_LANES), lambda b, h: (b, 0, 0))

    def row_spec(t):
        return pl.BlockSpec((None, None, 2, t), lambda b, h: (b, h, 0, 0))

    est = (2 * (3 * (T + Tc) * Dh * 2 + (T + Tc) * V7X_LANES * 2 + 8 * (T + Tc) * 4 + (T + Tc) * Dh * 4)
           + Dh * (Dh + V7X_LANES) * 4 + 16 * L * L * 4)
    return pl.pallas_call(
        functools.partial(_mlstm_kernel, L=L, nx=T // L, nc=Tc // L),
        grid=(B, H),
        in_specs=[seq_spec(T), seq_spec(T), seq_spec(T), col_spec(T), row_spec(T),
                  seq_spec(Tc), seq_spec(Tc), seq_spec(Tc), col_spec(Tc), row_spec(Tc)],
        out_specs=[seq_spec(T), seq_spec(Tc)],
        out_shape=[jax.ShapeDtypeStruct((B, T, D_A), f32), jax.ShapeDtypeStruct((B, Tc, D_A), f32)],
        scratch_shapes=[pltpu.VMEM((Dh, Dh + V7X_LANES), f32)] * 2,
        compiler_params=pltpu.CompilerParams(dimension_semantics=("parallel", "parallel"),
                                             vmem_limit_bytes=_vmem_limit(est)),
        name="mlstm",
    )(qx, kx, vx, colsx, rowsx, qc, kc, vc, colsc, rowsc)


def _fourier_w_kernel(cs_ref, fw_ref, o_ref, *, scale):
    o_ref[...] = (jnp.dot(cs_ref[...], fw_ref[...], preferred_element_type=f32,
                          precision=lax.Precision.HIGHEST) * scale).astype(bf16)


def _fourier_weights(fw, T):
    n = GROUP_B
    kk = np.outer(np.arange(n), np.arange(n)) % n
    ang = 2.0 * np.pi * kk / n
    cs = jnp.asarray(np.concatenate([np.cos(ang), np.sin(ang)], axis=0), f32)
    return pl.pallas_call(
        functools.partial(_fourier_w_kernel, scale=float(1.0 / np.sqrt(T * n))),
        grid=(N_GROUPS_B,),
        in_specs=[pl.BlockSpec((2 * n, n), lambda g: (0, 0)),
                  pl.BlockSpec((None, n, n), lambda g: (g, 0, 0))],
        out_specs=pl.BlockSpec((None, 2 * n, n), lambda g: (g, 0, 0)),
        out_shape=jax.ShapeDtypeStruct((N_GROUPS_B, 2 * n, n), bf16),
        compiler_params=pltpu.CompilerParams(dimension_semantics=("arbitrary",)),
        name="fourier_weights",
    )(cs, fw)


def _fft_tables():
    n1, n2 = FFT_N1, FFT_N2
    n = n1 * n2
    t1 = np.arange(n1)
    k1 = np.arange(n1)
    t2 = np.arange(n2)
    idx = (k1[None, :, None] * (n2 * t1[None, None, :] + t2[:, None, None])) % n
    ang = 2.0 * np.pi * idx / n
    m1 = np.concatenate([np.cos(ang), -np.sin(ang)], axis=1)
    k2 = np.arange(n2)
    ph = 2.0 * np.pi * (np.outer(k2, t2) % n2) / n2
    c, s = np.cos(ph), np.sin(ph)
    m2 = np.block([[c, s], [-s, c]])
    return jnp.asarray(m1, f32).astype(bf16), jnp.asarray(m2, f32).astype(bf16)


def _fourier_kernel(u_ref, z_ref, m1_ref, m2_ref, w_ref, o_ref, up_ref, yp_ref, xp_ref, x2_ref):
    n1, n2, P = FFT_N1, FFT_N2, FFT_PITCH
    nl = GROUP_B // V7X_LANES

    def fill(t1, carry):
        src = pl.multiple_of(t1 * n2, n2)
        dst = pl.multiple_of(t1 * P, V7X_SUBLANES)
        blk = u_ref[pl.ds(src, n2), :].astype(f32)
        for s in range(nl):
            up_ref[s, pl.ds(dst, n2), :] = blk[:, s * V7X_LANES:(s + 1) * V7X_LANES]
        return carry

    lax.fori_loop(0, n1, fill, 0, unroll=4)

    def stage1(t2, carry):
        rhs = jnp.concatenate([up_ref[s, pl.ds(t2, n1, stride=P), :] for s in range(nl)], axis=1)
        y = jnp.dot(m1_ref[t2], rhs.astype(bf16), preferred_element_type=f32)
        dst = pl.multiple_of(t2 * P, V7X_SUBLANES)
        for ri in range(2):
            for s in range(nl):
                yp_ref[ri * nl + s, pl.ds(dst, n1), :] = y[ri * n1:(ri + 1) * n1,
                                                           s * V7X_LANES:(s + 1) * V7X_LANES]
        return carry

    lax.fori_loop(0, n2, stage1, 0, unroll=FFT_UNROLL)

    m2 = m2_ref[...]

    def stage2(k1, carry):
        parts = []
        for ri in range(2):
            parts.append(jnp.concatenate(
                [yp_ref[ri * nl + s, pl.ds(k1, n2, stride=P), :] for s in range(nl)], axis=1))
        rhs = jnp.concatenate(parts, axis=0).astype(bf16)
        x = jnp.dot(m2, rhs, preferred_element_type=f32)
        for ri in range(2):
            for s in range(nl):
                xp_ref[ri * nl + s, pl.ds(k1, n2, stride=P), :] = x[ri * n2:(ri + 1) * n2,
                                                                    s * V7X_LANES:(s + 1) * V7X_LANES]
        return carry

    lax.fori_loop(0, n1, stage2, 0, unroll=FFT_UNROLL)

    def unpitch(k2, carry):
        src = pl.multiple_of(k2 * P, V7X_SUBLANES)
        dst = pl.multiple_of(k2 * n1, n1)
        for ri in range(2):
            for s in range(nl):
                c0 = (ri * nl + s) * V7X_LANES
                x2_ref[pl.ds(dst, n1), c0:c0 + V7X_LANES] = xp_ref[ri * nl + s, pl.ds(src, n1), :].astype(bf16)
        return carry

    lax.fori_loop(0, n2, unpitch, 0, unroll=4)

    tr = 512
    w = w_ref[...]

    def mix(i, carry):
        r0 = pl.multiple_of(i * tr, tr)
        y = jnp.dot(x2_ref[pl.ds(r0, tr), :], w, preferred_element_type=f32)
        o_ref[pl.ds(r0, tr), :] = (y * _silu(z_ref[pl.ds(r0, tr), :].astype(f32))).astype(bf16)
        return carry

    lax.fori_loop(0, (n1 * n2) // tr, mix, 0)


def _fourier_latent(ub, zb, wcs):
    B, T, _ = ub.shape
    assert T == FFT_N1 * FFT_N2
    m1, m2 = _fft_tables()
    G, n = N_GROUPS_B, GROUP_B
    nl = n // V7X_LANES
    rows_p = FFT_N1 * FFT_PITCH
    blk = pl.BlockSpec((None, T, n), lambda b, g: (b, 0, g))
    est = (3 * 2 * T * n * 2 + 5 * nl * rows_p * V7X_LANES * 4 + T * 2 * n * 2 + 2 * m1.size * 2 + 8 * 512 * n * 4)
    return pl.pallas_call(
        _fourier_kernel,
        grid=(B, G),
        in_specs=[blk, blk,
                  pl.BlockSpec(m1.shape, lambda b, g: (0, 0, 0)),
                  pl.BlockSpec(m2.shape, lambda b, g: (0, 0)),
                  pl.BlockSpec((None, 2 * n, n), lambda b, g: (g, 0, 0))],
        out_specs=blk,
        out_shape=jax.ShapeDtypeStruct((B, T, D_B), bf16),
        scratch_shapes=[pltpu.VMEM((nl, rows_p, V7X_LANES), f32),
                        pltpu.VMEM((2 * nl, rows_p, V7X_LANES), f32),
                        pltpu.VMEM((2 * nl, rows_p, V7X_LANES), f32),
                        pltpu.VMEM((T, 2 * n), bf16)],
        compiler_params=pltpu.CompilerParams(dimension_semantics=("parallel", "parallel"),
                                             vmem_limit_bytes=_vmem_limit(est)),
        name="fourier_latent",
    )(ub, zb, m1, m2, wcs)


def _fourier_ctx_kernel(u_ref, z_ref, cs_ref, w_ref, o_ref):
    u = u_ref[...]
    x = jnp.dot(cs_ref[...], u, preferred_element_type=f32).astype(bf16)
    t = u.shape[0]
    w = w_ref[...]
    n = GROUP_B
    y = (jnp.dot(x[:t], w[:n], preferred_element_type=f32) + jnp.dot(x[t:], w[n:], preferred_element_type=f32))
    o_ref[...] = (y * _silu(z_ref[...].astype(f32))).astype(bf16)


def _fourier_ctx(ub, zb, wcs):
    B, T, _ = ub.shape
    ang = 2.0 * np.pi * (np.outer(np.arange(T), np.arange(T)) % T) / T
    cs = jnp.asarray(np.concatenate([np.cos(ang), -np.sin(ang)], axis=0), f32).astype(bf16)
    n = GROUP_B
    blk = pl.BlockSpec((None, T, n), lambda b, g: (b, 0, g))
    return pl.pallas_call(
        _fourier_ctx_kernel,
        grid=(B, N_GROUPS_B),
        in_specs=[blk, blk, pl.BlockSpec(cs.shape, lambda b, g: (0, 0)),
                  pl.BlockSpec((None, 2 * n, n), lambda b, g: (g, 0, 0))],
        out_specs=blk,
        out_shape=jax.ShapeDtypeStruct((B, T, D_B), bf16),
        compiler_params=pltpu.CompilerParams(dimension_semantics=("parallel", "parallel")),
        name="fourier_ctx",
    )(ub, zb, cs, wcs)


def _outproj_even_kernel(x_ref, h_ref, o_ref, za_ref, yb_ref, gate_ref, hg_ref, w_ref, out_ref):
    parts = []
    for hd in range(N_HEADS_A):
        sl = slice(hd * HEAD_DIM_A, (hd + 1) * HEAD_DIM_A)
        hh = jax.nn.sigmoid(o_ref[:, sl].astype(f32)) * h_ref[:, sl]
        hh = hh * lax.rsqrt(jnp.mean(hh * hh, axis=-1, keepdims=True) + EPS)
        parts.append((hh * hg_ref[:, sl] * _silu(za_ref[:, sl].astype(f32))).astype(bf16))
    ya = jnp.concatenate(parts, axis=1)
    acc = jnp.dot(ya, w_ref[:D_A, :], preferred_element_type=f32)
    acc = acc + jnp.dot(yb_ref[...], w_ref[D_A:, :], preferred_element_type=f32)
    out_ref[...] = x_ref[...] + gate_ref[...] * acc


def _outproj_even(x, h, o, za, yb, gate, head_g, wout, tm):
    B, T, D = x.shape
    row = lambda n: pl.BlockSpec((None, tm, n), lambda b, i: (b, i, 0))
    est = 2 * wout.size * 2 + 2 * tm * (2 * D * 4 + D_A * 4 + 3 * D_A * 2) + 8 * tm * D * 4
    return pl.pallas_call(
        _outproj_even_kernel,
        grid=(B, T // tm),
        in_specs=[row(D), row(D_A), row(D_A), row(D_A), row(D_B),
                  pl.BlockSpec((None, 1, D), lambda b, i: (b, 0, 0)),
                  pl.BlockSpec((1, D_A), lambda b, i: (0, 0)),
                  pl.BlockSpec(wout.shape, lambda b, i: (0, 0))],
        out_specs=row(D),
        out_shape=jax.ShapeDtypeStruct((B, T, D), f32),
        compiler_params=pltpu.CompilerParams(dimension_semantics=("parallel", "parallel"),
                                             vmem_limit_bytes=_vmem_limit(est)),
        name="outproj_even",
    )(x, h, o, za, yb, gate, head_g.reshape(1, D_A), wout)


def _inproj_odd_kernel(x_ref, sh_ref, sc_ref, g_ref, w_ref, u_ref, z_ref):
    amp = g_ref[...] * (1.0 + sc_ref[...])
    hx = _normed(x_ref[...], amp, sh_ref[...]).astype(bf16)
    cn = 512
    for j in range(D_INNER // cn):
        u_ref[:, j * cn:(j + 1) * cn] = jnp.dot(hx, w_ref[:, j * cn:(j + 1) * cn],
                                                preferred_element_type=f32).astype(bf16)
        z_ref[:, j * cn:(j + 1) * cn] = jnp.dot(hx, w_ref[:, D_INNER + j * cn:D_INNER + (j + 1) * cn],
                                                preferred_element_type=f32).astype(bf16)


def _inproj_odd(x, shift, scale, norm_g, w, tm):
    B, T, D = x.shape
    row = lambda n: pl.BlockSpec((None, tm, n), lambda b, i: (b, i, 0))
    est = 2 * w.size * 2 + 2 * tm * D * 4 + 4 * tm * D_INNER * 2 + 6 * tm * 512 * 4
    return pl.pallas_call(
        _inproj_odd_kernel,
        grid=(B, T // tm),
        in_specs=[row(D),
                  pl.BlockSpec((None, 1, D), lambda b, i: (b, 0, 0)),
                  pl.BlockSpec((None, 1, D), lambda b, i: (b, 0, 0)),
                  pl.BlockSpec((1, D), lambda b, i: (0, 0)),
                  pl.BlockSpec(w.shape, lambda b, i: (0, 0))],
        out_specs=[row(D_INNER), row(D_INNER)],
        out_shape=[jax.ShapeDtypeStruct((B, T, D_INNER), bf16)] * 2,
        compiler_params=pltpu.CompilerParams(dimension_semantics=("parallel", "parallel"),
                                             vmem_limit_bytes=_vmem_limit(est)),
        name="inproj_odd",
    )(x, shift, scale, norm_g.reshape(1, D), w)


POOL_UNROLL = 8


def _pool_tables():
    w_idx = np.arange(GRID_W)
    band = np.zeros((N_GROUPS_C, GRID_W, GRID_W), np.float32)
    inv_w = np.zeros((N_GROUPS_C, GRID_W, V7X_LANES), np.float32)
    for g, win in enumerate(POOL_WINDOWS):
        lo = np.clip(w_idx - win // 2, 0, GRID_W)
        hi = np.clip(w_idx + win - win // 2, 0, GRID_W)
        band[g] = (w_idx[None, :] >= lo[:, None]) & (w_idx[None, :] < hi[:, None])
        inv_w[g] = (1.0 / (hi - lo))[:, None]
    return jnp.asarray(band, bf16), jnp.asarray(inv_w, f32)


def _pool_kernel(u_ref, z_ref, band_ref, invw_ref, pw_ref, sc_ref, o_ref, ps_ref, yp_ref, *, rows):
    g = pl.program_id(1)
    W = GRID_W
    band = band_ref[...]
    lo_off = hi_off = 0
    for gi, win in enumerate(POOL_WINDOWS):
        lo_off = jnp.where(g == gi, win // 2, lo_off)
        hi_off = jnp.where(g == gi, win - win // 2, hi_off)

    ps_ref[0:W, :] = jnp.zeros((W, GROUP_C), f32)

    def width_sum(r, carry):
        src = pl.multiple_of(r * W, W)
        dst = pl.multiple_of((r + 1) * W, W)
        s = jnp.dot(band, u_ref[pl.ds(src, W), :], preferred_element_type=f32)
        ps_ref[pl.ds(dst, W), :] = ps_ref[pl.ds(src, W), :] + s
        return carry

    lax.fori_loop(0, rows, width_sum, 0, unroll=POOL_UNROLL)

    inv_w = invw_ref[...]

    def rows_pass(r, carry):
        lo = jnp.maximum(r - lo_off, 0)
        hi = jnp.minimum(r + hi_off, rows)
        acc = (ps_ref[pl.ds(pl.multiple_of(hi * W, W), W), :]
               - ps_ref[pl.ds(pl.multiple_of(lo * W, W), W), :])
        inv = inv_w / (hi - lo).astype(f32)
        inv = jnp.concatenate([inv] * (GROUP_C // V7X_LANES), axis=1)
        src = pl.multiple_of(r * W, W)
        ug = u_ref[pl.ds(src, W), :].astype(f32)
        yp_ref[pl.ds(src, W), :] = (acc * inv - ug).astype(bf16)
        return carry

    lax.fori_loop(0, rows, rows_pass, 0, unroll=2)

    tr = 512
    pw = pw_ref[...]
    scale = sc_ref[...]

    def mix(i, carry):
        r0 = pl.multiple_of(i * tr, tr)
        y = jnp.dot(yp_ref[pl.ds(r0, tr), :], pw, preferred_element_type=f32) * scale
        o_ref[pl.ds(r0, tr), :] = (y * _silu(z_ref[pl.ds(r0, tr), :].astype(f32))).astype(bf16)
        return carry

    lax.fori_loop(0, (rows * W) // tr, mix, 0)


def _pool_mix(u, z, pool_w, scale):
    B, T, _ = u.shape
    rows = T // GRID_W
    band, inv_w = _pool_tables()
    n = GROUP_C
    blk = pl.BlockSpec((None, T, n), lambda b, g: (b, 0, g))
    est = 3 * 2 * T * n * 2 + (rows + 1) * GRID_W * n * 4 + T * n * 2 + 2 * n * n * 2 + 8 * 512 * n * 4
    return pl.pallas_call(
        functools.partial(_pool_kernel, rows=rows),
        grid=(B, N_GROUPS_C),
        in_specs=[blk, blk,
                  pl.BlockSpec((None, GRID_W, GRID_W), lambda b, g: (g, 0, 0)),
                  pl.BlockSpec((None, GRID_W, V7X_LANES), lambda b, g: (g, 0, 0)),
                  pl.BlockSpec((None, n, n), lambda b, g: (g, 0, 0)),
                  pl.BlockSpec((None, 1, n), lambda b, g: (g, 0, 0))],
        out_specs=blk,
        out_shape=jax.ShapeDtypeStruct((B, T, D_INNER), bf16),
        scratch_shapes=[pltpu.VMEM(((rows + 1) * GRID_W, n), f32),
                        pltpu.VMEM((T, n), bf16)],
        compiler_params=pltpu.CompilerParams(dimension_semantics=("parallel", "arbitrary"),
                                             vmem_limit_bytes=_vmem_limit(est)),
        name="pool_mix",
    )(u, z, band, inv_w, pool_w, scale.reshape(N_GROUPS_C, 1, n))


def _outproj_odd_kernel(x_ref, y_ref, gate_ref, fg_ref, w_ref, out_ref):
    acc = jnp.dot(y_ref[...], w_ref[...], preferred_element_type=f32)
    x = x_ref[...] + gate_ref[...] * acc
    out_ref[...] = x * lax.rsqrt(jnp.mean(x * x, axis=-1, keepdims=True) + EPS) * fg_ref[...]


def _outproj_odd(x, y, gate, final_g, wout, tm):
    B, T, D = x.shape
    row = lambda n: pl.BlockSpec((None, tm, n), lambda b, i: (b, i, 0))
    est = 2 * wout.size * 2 + 2 * tm * (2 * D * 4 + D_INNER * 2) + 6 * tm * D * 4
    return pl.pallas_call(
        _outproj_odd_kernel,
        grid=(B, T // tm),
        in_specs=[row(D), row(D_INNER),
                  pl.BlockSpec((None, 1, D), lambda b, i: (b, 0, 0)),
                  pl.BlockSpec((1, D), lambda b, i: (0, 0)),
                  pl.BlockSpec(wout.shape, lambda b, i: (0, 0))],
        out_specs=row(D),
        out_shape=jax.ShapeDtypeStruct((B, T, D), f32),
        compiler_params=pltpu.CompilerParams(dimension_semantics=("parallel", "parallel"),
                                             vmem_limit_bytes=_vmem_limit(est)),
        name="outproj_odd",
    )(x, y, gate, final_g.reshape(1, D), wout)


def kernel(x, c, ctx, c_ctx, ada_w, ada_b, norm_g, win_even, gate_b_even, conv_qk_even, head_norm_even,
           fourier_w_even, wout_even, win_odd, pool_w_odd, pool_scale_odd, wout_odd, final_g):
    B, T, D = x.shape
    Tc = ctx.shape[1]
    H = N_HEADS_A
    L = MLSTM_CHUNK

    nrow = -(-(B + 1) // V7X_SUBLANES) * V7X_SUBLANES
    rows_in = jnp.zeros((nrow, D), f32).at[:B].set(c).at[B].set(c_ctx)
    mod = _modulation(rows_in, ada_w, ada_b)

    def mod_parts(l, r0, r1, n):
        m = mod[l, r0:r1]
        parts = [jnp.broadcast_to(m[:, None, i * D:(i + 1) * D], (n, 1, D)) for i in range(3)]
        return parts

    we = win_even[0]
    w_main = we[:, :W_MAIN_EVEN].astype(bf16)
    gcols = we[:, W_MAIN_EVEN:].reshape(D, 4, H)
    gbias = gate_b_even[0].reshape(4, H)
    ig_w = gcols[:, 0::2, :].transpose(0, 2, 1).reshape(D, GATE_SLOTS)
    fg_w = gcols[:, 1::2, :].transpose(0, 2, 1).reshape(D, GATE_SLOTS)
    ig_b = gbias[0::2, :].T.reshape(GATE_SLOTS)
    fg_b = gbias[1::2, :].T.reshape(GATE_SLOTS)
    wgt =jnp.concatenate([ig_w, fg_w], axis=1).T.astype(bf16)
    gbt = jnp.concatenate([ig_b, fg_b]).reshape(N_GATES, 1)
    conv_w = conv_qk_even[0]

    shift_x, scale_x, gate_x = mod_parts(0, 0, B, B)
    shift_c, scale_c, gate_c = mod_parts(0, B, B + 1, B)

    qx, kx, vx, ox, zax, ubx, zbx, gtx = _inproj_even(
        x, shift_x, scale_x, norm_g[0], w_main, wgt, gbt, conv_w, tm=512)
    qc, kc, vc, oc, zac, ubc, zbc, gtc = _inproj_even(
        ctx, shift_c, scale_c, norm_g[0], w_main, wgt, gbt, conv_w, tm=Tc)

    colsx, rowsx = _gate_prep(gtx, L)
    colsc, rowsc = _gate_prep(gtc, L)
    h_x, h_c = _mlstm(qx, kx, vx, colsx, rowsx, qc, kc, vc, colsc, rowsc, L)

    wcs = _fourier_weights(fourier_w_even[0], T)
    yb_x = _fourier_latent(ubx, zbx, wcs)
    wout_e = wout_even[0].astype(bf16)
    x1 = _outproj_even(x, h_x, ox, zax, yb_x, gate_x, head_norm_even[0], wout_e, tm=512)

    wcs_c = _fourier_weights(fourier_w_even[0], Tc)
    yb_c = _fourier_ctx(ubc, zbc, wcs_c)
    ctx1 = _outproj_even(ctx, h_c, oc, zac, yb_c, gate_c, head_norm_even[0], wout_e, tm=Tc)
    del ctx1

    shift_x, scale_x, gate_x = mod_parts(1, 0, B, B)
    u1, z1 = _inproj_odd(x1, shift_x, scale_x, norm_g[1], win_odd[0].astype(bf16), tm=512)
    y1 = _pool_mix(u1, z1, pool_w_odd[0].astype(bf16), pool_scale_odd[0])
    return _outproj_odd(x1, y1, gate_x, final_g, wout_odd[0].astype(bf16), tm=512)
```

```python
import functools

import numpy as np
import jax
import jax.numpy as jnp
from jax import lax
from jax.experimental import pallas as pl
from jax.experimental.pallas import tpu as pltpu

D_MODEL = 1024
DEPTH = 2
CTX_LEN = 256
GRID_W = 64
D_INNER = 2 * D_MODEL
D_A = D_INNER // 2
D_B = D_INNER - D_A
N_HEADS_A = 4
HEAD_DIM_A = D_A // N_HEADS_A
N_GROUPS_B = 4
GROUP_B = D_B // N_GROUPS_B
N_GROUPS_C = 4
GROUP_C = D_INNER // N_GROUPS_C
POOL_WINDOWS = (2, 4, 8, 16)
CONV_W = 3
N_GATES = 4 * N_HEADS_A
W_MAIN_EVEN = 5 * D_A + 2 * D_B
EPS = 1e-6

f32 = jnp.float32
bf16 = jnp.bfloat16

V7X_VMEM_BYTES = 64 * 1024 * 1024
V7X_LANES = 128
V7X_SUBLANES = 8

MLSTM_CHUNK = 256
MLSTM_HEADS_PER_STEP = 2
FFT_N1 = 64
FFT_N2 = 64
FFT_PITCH = 72
FFT_UNROLL = 8
NEG_BIG = -1e30


def _vmem_limit(nbytes):
    return int(min(max(nbytes * 5 // 4 + (4 << 20), 16 << 20), V7X_VMEM_BYTES - (6 << 20)))


def _sigmoid(v):
    return 0.5 * jnp.tanh(0.5 * v) + 0.5


def _silu(v):
    return v * _sigmoid(v)


def _pipelined_matmuls(n, matmul, epilogue, pbuf_ref):
    zero = jnp.minimum(pl.program_id(0), 0)
    nbuf = pbuf_ref.shape[0]
    pbuf_ref[zero] = matmul(0)
    for t in range(n):
        if t + 1 < n:
            pbuf_ref[zero + (t + 1) % nbuf] = matmul(t + 1)
        epilogue(t, pbuf_ref[zero + t % nbuf])


def _log_sigmoid(v):
    return jnp.minimum(v, 0.0) - jnp.log1p(jnp.exp(-jnp.abs(v)))


def _mod_kernel(r_ref, w_ref, b_ref, o_ref):
    s = _silu(r_ref[...])
    o_ref[...] = jnp.dot(s, w_ref[...], preferred_element_type=f32,
                         precision=lax.Precision.HIGHEST) + b_ref[...]


def _modulation(rows, ada_w, ada_b):
    nrow = rows.shape[0]
    tn = 1024
    return pl.pallas_call(
        _mod_kernel,
        grid=(DEPTH, 3 * D_MODEL // tn),
        in_specs=[
            pl.BlockSpec((nrow, D_MODEL), lambda l, j: (0, 0)),
            pl.BlockSpec((None, D_MODEL, tn), lambda l, j: (l, 0, j)),
            pl.BlockSpec((None, 1, tn), lambda l, j: (l, 0, j)),
        ],
        out_specs=pl.BlockSpec((None, nrow, tn), lambda l, j: (l, 0, j)),
        out_shape=jax.ShapeDtypeStruct((DEPTH, nrow, 3 * D_MODEL), f32),
        compiler_params=pltpu.CompilerParams(dimension_semantics=("arbitrary", "arbitrary")),
        name="modulation",
    )(rows, ada_w, ada_b.reshape(DEPTH, 1, 3 * D_MODEL))


def _normed(x, amp, shift):
    ms = jnp.mean(x * x, axis=-1, keepdims=True)
    return (x * lax.rsqrt(ms + EPS)) * amp + shift


def _inproj_even_kernel(x_ref, xp_ref, xn_ref, sh_ref, sc_ref, g_ref, w_ref, wgt_ref, gbt_ref,
                        cw_ref, q_ref, k_ref, v_ref, o_ref, za_ref, ub_ref, zb_ref, gt_ref, pbuf_ref, *, tm, nt):
    i = pl.program_id(1)
    amp = g_ref[...] * (1.0 + sc_ref[...])
    shift = sh_ref[...]
    hx = _normed(x_ref[...], amp, shift).astype(bf16)
    halo = jnp.concatenate([xp_ref[...], xn_ref[...]], axis=0)
    hh = _normed(halo, amp, shift).astype(bf16)
    has_prev = (i > 0).astype(f32)
    has_next = (i < nt - 1).astype(f32)
    row = lax.broadcasted_iota(jnp.int32, (tm, 1), 0)
    cn = 512
    ph = jnp.dot(hh, w_ref[:, :2 * D_A], preferred_element_type=f32)
    prev = ph[V7X_SUBLANES - 1:V7X_SUBLANES, :] * has_prev
    nxt = ph[V7X_SUBLANES:V7X_SUBLANES + 1, :] * has_next

    def conv_store(j, p):
        cols = slice(j * cn, (j + 1) * cn)
        up = jnp.where(row == 0, prev[:, cols], pltpu.roll(p, 1, 0))
        dn = jnp.where(row == tm - 1, nxt[:, cols], pltpu.roll(p, tm - 1, 0))
        cw = cw_ref[:, cols]
        y = _silu(cw[0:1, :] * up + cw[1:2, :] * p + cw[2:3, :] * dn)
        if j < D_A // cn:
            q_ref[:, cols] = (y * (HEAD_DIM_A ** -0.5)).astype(bf16)
        else:
            jj = j - D_A // cn
            k_ref[:, jj * cn:(jj + 1) * cn] = y.astype(bf16)

    def plain_store(ref, jj):
        def store(p):
            ref[:, jj * cn:(jj + 1) * cn] = p.astype(bf16)
        return store

    tasks = [(j * cn, functools.partial(conv_store, j)) for j in range(2 * D_A // cn)]
    for idx, ref in enumerate((v_ref, o_ref, za_ref, ub_ref, zb_ref)):
        for jj in range(D_A // cn):
            tasks.append((2 * D_A + idx * D_A + jj * cn, plain_store(ref, jj)))

    def matmul(t):
        c0 = tasks[t][0]
        return jnp.dot(hx, w_ref[:, c0:c0 + cn], preferred_element_type=f32)

    _pipelined_matmuls(len(tasks), matmul, lambda t, p: tasks[t][1](p), pbuf_ref)
    gt_ref[...] = lax.dot_general(wgt_ref[...], hx, (((1,), (1,)), ((), ())),
                                  preferred_element_type=f32) + gbt_ref[...]


def _inproj_even(x, shift, scale, norm_g, w_main, wgt, gbt, conv_w, tm):
    B, T, D = x.shape
    nt = T // tm
    hb = tm // V7X_SUBLANES
    nhb = T // V7X_SUBLANES
    row_spec = pl.BlockSpec((None, tm, D_A), lambda b, i: (b, i, 0))
    vec = lambda n: pl.BlockSpec((1, n), lambda b, i: (0, 0))
    est = (2 * w_main.size * 2 + 2 * tm * D * 4 + 7 * 2 * tm * D_A * 2 + 6 * tm * 512 * 4)
    outs = pl.pallas_call(
        functools.partial(_inproj_even_kernel, tm=tm, nt=nt),
        grid=(B, nt),
        in_specs=[
            pl.BlockSpec((None, tm, D), lambda b, i: (b, i, 0)),
            pl.BlockSpec((None, V7X_SUBLANES, D), lambda b, i: (b, jnp.maximum(i * hb - 1, 0), 0)),
            pl.BlockSpec((None, V7X_SUBLANES, D), lambda b, i: (b, jnp.minimum((i + 1) * hb, nhb - 1), 0)),
            pl.BlockSpec((None, 1, D), lambda b, i: (b, 0, 0)),
            pl.BlockSpec((None, 1, D), lambda b, i: (b, 0, 0)),
            vec(D),
            pl.BlockSpec(w_main.shape, lambda b, i: (0, 0)),
            pl.BlockSpec(wgt.shape, lambda b, i: (0, 0)),
            pl.BlockSpec((N_GATES, 1), lambda b, i: (0, 0)),
            pl.BlockSpec(conv_w.shape, lambda b, i: (0, 0)),
        ],
        out_specs=[row_spec] * 7 + [pl.BlockSpec((None, N_GATES, tm), lambda b, i: (b, 0, i))],
        out_shape=[jax.ShapeDtypeStruct((B, T, D_A), bf16)] * 7 + [jax.ShapeDtypeStruct((B, N_GATES, T), f32)],
        scratch_shapes=[pltpu.VMEM((3, tm, 512), f32)],
        compiler_params=pltpu.CompilerParams(dimension_semantics=("parallel", "arbitrary"),
                                             vmem_limit_bytes=_vmem_limit(est)),
        name="inproj_even",
    )(x, x, x, shift, scale, norm_g.reshape(1, D), w_main, wgt, gbt, conv_w)
    return outs


GATE_SLOTS = 2 * N_HEADS_A
GATE_PIECES = 3
GATE_QUANTS = 3
assert GATE_SLOTS & (GATE_SLOTS - 1) == 0 and GATE_QUANTS * GATE_PIECES * GATE_SLOTS <= V7X_LANES


def _scan_max_lanes(x, reverse):
    n = x.shape[1]
    lane = lax.broadcasted_iota(jnp.int32, x.shape, 1)
    s = 1
    while s < n:
        if reverse:
            x = jnp.where(lane < n - s, jnp.maximum(x, pltpu.roll(x, n - s, 1)), x)
        else:
            x = jnp.where(lane >= s, jnp.maximum(x, pltpu.roll(x, s, 1)), x)
        s *= 2
    return x


def _split3(v):
    hi = v.astype(bf16)
    r1 = v - hi.astype(f32)
    mid = r1.astype(bf16)
    lo = (r1 - mid.astype(f32)).astype(bf16)
    return hi, mid, lo


def _gate_prep_kernel(gt_ref, cols_ref, arow_ref, *, L, nchunk):
    r = lax.broadcasted_iota(jnp.int32, (L, L), 0)
    c = lax.broadcasted_iota(jnp.int32, (L, L), 1)
    tri_l = (c <= r).astype(bf16)
    tri_u = (c >= r).astype(bf16)
    S = V7X_SUBLANES
    fwd = (lax.broadcasted_iota(jnp.int32, (S, L), 0) & 1) == 0
    fill = jnp.zeros((V7X_LANES - GATE_QUANTS * GATE_PIECES * S, L), f32)
    for ci in range(nchunk):
        sl = slice(ci * L, (ci + 1) * L)
        ig = gt_ref[:S, sl]
        lf = _log_sigmoid(gt_ref[S:, sl])
        pieces = jnp.concatenate([p.astype(f32) for p in _split3(lf)] + [jnp.zeros((S, L), f32)],
                                 axis=0).astype(bf16)
        pre = jnp.dot(pieces, tri_u, preferred_element_type=f32)
        suf = jnp.dot(pieces, tri_l, preferred_element_type=f32)
        pre = pre[:S] + pre[S:2 * S] + pre[2 * S:3 * S]
        suf = suf[:S] + suf[S:2 * S] + suf[2 * S:3 * S]
        b_row = jnp.where(fwd, pre, suf)
        a_row = ig - b_row
        cmax = jnp.where(fwd, _scan_max_lanes(a_row, False), _scan_max_lanes(a_row, True))
        arow_ref[:, sl] = a_row
        parts = [p.astype(f32) for quant in (b_row, a_row, cmax) for p in _split3(quant)]
        packed = jnp.concatenate(parts + [fill], axis=0)
        cols_ref[sl, :] = packed.T.astype(bf16)


def _gate_prep(gt, L):
    B, _, T = gt.shape
    nchunk = min(4, T // L)
    tb = nchunk * L
    cols, arow = pl.pallas_call(
        functools.partial(_gate_prep_kernel, L=L, nchunk=nchunk),
        grid=(B, T // tb),
        in_specs=[pl.BlockSpec((None, N_GATES, tb), lambda b, i: (b, 0, i))],
        out_specs=[
            pl.BlockSpec((None, tb, V7X_LANES), lambda b, i: (b, i, 0)),
            pl.BlockSpec((None, V7X_SUBLANES, tb), lambda b, i: (b, 0, i)),
        ],
        out_shape=[
            jax.ShapeDtypeStruct((B, T, V7X_LANES), bf16),
            jax.ShapeDtypeStruct((B, V7X_SUBLANES, T), f32),
        ],
        compiler_params=pltpu.CompilerParams(dimension_semantics=("parallel", "parallel")),
        name="gate_prep",
    )(gt)
    return cols, arow.reshape(B, N_HEADS_A, 2, T)


def _mlstm_kernel(qx_ref, kx_ref, vx_ref, cx_ref, rx_ref, qc_ref, kc_ref, vc_ref, cc_ref, rc_ref,
                  hx_ref, hc_ref, cn_ref, *, L, nx, nc):
    Dh, LN, HP = HEAD_DIM_A, V7X_LANES, MLSTM_HEADS_PER_STEP
    head0 = pl.program_id(1) * HP
    chains = [(hh, d) for hh in range(HP) for d in range(2)]
    r_i = lax.broadcasted_iota(jnp.int32, (L, L), 0)
    c_i = lax.broadcasted_iota(jnp.int32, (L, L), 1)
    masks = (c_i <= r_i, c_i >= r_i)

    sr = lax.broadcasted_iota(jnp.int32, (LN, GATE_QUANTS * LN), 0)
    sc = lax.broadcasted_iota(jnp.int32, (LN, GATE_QUANTS * LN), 1)
    span = GATE_PIECES * GATE_SLOTS
    in_block = None
    for qi in range(GATE_QUANTS):
        blk = (sr >= qi * span) & (sr < (qi + 1) * span) & (sc >= qi * LN) & (sc < (qi + 1) * LN)
        in_block = blk if in_block is None else in_block | blk
    slot = sr & (GATE_SLOTS - 1)
    sels = {(hh, d): (in_block & (slot == 2 * (head0 + hh) + d)).astype(bf16) for hh, d in chains}
    ones = jnp.ones((L, LN), bf16)

    def tile(v, width):
        return jnp.concatenate([v] * (width // LN), axis=1)

    def chunk(q_ref, k_ref, v_ref, col_ref, row_ref, r0, hh, d, m_prev):
        ci = chains.index((hh, d))
        rows = pl.ds(r0, L)
        hsl = slice(hh * Dh, (hh + 1) * Dh)
        q = q_ref[rows, hsl]
        k = k_ref[rows, hsl]
        vo = jnp.concatenate([v_ref[rows, hsl], ones], axis=1)
        rep = jnp.dot(col_ref[rows, :], sels[hh, d], preferred_element_type=f32)
        b_rep = rep[:, :LN]
        a_rep = rep[:, LN:2 * LN]
        g_rep = jnp.maximum(rep[:, 2 * LN:], m_prev)
        a_row = row_ref[hh, d:d + 1, rows]
        p = jnp.exp(jnp.where(masks[d], a_row - tile(g_rep, L), NEG_BIG))
        s = lax.dot_general(q, k, (((1,), (1,)), ((), ())), preferred_element_type=f32) * p
        sv = jnp.dot(s.astype(bf16), vo, preferred_element_type=f32)
        qcn = jnp.dot(q, cn_ref[ci].astype(bf16), preferred_element_type=f32)
        inter = jnp.exp(m_prev - g_rep)
        den = sv[:, Dh:] + inter * qcn[:, Dh:]
        rcp = 1.0 / jnp.maximum(jnp.abs(den), jnp.exp(-(b_rep + g_rep)))
        h = (sv[:, :Dh] + tile(inter, Dh) * qcn[:, :Dh]) * tile(rcp, Dh)

        b_end = b_rep[L - 1:L, :] if d == 0 else b_rep[0:1, :]
        w = b_end + a_rep
        m_new = jnp.maximum(b_end + m_prev, jnp.max(w, axis=0, keepdims=True))
        decay = jnp.exp(b_end + m_prev - m_new)
        kw = (k.astype(f32) * tile(jnp.exp(w - m_new), Dh)).astype(bf16)
        upd = lax.dot_general(kw, vo, (((0,), (0,)), ((), ())), preferred_element_type=f32)
        cn_ref[ci] = tile(decay, Dh + LN) * cn_ref[ci] + upd
        return h, m_new

    ctx = (qc_ref, kc_ref, vc_ref, cc_ref, rc_ref)
    lat = (qx_ref, kx_ref, vx_ref, cx_ref, rx_ref)

    cn_ref[...] = jnp.zeros_like(cn_ref)
    ms = [jnp.zeros((1, LN), f32) for _ in chains]
    written = set()
    for step in range(nc):
        for ci, (hh, d) in enumerate(chains):
            cj = step if d == 0 else nc - 1 - step
            h, ms[ci] = chunk(*ctx, cj * L, hh, d, ms[ci])
            dst = (slice(cj * L, (cj + 1) * L), slice(hh * Dh, (hh + 1) * Dh))
            if (cj, hh) in written:
                hc_ref[dst] += h
            else:
                hc_ref[dst] = h
                written.add((cj, hh))

    def make_body(accumulate):
        def body(i, ms):
            ms = list(ms)
            for ci, (hh, d) in enumerate(chains):
                r0 = pl.multiple_of((i if d == 0 else nx - 1 - i) * L, L)
                h, ms[ci] = chunk(*lat, r0, hh, d, ms[ci])
                dst = (pl.ds(r0, L), slice(hh * Dh, (hh + 1) * Dh))
                if accumulate:
                    hx_ref[dst] += h
                else:
                    hx_ref[dst] = h
            return tuple(ms)
        return body

    ms = lax.fori_loop(0, nx // 2, make_body(False), tuple(ms))
    lax.fori_loop(nx // 2, nx, make_body(True), ms)


def _mlstm(qx, kx, vx, colsx, rowsx, qc, kc, vc, colsc, rowsc, L):
    B, T, _ = qx.shape
    Tc = qc.shape[1]
    H, Dh, HP = N_HEADS_A, HEAD_DIM_A, MLSTM_HEADS_PER_STEP
    assert T % (2 * L) == 0 and Tc % L == 0 and H % HP == 0

    def seq_spec(t):
        return pl.BlockSpec((None, t, HP * Dh), lambda b, h: (b, 0, h))

    def col_spec(t):
        return pl.BlockSpec((None, t, V7X_LANES), lambda b, h: (b, 0, 0))

    def row_spec(t):
        return pl.BlockSpec((None, HP, 2, t), lambda b, h: (b, h, 0, 0))

    est = (2 * HP * (3 * (T + Tc) * Dh * 2 + 8 * (T + Tc) * 4 + (T + Tc) * Dh * 4)
           + 2 * (T + Tc) * V7X_LANES * 2 + 2 * HP * Dh * (Dh + V7X_LANES) * 4 + 16 * L * L * 4)
    return pl.pallas_call(
        functools.partial(_mlstm_kernel, L=L, nx=T // L, nc=Tc // L),
        grid=(B, H // HP),
        in_specs=[seq_spec(T), seq_spec(T), seq_spec(T), col_spec(T), row_spec(T),
                  seq_spec(Tc), seq_spec(Tc), seq_spec(Tc), col_spec(Tc), row_spec(Tc)],
        out_specs=[seq_spec(T), seq_spec(Tc)],
        out_shape=[jax.ShapeDtypeStruct((B, T, D_A), f32), jax.ShapeDtypeStruct((B, Tc, D_A), f32)],
        scratch_shapes=[pltpu.VMEM((2 * HP, Dh, Dh + V7X_LANES), f32)],
        compiler_params=pltpu.CompilerParams(dimension_semantics=("parallel", "parallel"),
                                             vmem_limit_bytes=_vmem_limit(est)),
        name="mlstm",
    )(qx, kx, vx, colsx, rowsx, qc, kc, vc, colsc, rowsc)


def _fourier_w_kernel(cs_ref, fw_ref, o_ref, *, scale):
    o_ref[...] = (jnp.dot(cs_ref[...], fw_ref[...], preferred_element_type=f32,
                          precision=lax.Precision.HIGHEST) * scale).astype(bf16)


def _fourier_weights(fw, T):
    n = GROUP_B
    kk = np.outer(np.arange(n), np.arange(n)) % n
    ang = 2.0 * np.pi * kk / n
    cs = jnp.asarray(np.concatenate([np.cos(ang), np.sin(ang)], axis=0), f32)
    return pl.pallas_call(
        functools.partial(_fourier_w_kernel, scale=float(1.0 / np.sqrt(T * n))),
        grid=(N_GROUPS_B,),
        in_specs=[pl.BlockSpec((2 * n, n), lambda g: (0, 0)),
                  pl.BlockSpec((None, n, n), lambda g: (g, 0, 0))],
        out_specs=pl.BlockSpec((None, 2 * n, n), lambda g: (g, 0, 0)),
        out_shape=jax.ShapeDtypeStruct((N_GROUPS_B, 2 * n, n), bf16),
        compiler_params=pltpu.CompilerParams(dimension_semantics=("arbitrary",)),
        name="fourier_weights",
    )(cs, fw)


def _fft_tables():
    n1, n2 = FFT_N1, FFT_N2
    n = n1 * n2
    t1 = np.arange(n1)
    k1 = np.arange(n1)
    t2 = np.arange(n2)
    idx = (k1[None, :, None] * (n2 * t1[None, None, :] + t2[:, None, None])) % n
    ang = 2.0 * np.pi * idx / n
    m1 = np.concatenate([np.cos(ang), -np.sin(ang)], axis=1)
    k2 = np.arange(n2)
    ph = 2.0 * np.pi * (np.outer(k2, t2) % n2) / n2
    c, s = np.cos(ph), np.sin(ph)
    m2 = np.block([[c, s], [-s, c]])
    return jnp.asarray(m1, f32).astype(bf16), jnp.asarray(m2, f32).astype(bf16)


def _fourier_kernel(u_ref, z_ref, m1_ref, m2_ref, w_ref, o_ref, up_ref, yp_ref, xp_ref, x2_ref, pbuf_ref):
    n1, n2, P = FFT_N1, FFT_N2, FFT_PITCH
    nl = GROUP_B // V7X_LANES

    def fill(t1, carry):
        src = pl.multiple_of(t1 * n2, n2)
        dst = pl.multiple_of(t1 * P, V7X_SUBLANES)
        blk = u_ref[pl.ds(src, n2), :].astype(f32)
        for s in range(nl):
            up_ref[s, pl.ds(dst, n2), :] = blk[:, s * V7X_LANES:(s + 1) * V7X_LANES]
        return carry

    lax.fori_loop(0, n1, fill, 0, unroll=4)

    def stage1(t2, carry):
        rhs = jnp.concatenate([up_ref[s, pl.ds(t2, n1, stride=P), :] for s in range(nl)], axis=1)
        y = jnp.dot(m1_ref[t2], rhs.astype(bf16), preferred_element_type=f32)
        dst = pl.multiple_of(t2 * P, V7X_SUBLANES)
        for ri in range(2):
            for s in range(nl):
                yp_ref[ri * nl + s, pl.ds(dst, n1), :] = y[ri * n1:(ri + 1) * n1,
                                                           s * V7X_LANES:(s + 1) * V7X_LANES]
        return carry

    lax.fori_loop(0, n2, stage1, 0, unroll=FFT_UNROLL)

    m2 = m2_ref[...]

    def stage2(k1, carry):
        parts = []
        for ri in range(2):
            parts.append(jnp.concatenate(
                [yp_ref[ri * nl + s, pl.ds(k1, n2, stride=P), :] for s in range(nl)], axis=1))
        rhs = jnp.concatenate(parts, axis=0).astype(bf16)
        x = jnp.dot(m2, rhs, preferred_element_type=f32)
        for ri in range(2):
            for s in range(nl):
                xp_ref[ri * nl + s, pl.ds(k1, n2, stride=P), :] = x[ri * n2:(ri + 1) * n2,
                                                                    s * V7X_LANES:(s + 1) * V7X_LANES]
        return carry

    lax.fori_loop(0, n1, stage2, 0, unroll=FFT_UNROLL)

    def unpitch(k2, carry):
        src = pl.multiple_of(k2 * P, V7X_SUBLANES)
        dst = pl.multiple_of(k2 * n1, n1)
        for ri in range(2):
            for s in range(nl):
                c0 = (ri * nl + s) * V7X_LANES
                x2_ref[pl.ds(dst, n1), c0:c0 + V7X_LANES] = xp_ref[ri * nl + s, pl.ds(src, n1), :].astype(bf16)
        return carry

    lax.fori_loop(0, n2, unpitch, 0, unroll=4)

    tr = pbuf_ref.shape[1]

    def matmul(t):
        return jnp.dot(x2_ref[t * tr:(t + 1) * tr, :], w_ref[...], preferred_element_type=f32)

    def gate_store(t, y):
        sl = slice(t * tr, (t + 1) * tr)
        o_ref[sl, :] = (y * _silu(z_ref[sl, :].astype(f32))).astype(bf16)

    _pipelined_matmuls((n1 * n2) // tr, matmul, gate_store, pbuf_ref)


def _fourier_latent(ub, zb, wcs):
    B, T, _ = ub.shape
    assert T == FFT_N1 * FFT_N2
    m1, m2 = _fft_tables()
    G, n = N_GROUPS_B, GROUP_B
    nl = n // V7X_LANES
    rows_p = FFT_N1 * FFT_PITCH
    blk = pl.BlockSpec((None, T, n), lambda b, g: (b, 0, g))
    est = (3 * 2 * T * n * 2 + 5 * nl * rows_p * V7X_LANES * 4 + T * 2 * n * 2 + 2 * m1.size * 2 + 8 * 512 * n * 4)
    return pl.pallas_call(
        _fourier_kernel,
        grid=(B, G),
        in_specs=[blk, blk,
                  pl.BlockSpec(m1.shape, lambda b, g: (0, 0, 0)),
                  pl.BlockSpec(m2.shape, lambda b, g: (0, 0)),
                  pl.BlockSpec((None, 2 * n, n), lambda b, g: (g, 0, 0))],
        out_specs=blk,
        out_shape=jax.ShapeDtypeStruct((B, T, D_B), bf16),
        scratch_shapes=[pltpu.VMEM((nl, rows_p, V7X_LANES), f32),
                        pltpu.VMEM((2 * nl, rows_p, V7X_LANES), f32),
                        pltpu.VMEM((2 * nl, rows_p, V7X_LANES), f32),
                        pltpu.VMEM((T, 2 * n), bf16),
                        pltpu.VMEM((3, 512, n), f32)],
        compiler_params=pltpu.CompilerParams(dimension_semantics=("parallel", "parallel"),
                                             vmem_limit_bytes=_vmem_limit(est)),
        name="fourier_latent",
    )(ub, zb, m1, m2, wcs)


def _fourier_ctx_kernel(u_ref, z_ref, cs_ref, w_ref, o_ref):
    u = u_ref[...]
    x = jnp.dot(cs_ref[...], u, preferred_element_type=f32).astype(bf16)
    t = u.shape[0]
    w = w_ref[...]
    n = GROUP_B
    y = (jnp.dot(x[:t], w[:n], preferred_element_type=f32) + jnp.dot(x[t:], w[n:], preferred_element_type=f32))
    o_ref[...] = (y * _silu(z_ref[...].astype(f32))).astype(bf16)


def _fourier_ctx(ub, zb, wcs):
    B, T, _ = ub.shape
    ang = 2.0 * np.pi * (np.outer(np.arange(T), np.arange(T)) % T) / T
    cs = jnp.asarray(np.concatenate([np.cos(ang), -np.sin(ang)], axis=0), f32).astype(bf16)
    n = GROUP_B
    blk = pl.BlockSpec((None, T, n), lambda b, g: (b, 0, g))
    return pl.pallas_call(
        _fourier_ctx_kernel,
        grid=(B, N_GROUPS_B),
        in_specs=[blk, blk, pl.BlockSpec(cs.shape, lambda b, g: (0, 0)),
                  pl.BlockSpec((None, 2 * n, n), lambda b, g: (g, 0, 0))],
        out_specs=blk,
        out_shape=jax.ShapeDtypeStruct((B, T, D_B), bf16),
        compiler_params=pltpu.CompilerParams(dimension_semantics=("parallel", "parallel")),
        name="fourier_ctx",
    )(ub, zb, cs, wcs)


def _outproj_even_kernel(x_ref, h_ref, o_ref, za_ref, yb_ref, gate_ref, hg_ref, w_ref, out_ref):
    parts = []
    for hd in range(N_HEADS_A):
        sl = slice(hd * HEAD_DIM_A, (hd + 1) * HEAD_DIM_A)
        hh = _sigmoid(o_ref[:, sl].astype(f32)) * h_ref[:, sl]
        hh = hh * lax.rsqrt(jnp.mean(hh * hh, axis=-1, keepdims=True) + EPS)
        parts.append((hh * hg_ref[:, sl] * _silu(za_ref[:, sl].astype(f32))).astype(bf16))
    ya = jnp.concatenate(parts, axis=1)
    acc = jnp.dot(ya, w_ref[:D_A, :], preferred_element_type=f32)
    acc = acc + jnp.dot(yb_ref[...], w_ref[D_A:, :], preferred_element_type=f32)
    out_ref[...] = x_ref[...] + gate_ref[...] * acc


def _outproj_even(x, h, o, za, yb, gate, head_g, wout, tm):
    B, T, D = x.shape
    row = lambda n: pl.BlockSpec((None, tm, n), lambda b, i: (b, i, 0))
    est = 2 * wout.size * 2 + 2 * tm * (2 * D * 4 + D_A * 4 + 3 * D_A * 2) + 8 * tm * D * 4
    return pl.pallas_call(
        _outproj_even_kernel,
        grid=(B, T // tm),
        in_specs=[row(D), row(D_A), row(D_A), row(D_A), row(D_B),
                  pl.BlockSpec((None, 1, D), lambda b, i: (b, 0, 0)),
                  pl.BlockSpec((1, D_A), lambda b, i: (0, 0)),
                  pl.BlockSpec(wout.shape, lambda b, i: (0, 0))],
        out_specs=row(D),
        out_shape=jax.ShapeDtypeStruct((B, T, D), f32),
        compiler_params=pltpu.CompilerParams(dimension_semantics=("parallel", "parallel"),
                                             vmem_limit_bytes=_vmem_limit(est)),
        name="outproj_even",
    )(x, h, o, za, yb, gate, head_g.reshape(1, D_A), wout)


def _inproj_odd_kernel(x_ref, sh_ref, sc_ref, g_ref, w_ref, u_ref, z_ref):
    amp = g_ref[...] * (1.0 + sc_ref[...])
    hx = _normed(x_ref[...], amp, sh_ref[...]).astype(bf16)
    cn = 512
    for j in range(D_INNER // cn):
        u_ref[:, j * cn:(j + 1) * cn] = jnp.dot(hx, w_ref[:, j * cn:(j + 1) * cn],
                                                preferred_element_type=f32).astype(bf16)
        z_ref[:, j * cn:(j + 1) * cn] = jnp.dot(hx, w_ref[:, D_INNER + j * cn:D_INNER + (j + 1) * cn],
                                                preferred_element_type=f32).astype(bf16)


def _inproj_odd(x, shift, scale, norm_g, w, tm):
    B, T, D = x.shape
    row = lambda n: pl.BlockSpec((None, tm, n), lambda b, i: (b, i, 0))
    est = 2 * w.size * 2 + 2 * tm * D * 4 + 4 * tm * D_INNER * 2 + 6 * tm * 512 * 4
    return pl.pallas_call(
        _inproj_odd_kernel,
        grid=(B, T // tm),
        in_specs=[row(D),
                  pl.BlockSpec((None, 1, D), lambda b, i: (b, 0, 0)),
                  pl.BlockSpec((None, 1, D), lambda b, i: (b, 0, 0)),
                  pl.BlockSpec((1, D), lambda b, i: (0, 0)),
                  pl.BlockSpec(w.shape, lambda b, i: (0, 0))],
        out_specs=[row(D_INNER), row(D_INNER)],
        out_shape=[jax.ShapeDtypeStruct((B, T, D_INNER), bf16)] * 2,
        compiler_params=pltpu.CompilerParams(dimension_semantics=("parallel", "parallel"),
                                             vmem_limit_bytes=_vmem_limit(est)),
        name="inproj_odd",
    )(x, shift, scale, norm_g.reshape(1, D), w)


POOL_UNROLL = 8


def _pool_tables():
    w_idx = np.arange(GRID_W)
    band = np.zeros((N_GROUPS_C, GRID_W, GRID_W), np.float32)
    inv_w = np.zeros((N_GROUPS_C, GRID_W, V7X_LANES), np.float32)
    for g, win in enumerate(POOL_WINDOWS):
        lo = np.clip(w_idx - win // 2, 0, GRID_W)
        hi = np.clip(w_idx + win - win // 2, 0, GRID_W)
        band[g] = (w_idx[None, :] >= lo[:, None]) & (w_idx[None, :] < hi[:, None])
        inv_w[g] = (1.0 / (hi - lo))[:, None]
    return jnp.asarray(band, bf16), jnp.asarray(inv_w, f32)


def _pool_kernel(u_ref, z_ref, band_ref, invw_ref, pw_ref, sc_ref, o_ref, ps_ref, yp_ref, pbuf_ref, *, rows):
    g = pl.program_id(1)
    W = GRID_W
    band = band_ref[...]
    lo_off = hi_off = 0
    for gi, win in enumerate(POOL_WINDOWS):
        lo_off = jnp.where(g == gi, win // 2, lo_off)
        hi_off = jnp.where(g == gi, win - win // 2, hi_off)

    ps_ref[0:W, :] = jnp.zeros((W, GROUP_C), f32)

    def width_sum(r, carry):
        src = pl.multiple_of(r * W, W)
        dst = pl.multiple_of((r + 1) * W, W)
        s = jnp.dot(band, u_ref[pl.ds(src, W), :], preferred_element_type=f32)
        ps_ref[pl.ds(dst, W), :] = ps_ref[pl.ds(src, W), :] + s
        return carry

    lax.fori_loop(0, rows, width_sum, 0, unroll=POOL_UNROLL)

    inv_w = invw_ref[...]

    def rows_pass(r, carry):
        lo = jnp.maximum(r - lo_off, 0)
        hi = jnp.minimum(r + hi_off, rows)
        acc = (ps_ref[pl.ds(pl.multiple_of(hi * W, W), W), :]
               - ps_ref[pl.ds(pl.multiple_of(lo * W, W), W), :])
        inv = inv_w / (hi - lo).astype(f32)
        inv = jnp.concatenate([inv] * (GROUP_C // V7X_LANES), axis=1)
        src = pl.multiple_of(r * W, W)
        ug = u_ref[pl.ds(src, W), :].astype(f32)
        yp_ref[pl.ds(src, W), :] = (acc * inv - ug).astype(bf16)
        return carry

    lax.fori_loop(0, rows, rows_pass, 0, unroll=2)

    tr = pbuf_ref.shape[1]
    pws = (pw_ref[...] * sc_ref[...]).astype(bf16)

    def matmul(t):
        return jnp.dot(yp_ref[t * tr:(t + 1) * tr, :], pws, preferred_element_type=f32)

    def gate_store(t, y):
        sl = slice(t * tr, (t + 1) * tr)
        o_ref[sl, :] = y.astype(bf16) * _silu(z_ref[sl, :])

    _pipelined_matmuls((rows * W) // tr, matmul, gate_store, pbuf_ref)


def _pool_mix(u, z, pool_w, scale):
    B, T, _ = u.shape
    rows = T // GRID_W
    band, inv_w = _pool_tables()
    n = GROUP_C
    blk = pl.BlockSpec((None, T, n), lambda b, g: (b, 0, g))
    est = 3 * 2 * T * n * 2 + (rows + 1) * GRID_W * n * 4 + T * n * 2 + 2 * n * n * 4 + 8 * 512 * n * 4
    return pl.pallas_call(
        functools.partial(_pool_kernel, rows=rows),
        grid=(B, N_GROUPS_C),
        in_specs=[blk, blk,
                  pl.BlockSpec((None, GRID_W, GRID_W), lambda b, g: (g, 0, 0)),
                  pl.BlockSpec((None, GRID_W, V7X_LANES), lambda b, g: (g, 0, 0)),
                  pl.BlockSpec((None, n, n), lambda b, g: (g, 0, 0)),
                  pl.BlockSpec((None, 1, n), lambda b, g: (g, 0, 0))],
        out_specs=blk,
        out_shape=jax.ShapeDtypeStruct((B, T, D_INNER), bf16),
        scratch_shapes=[pltpu.VMEM(((rows + 1) * GRID_W, n), f32),
                        pltpu.VMEM((T, n), bf16),
                        pltpu.VMEM((3, 512, n), f32)],
        compiler_params=pltpu.CompilerParams(dimension_semantics=("parallel", "arbitrary"),
                                             vmem_limit_bytes=_vmem_limit(est)),
        name="pool_mix",
    )(u, z, band, inv_w, pool_w, scale.reshape(N_GROUPS_C, 1, n))


def _outproj_odd_kernel(x_ref, y_ref, gate_ref, fg_ref, w_ref, out_ref):
    acc = jnp.dot(y_ref[...], w_ref[...], preferred_element_type=f32)
    x = x_ref[...] + gate_ref[...] * acc
    out_ref[...] = x * lax.rsqrt(jnp.mean(x * x, axis=-1, keepdims=True) + EPS) * fg_ref[...]


def _outproj_odd(x, y, gate, final_g, wout, tm):
    B, T, D = x.shape
    row = lambda n: pl.BlockSpec((None, tm, n), lambda b, i: (b, i, 0))
    est = 2 * wout.size * 2 + 2 * tm * (2 * D * 4 + D_INNER * 2) + 6 * tm * D * 4
    return pl.pallas_call(
        _outproj_odd_kernel,
        grid=(B, T // tm),
        in_specs=[row(D), row(D_INNER),
                  pl.BlockSpec((None, 1, D), lambda b, i: (b, 0, 0)),
                  pl.BlockSpec((1, D), lambda b, i: (0, 0)),
                  pl.BlockSpec(wout.shape, lambda b, i: (0, 0))],
        out_specs=row(D),
        out_shape=jax.ShapeDtypeStruct((B, T, D), f32),
        compiler_params=pltpu.CompilerParams(dimension_semantics=("parallel", "parallel"),
                                             vmem_limit_bytes=_vmem_limit(est)),
        name="outproj_odd",
    )(x, y, gate, final_g.reshape(1, D), wout)


def kernel(x, c, ctx, c_ctx, ada_w, ada_b, norm_g, win_even, gate_b_even, conv_qk_even, head_norm_even,
           fourier_w_even, wout_even, win_odd, pool_w_odd, pool_scale_odd, wout_odd, final_g):
    B, T, D = x.shape
    Tc = ctx.shape[1]
    H = N_HEADS_A
    L = MLSTM_CHUNK

    nrow = -(-(B + 1) // V7X_SUBLANES) * V7X_SUBLANES
    rows_in = jnp.zeros((nrow, D), f32).at[:B].set(c).at[B].set(c_ctx)
    mod = _modulation(rows_in, ada_w, ada_b)

    def mod_parts(l, r0, r1, n):
        m = mod[l, r0:r1]
        parts = [jnp.broadcast_to(m[:, None, i * D:(i + 1) * D], (n, 1, D)) for i in range(3)]
        return parts

    we = win_even[0]
    w_main = we[:, :W_MAIN_EVEN].astype(bf16)
    gcols = we[:, W_MAIN_EVEN:].reshape(D, 4, H)
    gbias = gate_b_even[0].reshape(4, H)
    ig_w = gcols[:, 0::2, :].transpose(0, 2, 1).reshape(D, GATE_SLOTS)
    fg_w = gcols[:, 1::2, :].transpose(0, 2, 1).reshape(D, GATE_SLOTS)
    ig_b = gbias[0::2, :].T.reshape(GATE_SLOTS)
    fg_b = gbias[1::2, :].T.reshape(GATE_SLOTS)
    wgt =jnp.concatenate([ig_w, fg_w], axis=1).T.astype(bf16)
    gbt = jnp.concatenate([ig_b, fg_b]).reshape(N_GATES, 1)
    conv_w = conv_qk_even[0]

    shift_x, scale_x, gate_x = mod_parts(0, 0, B, B)
    shift_c, scale_c, gate_c = mod_parts(0, B, B + 1, B)

    qx, kx, vx, ox, zax, ubx, zbx, gtx = _inproj_even(
        x, shift_x, scale_x, norm_g[0], w_main, wgt, gbt, conv_w, tm=512)
    qc, kc, vc, oc, zac, ubc, zbc, gtc = _inproj_even(
        ctx, shift_c, scale_c, norm_g[0], w_main, wgt, gbt, conv_w, tm=Tc)

    colsx, rowsx = _gate_prep(gtx, L)
    colsc, rowsc = _gate_prep(gtc, L)
    h_x, h_c = _mlstm(qx, kx, vx, colsx, rowsx, qc, kc, vc, colsc, rowsc, L)

    wcs = _fourier_weights(fourier_w_even[0], T)
    yb_x = _fourier_latent(ubx, zbx, wcs)
    wout_e = wout_even[0].astype(bf16)
    x1 = _outproj_even(x, h_x, ox, zax, yb_x, gate_x, head_norm_even[0], wout_e, tm=512)

    wcs_c = _fourier_weights(fourier_w_even[0], Tc)
    yb_c = _fourier_ctx(ubc, zbc, wcs_c)
    ctx1 = _outproj_even(ctx, h_c, oc, zac, yb_c, gate_c, head_norm_even[0], wout_e, tm=Tc)
    del ctx1

    shift_x, scale_x, gate_x = mod_parts(1, 0, B, B)
    u1, z1 = _inproj_odd(x1, shift_x, scale_x, norm_g[1], win_odd[0].astype(bf16), tm=512)
    y1 = _pool_mix(u1, z1, pool_w_odd[0], pool_scale_odd[0])
    return _outproj_odd(x1, y1, gate_x, final_g, wout_odd[0].astype(bf16), tm=512)
```

```python
import functools

import numpy as np
import jax
import jax.numpy as jnp
from jax import lax
from jax.experimental import pallas as pl
from jax.experimental.pallas import tpu as pltpu

D_MODEL = 1024
DEPTH = 2
CTX_LEN = 256
GRID_W = 64
D_INNER = 2 * D_MODEL
D_A = D_INNER // 2
D_B = D_INNER - D_A
N_HEADS_A = 4
HEAD_DIM_A = D_A // N_HEADS_A
N_GROUPS_B = 4
GROUP_B = D_B // N_GROUPS_B
N_GROUPS_C = 4
GROUP_C = D_INNER // N_GROUPS_C
POOL_WINDOWS = (2, 4, 8, 16)
CONV_W = 3
N_GATES = 4 * N_HEADS_A
W_MAIN_EVEN = 5 * D_A + 2 * D_B
EPS = 1e-6

f32 = jnp.float32
bf16 = jnp.bfloat16

V7X_VMEM_BYTES = 64 * 1024 * 1024
V7X_LANES = 128
V7X_SUBLANES = 8

MLSTM_CHUNK = 256
MLSTM_HEADS_PER_STEP = 2
FFT_N1 = 64
FFT_N2 = 64
FFT_PITCH = 72
FFT_UNROLL = 8
NEG_BIG = -1e30


def _vmem_limit(nbytes):
    return int(min(max(nbytes * 5 // 4 + (4 << 20), 16 << 20), V7X_VMEM_BYTES - (6 << 20)))


def _sigmoid(v):
    return 0.5 * jnp.tanh(0.5 * v) + 0.5


def _silu(v):
    return v * _sigmoid(v)


def _pipelined_matmuls(n, matmul, epilogue, pbuf_ref):
    zero = jnp.minimum(pl.program_id(0), 0)
    nbuf = pbuf_ref.shape[0]
    pbuf_ref[zero] = matmul(0)
    for t in range(n):
        if t + 1 < n:
            pbuf_ref[zero + (t + 1) % nbuf] = matmul(t + 1)
        epilogue(t, pbuf_ref[zero + t % nbuf])


def _log_sigmoid(v):
    return jnp.minimum(v, 0.0) - jnp.log1p(jnp.exp(-jnp.abs(v)))


def _mod_kernel(r_ref, w_ref, b_ref, o_ref):
    s = _silu(r_ref[...])
    o_ref[...] = jnp.dot(s, w_ref[...], preferred_element_type=f32,
                         precision=lax.Precision.HIGHEST) + b_ref[...]


def _modulation(rows, ada_w, ada_b):
    nrow = rows.shape[0]
    tn = 1024
    return pl.pallas_call(
        _mod_kernel,
        grid=(DEPTH, 3 * D_MODEL // tn),
        in_specs=[
            pl.BlockSpec((nrow, D_MODEL), lambda l, j: (0, 0)),
            pl.BlockSpec((None, D_MODEL, tn), lambda l, j: (l, 0, j)),
            pl.BlockSpec((None, 1, tn), lambda l, j: (l, 0, j)),
        ],
        out_specs=pl.BlockSpec((None, nrow, tn), lambda l, j: (l, 0, j)),
        out_shape=jax.ShapeDtypeStruct((DEPTH, nrow, 3 * D_MODEL), f32),
        compiler_params=pltpu.CompilerParams(dimension_semantics=("arbitrary", "arbitrary")),
        name="modulation",
    )(rows, ada_w, ada_b.reshape(DEPTH, 1, 3 * D_MODEL))


def _normed(x, amp, shift):
    ms = jnp.mean(x * x, axis=-1, keepdims=True)
    return (x * lax.rsqrt(ms + EPS)) * amp + shift


def _inproj_even_kernel(x_ref, xp_ref, xn_ref, sh_ref, sc_ref, g_ref, w_ref, wgt_ref, gbt_ref,
                        cw_ref, q_ref, k_ref, v_ref, o_ref, za_ref, ub_ref, zb_ref, gt_ref, pbuf_ref, *, tm, nt):
    i = pl.program_id(1)
    amp = g_ref[...] * (1.0 + sc_ref[...])
    shift = sh_ref[...]
    hx = _normed(x_ref[...], amp, shift).astype(bf16)
    halo = jnp.concatenate([xp_ref[...], xn_ref[...]], axis=0)
    hh = _normed(halo, amp, shift).astype(bf16)
    has_prev = (i > 0).astype(f32)
    has_next = (i < nt - 1).astype(f32)
    row = lax.broadcasted_iota(jnp.int32, (tm, 1), 0)
    cn = 512
    ph = jnp.dot(hh, w_ref[:, :2 * D_A], preferred_element_type=f32)
    prev = ph[V7X_SUBLANES - 1:V7X_SUBLANES, :] * has_prev
    nxt = ph[V7X_SUBLANES:V7X_SUBLANES + 1, :] * has_next

    def conv_store(j, p):
        cols = slice(j * cn, (j + 1) * cn)
        up = jnp.where(row == 0, prev[:, cols], pltpu.roll(p, 1, 0))
        dn = jnp.where(row == tm - 1, nxt[:, cols], pltpu.roll(p, tm - 1, 0))
        cw = cw_ref[:, cols]
        y = _silu(cw[0:1, :] * up + cw[1:2, :] * p + cw[2:3, :] * dn)
        if j < D_A // cn:
            q_ref[:, cols] = (y * (HEAD_DIM_A ** -0.5)).astype(bf16)
        else:
            jj = j - D_A // cn
            k_ref[:, jj * cn:(jj + 1) * cn] = y.astype(bf16)

    def plain_store(ref, jj):
        def store(p):
            ref[:, jj * cn:(jj + 1) * cn] = p.astype(bf16)
        return store

    tasks = [(j * cn, functools.partial(conv_store, j)) for j in range(2 * D_A // cn)]
    for idx, ref in enumerate((v_ref, o_ref, za_ref, ub_ref, zb_ref)):
        for jj in range(D_A // cn):
            tasks.append((2 * D_A + idx * D_A + jj * cn, plain_store(ref, jj)))

    def matmul(t):
        c0 = tasks[t][0]
        return jnp.dot(hx, w_ref[:, c0:c0 + cn], preferred_element_type=f32)

    _pipelined_matmuls(len(tasks), matmul, lambda t, p: tasks[t][1](p), pbuf_ref)
    gt_ref[...] = lax.dot_general(wgt_ref[...], hx, (((1,), (1,)), ((), ())),
                                  preferred_element_type=f32) + gbt_ref[...]


def _inproj_even(x, shift, scale, norm_g, w_main, wgt, gbt, conv_w, tm):
    B, T, D = x.shape
    nt = T // tm
    hb = tm // V7X_SUBLANES
    nhb = T // V7X_SUBLANES
    row_spec = pl.BlockSpec((None, tm, D_A), lambda b, i: (b, i, 0))
    vec = lambda n: pl.BlockSpec((1, n), lambda b, i: (0, 0))
    est = (2 * w_main.size * 2 + 2 * tm * D * 4 + 7 * 2 * tm * D_A * 2 + 6 * tm * 512 * 4)
    outs = pl.pallas_call(
        functools.partial(_inproj_even_kernel, tm=tm, nt=nt),
        grid=(B, nt),
        in_specs=[
            pl.BlockSpec((None, tm, D), lambda b, i: (b, i, 0)),
            pl.BlockSpec((None, V7X_SUBLANES, D), lambda b, i: (b, jnp.maximum(i * hb - 1, 0), 0)),
            pl.BlockSpec((None, V7X_SUBLANES, D), lambda b, i: (b, jnp.minimum((i + 1) * hb, nhb - 1), 0)),
            pl.BlockSpec((None, 1, D), lambda b, i: (b, 0, 0)),
            pl.BlockSpec((None, 1, D), lambda b, i: (b, 0, 0)),
            vec(D),
            pl.BlockSpec(w_main.shape, lambda b, i: (0, 0)),
            pl.BlockSpec(wgt.shape, lambda b, i: (0, 0)),
            pl.BlockSpec((N_GATES, 1), lambda b, i: (0, 0)),
            pl.BlockSpec(conv_w.shape, lambda b, i: (0, 0)),
        ],
        out_specs=[row_spec] * 7 + [pl.BlockSpec((None, N_GATES, tm), lambda b, i: (b, 0, i))],
        out_shape=[jax.ShapeDtypeStruct((B, T, D_A), bf16)] * 7 + [jax.ShapeDtypeStruct((B, N_GATES, T), f32)],
        scratch_shapes=[pltpu.VMEM((3, tm, 512), f32)],
        compiler_params=pltpu.CompilerParams(dimension_semantics=("parallel", "arbitrary"),
                                             vmem_limit_bytes=_vmem_limit(est)),
        name="inproj_even",
    )(x, x, x, shift, scale, norm_g.reshape(1, D), w_main, wgt, gbt, conv_w)
    return outs


GATE_SLOTS = 2 * N_HEADS_A
GATE_PIECES = 3
GATE_QUANTS = 3
assert GATE_SLOTS & (GATE_SLOTS - 1) == 0 and GATE_QUANTS * GATE_PIECES * GATE_SLOTS <= V7X_LANES


def _scan_max_lanes(x, reverse):
    n = x.shape[1]
    lane = lax.broadcasted_iota(jnp.int32, x.shape, 1)
    s = 1
    while s < n:
        if reverse:
            x = jnp.where(lane < n - s, jnp.maximum(x, pltpu.roll(x, n - s, 1)), x)
        else:
            x = jnp.where(lane >= s, jnp.maximum(x, pltpu.roll(x, s, 1)), x)
        s *= 2
    return x


def _split3(v):
    hi = v.astype(bf16)
    r1 = v - hi.astype(f32)
    mid = r1.astype(bf16)
    lo = (r1 - mid.astype(f32)).astype(bf16)
    return hi, mid, lo


def _gate_prep_kernel(gt_ref, cols_ref, arow_ref, *, L, nchunk):
    r = lax.broadcasted_iota(jnp.int32, (L, L), 0)
    c = lax.broadcasted_iota(jnp.int32, (L, L), 1)
    tri_l = (c <= r).astype(bf16)
    tri_u = (c >= r).astype(bf16)
    S = V7X_SUBLANES
    fwd = (lax.broadcasted_iota(jnp.int32, (S, L), 0) & 1) == 0
    fill = jnp.zeros((V7X_LANES - GATE_QUANTS * GATE_PIECES * S, L), f32)
    for ci in range(nchunk):
        sl = slice(ci * L, (ci + 1) * L)
        ig = gt_ref[:S, sl]
        lf = _log_sigmoid(gt_ref[S:, sl])
        pieces = jnp.concatenate([p.astype(f32) for p in _split3(lf)] + [jnp.zeros((S, L), f32)],
                                 axis=0).astype(bf16)
        pre = jnp.dot(pieces, tri_u, preferred_element_type=f32)
        suf = jnp.dot(pieces, tri_l, preferred_element_type=f32)
        pre = pre[:S] + pre[S:2 * S] + pre[2 * S:3 * S]
        suf = suf[:S] + suf[S:2 * S] + suf[2 * S:3 * S]
        b_row = jnp.where(fwd, pre, suf)
        a_row = ig - b_row
        cmax = jnp.where(fwd, _scan_max_lanes(a_row, False), _scan_max_lanes(a_row, True))
        arow_ref[:, sl] = a_row
        parts = [p.astype(f32) for quant in (b_row, a_row, cmax) for p in _split3(quant)]
        packed = jnp.concatenate(parts + [fill], axis=0)
        cols_ref[sl, :] = packed.T.astype(bf16)


def _gate_prep(gt, L):
    B, _, T = gt.shape
    nchunk = min(4, T // L)
    tb = nchunk * L
    cols, arow = pl.pallas_call(
        functools.partial(_gate_prep_kernel, L=L, nchunk=nchunk),
        grid=(B, T // tb),
        in_specs=[pl.BlockSpec((None, N_GATES, tb), lambda b, i: (b, 0, i))],
        out_specs=[
            pl.BlockSpec((None, tb, V7X_LANES), lambda b, i: (b, i, 0)),
            pl.BlockSpec((None, V7X_SUBLANES, tb), lambda b, i: (b, 0, i)),
        ],
        out_shape=[
            jax.ShapeDtypeStruct((B, T, V7X_LANES), bf16),
            jax.ShapeDtypeStruct((B, V7X_SUBLANES, T), f32),
        ],
        compiler_params=pltpu.CompilerParams(dimension_semantics=("parallel", "parallel")),
        name="gate_prep",
    )(gt)
    return cols, arow.reshape(B, N_HEADS_A, 2, T)


def _mlstm_kernel(qx_ref, kx_ref, vx_ref, cx_ref, rx_ref, qc_ref, kc_ref, vc_ref, cc_ref, rc_ref,
                  hx_ref, hc_ref, cn_ref, *, L, nx, nc):
    Dh, LN, HP = HEAD_DIM_A, V7X_LANES, MLSTM_HEADS_PER_STEP
    head0 = pl.program_id(1) * HP
    chains = [(hh, d) for hh in range(HP) for d in range(2)]
    r_i = lax.broadcasted_iota(jnp.int32, (L, L), 0)
    c_i = lax.broadcasted_iota(jnp.int32, (L, L), 1)
    masks = (c_i <= r_i, c_i >= r_i)

    sr = lax.broadcasted_iota(jnp.int32, (LN, GATE_QUANTS * LN), 0)
    sc = lax.broadcasted_iota(jnp.int32, (LN, GATE_QUANTS * LN), 1)
    span = GATE_PIECES * GATE_SLOTS
    in_block = None
    for qi in range(GATE_QUANTS):
        blk = (sr >= qi * span) & (sr < (qi + 1) * span) & (sc >= qi * LN) & (sc < (qi + 1) * LN)
        in_block = blk if in_block is None else in_block | blk
    slot = sr & (GATE_SLOTS - 1)
    sels = {(hh, d): (in_block & (slot == 2 * (head0 + hh) + d)).astype(bf16) for hh, d in chains}
    ones = jnp.ones((L, LN), bf16)

    def tile(v, width):
        return jnp.concatenate([v] * (width // LN), axis=1)

    def step(refs, r0s, ms):
        q_ref, k_ref, v_ref, col_ref, row_ref = refs
        st = []
        for ci, (hh, d) in enumerate(chains):
            rows = pl.ds(r0s[ci], L)
            hsl = slice(hh * Dh, (hh + 1) * Dh)
            q = q_ref[rows, hsl]
            k = k_ref[rows, hsl]
            vo = jnp.concatenate([v_ref[rows, hsl], ones], axis=1)
            rep = jnp.dot(col_ref[rows, :], sels[hh, d], preferred_element_type=f32)
            qk = lax.dot_general(q, k, (((1,), (1,)), ((), ())), preferred_element_type=f32)
            qcn = jnp.dot(q, cn_ref[ci].astype(bf16), preferred_element_type=f32)
            st.append(dict(k=k, vo=vo, rep=rep, qk=qk, qcn=qcn, a_row=row_ref[hh, d:d + 1, rows]))
        new_ms = []
        for ci, (hh, d) in enumerate(chains):
            c, m_prev = st[ci], ms[ci]
            b_rep, a_rep = c["rep"][:, :LN], c["rep"][:, LN:2 * LN]
            g_rep = jnp.maximum(c["rep"][:, 2 * LN:], m_prev)
            p = jnp.exp(jnp.where(masks[d], c["a_row"] - tile(g_rep, L), NEG_BIG))
            c["s"] = (c["qk"] * p).astype(bf16)
            b_end = b_rep[L - 1:L, :] if d == 0 else b_rep[0:1, :]
            w = b_end + a_rep
            m_new = jnp.maximum(b_end + m_prev, jnp.max(w, axis=0, keepdims=True))
            c["decay"] = jnp.exp(b_end + m_prev - m_new)
            c["kw"] = (c["k"].astype(f32) * tile(jnp.exp(w - m_new), Dh)).astype(bf16)
            c["inter"] = jnp.exp(m_prev - g_rep)
            c["floor"] = jnp.exp(-(b_rep + g_rep))
            new_ms.append(m_new)
        for c in st:
            c["sv"] = jnp.dot(c["s"], c["vo"], preferred_element_type=f32)
            c["upd"] = lax.dot_general(c["kw"], c["vo"], (((0,), (0,)), ((), ())), preferred_element_type=f32)
        hs = []
        for ci, c in enumerate(st):
            sv, qcn, inter = c["sv"], c["qcn"], c["inter"]
            den = sv[:, Dh:] + inter * qcn[:, Dh:]
            rcp = 1.0 / jnp.maximum(jnp.abs(den), c["floor"])
            hs.append((sv[:, :Dh] + tile(inter, Dh) * qcn[:, :Dh]) * tile(rcp, Dh))
            cn_ref[ci] = tile(c["decay"], Dh + LN) * cn_ref[ci] + c["upd"]
        return hs, new_ms

    ctx = (qc_ref, kc_ref, vc_ref, cc_ref, rc_ref)
    lat = (qx_ref, kx_ref, vx_ref, cx_ref, rx_ref)

    cn_ref[...] = jnp.zeros_like(cn_ref)
    ms = [jnp.zeros((1, LN), f32) for _ in chains]
    written = set()
    for j in range(nc):
        cjs = [j if d == 0 else nc - 1 - j for _, d in chains]
        hs, ms = step(ctx, [cj * L for cj in cjs], ms)
        for (hh, d), cj, h in zip(chains, cjs, hs):
            dst = (slice(cj * L, (cj + 1) * L), slice(hh * Dh, (hh + 1) * Dh))
            if (cj, hh) in written:
                hc_ref[dst] += h
            else:
                hc_ref[dst] = h
                written.add((cj, hh))

    def make_body(accumulate):
        def body(i, ms):
            r0s = [pl.multiple_of((i if d == 0 else nx - 1 - i) * L, L) for _, d in chains]
            hs, ms = step(lat, r0s, list(ms))
            for (hh, d), r0, h in zip(chains, r0s, hs):
                dst = (pl.ds(r0, L), slice(hh * Dh, (hh + 1) * Dh))
                if accumulate:
                    hx_ref[dst] += h
                else:
                    hx_ref[dst] = h
            return tuple(ms)
        return body

    ms = lax.fori_loop(0, nx // 2, make_body(False), tuple(ms))
    lax.fori_loop(nx // 2, nx, make_body(True), ms)


def _mlstm(qx, kx, vx, colsx, rowsx, qc, kc, vc, colsc, rowsc, L):
    B, T, _ = qx.shape
    Tc = qc.shape[1]
    H, Dh, HP = N_HEADS_A, HEAD_DIM_A, MLSTM_HEADS_PER_STEP
    assert T % (2 * L) == 0 and Tc % L == 0 and H % HP == 0

    def seq_spec(t):
        return pl.BlockSpec((None, t, HP * Dh), lambda b, h: (b, 0, h))

    def col_spec(t):
        return pl.BlockSpec((None, t, V7X_LANES), lambda b, h: (b, 0, 0))

    def row_spec(t):
        return pl.BlockSpec((None, HP, 2, t), lambda b, h: (b, h, 0, 0))

    est = (2 * HP * (3 * (T + Tc) * Dh * 2 + 8 * (T + Tc) * 4 + (T + Tc) * Dh * 4)
           + 2 * (T + Tc) * V7X_LANES * 2 + 2 * HP * Dh * (Dh + V7X_LANES) * 4 + 16 * L * L * 4)
    return pl.pallas_call(
        functools.partial(_mlstm_kernel, L=L, nx=T // L, nc=Tc // L),
        grid=(B, H // HP),
        in_specs=[seq_spec(T), seq_spec(T), seq_spec(T), col_spec(T), row_spec(T),
                  seq_spec(Tc), seq_spec(Tc), seq_spec(Tc), col_spec(Tc), row_spec(Tc)],
        out_specs=[seq_spec(T), seq_spec(Tc)],
        out_shape=[jax.ShapeDtypeStruct((B, T, D_A), f32), jax.ShapeDtypeStruct((B, Tc, D_A), f32)],
        scratch_shapes=[pltpu.VMEM((2 * HP, Dh, Dh + V7X_LANES), f32)],
        compiler_params=pltpu.CompilerParams(dimension_semantics=("parallel", "parallel"),
                                             vmem_limit_bytes=_vmem_limit(est)),
        name="mlstm",
    )(qx, kx, vx, colsx, rowsx, qc, kc, vc, colsc, rowsc)


def _fourier_w_kernel(cs_ref, fw_ref, o_ref, *, scale):
    o_ref[...] = (jnp.dot(cs_ref[...], fw_ref[...], preferred_element_type=f32,
                          precision=lax.Precision.HIGHEST) * scale).astype(bf16)


def _fourier_weights(fw, T):
    n = GROUP_B
    kk = np.outer(np.arange(n), np.arange(n)) % n
    ang = 2.0 * np.pi * kk / n
    cs = jnp.asarray(np.concatenate([np.cos(ang), np.sin(ang)], axis=0), f32)
    return pl.pallas_call(
        functools.partial(_fourier_w_kernel, scale=float(1.0 / np.sqrt(T * n))),
        grid=(N_GROUPS_B,),
        in_specs=[pl.BlockSpec((2 * n, n), lambda g: (0, 0)),
                  pl.BlockSpec((None, n, n), lambda g: (g, 0, 0))],
        out_specs=pl.BlockSpec((None, 2 * n, n), lambda g: (g, 0, 0)),
        out_shape=jax.ShapeDtypeStruct((N_GROUPS_B, 2 * n, n), bf16),
        compiler_params=pltpu.CompilerParams(dimension_semantics=("arbitrary",)),
        name="fourier_weights",
    )(cs, fw)


def _fft_tables():
    n1, n2 = FFT_N1, FFT_N2
    n = n1 * n2
    t1 = np.arange(n1)
    k1 = np.arange(n1)
    t2 = np.arange(n2)
    idx = (k1[None, :, None] * (n2 * t1[None, None, :] + t2[:, None, None])) % n
    ang = 2.0 * np.pi * idx / n
    m1 = np.concatenate([np.cos(ang), -np.sin(ang)], axis=1)
    k2 = np.arange(n2)
    ph = 2.0 * np.pi * (np.outer(k2, t2) % n2) / n2
    c, s = np.cos(ph), np.sin(ph)
    m2 = np.block([[c, s], [-s, c]])
    return jnp.asarray(m1, f32).astype(bf16), jnp.asarray(m2, f32).astype(bf16)


def _fourier_kernel(u_ref, z_ref, m1_ref, m2_ref, w_ref, o_ref, up_ref, yp_ref, xp_ref, x2_ref, pbuf_ref):
    n1, n2, P = FFT_N1, FFT_N2, FFT_PITCH
    nl = GROUP_B // V7X_LANES

    def fill(t1, carry):
        src = pl.multiple_of(t1 * n2, n2)
        dst = pl.multiple_of(t1 * P, V7X_SUBLANES)
        blk = u_ref[pl.ds(src, n2), :].astype(f32)
        for s in range(nl):
            up_ref[s, pl.ds(dst, n2), :] = blk[:, s * V7X_LANES:(s + 1) * V7X_LANES]
        return carry

    lax.fori_loop(0, n1, fill, 0, unroll=4)

    G = FFT_UNROLL

    def stage1(grp, carry):
        t2s = [grp * G + j for j in range(G)]
        rhs = [jnp.concatenate([up_ref[s, pl.ds(t2, n1, stride=P), :] for s in range(nl)],
                               axis=1).astype(bf16) for t2 in t2s]
        ys = [jnp.dot(m1_ref[t2], r, preferred_element_type=f32) for t2, r in zip(t2s, rhs)]
        for t2, y in zip(t2s, ys):
            dst = pl.multiple_of(t2 * P, V7X_SUBLANES)
            for ri in range(2):
                for s in range(nl):
                    yp_ref[ri * nl + s, pl.ds(dst, n1), :] = y[ri * n1:(ri + 1) * n1,
                                                               s * V7X_LANES:(s + 1) * V7X_LANES]
        return carry

    lax.fori_loop(0, n2 // G, stage1, 0)

    m2 = m2_ref[...]

    def stage2(grp, carry):
        k1s = [grp * G + j for j in range(G)]
        rhs = []
        for k1 in k1s:
            parts = [jnp.concatenate([yp_ref[ri * nl + s, pl.ds(k1, n2, stride=P), :] for s in range(nl)], axis=1)
                     for ri in range(2)]
            rhs.append(jnp.concatenate(parts, axis=0).astype(bf16))
        xs = [jnp.dot(m2, r, preferred_element_type=f32) for r in rhs]
        for k1, x in zip(k1s, xs):
            for ri in range(2):
                for s in range(nl):
                    xp_ref[ri * nl + s, pl.ds(k1, n2, stride=P), :] = x[ri * n2:(ri + 1) * n2,
                                                                        s * V7X_LANES:(s + 1) * V7X_LANES]
        return carry

    lax.fori_loop(0, n1 // G, stage2, 0)

    def unpitch(k2, carry):
        src = pl.multiple_of(k2 * P, V7X_SUBLANES)
        dst = pl.multiple_of(k2 * n1, n1)
        for ri in range(2):
            for s in range(nl):
                c0 = (ri * nl + s) * V7X_LANES
                x2_ref[pl.ds(dst, n1), c0:c0 + V7X_LANES] = xp_ref[ri * nl + s, pl.ds(src, n1), :].astype(bf16)
        return carry

    lax.fori_loop(0, n2, unpitch, 0, unroll=4)

    tr = pbuf_ref.shape[1]

    def matmul(t):
        return jnp.dot(x2_ref[t * tr:(t + 1) * tr, :], w_ref[...], preferred_element_type=f32)

    def gate_store(t, y):
        sl = slice(t * tr, (t + 1) * tr)
        o_ref[sl, :] = (y * _silu(z_ref[sl, :].astype(f32))).astype(bf16)

    _pipelined_matmuls((n1 * n2) // tr, matmul, gate_store, pbuf_ref)


def _fourier_latent(ub, zb, wcs):
    B, T, _ = ub.shape
    assert T == FFT_N1 * FFT_N2
    m1, m2 = _fft_tables()
    G, n = N_GROUPS_B, GROUP_B
    nl = n // V7X_LANES
    rows_p = FFT_N1 * FFT_PITCH
    blk = pl.BlockSpec((None, T, n), lambda b, g: (b, 0, g))
    est = (3 * 2 * T * n * 2 + 5 * nl * rows_p * V7X_LANES * 4 + T * 2 * n * 2 + 2 * m1.size * 2 + 8 * 512 * n * 4)
    return pl.pallas_call(
        _fourier_kernel,
        grid=(B, G),
        in_specs=[blk, blk,
                  pl.BlockSpec(m1.shape, lambda b, g: (0, 0, 0)),
                  pl.BlockSpec(m2.shape, lambda b, g: (0, 0)),
                  pl.BlockSpec((None, 2 * n, n), lambda b, g: (g, 0, 0))],
        out_specs=blk,
        out_shape=jax.ShapeDtypeStruct((B, T, D_B), bf16),
        scratch_shapes=[pltpu.VMEM((nl, rows_p, V7X_LANES), f32),
                        pltpu.VMEM((2 * nl, rows_p, V7X_LANES), f32),
                        pltpu.VMEM((2 * nl, rows_p, V7X_LANES), f32),
                        pltpu.VMEM((T, 2 * n), bf16),
                        pltpu.VMEM((3, 512, n), f32)],
        compiler_params=pltpu.CompilerParams(dimension_semantics=("parallel", "parallel"),
                                             vmem_limit_bytes=_vmem_limit(est)),
        name="fourier_latent",
    )(ub, zb, m1, m2, wcs)


def _fourier_ctx_kernel(u_ref, z_ref, cs_ref, w_ref, o_ref):
    u = u_ref[...]
    x = jnp.dot(cs_ref[...], u, preferred_element_type=f32).astype(bf16)
    t = u.shape[0]
    w = w_ref[...]
    n = GROUP_B
    y = (jnp.dot(x[:t], w[:n], preferred_element_type=f32) + jnp.dot(x[t:], w[n:], preferred_element_type=f32))
    o_ref[...] = (y * _silu(z_ref[...].astype(f32))).astype(bf16)


def _fourier_ctx(ub, zb, wcs):
    B, T, _ = ub.shape
    ang = 2.0 * np.pi * (np.outer(np.arange(T), np.arange(T)) % T) / T
    cs = jnp.asarray(np.concatenate([np.cos(ang), -np.sin(ang)], axis=0), f32).astype(bf16)
    n = GROUP_B
    blk = pl.BlockSpec((None, T, n), lambda b, g: (b, 0, g))
    return pl.pallas_call(
        _fourier_ctx_kernel,
        grid=(B, N_GROUPS_B),
        in_specs=[blk, blk, pl.BlockSpec(cs.shape, lambda b, g: (0, 0)),
                  pl.BlockSpec((None, 2 * n, n), lambda b, g: (g, 0, 0))],
        out_specs=blk,
        out_shape=jax.ShapeDtypeStruct((B, T, D_B), bf16),
        compiler_params=pltpu.CompilerParams(dimension_semantics=("parallel", "parallel")),
        name="fourier_ctx",
    )(ub, zb, cs, wcs)


def _outproj_even_kernel(x_ref, h_ref, o_ref, za_ref, yb_ref, gate_ref, hg_ref, w_ref, out_ref):
    parts = []
    for hd in range(N_HEADS_A):
        sl = slice(hd * HEAD_DIM_A, (hd + 1) * HEAD_DIM_A)
        hh = _sigmoid(o_ref[:, sl].astype(f32)) * h_ref[:, sl]
        hh = hh * lax.rsqrt(jnp.mean(hh * hh, axis=-1, keepdims=True) + EPS)
        parts.append((hh * hg_ref[:, sl] * _silu(za_ref[:, sl].astype(f32))).astype(bf16))
    ya = jnp.concatenate(parts, axis=1)
    acc = jnp.dot(ya, w_ref[:D_A, :], preferred_element_type=f32)
    acc = acc + jnp.dot(yb_ref[...], w_ref[D_A:, :], preferred_element_type=f32)
    out_ref[...] = x_ref[...] + gate_ref[...] * acc


def _outproj_even(x, h, o, za, yb, gate, head_g, wout, tm):
    B, T, D = x.shape
    row = lambda n: pl.BlockSpec((None, tm, n), lambda b, i: (b, i, 0))
    est = 2 * wout.size * 2 + 2 * tm * (2 * D * 4 + D_A * 4 + 3 * D_A * 2) + 8 * tm * D * 4
    return pl.pallas_call(
        _outproj_even_kernel,
        grid=(B, T // tm),
        in_specs=[row(D), row(D_A), row(D_A), row(D_A), row(D_B),
                  pl.BlockSpec((None, 1, D), lambda b, i: (b, 0, 0)),
                  pl.BlockSpec((1, D_A), lambda b, i: (0, 0)),
                  pl.BlockSpec(wout.shape, lambda b, i: (0, 0))],
        out_specs=row(D),
        out_shape=jax.ShapeDtypeStruct((B, T, D), f32),
        compiler_params=pltpu.CompilerParams(dimension_semantics=("parallel", "parallel"),
                                             vmem_limit_bytes=_vmem_limit(est)),
        name="outproj_even",
    )(x, h, o, za, yb, gate, head_g.reshape(1, D_A), wout)


def _inproj_odd_kernel(x_ref, sh_ref, sc_ref, g_ref, w_ref, u_ref, z_ref):
    amp = g_ref[...] * (1.0 + sc_ref[...])
    hx = _normed(x_ref[...], amp, sh_ref[...]).astype(bf16)
    cn = 512
    for j in range(D_INNER // cn):
        u_ref[:, j * cn:(j + 1) * cn] = jnp.dot(hx, w_ref[:, j * cn:(j + 1) * cn],
                                                preferred_element_type=f32).astype(bf16)
        z_ref[:, j * cn:(j + 1) * cn] = jnp.dot(hx, w_ref[:, D_INNER + j * cn:D_INNER + (j + 1) * cn],
                                                preferred_element_type=f32).astype(bf16)


def _inproj_odd(x, shift, scale, norm_g, w, tm):
    B, T, D = x.shape
    row = lambda n: pl.BlockSpec((None, tm, n), lambda b, i: (b, i, 0))
    est = 2 * w.size * 2 + 2 * tm * D * 4 + 4 * tm * D_INNER * 2 + 6 * tm * 512 * 4
    return pl.pallas_call(
        _inproj_odd_kernel,
        grid=(B, T // tm),
        in_specs=[row(D),
                  pl.BlockSpec((None, 1, D), lambda b, i: (b, 0, 0)),
                  pl.BlockSpec((None, 1, D), lambda b, i: (b, 0, 0)),
                  pl.BlockSpec((1, D), lambda b, i: (0, 0)),
                  pl.BlockSpec(w.shape, lambda b, i: (0, 0))],
        out_specs=[row(D_INNER), row(D_INNER)],
        out_shape=[jax.ShapeDtypeStruct((B, T, D_INNER), bf16)] * 2,
        compiler_params=pltpu.CompilerParams(dimension_semantics=("parallel", "parallel"),
                                             vmem_limit_bytes=_vmem_limit(est)),
        name="inproj_odd",
    )(x, shift, scale, norm_g.reshape(1, D), w)


POOL_UNROLL = 8


def _pool_tables():
    w_idx = np.arange(GRID_W)
    band = np.zeros((N_GROUPS_C, GRID_W, GRID_W), np.float32)
    inv_w = np.zeros((N_GROUPS_C, GRID_W, V7X_LANES), np.float32)
    for g, win in enumerate(POOL_WINDOWS):
        lo = np.clip(w_idx - win // 2, 0, GRID_W)
        hi = np.clip(w_idx + win - win // 2, 0, GRID_W)
        band[g] = (w_idx[None, :] >= lo[:, None]) & (w_idx[None, :] < hi[:, None])
        inv_w[g] = (1.0 / (hi - lo))[:, None]
    return jnp.asarray(band, bf16), jnp.asarray(inv_w, f32)


def _pool_kernel(u_ref, z_ref, band_ref, invw_ref, pw_ref, sc_ref, o_ref, ps_ref, yp_ref, pbuf_ref, *, rows):
    g = pl.program_id(1)
    W = GRID_W
    band = band_ref[...]
    lo_off = hi_off = 0
    for gi, win in enumerate(POOL_WINDOWS):
        lo_off = jnp.where(g == gi, win // 2, lo_off)
        hi_off = jnp.where(g == gi, win - win // 2, hi_off)

    ps_ref[0:W, :] = jnp.zeros((W, GROUP_C), f32)

    def width_sum(grp, carry):
        srcs = [pl.multiple_of((grp * POOL_UNROLL + j) * W, W) for j in range(POOL_UNROLL)]
        sums = [jnp.dot(band, u_ref[pl.ds(src, W), :], preferred_element_type=f32) for src in srcs]
        acc = ps_ref[pl.ds(srcs[0], W), :]
        for src, s in zip(srcs, sums):
            acc = acc + s
            ps_ref[pl.ds(src + W, W), :] = acc
        return carry

    lax.fori_loop(0, rows // POOL_UNROLL, width_sum, 0)

    inv_w = invw_ref[...]

    def rows_pass(r, carry):
        lo = jnp.maximum(r - lo_off, 0)
        hi = jnp.minimum(r + hi_off, rows)
        acc = (ps_ref[pl.ds(pl.multiple_of(hi * W, W), W), :]
               - ps_ref[pl.ds(pl.multiple_of(lo * W, W), W), :])
        inv = inv_w / (hi - lo).astype(f32)
        inv = jnp.concatenate([inv] * (GROUP_C // V7X_LANES), axis=1)
        src = pl.multiple_of(r * W, W)
        ug = u_ref[pl.ds(src, W), :].astype(f32)
        yp_ref[pl.ds(src, W), :] = (acc * inv - ug).astype(bf16)
        return carry

    lax.fori_loop(0, rows, rows_pass, 0, unroll=2)

    tr = pbuf_ref.shape[1]
    pws = (pw_ref[...] * sc_ref[...]).astype(bf16)

    def matmul(t):
        return jnp.dot(yp_ref[t * tr:(t + 1) * tr, :], pws, preferred_element_type=f32)

    def gate_store(t, y):
        sl = slice(t * tr, (t + 1) * tr)
        o_ref[sl, :] = y.astype(bf16) * _silu(z_ref[sl, :])

    _pipelined_matmuls((rows * W) // tr, matmul, gate_store, pbuf_ref)


def _pool_mix(u, z, pool_w, scale):
    B, T, _ = u.shape
    rows = T // GRID_W
    band, inv_w = _pool_tables()
    n = GROUP_C
    blk = pl.BlockSpec((None, T, n), lambda b, g: (b, 0, g))
    est = 3 * 2 * T * n * 2 + (rows + 1) * GRID_W * n * 4 + T * n * 2 + 2 * n * n * 4 + 8 * 512 * n * 4
    return pl.pallas_call(
        functools.partial(_pool_kernel, rows=rows),
        grid=(B, N_GROUPS_C),
        in_specs=[blk, blk,
                  pl.BlockSpec((None, GRID_W, GRID_W), lambda b, g: (g, 0, 0)),
                  pl.BlockSpec((None, GRID_W, V7X_LANES), lambda b, g: (g, 0, 0)),
                  pl.BlockSpec((None, n, n), lambda b, g: (g, 0, 0)),
                  pl.BlockSpec((None, 1, n), lambda b, g: (g, 0, 0))],
        out_specs=blk,
        out_shape=jax.ShapeDtypeStruct((B, T, D_INNER), bf16),
        scratch_shapes=[pltpu.VMEM(((rows + 1) * GRID_W, n), f32),
                        pltpu.VMEM((T, n), bf16),
                        pltpu.VMEM((3, 512, n), f32)],
        compiler_params=pltpu.CompilerParams(dimension_semantics=("parallel", "arbitrary"),
                                             vmem_limit_bytes=_vmem_limit(est)),
        name="pool_mix",
    )(u, z, band, inv_w, pool_w, scale.reshape(N_GROUPS_C, 1, n))


def _outproj_odd_kernel(x_ref, y_ref, gate_ref, fg_ref, w_ref, out_ref):
    acc = jnp.dot(y_ref[...], w_ref[...], preferred_element_type=f32)
    x = x_ref[...] + gate_ref[...] * acc
    out_ref[...] = x * lax.rsqrt(jnp.mean(x * x, axis=-1, keepdims=True) + EPS) * fg_ref[...]


def _outproj_odd(x, y, gate, final_g, wout, tm):
    B, T, D = x.shape
    row = lambda n: pl.BlockSpec((None, tm, n), lambda b, i: (b, i, 0))
    est = 2 * wout.size * 2 + 2 * tm * (2 * D * 4 + D_INNER * 2) + 6 * tm * D * 4
    return pl.pallas_call(
        _outproj_odd_kernel,
        grid=(B, T // tm),
        in_specs=[row(D), row(D_INNER),
                  pl.BlockSpec((None, 1, D), lambda b, i: (b, 0, 0)),
                  pl.BlockSpec((1, D), lambda b, i: (0, 0)),
                  pl.BlockSpec(wout.shape, lambda b, i: (0, 0))],
        out_specs=row(D),
        out_shape=jax.ShapeDtypeStruct((B, T, D), f32),
        compiler_params=pltpu.CompilerParams(dimension_semantics=("parallel", "parallel"),
                                             vmem_limit_bytes=_vmem_limit(est)),
        name="outproj_odd",
    )(x, y, gate, final_g.reshape(1, D), wout)


def kernel(x, c, ctx, c_ctx, ada_w, ada_b, norm_g, win_even, gate_b_even, conv_qk_even, head_norm_even,
           fourier_w_even, wout_even, win_odd, pool_w_odd, pool_scale_odd, wout_odd, final_g):
    B, T, D = x.shape
    Tc = ctx.shape[1]
    H = N_HEADS_A
    L = MLSTM_CHUNK

    nrow = -(-(B + 1) // V7X_SUBLANES) * V7X_SUBLANES
    rows_in = jnp.zeros((nrow, D), f32).at[:B].set(c).at[B].set(c_ctx)
    mod = _modulation(rows_in, ada_w, ada_b)

    def mod_parts(l, r0, r1, n):
        m = mod[l, r0:r1]
        parts = [jnp.broadcast_to(m[:, None, i * D:(i + 1) * D], (n, 1, D)) for i in range(3)]
        return parts

    we = win_even[0]
    w_main = we[:, :W_MAIN_EVEN].astype(bf16)
    gcols = we[:, W_MAIN_EVEN:].reshape(D, 4, H)
    gbias = gate_b_even[0].reshape(4, H)
    ig_w = gcols[:, 0::2, :].transpose(0, 2, 1).reshape(D, GATE_SLOTS)
    fg_w = gcols[:, 1::2, :].transpose(0, 2, 1).reshape(D, GATE_SLOTS)
    ig_b = gbias[0::2, :].T.reshape(GATE_SLOTS)
    fg_b = gbias[1::2, :].T.reshape(GATE_SLOTS)
    wgt =jnp.concatenate([ig_w, fg_w], axis=1).T.astype(bf16)
    gbt = jnp.concatenate([ig_b, fg_b]).reshape(N_GATES, 1)
    conv_w = conv_qk_even[0]

    shift_x, scale_x, gate_x = mod_parts(0, 0, B, B)
    shift_c, scale_c, gate_c = mod_parts(0, B, B + 1, B)

    qx, kx, vx, ox, zax, ubx, zbx, gtx = _inproj_even(
        x, shift_x, scale_x, norm_g[0], w_main, wgt, gbt, conv_w, tm=512)
    qc, kc, vc, oc, zac, ubc, zbc, gtc = _inproj_even(
        ctx, shift_c, scale_c, norm_g[0], w_main, wgt, gbt, conv_w, tm=Tc)

    colsx, rowsx = _gate_prep(gtx, L)
    colsc, rowsc = _gate_prep(gtc, L)
    h_x, h_c = _mlstm(qx, kx, vx, colsx, rowsx, qc, kc, vc, colsc, rowsc, L)

    wcs = _fourier_weights(fourier_w_even[0], T)
    yb_x = _fourier_latent(ubx, zbx, wcs)
    wout_e = wout_even[0].astype(bf16)
    x1 = _outproj_even(x, h_x, ox, zax, yb_x, gate_x, head_norm_even[0], wout_e, tm=512)

    wcs_c = _fourier_weights(fourier_w_even[0], Tc)
    yb_c = _fourier_ctx(ubc, zbc, wcs_c)
    ctx1 = _outproj_even(ctx, h_c, oc, zac, yb_c, gate_c, head_norm_even[0], wout_e, tm=Tc)
    del ctx1

    shift_x, scale_x, gate_x = mod_parts(1, 0, B, B)
    u1, z1 = _inproj_odd(x1, shift_x, scale_x, norm_g[1], win_odd[0].astype(bf16), tm=512)
    y1 = _pool_mix(u1, z1, pool_w_odd[0], pool_scale_odd[0])
    return _outproj_odd(x1, y1, gate_x, final_g, wout_odd[0].astype(bf16), tm=512)
```

```python
import functools

import numpy as np
import jax
import jax.numpy as jnp
from jax import lax
from jax.experimental import pallas as pl
from jax.experimental.pallas import tpu as pltpu

D_MODEL = 1024
DEPTH = 2
CTX_LEN = 256
GRID_W = 64
D_INNER = 2 * D_MODEL
D_A = D_INNER // 2
D_B = D_INNER - D_A
N_HEADS_A = 4
HEAD_DIM_A = D_A // N_HEADS_A
N_GROUPS_B = 4
GROUP_B = D_B // N_GROUPS_B
N_GROUPS_C = 4
GROUP_C = D_INNER // N_GROUPS_C
POOL_WINDOWS = (2, 4, 8, 16)
CONV_W = 3
N_GATES = 4 * N_HEADS_A
W_MAIN_EVEN = 5 * D_A + 2 * D_B
EPS = 1e-6

f32 = jnp.float32
bf16 = jnp.bfloat16

V7X_VMEM_BYTES = 64 * 1024 * 1024
V7X_LANES = 128
V7X_SUBLANES = 8

MLSTM_CHUNK = 256
MLSTM_HEADS_PER_STEP = 2
FFT_N1 = 64
FFT_N2 = 64
FFT_PITCH = 72
FFT_UNROLL = 8
NEG_BIG = -1e30


def _vmem_limit(nbytes):
    return int(min(max(nbytes * 5 // 4 + (4 << 20), 16 << 20), V7X_VMEM_BYTES - (6 << 20)))


def _sigmoid(v):
    return 0.5 * jnp.tanh(0.5 * v) + 0.5


def _silu(v):
    return v * _sigmoid(v)


def _pipelined_matmuls(n, matmul, epilogue, pbuf_ref):
    zero = jnp.minimum(pl.program_id(0), 0)
    nbuf = pbuf_ref.shape[0]
    pbuf_ref[zero] = matmul(0)
    for t in range(n):
        if t + 1 < n:
            pbuf_ref[zero + (t + 1) % nbuf] = matmul(t + 1)
        epilogue(t, pbuf_ref[zero + t % nbuf])


def _log_sigmoid(v):
    return jnp.minimum(v, 0.0) - jnp.log1p(jnp.exp(-jnp.abs(v)))


def _mod_kernel(r_ref, w_ref, b_ref, o_ref):
    s = _silu(r_ref[...])
    o_ref[...] = jnp.dot(s, w_ref[...], preferred_element_type=f32,
                         precision=lax.Precision.HIGHEST) + b_ref[...]


def _modulation(rows, ada_w, ada_b):
    nrow = rows.shape[0]
    tn = 1024
    return pl.pallas_call(
        _mod_kernel,
        grid=(DEPTH, 3 * D_MODEL // tn),
        in_specs=[
            pl.BlockSpec((nrow, D_MODEL), lambda l, j: (0, 0)),
            pl.BlockSpec((None, D_MODEL, tn), lambda l, j: (l, 0, j)),
            pl.BlockSpec((None, 1, tn), lambda l, j: (l, 0, j)),
        ],
        out_specs=pl.BlockSpec((None, nrow, tn), lambda l, j: (l, 0, j)),
        out_shape=jax.ShapeDtypeStruct((DEPTH, nrow, 3 * D_MODEL), f32),
        compiler_params=pltpu.CompilerParams(dimension_semantics=("arbitrary", "arbitrary")),
        name="modulation",
    )(rows, ada_w, ada_b.reshape(DEPTH, 1, 3 * D_MODEL))


def _normed(x, amp, shift):
    ms = jnp.mean(x * x, axis=-1, keepdims=True)
    return (x * lax.rsqrt(ms + EPS)) * amp + shift


def _inproj_even_kernel(x_ref, xp_ref, xn_ref, sh_ref, sc_ref, g_ref, w_ref, wgt_ref, gbt_ref,
                        cw_ref, q_ref, k_ref, v_ref, o_ref, za_ref, ub_ref, zb_ref, gt_ref, pbuf_ref, *, tm, nt):
    i = pl.program_id(1)
    amp = g_ref[...] * (1.0 + sc_ref[...])
    shift = sh_ref[...]
    hx = _normed(x_ref[...], amp, shift).astype(bf16)
    halo = jnp.concatenate([xp_ref[...], xn_ref[...]], axis=0)
    hh = _normed(halo, amp, shift).astype(bf16)
    has_prev = (i > 0).astype(f32)
    has_next = (i < nt - 1).astype(f32)
    row = lax.broadcasted_iota(jnp.int32, (tm, 1), 0)
    cn = 512
    ph = jnp.dot(hh, w_ref[:, :2 * D_A], preferred_element_type=f32)
    prev = ph[V7X_SUBLANES - 1:V7X_SUBLANES, :] * has_prev
    nxt = ph[V7X_SUBLANES:V7X_SUBLANES + 1, :] * has_next

    def conv_store(j, p):
        cols = slice(j * cn, (j + 1) * cn)
        up = jnp.where(row == 0, prev[:, cols], pltpu.roll(p, 1, 0))
        dn = jnp.where(row == tm - 1, nxt[:, cols], pltpu.roll(p, tm - 1, 0))
        cw = cw_ref[:, cols]
        y = _silu(cw[0:1, :] * up + cw[1:2, :] * p + cw[2:3, :] * dn)
        if j < D_A // cn:
            q_ref[:, cols] = (y * (HEAD_DIM_A ** -0.5)).astype(bf16)
        else:
            jj = j - D_A // cn
            k_ref[:, jj * cn:(jj + 1) * cn] = y.astype(bf16)

    def plain_store(ref, jj):
        def store(p):
            ref[:, jj * cn:(jj + 1) * cn] = p.astype(bf16)
        return store

    tasks = [(j * cn, functools.partial(conv_store, j)) for j in range(2 * D_A // cn)]
    for idx, ref in enumerate((v_ref, o_ref, za_ref, ub_ref, zb_ref)):
        for jj in range(D_A // cn):
            tasks.append((2 * D_A + idx * D_A + jj * cn, plain_store(ref, jj)))

    def matmul(t):
        c0 = tasks[t][0]
        return jnp.dot(hx, w_ref[:, c0:c0 + cn], preferred_element_type=f32)

    _pipelined_matmuls(len(tasks), matmul, lambda t, p: tasks[t][1](p), pbuf_ref)
    gt_ref[...] = lax.dot_general(wgt_ref[...], hx, (((1,), (1,)), ((), ())),
                                  preferred_element_type=f32) + gbt_ref[...]


def _inproj_even(x, shift, scale, norm_g, w_main, wgt, gbt, conv_w, tm):
    B, T, D = x.shape
    nt = T // tm
    hb = tm // V7X_SUBLANES
    nhb = T // V7X_SUBLANES
    row_spec = pl.BlockSpec((None, tm, D_A), lambda b, i: (b, i, 0))
    vec = lambda n: pl.BlockSpec((1, n), lambda b, i: (0, 0))
    est = (2 * w_main.size * 2 + 2 * tm * D * 4 + 7 * 2 * tm * D_A * 2 + 6 * tm * 512 * 4)
    outs = pl.pallas_call(
        functools.partial(_inproj_even_kernel, tm=tm, nt=nt),
        grid=(B, nt),
        in_specs=[
            pl.BlockSpec((None, tm, D), lambda b, i: (b, i, 0)),
            pl.BlockSpec((None, V7X_SUBLANES, D), lambda b, i: (b, jnp.maximum(i * hb - 1, 0), 0)),
            pl.BlockSpec((None, V7X_SUBLANES, D), lambda b, i: (b, jnp.minimum((i + 1) * hb, nhb - 1), 0)),
            pl.BlockSpec((None, 1, D), lambda b, i: (b, 0, 0)),
            pl.BlockSpec((None, 1, D), lambda b, i: (b, 0, 0)),
            vec(D),
            pl.BlockSpec(w_main.shape, lambda b, i: (0, 0)),
            pl.BlockSpec(wgt.shape, lambda b, i: (0, 0)),
            pl.BlockSpec((N_GATES, 1), lambda b, i: (0, 0)),
            pl.BlockSpec(conv_w.shape, lambda b, i: (0, 0)),
        ],
        out_specs=[row_spec] * 7 + [pl.BlockSpec((None, N_GATES, tm), lambda b, i: (b, 0, i))],
        out_shape=[jax.ShapeDtypeStruct((B, T, D_A), bf16)] * 7 + [jax.ShapeDtypeStruct((B, N_GATES, T), f32)],
        scratch_shapes=[pltpu.VMEM((3, tm, 512), f32)],
        compiler_params=pltpu.CompilerParams(dimension_semantics=("parallel", "arbitrary"),
                                             vmem_limit_bytes=_vmem_limit(est)),
        name="inproj_even",
    )(x, x, x, shift, scale, norm_g.reshape(1, D), w_main, wgt, gbt, conv_w)
    return outs


GATE_SLOTS = 2 * N_HEADS_A
GATE_PIECES = 3
GATE_QUANTS = 3
assert GATE_SLOTS & (GATE_SLOTS - 1) == 0 and GATE_QUANTS * GATE_PIECES * GATE_SLOTS <= V7X_LANES


def _scan_max_lanes(x, reverse):
    n = x.shape[1]
    lane = lax.broadcasted_iota(jnp.int32, x.shape, 1)
    s = 1
    while s < n:
        if reverse:
            x = jnp.where(lane < n - s, jnp.maximum(x, pltpu.roll(x, n - s, 1)), x)
        else:
            x = jnp.where(lane >= s, jnp.maximum(x, pltpu.roll(x, s, 1)), x)
        s *= 2
    return x


def _split3(v):
    hi = v.astype(bf16)
    r1 = v - hi.astype(f32)
    mid = r1.astype(bf16)
    lo = (r1 - mid.astype(f32)).astype(bf16)
    return hi, mid, lo


def _gate_prep_kernel(gt_ref, cols_ref, arow_ref, *, L, nchunk):
    r = lax.broadcasted_iota(jnp.int32, (L, L), 0)
    c = lax.broadcasted_iota(jnp.int32, (L, L), 1)
    tri_l = (c <= r).astype(bf16)
    tri_u = (c >= r).astype(bf16)
    S = V7X_SUBLANES
    fwd = (lax.broadcasted_iota(jnp.int32, (S, L), 0) & 1) == 0
    fill = jnp.zeros((V7X_LANES - GATE_QUANTS * GATE_PIECES * S, L), f32)
    for ci in range(nchunk):
        sl = slice(ci * L, (ci + 1) * L)
        ig = gt_ref[:S, sl]
        lf = _log_sigmoid(gt_ref[S:, sl])
        pieces = jnp.concatenate([p.astype(f32) for p in _split3(lf)] + [jnp.zeros((S, L), f32)],
                                 axis=0).astype(bf16)
        pre = jnp.dot(pieces, tri_u, preferred_element_type=f32)
        suf = jnp.dot(pieces, tri_l, preferred_element_type=f32)
        pre = pre[:S] + pre[S:2 * S] + pre[2 * S:3 * S]
        suf = suf[:S] + suf[S:2 * S] + suf[2 * S:3 * S]
        b_row = jnp.where(fwd, pre, suf)
        a_row = ig - b_row
        cmax = jnp.where(fwd, _scan_max_lanes(a_row, False), _scan_max_lanes(a_row, True))
        arow_ref[:, sl] = a_row
        parts = [p.astype(f32) for quant in (b_row, a_row, cmax) for p in _split3(quant)]
        packed = jnp.concatenate(parts + [fill], axis=0)
        cols_ref[sl, :] = packed.T.astype(bf16)


def _gate_prep(gt, L):
    B, _, T = gt.shape
    nchunk = min(4, T // L)
    tb = nchunk * L
    cols, arow = pl.pallas_call(
        functools.partial(_gate_prep_kernel, L=L, nchunk=nchunk),
        grid=(B, T // tb),
        in_specs=[pl.BlockSpec((None, N_GATES, tb), lambda b, i: (b, 0, i))],
        out_specs=[
            pl.BlockSpec((None, tb, V7X_LANES), lambda b, i: (b, i, 0)),
            pl.BlockSpec((None, V7X_SUBLANES, tb), lambda b, i: (b, 0, i)),
        ],
        out_shape=[
            jax.ShapeDtypeStruct((B, T, V7X_LANES), bf16),
            jax.ShapeDtypeStruct((B, V7X_SUBLANES, T), f32),
        ],
        compiler_params=pltpu.CompilerParams(dimension_semantics=("parallel", "parallel")),
        name="gate_prep",
    )(gt)
    return cols, arow.reshape(B, N_HEADS_A, 2, T)


def _mlstm_kernel(qx_ref, kx_ref, vx_ref, cx_ref, rx_ref, qc_ref, kc_ref, vc_ref, cc_ref, rc_ref,
                  hx_ref, hc_ref, cn_ref, *, L, nx, nc):
    Dh, LN, HP = HEAD_DIM_A, V7X_LANES, MLSTM_HEADS_PER_STEP
    head0 = pl.program_id(1) * HP
    chains = [(hh, d) for hh in range(HP) for d in range(2)]
    r_i = lax.broadcasted_iota(jnp.int32, (L, L), 0)
    c_i = lax.broadcasted_iota(jnp.int32, (L, L), 1)
    masks = (c_i <= r_i, c_i >= r_i)

    sr = lax.broadcasted_iota(jnp.int32, (LN, GATE_QUANTS * LN), 0)
    sc = lax.broadcasted_iota(jnp.int32, (LN, GATE_QUANTS * LN), 1)
    span = GATE_PIECES * GATE_SLOTS
    in_block = None
    for qi in range(GATE_QUANTS):
        blk = (sr >= qi * span) & (sr < (qi + 1) * span) & (sc >= qi * LN) & (sc < (qi + 1) * LN)
        in_block = blk if in_block is None else in_block | blk
    slot = sr & (GATE_SLOTS - 1)
    sels = {(hh, d): (in_block & (slot == 2 * (head0 + hh) + d)).astype(bf16) for hh, d in chains}
    ones = jnp.ones((L, LN), bf16)

    def tile(v, width):
        return jnp.concatenate([v] * (width // LN), axis=1)

    def step(refs, r0s, ms):
        q_ref, k_ref, v_ref, col_ref, row_ref = refs
        st = []
        for ci, (hh, d) in enumerate(chains):
            rows = pl.ds(r0s[ci], L)
            hsl = slice(hh * Dh, (hh + 1) * Dh)
            q = q_ref[rows, hsl]
            k = k_ref[rows, hsl]
            vo = jnp.concatenate([v_ref[rows, hsl], ones], axis=1)
            rep = jnp.dot(col_ref[rows, :], sels[hh, d], preferred_element_type=f32)
            qk = lax.dot_general(q, k, (((1,), (1,)), ((), ())), preferred_element_type=f32)
            qcn = jnp.dot(q, cn_ref[ci].astype(bf16), preferred_element_type=f32)
            st.append(dict(k=k, vo=vo, rep=rep, qk=qk, qcn=qcn, a_row=row_ref[hh, d:d + 1, rows]))
        new_ms = []
        for ci, (hh, d) in enumerate(chains):
            c, m_prev = st[ci], ms[ci]
            b_rep, a_rep = c["rep"][:, :LN], c["rep"][:, LN:2 * LN]
            g_rep = jnp.maximum(c["rep"][:, 2 * LN:], m_prev)
            p = jnp.exp(jnp.where(masks[d], c["a_row"] - tile(g_rep, L), NEG_BIG))
            c["s"] = (c["qk"] * p).astype(bf16)
            b_end = b_rep[L - 1:L, :] if d == 0 else b_rep[0:1, :]
            w = b_end + a_rep
            m_new = jnp.maximum(b_end + m_prev, jnp.max(w, axis=0, keepdims=True))
            c["decay"] = jnp.exp(b_end + m_prev - m_new)
            c["kw"] = (c["k"].astype(f32) * tile(jnp.exp(w - m_new), Dh)).astype(bf16)
            c["inter"] = jnp.exp(m_prev - g_rep)
            c["floor"] = jnp.exp(-(b_rep + g_rep))
            new_ms.append(m_new)
        for c in st:
            c["sv"] = jnp.dot(c["s"], c["vo"], preferred_element_type=f32)
            c["upd"] = lax.dot_general(c["kw"], c["vo"], (((0,), (0,)), ((), ())), preferred_element_type=f32)
        hs = []
        for ci, c in enumerate(st):
            sv, qcn, inter = c["sv"], c["qcn"], c["inter"]
            den = sv[:, Dh:] + inter * qcn[:, Dh:]
            rcp = 1.0 / jnp.maximum(jnp.abs(den), c["floor"])
            hs.append((sv[:, :Dh] + tile(inter, Dh) * qcn[:, :Dh]) * tile(rcp, Dh))
            cn_ref[ci] = tile(c["decay"], Dh + LN) * cn_ref[ci] + c["upd"]
        return hs, new_ms

    ctx = (qc_ref, kc_ref, vc_ref, cc_ref, rc_ref)
    lat = (qx_ref, kx_ref, vx_ref, cx_ref, rx_ref)

    cn_ref[...] = jnp.zeros_like(cn_ref)
    ms = [jnp.zeros((1, LN), f32) for _ in chains]
    written = set()
    for j in range(nc):
        cjs = [j if d == 0 else nc - 1 - j for _, d in chains]
        hs, ms = step(ctx, [cj * L for cj in cjs], ms)
        for (hh, d), cj, h in zip(chains, cjs, hs):
            dst = (slice(cj * L, (cj + 1) * L), slice(hh * Dh, (hh + 1) * Dh))
            if (cj, hh) in written:
                hc_ref[dst] += h
            else:
                hc_ref[dst] = h
                written.add((cj, hh))

    def make_body(accumulate):
        def body(i, ms):
            r0s = [pl.multiple_of((i if d == 0 else nx - 1 - i) * L, L) for _, d in chains]
            hs, ms = step(lat, r0s, list(ms))
            for (hh, d), r0, h in zip(chains, r0s, hs):
                dst = (pl.ds(r0, L), slice(hh * Dh, (hh + 1) * Dh))
                if accumulate:
                    hx_ref[dst] += h
                else:
                    hx_ref[dst] = h
            return tuple(ms)
        return body

    ms = lax.fori_loop(0, nx // 2, make_body(False), tuple(ms))
    lax.fori_loop(nx // 2, nx, make_body(True), ms)


def _mlstm(qx, kx, vx, colsx, rowsx, qc, kc, vc, colsc, rowsc, L):
    B, T, _ = qx.shape
    Tc = qc.shape[1]
    H, Dh, HP = N_HEADS_A, HEAD_DIM_A, MLSTM_HEADS_PER_STEP
    assert T % (2 * L) == 0 and Tc % L == 0 and H % HP == 0

    def seq_spec(t):
        return pl.BlockSpec((None, t, HP * Dh), lambda b, h: (b, 0, h))

    def col_spec(t):
        return pl.BlockSpec((None, t, V7X_LANES), lambda b, h: (b, 0, 0))

    def row_spec(t):
        return pl.BlockSpec((None, HP, 2, t), lambda b, h: (b, h, 0, 0))

    est = (2 * HP * (3 * (T + Tc) * Dh * 2 + 8 * (T + Tc) * 4 + (T + Tc) * Dh * 4)
           + 2 * (T + Tc) * V7X_LANES * 2 + 2 * HP * Dh * (Dh + V7X_LANES) * 4 + 16 * L * L * 4)
    return pl.pallas_call(
        functools.partial(_mlstm_kernel, L=L, nx=T // L, nc=Tc // L),
        grid=(B, H // HP),
        in_specs=[seq_spec(T), seq_spec(T), seq_spec(T), col_spec(T), row_spec(T),
                  seq_spec(Tc), seq_spec(Tc), seq_spec(Tc), col_spec(Tc), row_spec(Tc)],
        out_specs=[seq_spec(T), seq_spec(Tc)],
        out_shape=[jax.ShapeDtypeStruct((B, T, D_A), f32), jax.ShapeDtypeStruct((B, Tc, D_A), f32)],
        scratch_shapes=[pltpu.VMEM((2 * HP, Dh, Dh + V7X_LANES), f32)],
        compiler_params=pltpu.CompilerParams(dimension_semantics=("parallel", "parallel"),
                                             vmem_limit_bytes=_vmem_limit(est)),
        name="mlstm",
    )(qx, kx, vx, colsx, rowsx, qc, kc, vc, colsc, rowsc)


def _fourier_w_kernel(cs_ref, fw_ref, o_ref, *, scale):
    o_ref[...] = (jnp.dot(cs_ref[...], fw_ref[...], preferred_element_type=f32,
                          precision=lax.Precision.HIGHEST) * scale).astype(bf16)


def _fourier_weights(fw, T):
    n = GROUP_B
    kk = np.outer(np.arange(n), np.arange(n)) % n
    ang = 2.0 * np.pi * kk / n
    cs = jnp.asarray(np.concatenate([np.cos(ang), np.sin(ang)], axis=0), f32)
    return pl.pallas_call(
        functools.partial(_fourier_w_kernel, scale=float(1.0 / np.sqrt(T * n))),
        grid=(N_GROUPS_B,),
        in_specs=[pl.BlockSpec((2 * n, n), lambda g: (0, 0)),
                  pl.BlockSpec((None, n, n), lambda g: (g, 0, 0))],
        out_specs=pl.BlockSpec((None, 2 * n, n), lambda g: (g, 0, 0)),
        out_shape=jax.ShapeDtypeStruct((N_GROUPS_B, 2 * n, n), bf16),
        compiler_params=pltpu.CompilerParams(dimension_semantics=("arbitrary",)),
        name="fourier_weights",
    )(cs, fw)


def _fft_tables():
    n1, n2 = FFT_N1, FFT_N2
    n = n1 * n2
    t1 = np.arange(n1)
    k1 = np.arange(n1)
    t2 = np.arange(n2)
    idx = (k1[None, :, None] * (n2 * t1[None, None, :] + t2[:, None, None])) % n
    ang = 2.0 * np.pi * idx / n
    m1 = np.concatenate([np.cos(ang), -np.sin(ang)], axis=1)
    k2 = np.arange(n2)
    ph = 2.0 * np.pi * (np.outer(k2, t2) % n2) / n2
    c, s = np.cos(ph), np.sin(ph)
    m2 = np.block([[c, s], [-s, c]])
    return jnp.asarray(m1, f32).astype(bf16), jnp.asarray(m2, f32).astype(bf16)


def _fourier_kernel(u_ref, z_ref, m1_ref, m2_ref, w_ref, o_ref, up_ref, yp_ref, xp_ref, pbuf_ref):
    n1, n2, P = FFT_N1, FFT_N2, FFT_PITCH
    nl = GROUP_B // V7X_LANES

    def fill(t1, carry):
        src = pl.multiple_of(t1 * n2, n2)
        dst = pl.multiple_of(t1 * P, V7X_SUBLANES)
        blk = u_ref[pl.ds(src, n2), :].astype(f32)
        for s in range(nl):
            up_ref[s, pl.ds(dst, n2), :] = blk[:, s * V7X_LANES:(s + 1) * V7X_LANES]
        return carry

    lax.fori_loop(0, n1, fill, 0, unroll=4)

    G = FFT_UNROLL

    def stage1(grp, carry):
        t2s = [grp * G + j for j in range(G)]
        rhs = [jnp.concatenate([up_ref[s, pl.ds(t2, n1, stride=P), :] for s in range(nl)],
                               axis=1).astype(bf16) for t2 in t2s]
        ys = [jnp.dot(m1_ref[t2], r, preferred_element_type=f32) for t2, r in zip(t2s, rhs)]
        for t2, y in zip(t2s, ys):
            dst = pl.multiple_of(t2 * P, V7X_SUBLANES)
            for ri in range(2):
                for s in range(nl):
                    yp_ref[ri * nl + s, pl.ds(dst, n1), :] = y[ri * n1:(ri + 1) * n1,
                                                               s * V7X_LANES:(s + 1) * V7X_LANES]
        return carry

    lax.fori_loop(0, n2 // G, stage1, 0)

    m2 = m2_ref[...]

    def stage2(grp, carry):
        k1s = [grp * G + j for j in range(G)]
        rhs = []
        for k1 in k1s:
            parts = [jnp.concatenate([yp_ref[ri * nl + s, pl.ds(k1, n2, stride=P), :] for s in range(nl)], axis=1)
                     for ri in range(2)]
            rhs.append(jnp.concatenate(parts, axis=0).astype(bf16))
        xs = [jnp.dot(m2, r, preferred_element_type=f32) for r in rhs]
        for k1, x in zip(k1s, xs):
            for ri in range(2):
                for s in range(nl):
                    xp_ref[ri * nl + s, pl.ds(k1, n2, stride=P), :] = x[ri * n2:(ri + 1) * n2,
                                                                        s * V7X_LANES:(s + 1) * V7X_LANES]
        return carry

    lax.fori_loop(0, n1 // G, stage2, 0)

    tr = pbuf_ref.shape[1]

    def matmul(t):
        blocks = []
        for k2 in range(t * (tr // n1), (t + 1) * (tr // n1)):
            blocks.append(jnp.concatenate([xp_ref[sl, k2 * P:k2 * P + n1, :] for sl in range(2 * nl)],
                                          axis=1).astype(bf16))
        return jnp.dot(jnp.concatenate(blocks, axis=0), w_ref[...], preferred_element_type=f32)

    def gate_store(t, y):
        sl = slice(t * tr, (t + 1) * tr)
        o_ref[sl, :] = y.astype(bf16) * _silu(z_ref[sl, :])

    _pipelined_matmuls((n1 * n2) // tr, matmul, gate_store, pbuf_ref)


def _fourier_latent(ub, zb, wcs):
    B, T, _ = ub.shape
    assert T == FFT_N1 * FFT_N2
    m1, m2 = _fft_tables()
    G, n = N_GROUPS_B, GROUP_B
    nl = n // V7X_LANES
    rows_p = FFT_N1 * FFT_PITCH
    blk = pl.BlockSpec((None, T, n), lambda b, g: (b, 0, g))
    est = (3 * 2 * T * n * 2 + 5 * nl * rows_p * V7X_LANES * 4 + T * 2 * n * 2 + 2 * m1.size * 2 + 8 * 512 * n * 4)
    return pl.pallas_call(
        _fourier_kernel,
        grid=(B, G),
        in_specs=[blk, blk,
                  pl.BlockSpec(m1.shape, lambda b, g: (0, 0, 0)),
                  pl.BlockSpec(m2.shape, lambda b, g: (0, 0)),
                  pl.BlockSpec((None, 2 * n, n), lambda b, g: (g, 0, 0))],
        out_specs=blk,
        out_shape=jax.ShapeDtypeStruct((B, T, D_B), bf16),
        scratch_shapes=[pltpu.VMEM((nl, rows_p, V7X_LANES), f32),
                        pltpu.VMEM((2 * nl, rows_p, V7X_LANES), f32),
                        pltpu.VMEM((2 * nl, rows_p, V7X_LANES), f32),
                        pltpu.VMEM((3, 512, n), f32)],
        compiler_params=pltpu.CompilerParams(dimension_semantics=("parallel", "parallel"),
                                             vmem_limit_bytes=_vmem_limit(est)),
        name="fourier_latent",
    )(ub, zb, m1, m2, wcs)


def _fourier_ctx_kernel(u_ref, z_ref, cs_ref, w_ref, o_ref):
    u = u_ref[...]
    x = jnp.dot(cs_ref[...], u, preferred_element_type=f32).astype(bf16)
    t = u.shape[0]
    w = w_ref[...]
    n = GROUP_B
    y = (jnp.dot(x[:t], w[:n], preferred_element_type=f32) + jnp.dot(x[t:], w[n:], preferred_element_type=f32))
    o_ref[...] = (y * _silu(z_ref[...].astype(f32))).astype(bf16)


def _fourier_ctx(ub, zb, wcs):
    B, T, _ = ub.shape
    ang = 2.0 * np.pi * (np.outer(np.arange(T), np.arange(T)) % T) / T
    cs = jnp.asarray(np.concatenate([np.cos(ang), -np.sin(ang)], axis=0), f32).astype(bf16)
    n = GROUP_B
    blk = pl.BlockSpec((None, T, n), lambda b, g: (b, 0, g))
    return pl.pallas_call(
        _fourier_ctx_kernel,
        grid=(B, N_GROUPS_B),
        in_specs=[blk, blk, pl.BlockSpec(cs.shape, lambda b, g: (0, 0)),
                  pl.BlockSpec((None, 2 * n, n), lambda b, g: (g, 0, 0))],
        out_specs=blk,
        out_shape=jax.ShapeDtypeStruct((B, T, D_B), bf16),
        compiler_params=pltpu.CompilerParams(dimension_semantics=("parallel", "parallel")),
        name="fourier_ctx",
    )(ub, zb, cs, wcs)


def _outproj_even_kernel(x_ref, h_ref, o_ref, za_ref, yb_ref, gate_ref, hg_ref, w_ref, out_ref):
    parts = []
    for hd in range(N_HEADS_A):
        sl = slice(hd * HEAD_DIM_A, (hd + 1) * HEAD_DIM_A)
        hh = _sigmoid(o_ref[:, sl]).astype(f32) * h_ref[:, sl]
        hh = hh * lax.rsqrt(jnp.mean(hh * hh, axis=-1, keepdims=True) + EPS)
        parts.append(hh.astype(bf16) * (hg_ref[:, sl].astype(bf16) * _silu(za_ref[:, sl])))
    ya = jnp.concatenate(parts, axis=1)
    acc = jnp.dot(ya, w_ref[:D_A, :], preferred_element_type=f32)
    acc = acc + jnp.dot(yb_ref[...], w_ref[D_A:, :], preferred_element_type=f32)
    out_ref[...] = x_ref[...] + gate_ref[...] * acc


def _outproj_even(x, h, o, za, yb, gate, head_g, wout, tm):
    B, T, D = x.shape
    row = lambda n: pl.BlockSpec((None, tm, n), lambda b, i: (b, i, 0))
    est = 2 * wout.size * 2 + 2 * tm * (2 * D * 4 + D_A * 4 + 3 * D_A * 2) + 8 * tm * D * 4
    return pl.pallas_call(
        _outproj_even_kernel,
        grid=(B, T // tm),
        in_specs=[row(D), row(D_A), row(D_A), row(D_A), row(D_B),
                  pl.BlockSpec((None, 1, D), lambda b, i: (b, 0, 0)),
                  pl.BlockSpec((1, D_A), lambda b, i: (0, 0)),
                  pl.BlockSpec(wout.shape, lambda b, i: (0, 0))],
        out_specs=row(D),
        out_shape=jax.ShapeDtypeStruct((B, T, D), f32),
        compiler_params=pltpu.CompilerParams(dimension_semantics=("parallel", "parallel"),
                                             vmem_limit_bytes=_vmem_limit(est)),
        name="outproj_even",
    )(x, h, o, za, yb, gate, head_g.reshape(1, D_A), wout)


def _inproj_odd_kernel(x_ref, sh_ref, sc_ref, g_ref, w_ref, u_ref, z_ref):
    amp = g_ref[...] * (1.0 + sc_ref[...])
    hx = _normed(x_ref[...], amp, sh_ref[...]).astype(bf16)
    cn = 512
    for j in range(D_INNER // cn):
        u_ref[:, j * cn:(j + 1) * cn] = jnp.dot(hx, w_ref[:, j * cn:(j + 1) * cn],
                                                preferred_element_type=f32).astype(bf16)
        z_ref[:, j * cn:(j + 1) * cn] = jnp.dot(hx, w_ref[:, D_INNER + j * cn:D_INNER + (j + 1) * cn],
                                                preferred_element_type=f32).astype(bf16)


def _inproj_odd(x, shift, scale, norm_g, w, tm):
    B, T, D = x.shape
    row = lambda n: pl.BlockSpec((None, tm, n), lambda b, i: (b, i, 0))
    est = 2 * w.size * 2 + 2 * tm * D * 4 + 4 * tm * D_INNER * 2 + 6 * tm * 512 * 4
    return pl.pallas_call(
        _inproj_odd_kernel,
        grid=(B, T // tm),
        in_specs=[row(D),
                  pl.BlockSpec((None, 1, D), lambda b, i: (b, 0, 0)),
                  pl.BlockSpec((None, 1, D), lambda b, i: (b, 0, 0)),
                  pl.BlockSpec((1, D), lambda b, i: (0, 0)),
                  pl.BlockSpec(w.shape, lambda b, i: (0, 0))],
        out_specs=[row(D_INNER), row(D_INNER)],
        out_shape=[jax.ShapeDtypeStruct((B, T, D_INNER), bf16)] * 2,
        compiler_params=pltpu.CompilerParams(dimension_semantics=("parallel", "parallel"),
                                             vmem_limit_bytes=_vmem_limit(est)),
        name="inproj_odd",
    )(x, shift, scale, norm_g.reshape(1, D), w)


POOL_UNROLL = 8


def _pool_tables():
    w_idx = np.arange(GRID_W)
    band = np.zeros((N_GROUPS_C, GRID_W, GRID_W), np.float32)
    inv_w = np.zeros((N_GROUPS_C, GRID_W, V7X_LANES), np.float32)
    for g, win in enumerate(POOL_WINDOWS):
        lo = np.clip(w_idx - win // 2, 0, GRID_W)
        hi = np.clip(w_idx + win - win // 2, 0, GRID_W)
        band[g] = (w_idx[None, :] >= lo[:, None]) & (w_idx[None, :] < hi[:, None])
        inv_w[g] = (1.0 / (hi - lo))[:, None]
    return jnp.asarray(band, bf16), jnp.asarray(inv_w, f32)


def _pool_kernel(u_ref, z_ref, band_ref, invw_ref, pw_ref, sc_ref, o_ref, ps_ref, pbuf_ref, *, rows):
    g = pl.program_id(1)
    W = GRID_W
    band = band_ref[...]
    lo_off = hi_off = 0
    for gi, win in enumerate(POOL_WINDOWS):
        lo_off = jnp.where(g == gi, win // 2, lo_off)
        hi_off = jnp.where(g == gi, win - win // 2, hi_off)

    ps_ref[0:W, :] = jnp.zeros((W, GROUP_C), f32)

    def width_sum(grp, carry):
        srcs = [pl.multiple_of((grp * POOL_UNROLL + j) * W, W) for j in range(POOL_UNROLL)]
        sums = [jnp.dot(band, u_ref[pl.ds(src, W), :], preferred_element_type=f32) for src in srcs]
        acc = ps_ref[pl.ds(srcs[0], W), :]
        for src, s in zip(srcs, sums):
            acc = acc + s
            ps_ref[pl.ds(src + W, W), :] = acc
        return carry

    lax.fori_loop(0, rows // POOL_UNROLL, width_sum, 0)

    inv_w = invw_ref[...]

    def pooled_minus_self(r):
        lo = jnp.maximum(r - lo_off, 0)
        hi = jnp.minimum(r + hi_off, rows)
        acc = (ps_ref[pl.ds(pl.multiple_of(hi * W, W), W), :]
               - ps_ref[pl.ds(pl.multiple_of(lo * W, W), W), :])
        inv = inv_w / (hi - lo).astype(f32)
        inv = jnp.concatenate([inv] * (GROUP_C // V7X_LANES), axis=1)
        ug = u_ref[r * W:(r + 1) * W, :].astype(f32)
        return (acc * inv - ug).astype(bf16)

    tr = pbuf_ref.shape[1]
    pws = (pw_ref[...] * sc_ref[...]).astype(bf16)

    def matmul(t):
        lhs = jnp.concatenate([pooled_minus_self(t * (tr // W) + j) for j in range(tr // W)], axis=0)
        return jnp.dot(lhs, pws, preferred_element_type=f32)

    def gate_store(t, y):
        sl = slice(t * tr, (t + 1) * tr)
        o_ref[sl, :] = y.astype(bf16) * _silu(z_ref[sl, :])

    _pipelined_matmuls((rows * W) // tr, matmul, gate_store, pbuf_ref)


def _pool_mix(u, z, pool_w, scale):
    B, T, _ = u.shape
    rows = T // GRID_W
    band, inv_w = _pool_tables()
    n = GROUP_C
    blk = pl.BlockSpec((None, T, n), lambda b, g: (b, 0, g))
    est = 3 * 2 * T * n * 2 + (rows + 1) * GRID_W * n * 4 + T * n * 2 + 2 * n * n * 4 + 8 * 512 * n * 4
    return pl.pallas_call(
        functools.partial(_pool_kernel, rows=rows),
        grid=(B, N_GROUPS_C),
        in_specs=[blk, blk,
                  pl.BlockSpec((None, GRID_W, GRID_W), lambda b, g: (g, 0, 0)),
                  pl.BlockSpec((None, GRID_W, V7X_LANES), lambda b, g: (g, 0, 0)),
                  pl.BlockSpec((None, n, n), lambda b, g: (g, 0, 0)),
                  pl.BlockSpec((None, 1, n), lambda b, g: (g, 0, 0))],
        out_specs=blk,
        out_shape=jax.ShapeDtypeStruct((B, T, D_INNER), bf16),
        scratch_shapes=[pltpu.VMEM(((rows + 1) * GRID_W, n), f32),
                        pltpu.VMEM((3, 512, n), f32)],
        compiler_params=pltpu.CompilerParams(dimension_semantics=("parallel", "arbitrary"),
                                             vmem_limit_bytes=_vmem_limit(est)),
        name="pool_mix",
    )(u, z, band, inv_w, pool_w, scale.reshape(N_GROUPS_C, 1, n))


def _outproj_odd_kernel(x_ref, y_ref, gate_ref, fg_ref, w_ref, out_ref):
    acc = jnp.dot(y_ref[...], w_ref[...], preferred_element_type=f32)
    x = x_ref[...] + gate_ref[...] * acc
    out_ref[...] = x * lax.rsqrt(jnp.mean(x * x, axis=-1, keepdims=True) + EPS) * fg_ref[...]


def _outproj_odd(x, y, gate, final_g, wout, tm):
    B, T, D = x.shape
    row = lambda n: pl.BlockSpec((None, tm, n), lambda b, i: (b, i, 0))
    est = 2 * wout.size * 2 + 2 * tm * (2 * D * 4 + D_INNER * 2) + 6 * tm * D * 4
    return pl.pallas_call(
        _outproj_odd_kernel,
        grid=(B, T // tm),
        in_specs=[row(D), row(D_INNER),
                  pl.BlockSpec((None, 1, D), lambda b, i: (b, 0, 0)),
                  pl.BlockSpec((1, D), lambda b, i: (0, 0)),
                  pl.BlockSpec(wout.shape, lambda b, i: (0, 0))],
        out_specs=row(D),
        out_shape=jax.ShapeDtypeStruct((B, T, D), f32),
        compiler_params=pltpu.CompilerParams(dimension_semantics=("parallel", "parallel"),
                                             vmem_limit_bytes=_vmem_limit(est)),
        name="outproj_odd",
    )(x, y, gate, final_g.reshape(1, D), wout)


def kernel(x, c, ctx, c_ctx, ada_w, ada_b, norm_g, win_even, gate_b_even, conv_qk_even, head_norm_even,
           fourier_w_even, wout_even, win_odd, pool_w_odd, pool_scale_odd, wout_odd, final_g):
    B, T, D = x.shape
    Tc = ctx.shape[1]
    H = N_HEADS_A
    L = MLSTM_CHUNK

    nrow = -(-(B + 1) // V7X_SUBLANES) * V7X_SUBLANES
    rows_in = jnp.zeros((nrow, D), f32).at[:B].set(c).at[B].set(c_ctx)
    mod = _modulation(rows_in, ada_w, ada_b)

    def mod_parts(l, r0, r1, n):
        m = mod[l, r0:r1]
        parts = [jnp.broadcast_to(m[:, None, i * D:(i + 1) * D], (n, 1, D)) for i in range(3)]
        return parts

    we = win_even[0]
    w_main = we[:, :W_MAIN_EVEN].astype(bf16)
    gcols = we[:, W_MAIN_EVEN:].reshape(D, 4, H)
    gbias = gate_b_even[0].reshape(4, H)
    ig_w = gcols[:, 0::2, :].transpose(0, 2, 1).reshape(D, GATE_SLOTS)
    fg_w = gcols[:, 1::2, :].transpose(0, 2, 1).reshape(D, GATE_SLOTS)
    ig_b = gbias[0::2, :].T.reshape(GATE_SLOTS)
    fg_b = gbias[1::2, :].T.reshape(GATE_SLOTS)
    wgt =jnp.concatenate([ig_w, fg_w], axis=1).T.astype(bf16)
    gbt = jnp.concatenate([ig_b, fg_b]).reshape(N_GATES, 1)
    conv_w = conv_qk_even[0]

    shift_x, scale_x, gate_x = mod_parts(0, 0, B, B)
    shift_c, scale_c, gate_c = mod_parts(0, B, B + 1, B)

    qx, kx, vx, ox, zax, ubx, zbx, gtx = _inproj_even(
        x, shift_x, scale_x, norm_g[0], w_main, wgt, gbt, conv_w, tm=512)
    qc, kc, vc, oc, zac, ubc, zbc, gtc = _inproj_even(
        ctx, shift_c, scale_c, norm_g[0], w_main, wgt, gbt, conv_w, tm=Tc)

    colsx, rowsx = _gate_prep(gtx, L)
    colsc, rowsc = _gate_prep(gtc, L)
    h_x, h_c = _mlstm(qx, kx, vx, colsx, rowsx, qc, kc, vc, colsc, rowsc, L)

    wcs = _fourier_weights(fourier_w_even[0], T)
    yb_x = _fourier_latent(ubx, zbx, wcs)
    wout_e = wout_even[0].astype(bf16)
    x1 = _outproj_even(x, h_x, ox, zax, yb_x, gate_x, head_norm_even[0], wout_e, tm=1024)

    wcs_c = _fourier_weights(fourier_w_even[0], Tc)
    yb_c = _fourier_ctx(ubc, zbc, wcs_c)
    ctx1 = _outproj_even(ctx, h_c, oc, zac, yb_c, gate_c, head_norm_even[0], wout_e, tm=Tc)
    del ctx1

    shift_x, scale_x, gate_x = mod_parts(1, 0, B, B)
    u1, z1 = _inproj_odd(x1, shift_x, scale_x, norm_g[1], win_odd[0].astype(bf16), tm=1024)
    y1 = _pool_mix(u1, z1, pool_w_odd[0], pool_scale_odd[0])
    return _outproj_odd(x1, y1, gate_x, final_g, wout_odd[0].astype(bf16), tm=1024)
```

```python
import functools

import numpy as np
import jax
import jax.numpy as jnp
from jax import lax
from jax.experimental import pallas as pl
from jax.experimental.pallas import tpu as pltpu

D_MODEL = 1024
DEPTH = 2
CTX_LEN = 256
GRID_W = 64
D_INNER = 2 * D_MODEL
D_A = D_INNER // 2
D_B = D_INNER - D_A
N_HEADS_A = 4
HEAD_DIM_A = D_A // N_HEADS_A
N_GROUPS_B = 4
GROUP_B = D_B // N_GROUPS_B
N_GROUPS_C = 4
GROUP_C = D_INNER // N_GROUPS_C
POOL_WINDOWS = (2, 4, 8, 16)
CONV_W = 3
N_GATES = 4 * N_HEADS_A
W_MAIN_EVEN = 5 * D_A + 2 * D_B
EPS = 1e-6

f32 = jnp.float32
bf16 = jnp.bfloat16

V7X_VMEM_BYTES = 64 * 1024 * 1024
V7X_LANES = 128
V7X_SUBLANES = 8

MLSTM_CHUNK = 256
MLSTM_HEADS_PER_STEP = 2
FFT_N1 = 64
FFT_N2 = 64
FFT_PITCH = 72
FFT_UNROLL = 8
NEG_BIG = -1e30


def _vmem_limit(nbytes):
    return int(min(max(nbytes * 5 // 4 + (4 << 20), 16 << 20), V7X_VMEM_BYTES - (6 << 20)))


def _sigmoid(v):
    return 0.5 * jnp.tanh(0.5 * v) + 0.5


def _silu(v):
    return v * _sigmoid(v)


def _pipelined_matmuls(n, matmul, epilogue, pbuf_ref):
    zero = jnp.minimum(pl.program_id(0), 0)
    nbuf = pbuf_ref.shape[0]
    pbuf_ref[zero] = matmul(0)
    for t in range(n):
        if t + 1 < n:
            pbuf_ref[zero + (t + 1) % nbuf] = matmul(t + 1)
        epilogue(t, pbuf_ref[zero + t % nbuf])


def _log_sigmoid(v):
    return jnp.minimum(v, 0.0) - jnp.log1p(jnp.exp(-jnp.abs(v)))


def _mod_kernel(r_ref, w_ref, b_ref, o_ref):
    s = _silu(r_ref[...])
    o_ref[...] = jnp.dot(s, w_ref[...], preferred_element_type=f32,
                         precision=lax.Precision.HIGHEST) + b_ref[...]


def _modulation(rows, ada_w, ada_b):
    nrow = rows.shape[0]
    tn = 1024
    return pl.pallas_call(
        _mod_kernel,
        grid=(DEPTH, 3 * D_MODEL // tn),
        in_specs=[
            pl.BlockSpec((nrow, D_MODEL), lambda l, j: (0, 0)),
            pl.BlockSpec((None, D_MODEL, tn), lambda l, j: (l, 0, j)),
            pl.BlockSpec((None, 1, tn), lambda l, j: (l, 0, j)),
        ],
        out_specs=pl.BlockSpec((None, nrow, tn), lambda l, j: (l, 0, j)),
        out_shape=jax.ShapeDtypeStruct((DEPTH, nrow, 3 * D_MODEL), f32),
        compiler_params=pltpu.CompilerParams(dimension_semantics=("arbitrary", "arbitrary")),
        name="modulation",
    )(rows, ada_w, ada_b.reshape(DEPTH, 1, 3 * D_MODEL))


def _normed(x, amp, shift):
    ms = jnp.mean(x * x, axis=-1, keepdims=True)
    return (x * lax.rsqrt(ms + EPS)) * amp + shift


def _inproj_even_kernel(x_ref, xp_ref, xn_ref, sh_ref, sc_ref, g_ref, w_ref, wgt_ref, gbt_ref,
                        cw_ref, q_ref, k_ref, v_ref, o_ref, za_ref, ub_ref, zb_ref, gt_ref, pbuf_ref, *, tm, nt):
    i = pl.program_id(1)
    amp = g_ref[...] * (1.0 + sc_ref[...])
    shift = sh_ref[...]
    hx = _normed(x_ref[...], amp, shift).astype(bf16)
    halo = jnp.concatenate([xp_ref[...], xn_ref[...]], axis=0)
    hh = _normed(halo, amp, shift).astype(bf16)
    has_prev = (i > 0).astype(f32)
    has_next = (i < nt - 1).astype(f32)
    row = lax.broadcasted_iota(jnp.int32, (tm, 1), 0)
    cn = 512
    ph = jnp.dot(hh, w_ref[:, :2 * D_A], preferred_element_type=f32)
    prev = ph[V7X_SUBLANES - 1:V7X_SUBLANES, :] * has_prev
    nxt = ph[V7X_SUBLANES:V7X_SUBLANES + 1, :] * has_next

    def conv_store(j, p):
        cols = slice(j * cn, (j + 1) * cn)
        up = jnp.where(row == 0, prev[:, cols], pltpu.roll(p, 1, 0))
        dn = jnp.where(row == tm - 1, nxt[:, cols], pltpu.roll(p, tm - 1, 0))
        cw = cw_ref[:, cols]
        y = _silu(cw[0:1, :] * up + cw[1:2, :] * p + cw[2:3, :] * dn)
        if j < D_A // cn:
            q_ref[:, cols] = (y * (HEAD_DIM_A ** -0.5)).astype(bf16)
        else:
            jj = j - D_A // cn
            k_ref[:, jj * cn:(jj + 1) * cn] = y.astype(bf16)

    def plain_store(ref, jj):
        def store(p):
            ref[:, jj * cn:(jj + 1) * cn] = p.astype(bf16)
        return store

    tasks = [(j * cn, functools.partial(conv_store, j)) for j in range(2 * D_A // cn)]
    for idx, ref in enumerate((v_ref, o_ref, za_ref, ub_ref, zb_ref)):
        for jj in range(D_A // cn):
            tasks.append((2 * D_A + idx * D_A + jj * cn, plain_store(ref, jj)))

    def matmul(t):
        c0 = tasks[t][0]
        return jnp.dot(hx, w_ref[:, c0:c0 + cn], preferred_element_type=f32)

    _pipelined_matmuls(len(tasks), matmul, lambda t, p: tasks[t][1](p), pbuf_ref)
    gt_ref[...] = lax.dot_general(wgt_ref[...], hx, (((1,), (1,)), ((), ())),
                                  preferred_element_type=f32) + gbt_ref[...]


def _inproj_even(x, shift, scale, norm_g, w_main, wgt, gbt, conv_w, tm):
    B, T, D = x.shape
    nt = T // tm
    hb = tm // V7X_SUBLANES
    nhb = T // V7X_SUBLANES
    row_spec = pl.BlockSpec((None, tm, D_A), lambda b, i: (b, i, 0))
    vec = lambda n: pl.BlockSpec((1, n), lambda b, i: (0, 0))
    est = (2 * w_main.size * 2 + 2 * tm * D * 4 + 7 * 2 * tm * D_A * 2 + 6 * tm * 512 * 4)
    outs = pl.pallas_call(
        functools.partial(_inproj_even_kernel, tm=tm, nt=nt),
        grid=(B, nt),
        in_specs=[
            pl.BlockSpec((None, tm, D), lambda b, i: (b, i, 0)),
            pl.BlockSpec((None, V7X_SUBLANES, D), lambda b, i: (b, jnp.maximum(i * hb - 1, 0), 0)),
            pl.BlockSpec((None, V7X_SUBLANES, D), lambda b, i: (b, jnp.minimum((i + 1) * hb, nhb - 1), 0)),
            pl.BlockSpec((None, 1, D), lambda b, i: (b, 0, 0)),
            pl.BlockSpec((None, 1, D), lambda b, i: (b, 0, 0)),
            vec(D),
            pl.BlockSpec(w_main.shape, lambda b, i: (0, 0)),
            pl.BlockSpec(wgt.shape, lambda b, i: (0, 0)),
            pl.BlockSpec((N_GATES, 1), lambda b, i: (0, 0)),
            pl.BlockSpec(conv_w.shape, lambda b, i: (0, 0)),
        ],
        out_specs=[row_spec] * 7 + [pl.BlockSpec((None, N_GATES, tm), lambda b, i: (b, 0, i))],
        out_shape=[jax.ShapeDtypeStruct((B, T, D_A), bf16)] * 7 + [jax.ShapeDtypeStruct((B, N_GATES, T), f32)],
        scratch_shapes=[pltpu.VMEM((3, tm, 512), f32)],
        compiler_params=pltpu.CompilerParams(dimension_semantics=("parallel", "arbitrary"),
                                             vmem_limit_bytes=_vmem_limit(est)),
        name="inproj_even",
    )(x, x, x, shift, scale, norm_g.reshape(1, D), w_main, wgt, gbt, conv_w)
    return outs


GATE_SLOTS = 2 * N_HEADS_A
GATE_PIECES = 3
GATE_QUANTS = 3
assert GATE_SLOTS & (GATE_SLOTS - 1) == 0 and GATE_QUANTS * GATE_PIECES * GATE_SLOTS <= V7X_LANES


def _scan_max_lanes(x, seg, reverse):
    n = x.shape[1]
    pos = lax.broadcasted_iota(jnp.int32, x.shape, 1) & (seg - 1)
    s = 1
    while s < seg:
        if reverse:
            x = jnp.where(pos < seg - s, jnp.maximum(x, pltpu.roll(x, n - s, 1)), x)
        else:
            x = jnp.where(pos >= s, jnp.maximum(x, pltpu.roll(x, s, 1)), x)
        s *= 2
    return x


def _split3(v):
    hi = v.astype(bf16)
    r1 = v - hi.astype(f32)
    mid = r1.astype(bf16)
    lo = (r1 - mid.astype(f32)).astype(bf16)
    return hi, mid, lo


def _gate_prep_kernel(gt_ref, cols_ref, arow_ref, *, L, nchunk):
    r = lax.broadcasted_iota(jnp.int32, (L, L), 0)
    c = lax.broadcasted_iota(jnp.int32, (L, L), 1)
    tri_l = (c <= r).astype(bf16)
    tri_u = (c >= r).astype(bf16)
    S = V7X_SUBLANES
    tb = nchunk * L
    fwd = (lax.broadcasted_iota(jnp.int32, (S, tb), 0) & 1) == 0
    ig = gt_ref[:S, :]
    lf = _log_sigmoid(gt_ref[S:, :])
    pieces = jnp.concatenate([p.astype(f32) for p in _split3(lf)] + [jnp.zeros((S, tb), f32)],
                             axis=0).astype(bf16)
    chunks = [slice(ci * L, (ci + 1) * L) for ci in range(nchunk)]
    pre = jnp.concatenate([jnp.dot(pieces[:, sl], tri_u, preferred_element_type=f32) for sl in chunks], axis=1)
    suf = jnp.concatenate([jnp.dot(pieces[:, sl], tri_l, preferred_element_type=f32) for sl in chunks], axis=1)
    pre = pre[:S] + pre[S:2 * S] + pre[2 * S:3 * S]
    suf = suf[:S] + suf[S:2 * S] + suf[2 * S:3 * S]
    b_row = jnp.where(fwd, pre, suf)
    a_row = ig - b_row
    cmax = jnp.where(fwd, _scan_max_lanes(a_row, L, False), _scan_max_lanes(a_row, L, True))
    arow_ref[...] = a_row
    parts = [p.astype(f32) for quant in (b_row, a_row, cmax) for p in _split3(quant)]
    fill = jnp.zeros((V7X_LANES - len(parts) * S, tb), f32)
    packed = jnp.concatenate(parts + [fill], axis=0)
    for sl in chunks:
        cols_ref[sl, :] = packed[:, sl].T.astype(bf16)


def _gate_prep(gt, L):
    B, _, T = gt.shape
    nchunk = min(4, T // L)
    tb = nchunk * L
    cols, arow = pl.pallas_call(
        functools.partial(_gate_prep_kernel, L=L, nchunk=nchunk),
        grid=(B, T // tb),
        in_specs=[pl.BlockSpec((None, N_GATES, tb), lambda b, i: (b, 0, i))],
        out_specs=[
            pl.BlockSpec((None, tb, V7X_LANES), lambda b, i: (b, i, 0)),
            pl.BlockSpec((None, V7X_SUBLANES, tb), lambda b, i: (b, 0, i)),
        ],
        out_shape=[
            jax.ShapeDtypeStruct((B, T, V7X_LANES), bf16),
            jax.ShapeDtypeStruct((B, V7X_SUBLANES, T), f32),
        ],
        compiler_params=pltpu.CompilerParams(dimension_semantics=("parallel", "parallel")),
        name="gate_prep",
    )(gt)
    return cols, arow.reshape(B, N_HEADS_A, 2, T)


def _mlstm_kernel(qx_ref, kx_ref, vx_ref, cx_ref, rx_ref, qc_ref, kc_ref, vc_ref, cc_ref, rc_ref,
                  hx_ref, hc_ref, cn_ref, *, L, nx, nc):
    Dh, LN, HP = HEAD_DIM_A, V7X_LANES, MLSTM_HEADS_PER_STEP
    head0 = pl.program_id(1) * HP
    chains = [(hh, d) for hh in range(HP) for d in range(2)]
    r_i = lax.broadcasted_iota(jnp.int32, (L, L), 0)
    c_i = lax.broadcasted_iota(jnp.int32, (L, L), 1)
    masks = (c_i <= r_i, c_i >= r_i)

    sr = lax.broadcasted_iota(jnp.int32, (LN, GATE_QUANTS * LN), 0)
    sc = lax.broadcasted_iota(jnp.int32, (LN, GATE_QUANTS * LN), 1)
    span = GATE_PIECES * GATE_SLOTS
    in_block = None
    for qi in range(GATE_QUANTS):
        blk = (sr >= qi * span) & (sr < (qi + 1) * span) & (sc >= qi * LN) & (sc < (qi + 1) * LN)
        in_block = blk if in_block is None else in_block | blk
    slot = sr & (GATE_SLOTS - 1)
    sels = {(hh, d): (in_block & (slot == 2 * (head0 + hh) + d)).astype(bf16) for hh, d in chains}
    ones = jnp.ones((L, LN), bf16)

    def tile(v, width):
        return jnp.concatenate([v] * (width // LN), axis=1)

    def step(refs, r0s, ms):
        q_ref, k_ref, v_ref, col_ref, row_ref = refs
        st = []
        for ci, (hh, d) in enumerate(chains):
            rows = pl.ds(r0s[ci], L)
            hsl = slice(hh * Dh, (hh + 1) * Dh)
            q = q_ref[rows, hsl]
            k = k_ref[rows, hsl]
            vo = jnp.concatenate([v_ref[rows, hsl], ones], axis=1)
            rep = jnp.dot(col_ref[rows, :], sels[hh, d], preferred_element_type=f32)
            qk = lax.dot_general(q, k, (((1,), (1,)), ((), ())), preferred_element_type=f32)
            qcn = jnp.dot(q, cn_ref[ci].astype(bf16), preferred_element_type=f32)
            st.append(dict(k=k, vo=vo, rep=rep, qk=qk, qcn=qcn, a_row=row_ref[hh, d:d + 1, rows]))
        new_ms = []
        for ci, (hh, d) in enumerate(chains):
            c, m_prev = st[ci], ms[ci]
            b_rep, a_rep = c["rep"][:, :LN], c["rep"][:, LN:2 * LN]
            g_rep = jnp.maximum(c["rep"][:, 2 * LN:], m_prev)
            p = jnp.exp(jnp.where(masks[d], c["a_row"] - tile(g_rep, L), NEG_BIG))
            c["s"] = (c["qk"] * p).astype(bf16)
            b_end = b_rep[L - 1:L, :] if d == 0 else b_rep[0:1, :]
            w = b_end + a_rep
            m_new = jnp.maximum(b_end + m_prev, jnp.max(w, axis=0, keepdims=True))
            c["decay"] = jnp.exp(b_end + m_prev - m_new)
            c["kw"] = (c["k"].astype(f32) * tile(jnp.exp(w - m_new), Dh)).astype(bf16)
            c["inter"] = jnp.exp(m_prev - g_rep)
            c["floor"] = jnp.exp(-(b_rep + g_rep))
            new_ms.append(m_new)
        for c in st:
            c["sv"] = jnp.dot(c["s"], c["vo"], preferred_element_type=f32)
            c["upd"] = lax.dot_general(c["kw"], c["vo"], (((0,), (0,)), ((), ())), preferred_element_type=f32)
        hs = []
        for ci, c in enumerate(st):
            sv, qcn, inter = c["sv"], c["qcn"], c["inter"]
            den = sv[:, Dh:] + inter * qcn[:, Dh:]
            rcp = 1.0 / jnp.maximum(jnp.abs(den), c["floor"])
            hs.append((sv[:, :Dh] + tile(inter, Dh) * qcn[:, :Dh]) * tile(rcp, Dh))
            cn_ref[ci] = tile(c["decay"], Dh + LN) * cn_ref[ci] + c["upd"]
        return hs, new_ms

    ctx = (qc_ref, kc_ref, vc_ref, cc_ref, rc_ref)
    lat = (qx_ref, kx_ref, vx_ref, cx_ref, rx_ref)

    cn_ref[...] = jnp.zeros_like(cn_ref)
    ms = [jnp.zeros((1, LN), f32) for _ in chains]
    written = set()
    for j in range(nc):
        cjs = [j if d == 0 else nc - 1 - j for _, d in chains]
        hs, ms = step(ctx, [cj * L for cj in cjs], ms)
        for (hh, d), cj, h in zip(chains, cjs, hs):
            dst = (slice(cj * L, (cj + 1) * L), slice(hh * Dh, (hh + 1) * Dh))
            if (cj, hh) in written:
                hc_ref[dst] = (hc_ref[dst].astype(f32) + h).astype(hc_ref.dtype)
            else:
                hc_ref[dst] = h.astype(hc_ref.dtype)
                written.add((cj, hh))

    def make_body(accumulate):
        def body(i, ms):
            r0s = [pl.multiple_of((i if d == 0 else nx - 1 - i) * L, L) for _, d in chains]
            hs, ms = step(lat, r0s, list(ms))
            for (hh, d), r0, h in zip(chains, r0s, hs):
                dst = (pl.ds(r0, L), slice(hh * Dh, (hh + 1) * Dh))
                if accumulate:
                    hx_ref[dst] = (hx_ref[dst].astype(f32) + h).astype(hx_ref.dtype)
                else:
                    hx_ref[dst] = h.astype(hx_ref.dtype)
            return tuple(ms)
        return body

    ms = lax.fori_loop(0, nx // 2, make_body(False), tuple(ms))
    lax.fori_loop(nx // 2, nx, make_body(True), ms)


def _mlstm(qx, kx, vx, colsx, rowsx, qc, kc, vc, colsc, rowsc, L):
    B, T, _ = qx.shape
    Tc = qc.shape[1]
    H, Dh, HP = N_HEADS_A, HEAD_DIM_A, MLSTM_HEADS_PER_STEP
    assert T % (2 * L) == 0 and Tc % L == 0 and H % HP == 0

    def seq_spec(t):
        return pl.BlockSpec((None, t, HP * Dh), lambda b, h: (b, 0, h))

    def col_spec(t):
        return pl.BlockSpec((None, t, V7X_LANES), lambda b, h: (b, 0, 0))

    def row_spec(t):
        return pl.BlockSpec((None, HP, 2, t), lambda b, h: (b, h, 0, 0))

    est = (2 * HP * (3 * (T + Tc) * Dh * 2 + 8 * (T + Tc) * 4 + (T + Tc) * Dh * 2)
           + 2 * (T + Tc) * V7X_LANES * 2 + 2 * HP * Dh * (Dh + V7X_LANES) * 4 + 16 * L * L * 4)
    return pl.pallas_call(
        functools.partial(_mlstm_kernel, L=L, nx=T // L, nc=Tc // L),
        grid=(B, H // HP),
        in_specs=[seq_spec(T), seq_spec(T), seq_spec(T), col_spec(T), row_spec(T),
                  seq_spec(Tc), seq_spec(Tc), seq_spec(Tc), col_spec(Tc), row_spec(Tc)],
        out_specs=[seq_spec(T), seq_spec(Tc)],
        out_shape=[jax.ShapeDtypeStruct((B, T, D_A), bf16), jax.ShapeDtypeStruct((B, Tc, D_A), bf16)],
        scratch_shapes=[pltpu.VMEM((2 * HP, Dh, Dh + V7X_LANES), f32)],
        compiler_params=pltpu.CompilerParams(dimension_semantics=("parallel", "parallel"),
                                             vmem_limit_bytes=_vmem_limit(est)),
        name="mlstm",
    )(qx, kx, vx, colsx, rowsx, qc, kc, vc, colsc, rowsc)


def _fourier_w_kernel(cs_ref, fw_ref, o_ref, *, scale):
    o_ref[...] = (jnp.dot(cs_ref[...], fw_ref[...], preferred_element_type=f32,
                          precision=lax.Precision.HIGHEST) * scale).astype(bf16)


def _fourier_weights(fw, T):
    n = GROUP_B
    kk = np.outer(np.arange(n), np.arange(n)) % n
    ang = 2.0 * np.pi * kk / n
    cs = jnp.asarray(np.concatenate([np.cos(ang), np.sin(ang)], axis=0), f32)
    return pl.pallas_call(
        functools.partial(_fourier_w_kernel, scale=float(1.0 / np.sqrt(T * n))),
        grid=(N_GROUPS_B,),
        in_specs=[pl.BlockSpec((2 * n, n), lambda g: (0, 0)),
                  pl.BlockSpec((None, n, n), lambda g: (g, 0, 0))],
        out_specs=pl.BlockSpec((None, 2 * n, n), lambda g: (g, 0, 0)),
        out_shape=jax.ShapeDtypeStruct((N_GROUPS_B, 2 * n, n), bf16),
        compiler_params=pltpu.CompilerParams(dimension_semantics=("arbitrary",)),
        name="fourier_weights",
    )(cs, fw)


def _fft_tables():
    n1, n2 = FFT_N1, FFT_N2
    n = n1 * n2
    t1 = np.arange(n1)
    k1 = np.arange(n1)
    t2 = np.arange(n2)
    idx = (k1[None, :, None] * (n2 * t1[None, None, :] + t2[:, None, None])) % n
    ang = 2.0 * np.pi * idx / n
    m1 = np.concatenate([np.cos(ang), -np.sin(ang)], axis=1)
    k2 = np.arange(n2)
    ph = 2.0 * np.pi * (np.outer(k2, t2) % n2) / n2
    c, s = np.cos(ph), np.sin(ph)
    m2 = np.block([[c, s], [-s, c]])
    return jnp.asarray(m1, f32).astype(bf16), jnp.asarray(m2, f32).astype(bf16)


def _fourier_kernel(u_ref, z_ref, m1_ref, m2_ref, w_ref, o_ref, up_ref, yp_ref, xp_ref, pbuf_ref):
    n1, n2, P = FFT_N1, FFT_N2, FFT_PITCH
    nl = GROUP_B // V7X_LANES

    def fill(t1, carry):
        src = pl.multiple_of(t1 * n2, n2)
        dst = pl.multiple_of(t1 * P, V7X_SUBLANES)
        blk = u_ref[pl.ds(src, n2), :].astype(f32)
        for s in range(nl):
            up_ref[s, pl.ds(dst, n2), :] = blk[:, s * V7X_LANES:(s + 1) * V7X_LANES]
        return carry

    lax.fori_loop(0, n1, fill, 0, unroll=4)

    G = FFT_UNROLL

    def stage1(grp, carry):
        t2s = [grp * G + j for j in range(G)]
        rhs = [jnp.concatenate([up_ref[s, pl.ds(t2, n1, stride=P), :] for s in range(nl)],
                               axis=1).astype(bf16) for t2 in t2s]
        ys = [jnp.dot(m1_ref[t2], r, preferred_element_type=f32) for t2, r in zip(t2s, rhs)]
        for t2, y in zip(t2s, ys):
            dst = pl.multiple_of(t2 * P, V7X_SUBLANES)
            for ri in range(2):
                for s in range(nl):
                    yp_ref[ri * nl + s, pl.ds(dst, n1), :] = y[ri * n1:(ri + 1) * n1,
                                                               s * V7X_LANES:(s + 1) * V7X_LANES]
        return carry

    lax.fori_loop(0, n2 // G, stage1, 0)

    m2 = m2_ref[...]

    def stage2(grp, carry):
        k1s = [grp * G + j for j in range(G)]
        rhs = []
        for k1 in k1s:
            parts = [jnp.concatenate([yp_ref[ri * nl + s, pl.ds(k1, n2, stride=P), :] for s in range(nl)], axis=1)
                     for ri in range(2)]
            rhs.append(jnp.concatenate(parts, axis=0).astype(bf16))
        xs = [jnp.dot(m2, r, preferred_element_type=f32) for r in rhs]
        for k1, x in zip(k1s, xs):
            for ri in range(2):
                for s in range(nl):
                    xp_ref[ri * nl + s, pl.ds(k1, n2, stride=P), :] = x[ri * n2:(ri + 1) * n2,
                                                                        s * V7X_LANES:(s + 1) * V7X_LANES]
        return carry

    lax.fori_loop(0, n1 // G, stage2, 0)

    tr = pbuf_ref.shape[1]

    def matmul(t):
        blocks = []
        for k2 in range(t * (tr // n1), (t + 1) * (tr // n1)):
            blocks.append(jnp.concatenate([xp_ref[sl, k2 * P:k2 * P + n1, :] for sl in range(2 * nl)],
                                          axis=1).astype(bf16))
        return jnp.dot(jnp.concatenate(blocks, axis=0), w_ref[...], preferred_element_type=f32)

    def gate_store(t, y):
        sl = slice(t * tr, (t + 1) * tr)
        o_ref[sl, :] = y.astype(bf16) * _silu(z_ref[sl, :])

    _pipelined_matmuls((n1 * n2) // tr, matmul, gate_store, pbuf_ref)


def _fourier_latent(ub, zb, wcs):
    B, T, _ = ub.shape
    assert T == FFT_N1 * FFT_N2
    m1, m2 = _fft_tables()
    G, n = N_GROUPS_B, GROUP_B
    nl = n // V7X_LANES
    rows_p = FFT_N1 * FFT_PITCH
    blk = pl.BlockSpec((None, T, n), lambda b, g: (b, 0, g))
    est = (3 * 2 * T * n * 2 + 5 * nl * rows_p * V7X_LANES * 4 + T * 2 * n * 2 + 2 * m1.size * 2 + 8 * 512 * n * 4)
    return pl.pallas_call(
        _fourier_kernel,
        grid=(B, G),
        in_specs=[blk, blk,
                  pl.BlockSpec(m1.shape, lambda b, g: (0, 0, 0)),
                  pl.BlockSpec(m2.shape, lambda b, g: (0, 0)),
                  pl.BlockSpec((None, 2 * n, n), lambda b, g: (g, 0, 0))],
        out_specs=blk,
        out_shape=jax.ShapeDtypeStruct((B, T, D_B), bf16),
        scratch_shapes=[pltpu.VMEM((nl, rows_p, V7X_LANES), f32),
                        pltpu.VMEM((2 * nl, rows_p, V7X_LANES), f32),
                        pltpu.VMEM((2 * nl, rows_p, V7X_LANES), f32),
                        pltpu.VMEM((3, 512, n), f32)],
        compiler_params=pltpu.CompilerParams(dimension_semantics=("parallel", "parallel"),
                                             vmem_limit_bytes=_vmem_limit(est)),
        name="fourier_latent",
    )(ub, zb, m1, m2, wcs)


def _fourier_ctx_kernel(u_ref, z_ref, cs_ref, w_ref, o_ref):
    u = u_ref[...]
    x = jnp.dot(cs_ref[...], u, preferred_element_type=f32).astype(bf16)
    t = u.shape[0]
    w = w_ref[...]
    n = GROUP_B
    y = (jnp.dot(x[:t], w[:n], preferred_element_type=f32) + jnp.dot(x[t:], w[n:], preferred_element_type=f32))
    o_ref[...] = (y * _silu(z_ref[...].astype(f32))).astype(bf16)


def _fourier_ctx(ub, zb, wcs):
    B, T, _ = ub.shape
    ang = 2.0 * np.pi * (np.outer(np.arange(T), np.arange(T)) % T) / T
    cs = jnp.asarray(np.concatenate([np.cos(ang), -np.sin(ang)], axis=0), f32).astype(bf16)
    n = GROUP_B
    blk = pl.BlockSpec((None, T, n), lambda b, g: (b, 0, g))
    return pl.pallas_call(
        _fourier_ctx_kernel,
        grid=(B, N_GROUPS_B),
        in_specs=[blk, blk, pl.BlockSpec(cs.shape, lambda b, g: (0, 0)),
                  pl.BlockSpec((None, 2 * n, n), lambda b, g: (g, 0, 0))],
        out_specs=blk,
        out_shape=jax.ShapeDtypeStruct((B, T, D_B), bf16),
        compiler_params=pltpu.CompilerParams(dimension_semantics=("parallel", "parallel")),
        name="fourier_ctx",
    )(ub, zb, cs, wcs)


def _outproj_even_kernel(x_ref, h_ref, o_ref, za_ref, yb_ref, gate_ref, hg_ref, w_ref, out_ref):
    parts = []
    for hd in range(N_HEADS_A):
        sl = slice(hd * HEAD_DIM_A, (hd + 1) * HEAD_DIM_A)
        hh = _sigmoid(o_ref[:, sl]).astype(f32) * h_ref[:, sl]
        hh = hh * lax.rsqrt(jnp.mean(hh * hh, axis=-1, keepdims=True) + EPS)
        parts.append(hh.astype(bf16) * (hg_ref[:, sl].astype(bf16) * _silu(za_ref[:, sl])))
    ya = jnp.concatenate(parts, axis=1)
    acc = jnp.dot(ya, w_ref[:D_A, :], preferred_element_type=f32)
    acc = acc + jnp.dot(yb_ref[...], w_ref[D_A:, :], preferred_element_type=f32)
    out_ref[...] = x_ref[...] + gate_ref[...] * acc


def _outproj_even(x, h, o, za, yb, gate, head_g, wout, tm):
    B, T, D = x.shape
    row = lambda n: pl.BlockSpec((None, tm, n), lambda b, i: (b, i, 0))
    est = 2 * wout.size * 2 + 2 * tm * (2 * D * 4 + D_A * 4 + 3 * D_A * 2) + 8 * tm * D * 4
    return pl.pallas_call(
        _outproj_even_kernel,
        grid=(B, T // tm),
        in_specs=[row(D), row(D_A), row(D_A), row(D_A), row(D_B),
                  pl.BlockSpec((None, 1, D), lambda b, i: (b, 0, 0)),
                  pl.BlockSpec((1, D_A), lambda b, i: (0, 0)),
                  pl.BlockSpec(wout.shape, lambda b, i: (0, 0))],
        out_specs=row(D),
        out_shape=jax.ShapeDtypeStruct((B, T, D), f32),
        compiler_params=pltpu.CompilerParams(dimension_semantics=("parallel", "parallel"),
                                             vmem_limit_bytes=_vmem_limit(est)),
        name="outproj_even",
    )(x, h, o, za, yb, gate, head_g.reshape(1, D_A), wout)


def _inproj_odd_kernel(x_ref, sh_ref, sc_ref, g_ref, w_ref, u_ref, z_ref):
    amp = g_ref[...] * (1.0 + sc_ref[...])
    hx = _normed(x_ref[...], amp, sh_ref[...]).astype(bf16)
    cn = 512
    for j in range(D_INNER // cn):
        u_ref[:, j * cn:(j + 1) * cn] = jnp.dot(hx, w_ref[:, j * cn:(j + 1) * cn],
                                                preferred_element_type=f32).astype(bf16)
        z_ref[:, j * cn:(j + 1) * cn] = jnp.dot(hx, w_ref[:, D_INNER + j * cn:D_INNER + (j + 1) * cn],
                                                preferred_element_type=f32).astype(bf16)


def _inproj_odd(x, shift, scale, norm_g, w, tm):
    B, T, D = x.shape
    row = lambda n: pl.BlockSpec((None, tm, n), lambda b, i: (b, i, 0))
    est = 2 * w.size * 2 + 2 * tm * D * 4 + 4 * tm * D_INNER * 2 + 6 * tm * 512 * 4
    return pl.pallas_call(
        _inproj_odd_kernel,
        grid=(B, T // tm),
        in_specs=[row(D),
                  pl.BlockSpec((None, 1, D), lambda b, i: (b, 0, 0)),
                  pl.BlockSpec((None, 1, D), lambda b, i: (b, 0, 0)),
                  pl.BlockSpec((1, D), lambda b, i: (0, 0)),
                  pl.BlockSpec(w.shape, lambda b, i: (0, 0))],
        out_specs=[row(D_INNER), row(D_INNER)],
        out_shape=[jax.ShapeDtypeStruct((B, T, D_INNER), bf16)] * 2,
        compiler_params=pltpu.CompilerParams(dimension_semantics=("parallel", "parallel"),
                                             vmem_limit_bytes=_vmem_limit(est)),
        name="inproj_odd",
    )(x, shift, scale, norm_g.reshape(1, D), w)


POOL_UNROLL = 8


def _pool_tables():
    w_idx = np.arange(GRID_W)
    band = np.zeros((N_GROUPS_C, GRID_W, GRID_W), np.float32)
    inv_w = np.zeros((N_GROUPS_C, GRID_W, V7X_LANES), np.float32)
    for g, win in enumerate(POOL_WINDOWS):
        lo = np.clip(w_idx - win // 2, 0, GRID_W)
        hi = np.clip(w_idx + win - win // 2, 0, GRID_W)
        band[g] = (w_idx[None, :] >= lo[:, None]) & (w_idx[None, :] < hi[:, None])
        inv_w[g] = (1.0 / (hi - lo))[:, None]
    return jnp.asarray(band, bf16), jnp.asarray(inv_w, f32)


def _pool_kernel(u_ref, z_ref, band_ref, invw_ref, pw_ref, sc_ref, o_ref, ps_ref, pbuf_ref, *, rows):
    g = pl.program_id(1)
    W = GRID_W
    band = band_ref[...]
    lo_off = hi_off = 0
    for gi, win in enumerate(POOL_WINDOWS):
        lo_off = jnp.where(g == gi, win // 2, lo_off)
        hi_off = jnp.where(g == gi, win - win // 2, hi_off)

    ps_ref[0:W, :] = jnp.zeros((W, GROUP_C), f32)

    def width_sum(grp, carry):
        srcs = [pl.multiple_of((grp * POOL_UNROLL + j) * W, W) for j in range(POOL_UNROLL)]
        sums = [jnp.dot(band, u_ref[pl.ds(src, W), :], preferred_element_type=f32) for src in srcs]
        acc = ps_ref[pl.ds(srcs[0], W), :]
        for src, s in zip(srcs, sums):
            acc = acc + s
            ps_ref[pl.ds(src + W, W), :] = acc
        return carry

    lax.fori_loop(0, rows // POOL_UNROLL, width_sum, 0)

    inv_w = invw_ref[...]

    def pooled_minus_self(r):
        lo = jnp.maximum(r - lo_off, 0)
        hi = jnp.minimum(r + hi_off, rows)
        acc = (ps_ref[pl.ds(pl.multiple_of(hi * W, W), W), :]
               - ps_ref[pl.ds(pl.multiple_of(lo * W, W), W), :])
        inv = inv_w / (hi - lo).astype(f32)
        inv = jnp.concatenate([inv] * (GROUP_C // V7X_LANES), axis=1)
        ug = u_ref[r * W:(r + 1) * W, :].astype(f32)
        return (acc * inv - ug).astype(bf16)

    tr = pbuf_ref.shape[1]
    pws = (pw_ref[...] * sc_ref[...]).astype(bf16)

    def matmul(t):
        lhs = jnp.concatenate([pooled_minus_self(t * (tr // W) + j) for j in range(tr // W)], axis=0)
        return jnp.dot(lhs, pws, preferred_element_type=f32)

    def gate_store(t, y):
        sl = slice(t * tr, (t + 1) * tr)
        o_ref[sl, :] = y.astype(bf16) * _silu(z_ref[sl, :])

    _pipelined_matmuls((rows * W) // tr, matmul, gate_store, pbuf_ref)


def _pool_mix(u, z, pool_w, scale):
    B, T, _ = u.shape
    rows = T // GRID_W
    band, inv_w = _pool_tables()
    n = GROUP_C
    blk = pl.BlockSpec((None, T, n), lambda b, g: (b, 0, g))
    est = 3 * 2 * T * n * 2 + (rows + 1) * GRID_W * n * 4 + T * n * 2 + 2 * n * n * 4 + 8 * 512 * n * 4
    return pl.pallas_call(
        functools.partial(_pool_kernel, rows=rows),
        grid=(B, N_GROUPS_C),
        in_specs=[blk, blk,
                  pl.BlockSpec((None, GRID_W, GRID_W), lambda b, g: (g, 0, 0)),
                  pl.BlockSpec((None, GRID_W, V7X_LANES), lambda b, g: (g, 0, 0)),
                  pl.BlockSpec((None, n, n), lambda b, g: (g, 0, 0)),
                  pl.BlockSpec((None, 1, n), lambda b, g: (g, 0, 0))],
        out_specs=blk,
        out_shape=jax.ShapeDtypeStruct((B, T, D_INNER), bf16),
        scratch_shapes=[pltpu.VMEM(((rows + 1) * GRID_W, n), f32),
                        pltpu.VMEM((3, 512, n), f32)],
        compiler_params=pltpu.CompilerParams(dimension_semantics=("parallel", "arbitrary"),
                                             vmem_limit_bytes=_vmem_limit(est)),
        name="pool_mix",
    )(u, z, band, inv_w, pool_w, scale.reshape(N_GROUPS_C, 1, n))


def _outproj_odd_kernel(x_ref, y_ref, gate_ref, fg_ref, w_ref, out_ref):
    acc = jnp.dot(y_ref[...], w_ref[...], preferred_element_type=f32)
    x = x_ref[...] + gate_ref[...] * acc
    out_ref[...] = x * lax.rsqrt(jnp.mean(x * x, axis=-1, keepdims=True) + EPS) * fg_ref[...]


def _outproj_odd(x, y, gate, final_g, wout, tm):
    B, T, D = x.shape
    row = lambda n: pl.BlockSpec((None, tm, n), lambda b, i: (b, i, 0))
    est = 2 * wout.size * 2 + 2 * tm * (2 * D * 4 + D_INNER * 2) + 6 * tm * D * 4
    return pl.pallas_call(
        _outproj_odd_kernel,
        grid=(B, T // tm),
        in_specs=[row(D), row(D_INNER),
                  pl.BlockSpec((None, 1, D), lambda b, i: (b, 0, 0)),
                  pl.BlockSpec((1, D), lambda b, i: (0, 0)),
                  pl.BlockSpec(wout.shape, lambda b, i: (0, 0))],
        out_specs=row(D),
        out_shape=jax.ShapeDtypeStruct((B, T, D), f32),
        compiler_params=pltpu.CompilerParams(dimension_semantics=("parallel", "parallel"),
                                             vmem_limit_bytes=_vmem_limit(est)),
        name="outproj_odd",
    )(x, y, gate, final_g.reshape(1, D), wout)


def kernel(x, c, ctx, c_ctx, ada_w, ada_b, norm_g, win_even, gate_b_even, conv_qk_even, head_norm_even,
           fourier_w_even, wout_even, win_odd, pool_w_odd, pool_scale_odd, wout_odd, final_g):
    B, T, D = x.shape
    Tc = ctx.shape[1]
    H = N_HEADS_A
    L = MLSTM_CHUNK

    nrow = -(-(B + 1) // V7X_SUBLANES) * V7X_SUBLANES
    rows_in = jnp.zeros((nrow, D), f32).at[:B].set(c).at[B].set(c_ctx)
    mod = _modulation(rows_in, ada_w, ada_b)

    def mod_parts(l, r0, r1, n):
        m = mod[l, r0:r1]
        parts = [jnp.broadcast_to(m[:, None, i * D:(i + 1) * D], (n, 1, D)) for i in range(3)]
        return parts

    we = win_even[0]
    w_main = we.astype(bf16)
    gcols = we[:, W_MAIN_EVEN:].reshape(D, 4, H)
    gbias = gate_b_even[0].reshape(4, H)
    ig_w = gcols[:, 0::2, :].transpose(0, 2, 1).reshape(D, GATE_SLOTS)
    fg_w = gcols[:, 1::2, :].transpose(0, 2, 1).reshape(D, GATE_SLOTS)
    ig_b = gbias[0::2, :].T.reshape(GATE_SLOTS)
    fg_b = gbias[1::2, :].T.reshape(GATE_SLOTS)
    wgt =jnp.concatenate([ig_w, fg_w], axis=1).T.astype(bf16)
    gbt = jnp.concatenate([ig_b, fg_b]).reshape(N_GATES, 1)
    conv_w = conv_qk_even[0]

    shift_x, scale_x, gate_x = mod_parts(0, 0, B, B)
    shift_c, scale_c, gate_c = mod_parts(0, B, B + 1, B)

    qx, kx, vx, ox, zax, ubx, zbx, gtx = _inproj_even(
        x, shift_x, scale_x, norm_g[0], w_main, wgt, gbt, conv_w, tm=512)
    qc, kc, vc, oc, zac, ubc, zbc, gtc = _inproj_even(
        ctx, shift_c, scale_c, norm_g[0], w_main, wgt, gbt, conv_w, tm=Tc)

    colsx, rowsx = _gate_prep(gtx, L)
    colsc, rowsc = _gate_prep(gtc, L)
    h_x, h_c = _mlstm(qx, kx, vx, colsx, rowsx, qc, kc, vc, colsc, rowsc, L)

    wcs = _fourier_weights(fourier_w_even[0], T)
    yb_x = _fourier_latent(ubx, zbx, wcs)
    wout_e = wout_even[0].astype(bf16)
    x1 = _outproj_even(x, h_x, ox, zax, yb_x, gate_x, head_norm_even[0], wout_e, tm=1024)

    wcs_c = _fourier_weights(fourier_w_even[0], Tc)
    yb_c = _fourier_ctx(ubc, zbc, wcs_c)
    ctx1 = _outproj_even(ctx, h_c, oc, zac, yb_c, gate_c, head_norm_even[0], wout_e, tm=Tc)
    del ctx1

    shift_x, scale_x, gate_x = mod_parts(1, 0, B, B)
    u1, z1 = _inproj_odd(x1, shift_x, scale_x, norm_g[1], win_odd[0].astype(bf16), tm=1024)
    y1 = _pool_mix(u1, z1, pool_w_odd[0], pool_scale_odd[0])
    return _outproj_odd(x1, y1, gate_x, final_g, wout_odd[0].astype(bf16), tm=1024)
```

```python
import functools

import numpy as np
import jax
import jax.numpy as jnp
from jax import lax
from jax.experimental import pallas as pl
from jax.experimental.pallas import tpu as pltpu

D_MODEL = 1024
DEPTH = 2
CTX_LEN = 256
GRID_W = 64
D_INNER = 2 * D_MODEL
D_A = D_INNER // 2
D_B = D_INNER - D_A
N_HEADS_A = 4
HEAD_DIM_A = D_A // N_HEADS_A
N_GROUPS_B = 4
GROUP_B = D_B // N_GROUPS_B
N_GROUPS_C = 4
GROUP_C = D_INNER // N_GROUPS_C
POOL_WINDOWS = (2, 4, 8, 16)
CONV_W = 3
N_GATES = 4 * N_HEADS_A
W_MAIN_EVEN = 5 * D_A + 2 * D_B
EPS = 1e-6

f32 = jnp.float32
bf16 = jnp.bfloat16

V7X_VMEM_BYTES = 64 * 1024 * 1024
V7X_LANES = 128
V7X_SUBLANES = 8

MLSTM_CHUNK = 256
MLSTM_HEADS_PER_STEP = 2
FFT_N1 = 64
FFT_N2 = 64
FFT_PITCH = 72
FFT_UNROLL = 8
NEG_BIG = -1e30


def _vmem_limit(nbytes):
    return int(min(max(nbytes * 5 // 4 + (4 << 20), 16 << 20), V7X_VMEM_BYTES - (6 << 20)))


def _sigmoid(v):
    return 0.5 * jnp.tanh(0.5 * v) + 0.5


def _silu(v):
    return v * _sigmoid(v)


def _pipelined_matmuls(n, matmul, epilogue, pbuf_ref):
    zero = jnp.minimum(pl.program_id(0), 0)
    nbuf = pbuf_ref.shape[0]
    pbuf_ref[zero] = matmul(0)
    for t in range(n):
        if t + 1 < n:
            pbuf_ref[zero + (t + 1) % nbuf] = matmul(t + 1)
        epilogue(t, pbuf_ref[zero + t % nbuf])


def _log_sigmoid(v):
    return jnp.minimum(v, 0.0) - jnp.log1p(jnp.exp(-jnp.abs(v)))


def _mod_kernel(r_ref, w_ref, b_ref, o_ref):
    s = _silu(r_ref[...])
    o_ref[...] = jnp.dot(s, w_ref[...], preferred_element_type=f32,
                         precision=lax.Precision.HIGHEST) + b_ref[...]


def _modulation(rows, ada_w, ada_b):
    nrow = rows.shape[0]
    tn = 1024
    return pl.pallas_call(
        _mod_kernel,
        grid=(DEPTH, 3 * D_MODEL // tn),
        in_specs=[
            pl.BlockSpec((nrow, D_MODEL), lambda l, j: (0, 0)),
            pl.BlockSpec((None, D_MODEL, tn), lambda l, j: (l, 0, j)),
            pl.BlockSpec((None, 1, tn), lambda l, j: (l, 0, j)),
        ],
        out_specs=pl.BlockSpec((None, nrow, tn), lambda l, j: (l, 0, j)),
        out_shape=jax.ShapeDtypeStruct((DEPTH, nrow, 3 * D_MODEL), f32),
        compiler_params=pltpu.CompilerParams(dimension_semantics=("arbitrary", "arbitrary")),
        name="modulation",
    )(rows, ada_w, ada_b.reshape(DEPTH, 1, 3 * D_MODEL))


def _normed(x, amp, shift):
    ms = jnp.mean(x * x, axis=-1, keepdims=True)
    return (x * lax.rsqrt(ms + EPS)) * amp + shift


def _inproj_even_kernel(x_ref, xp_ref, xn_ref, sh_ref, sc_ref, g_ref, w_ref, wgt_ref, gbt_ref,
                        cw_ref, q_ref, k_ref, v_ref, o_ref, za_ref, ub_ref, zb_ref, gt_ref, pbuf_ref, *, tm, nt):
    i = pl.program_id(1)
    amp = g_ref[...] * (1.0 + sc_ref[...])
    shift = sh_ref[...]
    hx = _normed(x_ref[...], amp, shift).astype(bf16)
    halo = jnp.concatenate([xp_ref[...], xn_ref[...]], axis=0)
    hh = _normed(halo, amp, shift).astype(bf16)
    has_prev = (i > 0).astype(f32)
    has_next = (i < nt - 1).astype(f32)
    row = lax.broadcasted_iota(jnp.int32, (tm, 1), 0)
    cn = 512
    ph = jnp.dot(hh, w_ref[:, :2 * D_A], preferred_element_type=f32)
    prev = ph[V7X_SUBLANES - 1:V7X_SUBLANES, :] * has_prev
    nxt = ph[V7X_SUBLANES:V7X_SUBLANES + 1, :] * has_next

    def conv_store(j, p):
        cols = slice(j * cn, (j + 1) * cn)
        up = jnp.where(row == 0, prev[:, cols], pltpu.roll(p, 1, 0))
        dn = jnp.where(row == tm - 1, nxt[:, cols], pltpu.roll(p, tm - 1, 0))
        cw = cw_ref[:, cols]
        y = _silu(cw[0:1, :] * up + cw[1:2, :] * p + cw[2:3, :] * dn)
        if j < D_A // cn:
            q_ref[:, cols] = (y * (HEAD_DIM_A ** -0.5)).astype(bf16)
        else:
            jj = j - D_A // cn
            k_ref[:, jj * cn:(jj + 1) * cn] = y.astype(bf16)

    def plain_store(ref, jj):
        def store(p):
            ref[:, jj * cn:(jj + 1) * cn] = p.astype(bf16)
        return store

    tasks = [(j * cn, functools.partial(conv_store, j)) for j in range(2 * D_A // cn)]
    for idx, ref in enumerate((v_ref, o_ref, za_ref, ub_ref, zb_ref)):
        for jj in range(D_A // cn):
            tasks.append((2 * D_A + idx * D_A + jj * cn, plain_store(ref, jj)))

    def matmul(t):
        c0 = tasks[t][0]
        return jnp.dot(hx, w_ref[:, c0:c0 + cn], preferred_element_type=f32)

    _pipelined_matmuls(len(tasks), matmul, lambda t, p: tasks[t][1](p), pbuf_ref)
    gt_ref[...] = lax.dot_general(wgt_ref[...], hx, (((1,), (1,)), ((), ())),
                                  preferred_element_type=f32) + gbt_ref[...]


def _inproj_even(x, shift, scale, norm_g, w_main, wgt, gbt, conv_w, tm):
    B, T, D = x.shape
    nt = T // tm
    hb = tm // V7X_SUBLANES
    nhb = T // V7X_SUBLANES
    row_spec = pl.BlockSpec((None, tm, D_A), lambda b, i: (b, i, 0))
    vec = lambda n: pl.BlockSpec((1, n), lambda b, i: (0, 0))
    est = (2 * w_main.size * 2 + 2 * tm * D * 4 + 7 * 2 * tm * D_A * 2 + 6 * tm * 512 * 4)
    outs = pl.pallas_call(
        functools.partial(_inproj_even_kernel, tm=tm, nt=nt),
        grid=(B, nt),
        in_specs=[
            pl.BlockSpec((None, tm, D), lambda b, i: (b, i, 0)),
            pl.BlockSpec((None, V7X_SUBLANES, D), lambda b, i: (b, jnp.maximum(i * hb - 1, 0), 0)),
            pl.BlockSpec((None, V7X_SUBLANES, D), lambda b, i: (b, jnp.minimum((i + 1) * hb, nhb - 1), 0)),
            pl.BlockSpec((None, 1, D), lambda b, i: (b, 0, 0)),
            pl.BlockSpec((None, 1, D), lambda b, i: (b, 0, 0)),
            vec(D),
            pl.BlockSpec(w_main.shape, lambda b, i: (0, 0)),
            pl.BlockSpec(wgt.shape, lambda b, i: (0, 0)),
            pl.BlockSpec((N_GATES, 1), lambda b, i: (0, 0)),
            pl.BlockSpec(conv_w.shape, lambda b, i: (0, 0)),
        ],
        out_specs=[row_spec] * 7 + [pl.BlockSpec((None, N_GATES, tm), lambda b, i: (b, 0, i))],
        out_shape=[jax.ShapeDtypeStruct((B, T, D_A), bf16)] * 7 + [jax.ShapeDtypeStruct((B, N_GATES, T), f32)],
        scratch_shapes=[pltpu.VMEM((3, tm, 512), f32)],
        compiler_params=pltpu.CompilerParams(dimension_semantics=("parallel", "arbitrary"),
                                             vmem_limit_bytes=_vmem_limit(est)),
        name="inproj_even",
    )(x, x, x, shift, scale, norm_g.reshape(1, D), w_main, wgt, gbt, conv_w)
    return outs


GATE_SLOTS = 2 * N_HEADS_A
GATE_PIECES = 3
GATE_QUANTS = 3
assert GATE_SLOTS & (GATE_SLOTS - 1) == 0 and GATE_QUANTS * GATE_PIECES * GATE_SLOTS <= V7X_LANES


def _scan_max_lanes(x, seg, reverse):
    n = x.shape[1]
    pos = lax.broadcasted_iota(jnp.int32, x.shape, 1) & (seg - 1)
    s = 1
    while s < seg:
        if reverse:
            x = jnp.where(pos < seg - s, jnp.maximum(x, pltpu.roll(x, n - s, 1)), x)
        else:
            x = jnp.where(pos >= s, jnp.maximum(x, pltpu.roll(x, s, 1)), x)
        s *= 2
    return x


def _split3(v):
    hi = v.astype(bf16)
    r1 = v - hi.astype(f32)
    mid = r1.astype(bf16)
    lo = (r1 - mid.astype(f32)).astype(bf16)
    return hi, mid, lo


def _gate_prep_kernel(gt_ref, cols_ref, arow_ref, *, L, nchunk):
    r = lax.broadcasted_iota(jnp.int32, (L, L), 0)
    c = lax.broadcasted_iota(jnp.int32, (L, L), 1)
    tri_l = (c <= r).astype(bf16)
    tri_u = (c >= r).astype(bf16)
    S = V7X_SUBLANES
    tb = nchunk * L
    fwd = (lax.broadcasted_iota(jnp.int32, (S, tb), 0) & 1) == 0
    ig = gt_ref[:S, :]
    lf = _log_sigmoid(gt_ref[S:, :])
    pieces = jnp.concatenate([p.astype(f32) for p in _split3(lf)] + [jnp.zeros((S, tb), f32)],
                             axis=0).astype(bf16)
    chunks = [slice(ci * L, (ci + 1) * L) for ci in range(nchunk)]
    pre = jnp.concatenate([jnp.dot(pieces[:, sl], tri_u, preferred_element_type=f32) for sl in chunks], axis=1)
    suf = jnp.concatenate([jnp.dot(pieces[:, sl], tri_l, preferred_element_type=f32) for sl in chunks], axis=1)
    pre = pre[:S] + pre[S:2 * S] + pre[2 * S:3 * S]
    suf = suf[:S] + suf[S:2 * S] + suf[2 * S:3 * S]
    b_row = jnp.where(fwd, pre, suf)
    a_row = ig - b_row
    cmax = jnp.where(fwd, _scan_max_lanes(a_row, L, False), _scan_max_lanes(a_row, L, True))
    arow_ref[...] = a_row
    parts = [p.astype(f32) for quant in (b_row, a_row, cmax) for p in _split3(quant)]
    fill = jnp.zeros((V7X_LANES - len(parts) * S, tb), f32)
    packed = jnp.concatenate(parts + [fill], axis=0)
    for sl in chunks:
        cols_ref[sl, :] = packed[:, sl].T.astype(bf16)


def _gate_prep(gt, L):
    B, _, T = gt.shape
    nchunk = min(4, T // L)
    tb = nchunk * L
    cols, arow = pl.pallas_call(
        functools.partial(_gate_prep_kernel, L=L, nchunk=nchunk),
        grid=(B, T // tb),
        in_specs=[pl.BlockSpec((None, N_GATES, tb), lambda b, i: (b, 0, i))],
        out_specs=[
            pl.BlockSpec((None, tb, V7X_LANES), lambda b, i: (b, i, 0)),
            pl.BlockSpec((None, V7X_SUBLANES, tb), lambda b, i: (b, 0, i)),
        ],
        out_shape=[
            jax.ShapeDtypeStruct((B, T, V7X_LANES), bf16),
            jax.ShapeDtypeStruct((B, V7X_SUBLANES, T), f32),
        ],
        compiler_params=pltpu.CompilerParams(dimension_semantics=("parallel", "parallel")),
        name="gate_prep",
    )(gt)
    return cols, arow.reshape(B, N_HEADS_A, 2, T)


def _mlstm_kernel(qx_ref, kx_ref, vx_ref, cx_ref, rx_ref, qc_ref, kc_ref, vc_ref, cc_ref, rc_ref,
                  hx_ref, hc_ref, c_ref, n_ref, *, L, nx, nc):
    Dh, LN, HP = HEAD_DIM_A, V7X_LANES, MLSTM_HEADS_PER_STEP
    head0 = pl.program_id(1) * HP
    chains = [(hh, d) for hh in range(HP) for d in range(2)]
    r_i = lax.broadcasted_iota(jnp.int32, (L, L), 0)
    c_i = lax.broadcasted_iota(jnp.int32, (L, L), 1)
    masks = (c_i <= r_i, c_i >= r_i)

    sr = lax.broadcasted_iota(jnp.int32, (LN, GATE_QUANTS * LN), 0)
    sc = lax.broadcasted_iota(jnp.int32, (LN, GATE_QUANTS * LN), 1)
    span = GATE_PIECES * GATE_SLOTS
    in_block = None
    for qi in range(GATE_QUANTS):
        blk = (sr >= qi * span) & (sr < (qi + 1) * span) & (sc >= qi * LN) & (sc < (qi + 1) * LN)
        in_block = blk if in_block is None else in_block | blk
    slot = sr & (GATE_SLOTS - 1)
    sels = {(hh, d): (in_block & (slot == 2 * (head0 + hh) + d)).astype(bf16) for hh, d in chains}

    def tile(v, width):
        return jnp.concatenate([v] * (width // LN), axis=1)

    def step(refs, r0s, ms):
        q_ref, k_ref, v_ref, col_ref, row_ref = refs
        st = []
        for ci, (hh, d) in enumerate(chains):
            rows = pl.ds(r0s[ci], L)
            hsl = slice(hh * Dh, (hh + 1) * Dh)
            q = q_ref[rows, hsl]
            k = k_ref[rows, hsl]
            rep = jnp.dot(col_ref[rows, :], sels[hh, d], preferred_element_type=f32)
            qk = lax.dot_general(q, k, (((1,), (1,)), ((), ())), preferred_element_type=f32)
            qc = jnp.dot(q, c_ref[ci].astype(bf16), preferred_element_type=f32)
            st.append(dict(q=q, k=k, v=v_ref[rows, hsl], rep=rep, qk=qk, qc=qc, a_row=row_ref[hh, d:d + 1, rows]))
        new_ms = []
        for ci, (hh, d) in enumerate(chains):
            c, m_prev = st[ci], ms[ci]
            b_rep, a_rep = c["rep"][:, :LN], c["rep"][:, LN:2 * LN]
            g_rep = jnp.maximum(c["rep"][:, 2 * LN:], m_prev)
            p = jnp.exp(jnp.where(masks[d], c["a_row"] - tile(g_rep, L), NEG_BIG))
            s = c["qk"] * p
            c["s"] = s.astype(bf16)
            inter = jnp.exp(m_prev - g_rep)
            qn = jnp.sum(c["q"].astype(f32) * n_ref[ci], axis=1, keepdims=True)
            den = jnp.sum(s, axis=1, keepdims=True) + inter[:, :1] * qn
            floor = jnp.exp(-(b_rep + g_rep))
            rcp = 1.0 / jnp.maximum(jnp.abs(den), floor[:, :1])
            c["rcp"] = jnp.broadcast_to(rcp, (L, LN))
            c["inter"] = inter
            b_end = b_rep[L - 1:L, :] if d == 0 else b_rep[0:1, :]
            w = b_end + a_rep
            m_new = jnp.maximum(b_end + m_prev, jnp.max(w, axis=0, keepdims=True))
            c["decay"] = jnp.exp(b_end + m_prev - m_new)
            kw = c["k"].astype(f32) * tile(jnp.exp(w - m_new), Dh)
            c["kw"] = kw.astype(bf16)
            c["ksum"] = jnp.sum(kw, axis=0, keepdims=True)
            new_ms.append(m_new)
        for c in st:
            c["sv"] = jnp.dot(c["s"], c["v"], preferred_element_type=f32)
            c["upd"] = lax.dot_general(c["kw"], c["v"], (((0,), (0,)), ((), ())), preferred_element_type=f32)
        hs = []
        for ci, c in enumerate(st):
            hs.append((c["sv"] + tile(c["inter"], Dh) * c["qc"]) * tile(c["rcp"], Dh))
            decay = tile(c["decay"], Dh)
            c_ref[ci] = decay * c_ref[ci] + c["upd"]
            n_ref[ci] = decay * n_ref[ci] + c["ksum"]
        return hs, new_ms

    ctx = (qc_ref, kc_ref, vc_ref, cc_ref, rc_ref)
    lat = (qx_ref, kx_ref, vx_ref, cx_ref, rx_ref)

    c_ref[...] = jnp.zeros_like(c_ref)
    n_ref[...] = jnp.zeros_like(n_ref)
    ms = [jnp.zeros((1, LN), f32) for _ in chains]
    written = set()
    for j in range(nc):
        cjs = [j if d == 0 else nc - 1 - j for _, d in chains]
        hs, ms = step(ctx, [cj * L for cj in cjs], ms)
        for (hh, d), cj, h in zip(chains, cjs, hs):
            dst = (slice(cj * L, (cj + 1) * L), slice(hh * Dh, (hh + 1) * Dh))
            if (cj, hh) in written:
                hc_ref[dst] = (hc_ref[dst].astype(f32) + h).astype(hc_ref.dtype)
            else:
                hc_ref[dst] = h.astype(hc_ref.dtype)
                written.add((cj, hh))

    def make_body(accumulate):
        def body(i, ms):
            r0s = [pl.multiple_of((i if d == 0 else nx - 1 - i) * L, L) for _, d in chains]
            hs, ms = step(lat, r0s, list(ms))
            for (hh, d), r0, h in zip(chains, r0s, hs):
                dst = (pl.ds(r0, L), slice(hh * Dh, (hh + 1) * Dh))
                if accumulate:
                    hx_ref[dst] = (hx_ref[dst].astype(f32) + h).astype(hx_ref.dtype)
                else:
                    hx_ref[dst] = h.astype(hx_ref.dtype)
            return tuple(ms)
        return body

    ms = lax.fori_loop(0, nx // 2, make_body(False), tuple(ms))
    lax.fori_loop(nx // 2, nx, make_body(True), ms)


def _mlstm(qx, kx, vx, colsx, rowsx, qc, kc, vc, colsc, rowsc, L):
    B, T, _ = qx.shape
    Tc = qc.shape[1]
    H, Dh, HP = N_HEADS_A, HEAD_DIM_A, MLSTM_HEADS_PER_STEP
    assert T % (2 * L) == 0 and Tc % L == 0 and H % HP == 0

    def seq_spec(t):
        return pl.BlockSpec((None, t, HP * Dh), lambda b, h: (b, 0, h))

    def col_spec(t):
        return pl.BlockSpec((None, t, V7X_LANES), lambda b, h: (b, 0, 0))

    def row_spec(t):
        return pl.BlockSpec((None, HP, 2, t), lambda b, h: (b, h, 0, 0))

    est = (2 * HP * (3 * (T + Tc) * Dh * 2 + 8 * (T + Tc) * 4 + (T + Tc) * Dh * 2)
           + 2 * (T + Tc) * V7X_LANES * 2 + 2 * HP * Dh * (Dh + V7X_LANES) * 4 + 16 * L * L * 4)
    return pl.pallas_call(
        functools.partial(_mlstm_kernel, L=L, nx=T // L, nc=Tc // L),
        grid=(B, H // HP),
        in_specs=[seq_spec(T), seq_spec(T), seq_spec(T), col_spec(T), row_spec(T),
                  seq_spec(Tc), seq_spec(Tc), seq_spec(Tc), col_spec(Tc), row_spec(Tc)],
        out_specs=[seq_spec(T), seq_spec(Tc)],
        out_shape=[jax.ShapeDtypeStruct((B, T, D_A), bf16), jax.ShapeDtypeStruct((B, Tc, D_A), bf16)],
        scratch_shapes=[pltpu.VMEM((2 * HP, Dh, Dh), f32), pltpu.VMEM((2 * HP, 1, Dh), f32)],
        compiler_params=pltpu.CompilerParams(dimension_semantics=("parallel", "parallel"),
                                             vmem_limit_bytes=_vmem_limit(est)),
        name="mlstm",
    )(qx, kx, vx, colsx, rowsx, qc, kc, vc, colsc, rowsc)


def _fourier_w_kernel(cs_ref, fw_ref, o_ref, *, scale):
    o_ref[...] = (jnp.dot(cs_ref[...], fw_ref[...], preferred_element_type=f32,
                          precision=lax.Precision.HIGHEST) * scale).astype(bf16)


def _fourier_weights(fw, T):
    n = GROUP_B
    kk = np.outer(np.arange(n), np.arange(n)) % n
    ang = 2.0 * np.pi * kk / n
    cs = jnp.asarray(np.concatenate([np.cos(ang), np.sin(ang)], axis=0), f32)
    return pl.pallas_call(
        functools.partial(_fourier_w_kernel, scale=float(1.0 / np.sqrt(T * n))),
        grid=(N_GROUPS_B,),
        in_specs=[pl.BlockSpec((2 * n, n), lambda g: (0, 0)),
                  pl.BlockSpec((None, n, n), lambda g: (g, 0, 0))],
        out_specs=pl.BlockSpec((None, 2 * n, n), lambda g: (g, 0, 0)),
        out_shape=jax.ShapeDtypeStruct((N_GROUPS_B, 2 * n, n), bf16),
        compiler_params=pltpu.CompilerParams(dimension_semantics=("arbitrary",)),
        name="fourier_weights",
    )(cs, fw)


def _fft_tables():
    n1, n2 = FFT_N1, FFT_N2
    n = n1 * n2
    t1 = np.arange(n1)
    k1 = np.arange(n1)
    t2 = np.arange(n2)
    idx = (k1[None, :, None] * (n2 * t1[None, None, :] + t2[:, None, None])) % n
    ang = 2.0 * np.pi * idx / n
    m1 = np.concatenate([np.cos(ang), -np.sin(ang)], axis=1)
    k2 = np.arange(n2)
    ph = 2.0 * np.pi * (np.outer(k2, t2) % n2) / n2
    c, s = np.cos(ph), np.sin(ph)
    m2 = np.block([[c, s], [-s, c]])
    return jnp.asarray(m1, f32).astype(bf16), jnp.asarray(m2, f32).astype(bf16)


def _fourier_kernel(u_ref, z_ref, m1_ref, m2_ref, w_ref, o_ref, up_ref, yp_ref, xp_ref, pbuf_ref):
    n1, n2, P = FFT_N1, FFT_N2, FFT_PITCH
    nl = GROUP_B // V7X_LANES

    def fill(t1, carry):
        src = pl.multiple_of(t1 * n2, n2)
        dst = pl.multiple_of(t1 * P, V7X_SUBLANES)
        blk = u_ref[pl.ds(src, n2), :].astype(f32)
        for s in range(nl):
            up_ref[s, pl.ds(dst, n2), :] = blk[:, s * V7X_LANES:(s + 1) * V7X_LANES]
        return carry

    lax.fori_loop(0, n1, fill, 0, unroll=4)

    G = FFT_UNROLL

    def stage1(grp, carry):
        t2s = [grp * G + j for j in range(G)]
        rhs = [jnp.concatenate([up_ref[s, pl.ds(t2, n1, stride=P), :] for s in range(nl)],
                               axis=1).astype(bf16) for t2 in t2s]
        ys = [jnp.dot(m1_ref[t2], r, preferred_element_type=f32) for t2, r in zip(t2s, rhs)]
        for t2, y in zip(t2s, ys):
            dst = pl.multiple_of(t2 * P, V7X_SUBLANES)
            for ri in range(2):
                for s in range(nl):
                    yp_ref[ri * nl + s, pl.ds(dst, n1), :] = y[ri * n1:(ri + 1) * n1,
                                                               s * V7X_LANES:(s + 1) * V7X_LANES]
        return carry

    lax.fori_loop(0, n2 // G, stage1, 0)

    m2 = m2_ref[...]

    def stage2(grp, carry):
        k1s = [grp * G + j for j in range(G)]
        rhs = []
        for k1 in k1s:
            parts = [jnp.concatenate([yp_ref[ri * nl + s, pl.ds(k1, n2, stride=P), :] for s in range(nl)], axis=1)
                     for ri in range(2)]
            rhs.append(jnp.concatenate(parts, axis=0).astype(bf16))
        xs = [jnp.dot(m2, r, preferred_element_type=f32) for r in rhs]
        for k1, x in zip(k1s, xs):
            for ri in range(2):
                for s in range(nl):
                    xp_ref[ri * nl + s, pl.ds(k1, n2, stride=P), :] = x[ri * n2:(ri + 1) * n2,
                                                                        s * V7X_LANES:(s + 1) * V7X_LANES]
        return carry

    lax.fori_loop(0, n1 // G, stage2, 0)

    tr = pbuf_ref.shape[1]

    def matmul(t):
        blocks = []
        for k2 in range(t * (tr // n1), (t + 1) * (tr // n1)):
            blocks.append(jnp.concatenate([xp_ref[sl, k2 * P:k2 * P + n1, :] for sl in range(2 * nl)],
                                          axis=1).astype(bf16))
        return jnp.dot(jnp.concatenate(blocks, axis=0), w_ref[...], preferred_element_type=f32)

    def gate_store(t, y):
        sl = slice(t * tr, (t + 1) * tr)
        o_ref[sl, :] = y.astype(bf16) * _silu(z_ref[sl, :])

    _pipelined_matmuls((n1 * n2) // tr, matmul, gate_store, pbuf_ref)


def _fourier_latent(ub, zb, wcs):
    B, T, _ = ub.shape
    assert T == FFT_N1 * FFT_N2
    m1, m2 = _fft_tables()
    G, n = N_GROUPS_B, GROUP_B
    nl = n // V7X_LANES
    rows_p = FFT_N1 * FFT_PITCH
    blk = pl.BlockSpec((None, T, n), lambda b, g: (b, 0, g))
    est = (3 * 2 * T * n * 2 + 5 * nl * rows_p * V7X_LANES * 4 + T * 2 * n * 2 + 2 * m1.size * 2 + 8 * 512 * n * 4)
    return pl.pallas_call(
        _fourier_kernel,
        grid=(B, G),
        in_specs=[blk, blk,
                  pl.BlockSpec(m1.shape, lambda b, g: (0, 0, 0)),
                  pl.BlockSpec(m2.shape, lambda b, g: (0, 0)),
                  pl.BlockSpec((None, 2 * n, n), lambda b, g: (g, 0, 0))],
        out_specs=blk,
        out_shape=jax.ShapeDtypeStruct((B, T, D_B), bf16),
        scratch_shapes=[pltpu.VMEM((nl, rows_p, V7X_LANES), f32),
                        pltpu.VMEM((2 * nl, rows_p, V7X_LANES), f32),
                        pltpu.VMEM((2 * nl, rows_p, V7X_LANES), f32),
                        pltpu.VMEM((3, 512, n), f32)],
        compiler_params=pltpu.CompilerParams(dimension_semantics=("parallel", "parallel"),
                                             vmem_limit_bytes=_vmem_limit(est)),
        name="fourier_latent",
    )(ub, zb, m1, m2, wcs)


def _fourier_ctx_kernel(u_ref, z_ref, cs_ref, w_ref, o_ref):
    u = u_ref[...]
    x = jnp.dot(cs_ref[...], u, preferred_element_type=f32).astype(bf16)
    t = u.shape[0]
    w = w_ref[...]
    n = GROUP_B
    y = (jnp.dot(x[:t], w[:n], preferred_element_type=f32) + jnp.dot(x[t:], w[n:], preferred_element_type=f32))
    o_ref[...] = (y * _silu(z_ref[...].astype(f32))).astype(bf16)


def _fourier_ctx(ub, zb, wcs):
    B, T, _ = ub.shape
    ang = 2.0 * np.pi * (np.outer(np.arange(T), np.arange(T)) % T) / T
    cs = jnp.asarray(np.concatenate([np.cos(ang), -np.sin(ang)], axis=0), f32).astype(bf16)
    n = GROUP_B
    blk = pl.BlockSpec((None, T, n), lambda b, g: (b, 0, g))
    return pl.pallas_call(
        _fourier_ctx_kernel,
        grid=(B, N_GROUPS_B),
        in_specs=[blk, blk, pl.BlockSpec(cs.shape, lambda b, g: (0, 0)),
                  pl.BlockSpec((None, 2 * n, n), lambda b, g: (g, 0, 0))],
        out_specs=blk,
        out_shape=jax.ShapeDtypeStruct((B, T, D_B), bf16),
        compiler_params=pltpu.CompilerParams(dimension_semantics=("parallel", "parallel")),
        name="fourier_ctx",
    )(ub, zb, cs, wcs)


def _outproj_even_kernel(x_ref, h_ref, o_ref, za_ref, yb_ref, gate_ref, hg_ref, w_ref, out_ref):
    parts = []
    for hd in range(N_HEADS_A):
        sl = slice(hd * HEAD_DIM_A, (hd + 1) * HEAD_DIM_A)
        hh = _sigmoid(o_ref[:, sl]).astype(f32) * h_ref[:, sl]
        hh = hh * lax.rsqrt(jnp.mean(hh * hh, axis=-1, keepdims=True) + EPS)
        parts.append(hh.astype(bf16) * (hg_ref[:, sl].astype(bf16) * _silu(za_ref[:, sl])))
    ya = jnp.concatenate(parts, axis=1)
    acc = jnp.dot(ya, w_ref[:D_A, :], preferred_element_type=f32)
    acc = acc + jnp.dot(yb_ref[...], w_ref[D_A:, :], preferred_element_type=f32)
    out_ref[...] = x_ref[...] + gate_ref[...] * acc


def _outproj_even(x, h, o, za, yb, gate, head_g, wout, tm):
    B, T, D = x.shape
    row = lambda n: pl.BlockSpec((None, tm, n), lambda b, i: (b, i, 0))
    est = 2 * wout.size * 2 + 2 * tm * (2 * D * 4 + D_A * 4 + 3 * D_A * 2) + 8 * tm * D * 4
    return pl.pallas_call(
        _outproj_even_kernel,
        grid=(B, T // tm),
        in_specs=[row(D), row(D_A), row(D_A), row(D_A), row(D_B),
                  pl.BlockSpec((None, 1, D), lambda b, i: (b, 0, 0)),
                  pl.BlockSpec((1, D_A), lambda b, i: (0, 0)),
                  pl.BlockSpec(wout.shape, lambda b, i: (0, 0))],
        out_specs=row(D),
        out_shape=jax.ShapeDtypeStruct((B, T, D), f32),
        compiler_params=pltpu.CompilerParams(dimension_semantics=("parallel", "parallel"),
                                             vmem_limit_bytes=_vmem_limit(est)),
        name="outproj_even",
    )(x, h, o, za, yb, gate, head_g.reshape(1, D_A), wout)


def _inproj_odd_kernel(x_ref, sh_ref, sc_ref, g_ref, w_ref, u_ref, z_ref):
    amp = g_ref[...] * (1.0 + sc_ref[...])
    hx = _normed(x_ref[...], amp, sh_ref[...]).astype(bf16)
    cn = 512
    for j in range(D_INNER // cn):
        u_ref[:, j * cn:(j + 1) * cn] = jnp.dot(hx, w_ref[:, j * cn:(j + 1) * cn],
                                                preferred_element_type=f32).astype(bf16)
        z_ref[:, j * cn:(j + 1) * cn] = jnp.dot(hx, w_ref[:, D_INNER + j * cn:D_INNER + (j + 1) * cn],
                                                preferred_element_type=f32).astype(bf16)


def _inproj_odd(x, shift, scale, norm_g, w, tm):
    B, T, D = x.shape
    row = lambda n: pl.BlockSpec((None, tm, n), lambda b, i: (b, i, 0))
    est = 2 * w.size * 2 + 2 * tm * D * 4 + 4 * tm * D_INNER * 2 + 6 * tm * 512 * 4
    return pl.pallas_call(
        _inproj_odd_kernel,
        grid=(B, T // tm),
        in_specs=[row(D),
                  pl.BlockSpec((None, 1, D), lambda b, i: (b, 0, 0)),
                  pl.BlockSpec((None, 1, D), lambda b, i: (b, 0, 0)),
                  pl.BlockSpec((1, D), lambda b, i: (0, 0)),
                  pl.BlockSpec(w.shape, lambda b, i: (0, 0))],
        out_specs=[row(D_INNER), row(D_INNER)],
        out_shape=[jax.ShapeDtypeStruct((B, T, D_INNER), bf16)] * 2,
        compiler_params=pltpu.CompilerParams(dimension_semantics=("parallel", "parallel"),
                                             vmem_limit_bytes=_vmem_limit(est)),
        name="inproj_odd",
    )(x, shift, scale, norm_g.reshape(1, D), w)


POOL_UNROLL = 8


def _pool_tables():
    w_idx = np.arange(GRID_W)
    band = np.zeros((N_GROUPS_C, GRID_W, GRID_W), np.float32)
    inv_w = np.zeros((N_GROUPS_C, GRID_W, V7X_LANES), np.float32)
    for g, win in enumerate(POOL_WINDOWS):
        lo = np.clip(w_idx - win // 2, 0, GRID_W)
        hi = np.clip(w_idx + win - win // 2, 0, GRID_W)
        band[g] = (w_idx[None, :] >= lo[:, None]) & (w_idx[None, :] < hi[:, None])
        inv_w[g] = (1.0 / (hi - lo))[:, None]
    return jnp.asarray(band, bf16), jnp.asarray(inv_w, f32)


def _pool_kernel(u_ref, z_ref, band_ref, invw_ref, pw_ref, sc_ref, o_ref, ps_ref, pbuf_ref, *, rows):
    g = pl.program_id(1)
    W = GRID_W
    band = band_ref[...]
    lo_off = hi_off = 0
    for gi, win in enumerate(POOL_WINDOWS):
        lo_off = jnp.where(g == gi, win // 2, lo_off)
        hi_off = jnp.where(g == gi, win - win // 2, hi_off)

    ps_ref[0:W, :] = jnp.zeros((W, GROUP_C), f32)

    def width_sum(grp, carry):
        srcs = [pl.multiple_of((grp * POOL_UNROLL + j) * W, W) for j in range(POOL_UNROLL)]
        sums = [jnp.dot(band, u_ref[pl.ds(src, W), :], preferred_element_type=f32) for src in srcs]
        acc = ps_ref[pl.ds(srcs[0], W), :]
        for src, s in zip(srcs, sums):
            acc = acc + s
            ps_ref[pl.ds(src + W, W), :] = acc
        return carry

    lax.fori_loop(0, rows // POOL_UNROLL, width_sum, 0)

    inv_w = invw_ref[...]

    def pooled_minus_self(r):
        lo = jnp.maximum(r - lo_off, 0)
        hi = jnp.minimum(r + hi_off, rows)
        acc = (ps_ref[pl.ds(pl.multiple_of(hi * W, W), W), :]
               - ps_ref[pl.ds(pl.multiple_of(lo * W, W), W), :])
        inv = inv_w / (hi - lo).astype(f32)
        inv = jnp.concatenate([inv] * (GROUP_C // V7X_LANES), axis=1)
        ug = u_ref[r * W:(r + 1) * W, :].astype(f32)
        return (acc * inv - ug).astype(bf16)

    tr = pbuf_ref.shape[1]
    pws = (pw_ref[...] * sc_ref[...]).astype(bf16)

    def matmul(t):
        lhs = jnp.concatenate([pooled_minus_self(t * (tr // W) + j) for j in range(tr // W)], axis=0)
        return jnp.dot(lhs, pws, preferred_element_type=f32)

    def gate_store(t, y):
        sl = slice(t * tr, (t + 1) * tr)
        o_ref[sl, :] = y.astype(bf16) * _silu(z_ref[sl, :])

    _pipelined_matmuls((rows * W) // tr, matmul, gate_store, pbuf_ref)


def _pool_mix(u, z, pool_w, scale):
    B, T, _ = u.shape
    rows = T // GRID_W
    band, inv_w = _pool_tables()
    n = GROUP_C
    blk = pl.BlockSpec((None, T, n), lambda b, g: (b, 0, g))
    est = 3 * 2 * T * n * 2 + (rows + 1) * GRID_W * n * 4 + T * n * 2 + 2 * n * n * 4 + 8 * 512 * n * 4
    return pl.pallas_call(
        functools.partial(_pool_kernel, rows=rows),
        grid=(B, N_GROUPS_C),
        in_specs=[blk, blk,
                  pl.BlockSpec((None, GRID_W, GRID_W), lambda b, g: (g, 0, 0)),
                  pl.BlockSpec((None, GRID_W, V7X_LANES), lambda b, g: (g, 0, 0)),
                  pl.BlockSpec((None, n, n), lambda b, g: (g, 0, 0)),
                  pl.BlockSpec((None, 1, n), lambda b, g: (g, 0, 0))],
        out_specs=blk,
        out_shape=jax.ShapeDtypeStruct((B, T, D_INNER), bf16),
        scratch_shapes=[pltpu.VMEM(((rows + 1) * GRID_W, n), f32),
                        pltpu.VMEM((3, 512, n), f32)],
        compiler_params=pltpu.CompilerParams(dimension_semantics=("parallel", "arbitrary"),
                                             vmem_limit_bytes=_vmem_limit(est)),
        name="pool_mix",
    )(u, z, band, inv_w, pool_w, scale.reshape(N_GROUPS_C, 1, n))


def _outproj_odd_kernel(x_ref, y_ref, gate_ref, fg_ref, w_ref, out_ref):
    acc = jnp.dot(y_ref[...], w_ref[...], preferred_element_type=f32)
    x = x_ref[...] + gate_ref[...] * acc
    out_ref[...] = x * lax.rsqrt(jnp.mean(x * x, axis=-1, keepdims=True) + EPS) * fg_ref[...]


def _outproj_odd(x, y, gate, final_g, wout, tm):
    B, T, D = x.shape
    row = lambda n: pl.BlockSpec((None, tm, n), lambda b, i: (b, i, 0))
    est = 2 * wout.size * 2 + 2 * tm * (2 * D * 4 + D_INNER * 2) + 6 * tm * D * 4
    return pl.pallas_call(
        _outproj_odd_kernel,
        grid=(B, T // tm),
        in_specs=[row(D), row(D_INNER),
                  pl.BlockSpec((None, 1, D), lambda b, i: (b, 0, 0)),
                  pl.BlockSpec((1, D), lambda b, i: (0, 0)),
                  pl.BlockSpec(wout.shape, lambda b, i: (0, 0))],
        out_specs=row(D),
        out_shape=jax.ShapeDtypeStruct((B, T, D), f32),
        compiler_params=pltpu.CompilerParams(dimension_semantics=("parallel", "parallel"),
                                             vmem_limit_bytes=_vmem_limit(est)),
        name="outproj_odd",
    )(x, y, gate, final_g.reshape(1, D), wout)


def kernel(x, c, ctx, c_ctx, ada_w, ada_b, norm_g, win_even, gate_b_even, conv_qk_even, head_norm_even,
           fourier_w_even, wout_even, win_odd, pool_w_odd, pool_scale_odd, wout_odd, final_g):
    B, T, D = x.shape
    Tc = ctx.shape[1]
    H = N_HEADS_A
    L = MLSTM_CHUNK

    nrow = -(-(B + 1) // V7X_SUBLANES) * V7X_SUBLANES
    rows_in = jnp.zeros((nrow, D), f32).at[:B].set(c).at[B].set(c_ctx)
    mod = _modulation(rows_in, ada_w, ada_b)

    def mod_parts(l, r0, r1, n):
        m = mod[l, r0:r1]
        parts = [jnp.broadcast_to(m[:, None, i * D:(i + 1) * D], (n, 1, D)) for i in range(3)]
        return parts

    we = win_even[0]
    w_main = we.astype(bf16)
    gcols = we[:, W_MAIN_EVEN:].reshape(D, 4, H)
    gbias = gate_b_even[0].reshape(4, H)
    ig_w = gcols[:, 0::2, :].transpose(0, 2, 1).reshape(D, GATE_SLOTS)
    fg_w = gcols[:, 1::2, :].transpose(0, 2, 1).reshape(D, GATE_SLOTS)
    ig_b = gbias[0::2, :].T.reshape(GATE_SLOTS)
    fg_b = gbias[1::2, :].T.reshape(GATE_SLOTS)
    wgt =jnp.concatenate([ig_w, fg_w], axis=1).T.astype(bf16)
    gbt = jnp.concatenate([ig_b, fg_b]).reshape(N_GATES, 1)
    conv_w = conv_qk_even[0]

    shift_x, scale_x, gate_x = mod_parts(0, 0, B, B)
    shift_c, scale_c, gate_c = mod_parts(0, B, B + 1, B)

    qx, kx, vx, ox, zax, ubx, zbx, gtx = _inproj_even(
        x, shift_x, scale_x, norm_g[0], w_main, wgt, gbt, conv_w, tm=512)
    qc, kc, vc, oc, zac, ubc, zbc, gtc = _inproj_even(
        ctx, shift_c, scale_c, norm_g[0], w_main, wgt, gbt, conv_w, tm=Tc)

    colsx, rowsx = _gate_prep(gtx, L)
    colsc, rowsc = _gate_prep(gtc, L)
    h_x, h_c = _mlstm(qx, kx, vx, colsx, rowsx, qc, kc, vc, colsc, rowsc, L)

    wcs = _fourier_weights(fourier_w_even[0], T)
    yb_x = _fourier_latent(ubx, zbx, wcs)
    wout_e = wout_even[0].astype(bf16)
    x1 = _outproj_even(x, h_x, ox, zax, yb_x, gate_x, head_norm_even[0], wout_e, tm=1024)

    wcs_c = _fourier_weights(fourier_w_even[0], Tc)
    yb_c = _fourier_ctx(ubc, zbc, wcs_c)
    ctx1 = _outproj_even(ctx, h_c, oc, zac, yb_c, gate_c, head_norm_even[0], wout_e, tm=Tc)
    del ctx1

    shift_x, scale_x, gate_x = mod_parts(1, 0, B, B)
    u1, z1 = _inproj_odd(x1, shift_x, scale_x, norm_g[1], win_odd[0].astype(bf16), tm=1024)
    y1 = _pool_mix(u1, z1, pool_w_odd[0], pool_scale_odd[0])
    return _outproj_odd(x1, y1, gate_x, final_g, wout_odd[0].astype(bf16), tm=1024)
```

```python
import functools

import numpy as np
import jax
import jax.numpy as jnp
from jax import lax
from jax.experimental import pallas as pl
from jax.experimental.pallas import tpu as pltpu

D_MODEL = 1024
DEPTH = 2
CTX_LEN = 256
GRID_W = 64
D_INNER = 2 * D_MODEL
D_A = D_INNER // 2
D_B = D_INNER - D_A
N_HEADS_A = 4
HEAD_DIM_A = D_A // N_HEADS_A
N_GROUPS_B = 4
GROUP_B = D_B // N_GROUPS_B
N_GROUPS_C = 4
GROUP_C = D_INNER // N_GROUPS_C
POOL_WINDOWS = (2, 4, 8, 16)
CONV_W = 3
N_GATES = 4 * N_HEADS_A
W_MAIN_EVEN = 5 * D_A + 2 * D_B
EPS = 1e-6

f32 = jnp.float32
bf16 = jnp.bfloat16

V7X_VMEM_BYTES = 64 * 1024 * 1024
V7X_LANES = 128
V7X_SUBLANES = 8

MLSTM_CHUNK = 256
MLSTM_HEADS_PER_STEP = 2
FFT_N1 = 64
FFT_N2 = 64
FFT_PITCH = 72
FFT_UNROLL = 8
NEG_BIG = -1e30


def _vmem_limit(nbytes):
    return int(min(max(nbytes * 5 // 4 + (4 << 20), 16 << 20), V7X_VMEM_BYTES - (6 << 20)))


def _sigmoid(v):
    return 0.5 * jnp.tanh(0.5 * v) + 0.5


def _silu(v):
    return v * _sigmoid(v)


def _pipelined_matmuls(n, matmul, epilogue, pbuf_ref):
    zero = jnp.minimum(pl.program_id(0), 0)
    nbuf = pbuf_ref.shape[0]
    pbuf_ref[zero] = matmul(0)
    for t in range(n):
        if t + 1 < n:
            pbuf_ref[zero + (t + 1) % nbuf] = matmul(t + 1)
        epilogue(t, pbuf_ref[zero + t % nbuf])


def _log_sigmoid(v):
    return jnp.minimum(v, 0.0) - jnp.log1p(jnp.exp(-jnp.abs(v)))


def _mod_kernel(r_ref, w_ref, b_ref, o_ref):
    s = _silu(r_ref[...])
    o_ref[...] = jnp.dot(s, w_ref[...], preferred_element_type=f32,
                         precision=lax.Precision.HIGHEST) + b_ref[...]


def _modulation(rows, ada_w, ada_b):
    nrow = rows.shape[0]
    tn = 1024
    return pl.pallas_call(
        _mod_kernel,
        grid=(DEPTH, 3 * D_MODEL // tn),
        in_specs=[
            pl.BlockSpec((nrow, D_MODEL), lambda l, j: (0, 0)),
            pl.BlockSpec((None, D_MODEL, tn), lambda l, j: (l, 0, j)),
            pl.BlockSpec((None, 1, tn), lambda l, j: (l, 0, j)),
        ],
        out_specs=pl.BlockSpec((None, nrow, tn), lambda l, j: (l, 0, j)),
        out_shape=jax.ShapeDtypeStruct((DEPTH, nrow, 3 * D_MODEL), f32),
        compiler_params=pltpu.CompilerParams(dimension_semantics=("arbitrary", "arbitrary")),
        name="modulation",
    )(rows, ada_w, ada_b.reshape(DEPTH, 1, 3 * D_MODEL))


def _normed(x, amp, shift):
    ms = jnp.mean(x * x, axis=-1, keepdims=True)
    return (x * lax.rsqrt(ms + EPS)) * amp + shift


def _inproj_even_kernel(x_ref, xp_ref, xn_ref, sh_ref, sc_ref, g_ref, w_ref, wgt_ref, gbt_ref,
                        cw_ref, q_ref, k_ref, v_ref, o_ref, za_ref, ub_ref, zb_ref, gt_ref, pbuf_ref, *, tm, nt):
    i = pl.program_id(1)
    amp = g_ref[...] * (1.0 + sc_ref[...])
    shift = sh_ref[...]
    hx = _normed(x_ref[...], amp, shift).astype(bf16)
    halo = jnp.concatenate([xp_ref[...], xn_ref[...]], axis=0)
    hh = _normed(halo, amp, shift).astype(bf16)
    has_prev = (i > 0).astype(f32)
    has_next = (i < nt - 1).astype(f32)
    row = lax.broadcasted_iota(jnp.int32, (tm, 1), 0)
    cn = 512
    ph = jnp.dot(hh, w_ref[:, :2 * D_A], preferred_element_type=f32)
    prev = ph[V7X_SUBLANES - 1:V7X_SUBLANES, :] * has_prev
    nxt = ph[V7X_SUBLANES:V7X_SUBLANES + 1, :] * has_next

    def conv_store(j, p):
        cols = slice(j * cn, (j + 1) * cn)
        up = jnp.where(row == 0, prev[:, cols], pltpu.roll(p, 1, 0))
        dn = jnp.where(row == tm - 1, nxt[:, cols], pltpu.roll(p, tm - 1, 0))
        cw = cw_ref[:, cols]
        y = _silu(cw[0:1, :] * up + cw[1:2, :] * p + cw[2:3, :] * dn)
        if j < D_A // cn:
            q_ref[:, cols] = (y * (HEAD_DIM_A ** -0.5)).astype(bf16)
        else:
            jj = j - D_A // cn
            k_ref[:, jj * cn:(jj + 1) * cn] = y.astype(bf16)

    def plain_store(ref, jj):
        def store(p):
            ref[:, jj * cn:(jj + 1) * cn] = p.astype(bf16)
        return store

    tasks = [(j * cn, functools.partial(conv_store, j)) for j in range(2 * D_A // cn)]
    for idx, ref in enumerate((v_ref, o_ref, za_ref, ub_ref, zb_ref)):
        for jj in range(D_A // cn):
            tasks.append((2 * D_A + idx * D_A + jj * cn, plain_store(ref, jj)))

    def matmul(t):
        c0 = tasks[t][0]
        return jnp.dot(hx, w_ref[:, c0:c0 + cn], preferred_element_type=f32)

    _pipelined_matmuls(len(tasks), matmul, lambda t, p: tasks[t][1](p), pbuf_ref)
    gt_ref[...] = lax.dot_general(wgt_ref[...], hx, (((1,), (1,)), ((), ())),
                                  preferred_element_type=f32) + gbt_ref[...]


def _inproj_even(x, shift, scale, norm_g, w_main, wgt, gbt, conv_w, tm):
    B, T, D = x.shape
    nt = T // tm
    hb = tm // V7X_SUBLANES
    nhb = T // V7X_SUBLANES
    row_spec = pl.BlockSpec((None, tm, D_A), lambda b, i: (b, i, 0))
    vec = lambda n: pl.BlockSpec((1, n), lambda b, i: (0, 0))
    est = (2 * w_main.size * 2 + 2 * tm * D * 4 + 7 * 2 * tm * D_A * 2 + 6 * tm * 512 * 4)
    outs = pl.pallas_call(
        functools.partial(_inproj_even_kernel, tm=tm, nt=nt),
        grid=(B, nt),
        in_specs=[
            pl.BlockSpec((None, tm, D), lambda b, i: (b, i, 0)),
            pl.BlockSpec((None, V7X_SUBLANES, D), lambda b, i: (b, jnp.maximum(i * hb - 1, 0), 0)),
            pl.BlockSpec((None, V7X_SUBLANES, D), lambda b, i: (b, jnp.minimum((i + 1) * hb, nhb - 1), 0)),
            pl.BlockSpec((None, 1, D), lambda b, i: (b, 0, 0)),
            pl.BlockSpec((None, 1, D), lambda b, i: (b, 0, 0)),
            vec(D),
            pl.BlockSpec(w_main.shape, lambda b, i: (0, 0)),
            pl.BlockSpec(wgt.shape, lambda b, i: (0, 0)),
            pl.BlockSpec((N_GATES, 1), lambda b, i: (0, 0)),
            pl.BlockSpec(conv_w.shape, lambda b, i: (0, 0)),
        ],
        out_specs=[row_spec] * 7 + [pl.BlockSpec((None, N_GATES, tm), lambda b, i: (b, 0, i))],
        out_shape=[jax.ShapeDtypeStruct((B, T, D_A), bf16)] * 7 + [jax.ShapeDtypeStruct((B, N_GATES, T), f32)],
        scratch_shapes=[pltpu.VMEM((3, tm, 512), f32)],
        compiler_params=pltpu.CompilerParams(dimension_semantics=("parallel", "arbitrary"),
                                             vmem_limit_bytes=_vmem_limit(est)),
        name="inproj_even",
    )(x, x, x, shift, scale, norm_g.reshape(1, D), w_main, wgt, gbt, conv_w)
    return outs


GATE_SLOTS = 2 * N_HEADS_A
GATE_PIECES = 3
GATE_QUANTS = 3
assert GATE_SLOTS & (GATE_SLOTS - 1) == 0 and GATE_QUANTS * GATE_PIECES * GATE_SLOTS <= V7X_LANES


def _scan_max_lanes(x, seg, reverse):
    n = x.shape[1]
    pos = lax.broadcasted_iota(jnp.int32, x.shape, 1) & (seg - 1)
    s = 1
    while s < seg:
        if reverse:
            x = jnp.where(pos < seg - s, jnp.maximum(x, pltpu.roll(x, n - s, 1)), x)
        else:
            x = jnp.where(pos >= s, jnp.maximum(x, pltpu.roll(x, s, 1)), x)
        s *= 2
    return x


def _split3(v):
    hi = v.astype(bf16)
    r1 = v - hi.astype(f32)
    mid = r1.astype(bf16)
    lo = (r1 - mid.astype(f32)).astype(bf16)
    return hi, mid, lo


def _gate_prep_kernel(gt_ref, cols_ref, arow_ref, *, L, nchunk):
    r = lax.broadcasted_iota(jnp.int32, (L, L), 0)
    c = lax.broadcasted_iota(jnp.int32, (L, L), 1)
    tri_l = (c <= r).astype(bf16)
    tri_u = (c >= r).astype(bf16)
    S = V7X_SUBLANES
    tb = nchunk * L
    fwd = (lax.broadcasted_iota(jnp.int32, (S, tb), 0) & 1) == 0
    ig = gt_ref[:S, :]
    lf = _log_sigmoid(gt_ref[S:, :])
    pieces = jnp.concatenate([p.astype(f32) for p in _split3(lf)] + [jnp.zeros((S, tb), f32)],
                             axis=0).astype(bf16)
    chunks = [slice(ci * L, (ci + 1) * L) for ci in range(nchunk)]
    pre = jnp.concatenate([jnp.dot(pieces[:, sl], tri_u, preferred_element_type=f32) for sl in chunks], axis=1)
    suf = jnp.concatenate([jnp.dot(pieces[:, sl], tri_l, preferred_element_type=f32) for sl in chunks], axis=1)
    pre = pre[:S] + pre[S:2 * S] + pre[2 * S:3 * S]
    suf = suf[:S] + suf[S:2 * S] + suf[2 * S:3 * S]
    b_row = jnp.where(fwd, pre, suf)
    a_row = ig - b_row
    cmax = jnp.where(fwd, _scan_max_lanes(a_row, L, False), _scan_max_lanes(a_row, L, True))
    arow_ref[...] = a_row
    parts = [p.astype(f32) for quant in (b_row, a_row, cmax) for p in _split3(quant)]
    fill = jnp.zeros((V7X_LANES - len(parts) * S, tb), f32)
    packed = jnp.concatenate(parts + [fill], axis=0)
    for sl in chunks:
        cols_ref[sl, :] = packed[:, sl].T.astype(bf16)


def _gate_prep(gt, L):
    B, _, T = gt.shape
    nchunk = min(4, T // L)
    tb = nchunk * L
    cols, arow = pl.pallas_call(
        functools.partial(_gate_prep_kernel, L=L, nchunk=nchunk),
        grid=(B, T // tb),
        in_specs=[pl.BlockSpec((None, N_GATES, tb), lambda b, i: (b, 0, i))],
        out_specs=[
            pl.BlockSpec((None, tb, V7X_LANES), lambda b, i: (b, i, 0)),
            pl.BlockSpec((None, V7X_SUBLANES, tb), lambda b, i: (b, 0, i)),
        ],
        out_shape=[
            jax.ShapeDtypeStruct((B, T, V7X_LANES), bf16),
            jax.ShapeDtypeStruct((B, V7X_SUBLANES, T), f32),
        ],
        compiler_params=pltpu.CompilerParams(dimension_semantics=("parallel", "parallel")),
        name="gate_prep",
    )(gt)
    return cols, arow.reshape(B, N_HEADS_A, 2, T)


def _mlstm_kernel(qx_ref, kx_ref, vx_ref, cx_ref, rx_ref, qc_ref, kc_ref, vc_ref, cc_ref, rc_ref,
                  hx_ref, hc_ref, c_ref, n_ref, *, L, nx, nc):
    Dh, LN, HP = HEAD_DIM_A, V7X_LANES, MLSTM_HEADS_PER_STEP
    head0 = pl.program_id(1) * HP
    chains = [(hh, d) for hh in range(HP) for d in range(2)]
    r_i = lax.broadcasted_iota(jnp.int32, (L, L), 0)
    c_i = lax.broadcasted_iota(jnp.int32, (L, L), 1)
    masks = (c_i <= r_i, c_i >= r_i)

    sr = lax.broadcasted_iota(jnp.int32, (LN, GATE_QUANTS * LN), 0)
    sc = lax.broadcasted_iota(jnp.int32, (LN, GATE_QUANTS * LN), 1)
    span = GATE_PIECES * GATE_SLOTS
    in_block = None
    for qi in range(GATE_QUANTS):
        blk = (sr >= qi * span) & (sr < (qi + 1) * span) & (sc >= qi * LN) & (sc < (qi + 1) * LN)
        in_block = blk if in_block is None else in_block | blk
    slot = sr & (GATE_SLOTS - 1)
    sels = {(hh, d): (in_block & (slot == 2 * (head0 + hh) + d)).astype(bf16) for hh, d in chains}

    def tile(v, width):
        return jnp.concatenate([v] * (width // LN), axis=1)

    def step(refs, r0s, ms):
        q_ref, k_ref, v_ref, col_ref, row_ref = refs
        st = []
        for ci, (hh, d) in enumerate(chains):
            rows = pl.ds(r0s[ci], L)
            hsl = slice(hh * Dh, (hh + 1) * Dh)
            q = q_ref[rows, hsl]
            k = k_ref[rows, hsl]
            rep = jnp.dot(col_ref[rows, :], sels[hh, d], preferred_element_type=f32)
            qk = lax.dot_general(q, k, (((1,), (1,)), ((), ())), preferred_element_type=f32)
            qc = jnp.dot(q, c_ref[ci].astype(bf16), preferred_element_type=f32)
            st.append(dict(q=q, k=k, v=v_ref[rows, hsl], rep=rep, qk=qk, qc=qc, a_row=row_ref[hh, d:d + 1, rows]))
        new_ms = []
        for ci, (hh, d) in enumerate(chains):
            c, m_prev = st[ci], ms[ci]
            b_rep, a_rep = c["rep"][:, :LN], c["rep"][:, LN:2 * LN]
            g_rep = jnp.maximum(c["rep"][:, 2 * LN:], m_prev)
            p = jnp.exp(jnp.where(masks[d], c["a_row"] - tile(g_rep, L), NEG_BIG))
            s = c["qk"] * p
            c["s"] = s.astype(bf16)
            inter = jnp.exp(m_prev - g_rep)
            qn = jnp.sum(c["q"].astype(f32) * n_ref[ci], axis=1, keepdims=True)
            den = jnp.sum(s, axis=1, keepdims=True) + inter[:, :1] * qn
            floor = jnp.exp(-(b_rep + g_rep))
            rcp = 1.0 / jnp.maximum(jnp.abs(den), floor[:, :1])
            c["rcp"] = jnp.broadcast_to(rcp, (L, LN))
            c["inter"] = inter
            b_end = b_rep[L - 1:L, :] if d == 0 else b_rep[0:1, :]
            w = b_end + a_rep
            m_new = jnp.maximum(b_end + m_prev, jnp.max(w, axis=0, keepdims=True))
            c["decay"] = jnp.exp(b_end + m_prev - m_new)
            kw = c["k"].astype(f32) * tile(jnp.exp(w - m_new), Dh)
            c["kw"] = kw.astype(bf16)
            c["ksum"] = jnp.sum(kw, axis=0, keepdims=True)
            new_ms.append(m_new)
        for c in st:
            c["sv"] = jnp.dot(c["s"], c["v"], preferred_element_type=f32)
            c["upd"] = lax.dot_general(c["kw"], c["v"], (((0,), (0,)), ((), ())), preferred_element_type=f32)
        hs = []
        for ci, c in enumerate(st):
            hs.append((c["sv"] + tile(c["inter"], Dh) * c["qc"]) * tile(c["rcp"], Dh))
            decay = tile(c["decay"], Dh)
            c_ref[ci] = decay * c_ref[ci] + c["upd"]
            n_ref[ci] = decay * n_ref[ci] + c["ksum"]
        return hs, new_ms

    ctx = (qc_ref, kc_ref, vc_ref, cc_ref, rc_ref)
    lat = (qx_ref, kx_ref, vx_ref, cx_ref, rx_ref)

    c_ref[...] = jnp.zeros_like(c_ref)
    n_ref[...] = jnp.zeros_like(n_ref)
    ms = [jnp.zeros((1, LN), f32) for _ in chains]
    written = set()
    for j in range(nc):
        cjs = [j if d == 0 else nc - 1 - j for _, d in chains]
        hs, ms = step(ctx, [cj * L for cj in cjs], ms)
        for (hh, d), cj, h in zip(chains, cjs, hs):
            dst = (slice(cj * L, (cj + 1) * L), slice(hh * Dh, (hh + 1) * Dh))
            if (cj, hh) in written:
                hc_ref[dst] = (hc_ref[dst].astype(f32) + h).astype(hc_ref.dtype)
            else:
                hc_ref[dst] = h.astype(hc_ref.dtype)
                written.add((cj, hh))

    def make_body(accumulate):
        def body(i, ms):
            r0s = [pl.multiple_of((i if d == 0 else nx - 1 - i) * L, L) for _, d in chains]
            hs, ms = step(lat, r0s, list(ms))
            for (hh, d), r0, h in zip(chains, r0s, hs):
                dst = (pl.ds(r0, L), slice(hh * Dh, (hh + 1) * Dh))
                if accumulate:
                    hx_ref[dst] = (hx_ref[dst].astype(f32) + h).astype(hx_ref.dtype)
                else:
                    hx_ref[dst] = h.astype(hx_ref.dtype)
            return tuple(ms)
        return body

    ms = lax.fori_loop(0, nx // 2, make_body(False), tuple(ms))
    lax.fori_loop(nx // 2, nx, make_body(True), ms)


def _mlstm(qx, kx, vx, colsx, rowsx, qc, kc, vc, colsc, rowsc, L):
    B, T, _ = qx.shape
    Tc = qc.shape[1]
    H, Dh, HP = N_HEADS_A, HEAD_DIM_A, MLSTM_HEADS_PER_STEP
    assert T % (2 * L) == 0 and Tc % L == 0 and H % HP == 0

    def seq_spec(t):
        return pl.BlockSpec((None, t, HP * Dh), lambda b, h: (b, 0, h))

    def col_spec(t):
        return pl.BlockSpec((None, t, V7X_LANES), lambda b, h: (b, 0, 0))

    def row_spec(t):
        return pl.BlockSpec((None, HP, 2, t), lambda b, h: (b, h, 0, 0))

    est = (2 * HP * (3 * (T + Tc) * Dh * 2 + 8 * (T + Tc) * 4 + (T + Tc) * Dh * 2)
           + 2 * (T + Tc) * V7X_LANES * 2 + 2 * HP * Dh * (Dh + V7X_LANES) * 4 + 16 * L * L * 4)
    return pl.pallas_call(
        functools.partial(_mlstm_kernel, L=L, nx=T // L, nc=Tc // L),
        grid=(B, H // HP),
        in_specs=[seq_spec(T), seq_spec(T), seq_spec(T), col_spec(T), row_spec(T),
                  seq_spec(Tc), seq_spec(Tc), seq_spec(Tc), col_spec(Tc), row_spec(Tc)],
        out_specs=[seq_spec(T), seq_spec(Tc)],
        out_shape=[jax.ShapeDtypeStruct((B, T, D_A), bf16), jax.ShapeDtypeStruct((B, Tc, D_A), bf16)],
        scratch_shapes=[pltpu.VMEM((2 * HP, Dh, Dh), f32), pltpu.VMEM((2 * HP, 1, Dh), f32)],
        compiler_params=pltpu.CompilerParams(dimension_semantics=("parallel", "parallel"),
                                             vmem_limit_bytes=_vmem_limit(est)),
        name="mlstm",
    )(qx, kx, vx, colsx, rowsx, qc, kc, vc, colsc, rowsc)


def _fourier_w_kernel(cs_ref, fw_ref, o_ref, *, scale):
    o_ref[...] = (jnp.dot(cs_ref[...], fw_ref[...], preferred_element_type=f32,
                          precision=lax.Precision.HIGHEST) * scale).astype(bf16)


def _fourier_weights(fw, T):
    n = GROUP_B
    kk = np.outer(np.arange(n), np.arange(n)) % n
    ang = 2.0 * np.pi * kk / n
    cs = jnp.asarray(np.concatenate([np.cos(ang), np.sin(ang)], axis=0), f32)
    return pl.pallas_call(
        functools.partial(_fourier_w_kernel, scale=float(1.0 / np.sqrt(T * n))),
        grid=(N_GROUPS_B,),
        in_specs=[pl.BlockSpec((2 * n, n), lambda g: (0, 0)),
                  pl.BlockSpec((None, n, n), lambda g: (g, 0, 0))],
        out_specs=pl.BlockSpec((None, 2 * n, n), lambda g: (g, 0, 0)),
        out_shape=jax.ShapeDtypeStruct((N_GROUPS_B, 2 * n, n), bf16),
        compiler_params=pltpu.CompilerParams(dimension_semantics=("arbitrary",)),
        name="fourier_weights",
    )(cs, fw)


def _fft_tables():
    n1, n2 = FFT_N1, FFT_N2
    n = n1 * n2
    t1 = np.arange(n1)
    k1 = np.arange(n1)
    t2 = np.arange(n2)
    idx = (k1[None, :, None] * (n2 * t1[None, None, :] + t2[:, None, None])) % n
    ang = 2.0 * np.pi * idx / n
    m1 = np.concatenate([np.cos(ang), -np.sin(ang)], axis=1)
    k2 = np.arange(n2)
    ph = 2.0 * np.pi * (np.outer(k2, t2) % n2) / n2
    c, s = np.cos(ph), np.sin(ph)
    m2 = np.block([[c, s], [-s, c]])
    return jnp.asarray(m1, f32).astype(bf16), jnp.asarray(m2, f32).astype(bf16)


def _fourier_kernel(u_ref, z_ref, m1_ref, m2_ref, w_ref, o_ref, up_ref, yp_ref, xp_ref, pbuf_ref):
    n1, n2, P = FFT_N1, FFT_N2, FFT_PITCH
    nl = GROUP_B // V7X_LANES

    def fill(t1, carry):
        src = pl.multiple_of(t1 * n2, n2)
        dst = pl.multiple_of(t1 * P, V7X_SUBLANES)
        blk = u_ref[pl.ds(src, n2), :].astype(f32)
        for s in range(nl):
            up_ref[s, pl.ds(dst, n2), :] = blk[:, s * V7X_LANES:(s + 1) * V7X_LANES]
        return carry

    lax.fori_loop(0, n1, fill, 0, unroll=4)

    G = FFT_UNROLL

    def stage1(grp, carry):
        t2s = [grp * G + j for j in range(G)]
        rhs = [jnp.concatenate([up_ref[s, pl.ds(t2, n1, stride=P), :] for s in range(nl)],
                               axis=1).astype(bf16) for t2 in t2s]
        ys = [jnp.dot(m1_ref[t2], r, preferred_element_type=f32) for t2, r in zip(t2s, rhs)]
        for t2, y in zip(t2s, ys):
            dst = pl.multiple_of(t2 * P, V7X_SUBLANES)
            for ri in range(2):
                for s in range(nl):
                    yp_ref[ri * nl + s, pl.ds(dst, n1), :] = y[ri * n1:(ri + 1) * n1,
                                                               s * V7X_LANES:(s + 1) * V7X_LANES]
        return carry

    lax.fori_loop(0, n2 // G, stage1, 0)

    m2 = m2_ref[...]

    def stage2(grp, carry):
        k1s = [grp * G + j for j in range(G)]
        rhs = []
        for k1 in k1s:
            parts = [jnp.concatenate([yp_ref[ri * nl + s, pl.ds(k1, n2, stride=P), :] for s in range(nl)], axis=1)
                     for ri in range(2)]
            rhs.append(jnp.concatenate(parts, axis=0).astype(bf16))
        xs = [jnp.dot(m2, r, preferred_element_type=f32) for r in rhs]
        for k1, x in zip(k1s, xs):
            for ri in range(2):
                for s in range(nl):
                    xp_ref[ri * nl + s, pl.ds(k1, n2, stride=P), :] = x[ri * n2:(ri + 1) * n2,
                                                                        s * V7X_LANES:(s + 1) * V7X_LANES]
        return carry

    lax.fori_loop(0, n1 // G, stage2, 0)

    tr = pbuf_ref.shape[1]

    def matmul(t):
        blocks = []
        for k2 in range(t * (tr // n1), (t + 1) * (tr // n1)):
            blocks.append(jnp.concatenate([xp_ref[sl, k2 * P:k2 * P + n1, :] for sl in range(2 * nl)],
                                          axis=1).astype(bf16))
        return jnp.dot(jnp.concatenate(blocks, axis=0), w_ref[...], preferred_element_type=f32)

    def gate_store(t, y):
        sl = slice(t * tr, (t + 1) * tr)
        o_ref[sl, :] = y.astype(bf16) * _silu(z_ref[sl, :])

    _pipelined_matmuls((n1 * n2) // tr, matmul, gate_store, pbuf_ref)


def _fourier_latent(ub, zb, wcs):
    B, T, _ = ub.shape
    assert T == FFT_N1 * FFT_N2
    m1, m2 = _fft_tables()
    G, n = N_GROUPS_B, GROUP_B
    nl = n // V7X_LANES
    rows_p = FFT_N1 * FFT_PITCH
    blk = pl.BlockSpec((None, T, n), lambda b, g: (b, 0, g))
    est = (3 * 2 * T * n * 2 + 5 * nl * rows_p * V7X_LANES * 4 + T * 2 * n * 2 + 2 * m1.size * 2 + 8 * 512 * n * 4)
    return pl.pallas_call(
        _fourier_kernel,
        grid=(B, G),
        in_specs=[blk, blk,
                  pl.BlockSpec(m1.shape, lambda b, g: (0, 0, 0)),
                  pl.BlockSpec(m2.shape, lambda b, g: (0, 0)),
                  pl.BlockSpec((None, 2 * n, n), lambda b, g: (g, 0, 0))],
        out_specs=blk,
        out_shape=jax.ShapeDtypeStruct((B, T, D_B), bf16),
        scratch_shapes=[pltpu.VMEM((nl, rows_p, V7X_LANES), f32),
                        pltpu.VMEM((2 * nl, rows_p, V7X_LANES), f32),
                        pltpu.VMEM((2 * nl, rows_p, V7X_LANES), f32),
                        pltpu.VMEM((3, 512, n), f32)],
        compiler_params=pltpu.CompilerParams(dimension_semantics=("parallel", "parallel"),
                                             vmem_limit_bytes=_vmem_limit(est)),
        name="fourier_latent",
    )(ub, zb, m1, m2, wcs)


def _fourier_ctx_kernel(u_ref, z_ref, cs_ref, w_ref, o_ref):
    u = u_ref[...]
    x = jnp.dot(cs_ref[...], u, preferred_element_type=f32).astype(bf16)
    t = u.shape[0]
    w = w_ref[...]
    n = GROUP_B
    y = (jnp.dot(x[:t], w[:n], preferred_element_type=f32) + jnp.dot(x[t:], w[n:], preferred_element_type=f32))
    o_ref[...] = (y * _silu(z_ref[...].astype(f32))).astype(bf16)


def _fourier_ctx(ub, zb, wcs):
    B, T, _ = ub.shape
    ang = 2.0 * np.pi * (np.outer(np.arange(T), np.arange(T)) % T) / T
    cs = jnp.asarray(np.concatenate([np.cos(ang), -np.sin(ang)], axis=0), f32).astype(bf16)
    n = GROUP_B
    blk = pl.BlockSpec((None, T, n), lambda b, g: (b, 0, g))
    return pl.pallas_call(
        _fourier_ctx_kernel,
        grid=(B, N_GROUPS_B),
        in_specs=[blk, blk, pl.BlockSpec(cs.shape, lambda b, g: (0, 0)),
                  pl.BlockSpec((None, 2 * n, n), lambda b, g: (g, 0, 0))],
        out_specs=blk,
        out_shape=jax.ShapeDtypeStruct((B, T, D_B), bf16),
        compiler_params=pltpu.CompilerParams(dimension_semantics=("parallel", "parallel")),
        name="fourier_ctx",
    )(ub, zb, cs, wcs)


def _outproj_even_kernel(x_ref, h_ref, o_ref, za_ref, yb_ref, gate_ref, hg_ref, w_ref, out_ref):
    parts = []
    for hd in range(N_HEADS_A):
        sl = slice(hd * HEAD_DIM_A, (hd + 1) * HEAD_DIM_A)
        hh = _sigmoid(o_ref[:, sl]).astype(f32) * h_ref[:, sl]
        hh = hh * lax.rsqrt(jnp.mean(hh * hh, axis=-1, keepdims=True) + EPS)
        parts.append(hh.astype(bf16) * (hg_ref[:, sl].astype(bf16) * _silu(za_ref[:, sl])))
    ya = jnp.concatenate(parts, axis=1)
    acc = jnp.dot(ya, w_ref[:D_A, :], preferred_element_type=f32)
    acc = acc + jnp.dot(yb_ref[...], w_ref[D_A:, :], preferred_element_type=f32)
    out_ref[...] = x_ref[...] + gate_ref[...] * acc


def _outproj_even(x, h, o, za, yb, gate, head_g, wout, tm):
    B, T, D = x.shape
    row = lambda n: pl.BlockSpec((None, tm, n), lambda b, i: (b, i, 0))
    est = 2 * wout.size * 2 + 2 * tm * (2 * D * 4 + D_A * 4 + 3 * D_A * 2) + 8 * tm * D * 4
    return pl.pallas_call(
        _outproj_even_kernel,
        grid=(B, T // tm),
        in_specs=[row(D), row(D_A), row(D_A), row(D_A), row(D_B),
                  pl.BlockSpec((None, 1, D), lambda b, i: (b, 0, 0)),
                  pl.BlockSpec((1, D_A), lambda b, i: (0, 0)),
                  pl.BlockSpec(wout.shape, lambda b, i: (0, 0))],
        out_specs=row(D),
        out_shape=jax.ShapeDtypeStruct((B, T, D), f32),
        compiler_params=pltpu.CompilerParams(dimension_semantics=("parallel", "parallel"),
                                             vmem_limit_bytes=_vmem_limit(est)),
        name="outproj_even",
    )(x, h, o, za, yb, gate, head_g.reshape(1, D_A), wout)


def _inproj_odd_kernel(x_ref, sh_ref, sc_ref, g_ref, w_ref, u_ref, z_ref):
    amp = g_ref[...] * (1.0 + sc_ref[...])
    hx = _normed(x_ref[...], amp, sh_ref[...]).astype(bf16)
    cn = 512
    for j in range(D_INNER // cn):
        u_ref[:, j * cn:(j + 1) * cn] = jnp.dot(hx, w_ref[:, j * cn:(j + 1) * cn],
                                                preferred_element_type=f32).astype(bf16)
        z_ref[:, j * cn:(j + 1) * cn] = jnp.dot(hx, w_ref[:, D_INNER + j * cn:D_INNER + (j + 1) * cn],
                                                preferred_element_type=f32).astype(bf16)


def _even_out_odd_in_kernel(x_ref, h_ref, o_ref, za_ref, yb_ref, gate_ref, hg_ref, wo_ref,
                            sh_ref, sc_ref, g_ref, wi_ref, x1_ref, u_ref, z_ref):
    _outproj_even_kernel(x_ref, h_ref, o_ref, za_ref, yb_ref, gate_ref, hg_ref, wo_ref, x1_ref)
    _inproj_odd_kernel(x1_ref, sh_ref, sc_ref, g_ref, wi_ref, u_ref, z_ref)


def _even_out_odd_in(x, h, o, za, yb, gate, head_g, wout, shift, scale, norm_g, win, tm):
    B, T, D = x.shape
    row = lambda n: pl.BlockSpec((None, tm, n), lambda b, i: (b, i, 0))
    bvec = pl.BlockSpec((None, 1, D), lambda b, i: (b, 0, 0))
    est = (2 * (wout.size + win.size) * 2 + 2 * tm * (2 * D * 4 + 4 * D_A * 2 + 2 * D_INNER * 2)
           + 8 * tm * D * 4 + 6 * tm * 512 * 4)
    return pl.pallas_call(
        _even_out_odd_in_kernel,
        grid=(B, T // tm),
        in_specs=[row(D), row(D_A), row(D_A), row(D_A), row(D_B), bvec,
                  pl.BlockSpec((1, D_A), lambda b, i: (0, 0)),
                  pl.BlockSpec(wout.shape, lambda b, i: (0, 0)),
                  bvec, bvec,
                  pl.BlockSpec((1, D), lambda b, i: (0, 0)),
                  pl.BlockSpec(win.shape, lambda b, i: (0, 0))],
        out_specs=[row(D), row(D_INNER), row(D_INNER)],
        out_shape=[jax.ShapeDtypeStruct((B, T, D), f32)] + [jax.ShapeDtypeStruct((B, T, D_INNER), bf16)] * 2,
        compiler_params=pltpu.CompilerParams(dimension_semantics=("parallel", "parallel"),
                                             vmem_limit_bytes=_vmem_limit(est)),
        name="even_out_odd_in",
    )(x, h, o, za, yb, gate, head_g.reshape(1, D_A), wout, shift, scale, norm_g.reshape(1, D), win)


POOL_UNROLL = 8


def _pool_tables():
    w_idx = np.arange(GRID_W)
    band = np.zeros((N_GROUPS_C, GRID_W, GRID_W), np.float32)
    inv_w = np.zeros((N_GROUPS_C, GRID_W, V7X_LANES), np.float32)
    for g, win in enumerate(POOL_WINDOWS):
        lo = np.clip(w_idx - win // 2, 0, GRID_W)
        hi = np.clip(w_idx + win - win // 2, 0, GRID_W)
        band[g] = (w_idx[None, :] >= lo[:, None]) & (w_idx[None, :] < hi[:, None])
        inv_w[g] = (1.0 / (hi - lo))[:, None]
    return jnp.asarray(band, bf16), jnp.asarray(inv_w, f32)


def _pool_kernel(u_ref, z_ref, band_ref, invw_ref, pw_ref, sc_ref, o_ref, ps_ref, pbuf_ref, *, rows):
    g = pl.program_id(1)
    W = GRID_W
    band = band_ref[...]
    lo_off = hi_off = 0
    for gi, win in enumerate(POOL_WINDOWS):
        lo_off = jnp.where(g == gi, win // 2, lo_off)
        hi_off = jnp.where(g == gi, win - win // 2, hi_off)

    ps_ref[0:W, :] = jnp.zeros((W, GROUP_C), f32)

    def width_sum(grp, carry):
        srcs = [pl.multiple_of((grp * POOL_UNROLL + j) * W, W) for j in range(POOL_UNROLL)]
        sums = [jnp.dot(band, u_ref[pl.ds(src, W), :], preferred_element_type=f32) for src in srcs]
        acc = ps_ref[pl.ds(srcs[0], W), :]
        for src, s in zip(srcs, sums):
            acc = acc + s
            ps_ref[pl.ds(src + W, W), :] = acc
        return carry

    lax.fori_loop(0, rows // POOL_UNROLL, width_sum, 0)

    inv_w = invw_ref[...]

    def pooled_minus_self(r):
        lo = jnp.maximum(r - lo_off, 0)
        hi = jnp.minimum(r + hi_off, rows)
        acc = (ps_ref[pl.ds(pl.multiple_of(hi * W, W), W), :]
               - ps_ref[pl.ds(pl.multiple_of(lo * W, W), W), :])
        inv = inv_w / (hi - lo).astype(f32)
        inv = jnp.concatenate([inv] * (GROUP_C // V7X_LANES), axis=1)
        ug = u_ref[r * W:(r + 1) * W, :].astype(f32)
        return (acc * inv - ug).astype(bf16)

    tr = pbuf_ref.shape[1]
    pws = (pw_ref[...] * sc_ref[...]).astype(bf16)

    def matmul(t):
        lhs = jnp.concatenate([pooled_minus_self(t * (tr // W) + j) for j in range(tr // W)], axis=0)
        return jnp.dot(lhs, pws, preferred_element_type=f32)

    def gate_store(t, y):
        sl = slice(t * tr, (t + 1) * tr)
        o_ref[sl, :] = y.astype(bf16) * _silu(z_ref[sl, :])

    _pipelined_matmuls((rows * W) // tr, matmul, gate_store, pbuf_ref)


def _pool_mix(u, z, pool_w, scale):
    B, T, _ = u.shape
    rows = T // GRID_W
    band, inv_w = _pool_tables()
    n = GROUP_C
    blk = pl.BlockSpec((None, T, n), lambda b, g: (b, 0, g))
    est = 3 * 2 * T * n * 2 + (rows + 1) * GRID_W * n * 4 + T * n * 2 + 2 * n * n * 4 + 8 * 512 * n * 4
    return pl.pallas_call(
        functools.partial(_pool_kernel, rows=rows),
        grid=(B, N_GROUPS_C),
        in_specs=[blk, blk,
                  pl.BlockSpec((None, GRID_W, GRID_W), lambda b, g: (g, 0, 0)),
                  pl.BlockSpec((None, GRID_W, V7X_LANES), lambda b, g: (g, 0, 0)),
                  pl.BlockSpec((None, n, n), lambda b, g: (g, 0, 0)),
                  pl.BlockSpec((None, 1, n), lambda b, g: (g, 0, 0))],
        out_specs=blk,
        out_shape=jax.ShapeDtypeStruct((B, T, D_INNER), bf16),
        scratch_shapes=[pltpu.VMEM(((rows + 1) * GRID_W, n), f32),
                        pltpu.VMEM((3, 512, n), f32)],
        compiler_params=pltpu.CompilerParams(dimension_semantics=("parallel", "arbitrary"),
                                             vmem_limit_bytes=_vmem_limit(est)),
        name="pool_mix",
    )(u, z, band, inv_w, pool_w, scale.reshape(N_GROUPS_C, 1, n))


def _outproj_odd_kernel(x_ref, y_ref, gate_ref, fg_ref, w_ref, out_ref):
    acc = jnp.dot(y_ref[...], w_ref[...], preferred_element_type=f32)
    x = x_ref[...] + gate_ref[...] * acc
    out_ref[...] = x * lax.rsqrt(jnp.mean(x * x, axis=-1, keepdims=True) + EPS) * fg_ref[...]


def _outproj_odd(x, y, gate, final_g, wout, tm):
    B, T, D = x.shape
    row = lambda n: pl.BlockSpec((None, tm, n), lambda b, i: (b, i, 0))
    est = 2 * wout.size * 2 + 2 * tm * (2 * D * 4 + D_INNER * 2) + 6 * tm * D * 4
    return pl.pallas_call(
        _outproj_odd_kernel,
        grid=(B, T // tm),
        in_specs=[row(D), row(D_INNER),
                  pl.BlockSpec((None, 1, D), lambda b, i: (b, 0, 0)),
                  pl.BlockSpec((1, D), lambda b, i: (0, 0)),
                  pl.BlockSpec(wout.shape, lambda b, i: (0, 0))],
        out_specs=row(D),
        out_shape=jax.ShapeDtypeStruct((B, T, D), f32),
        compiler_params=pltpu.CompilerParams(dimension_semantics=("parallel", "parallel"),
                                             vmem_limit_bytes=_vmem_limit(est)),
        name="outproj_odd",
    )(x, y, gate, final_g.reshape(1, D), wout)


def kernel(x, c, ctx, c_ctx, ada_w, ada_b, norm_g, win_even, gate_b_even, conv_qk_even, head_norm_even,
           fourier_w_even, wout_even, win_odd, pool_w_odd, pool_scale_odd, wout_odd, final_g):
    B, T, D = x.shape
    Tc = ctx.shape[1]
    H = N_HEADS_A
    L = MLSTM_CHUNK

    nrow = -(-(B + 1) // V7X_SUBLANES) * V7X_SUBLANES
    rows_in = jnp.zeros((nrow, D), f32).at[:B].set(c).at[B].set(c_ctx)
    mod = _modulation(rows_in, ada_w, ada_b)

    def mod_parts(l, r0, r1, n):
        m = mod[l, r0:r1]
        parts = [jnp.broadcast_to(m[:, None, i * D:(i + 1) * D], (n, 1, D)) for i in range(3)]
        return parts

    we = win_even[0]
    w_main = we.astype(bf16)
    gcols = we[:, W_MAIN_EVEN:].reshape(D, 4, H)
    gbias = gate_b_even[0].reshape(4, H)
    ig_w = gcols[:, 0::2, :].transpose(0, 2, 1).reshape(D, GATE_SLOTS)
    fg_w = gcols[:, 1::2, :].transpose(0, 2, 1).reshape(D, GATE_SLOTS)
    ig_b = gbias[0::2, :].T.reshape(GATE_SLOTS)
    fg_b = gbias[1::2, :].T.reshape(GATE_SLOTS)
    wgt =jnp.concatenate([ig_w, fg_w], axis=1).T.astype(bf16)
    gbt = jnp.concatenate([ig_b, fg_b]).reshape(N_GATES, 1)
    conv_w = conv_qk_even[0]

    shift_x, scale_x, gate_x = mod_parts(0, 0, B, B)
    shift_c, scale_c, gate_c = mod_parts(0, B, B + 1, B)

    qx, kx, vx, ox, zax, ubx, zbx, gtx = _inproj_even(
        x, shift_x, scale_x, norm_g[0], w_main, wgt, gbt, conv_w, tm=512)
    qc, kc, vc, oc, zac, ubc, zbc, gtc = _inproj_even(
        ctx, shift_c, scale_c, norm_g[0], w_main, wgt, gbt, conv_w, tm=Tc)

    colsx, rowsx = _gate_prep(gtx, L)
    colsc, rowsc = _gate_prep(gtc, L)
    h_x, h_c = _mlstm(qx, kx, vx, colsx, rowsx, qc, kc, vc, colsc, rowsc, L)

    wcs = _fourier_weights(fourier_w_even[0], T)
    yb_x = _fourier_latent(ubx, zbx, wcs)
    wout_e = wout_even[0].astype(bf16)
    shift_1, scale_1, gate_1 = mod_parts(1, 0, B, B)
    x1, u1, z1 = _even_out_odd_in(x, h_x, ox, zax, yb_x, gate_x, head_norm_even[0], wout_e,
                                  shift_1, scale_1, norm_g[1], win_odd[0].astype(bf16), tm=512)

    wcs_c = _fourier_weights(fourier_w_even[0], Tc)
    yb_c = _fourier_ctx(ubc, zbc, wcs_c)
    ctx1 = _outproj_even(ctx, h_c, oc, zac, yb_c, gate_c, head_norm_even[0], wout_e, tm=Tc)
    del ctx1

    y1 = _pool_mix(u1, z1, pool_w_odd[0], pool_scale_odd[0])
    return _outproj_odd(x1, y1, gate_1, final_g, wout_odd[0].astype(bf16), tm=1024)
```

```python
import functools

import numpy as np
import jax
import jax.numpy as jnp
from jax import lax
from jax.experimental import pallas as pl
from jax.experimental.pallas import tpu as pltpu

D_MODEL = 1024
DEPTH = 2
CTX_LEN = 256
GRID_W = 64
D_INNER = 2 * D_MODEL
D_A = D_INNER // 2
D_B = D_INNER - D_A
N_HEADS_A = 4
HEAD_DIM_A = D_A // N_HEADS_A
N_GROUPS_B = 4
GROUP_B = D_B // N_GROUPS_B
N_GROUPS_C = 4
GROUP_C = D_INNER // N_GROUPS_C
POOL_WINDOWS = (2, 4, 8, 16)
CONV_W = 3
N_GATES = 4 * N_HEADS_A
W_MAIN_EVEN = 5 * D_A + 2 * D_B
EPS = 1e-6

f32 = jnp.float32
bf16 = jnp.bfloat16

V7X_VMEM_BYTES = 64 * 1024 * 1024
V7X_LANES = 128
V7X_SUBLANES = 8

MLSTM_CHUNK = 256
MLSTM_HEADS_PER_STEP = 1
FFT_N1 = 64
FFT_N2 = 64
FFT_PITCH = 72
FFT_UNROLL = 8
NEG_BIG = -1e30


def _vmem_limit(nbytes):
    return int(min(max(nbytes * 5 // 4 + (4 << 20), 16 << 20), V7X_VMEM_BYTES - (6 << 20)))


def _sigmoid(v):
    return 0.5 * jnp.tanh(0.5 * v) + 0.5


def _silu(v):
    return v * _sigmoid(v)


def _pipelined_matmuls(n, matmul, epilogue, pbuf_ref):
    zero = jnp.minimum(pl.program_id(0), 0)
    nbuf = pbuf_ref.shape[0]
    pbuf_ref[zero] = matmul(0)
    for t in range(n):
        if t + 1 < n:
            pbuf_ref[zero + (t + 1) % nbuf] = matmul(t + 1)
        epilogue(t, pbuf_ref[zero + t % nbuf])


def _log_sigmoid(v):
    return jnp.minimum(v, 0.0) - jnp.log1p(jnp.exp(-jnp.abs(v)))


def _mod_kernel(r_ref, w_ref, b_ref, o_ref):
    s = _silu(r_ref[...])
    o_ref[...] = jnp.dot(s, w_ref[...], preferred_element_type=f32,
                         precision=lax.Precision.HIGHEST) + b_ref[...]


def _modulation(rows, ada_w, ada_b):
    nrow = rows.shape[0]
    tn = 1024
    return pl.pallas_call(
        _mod_kernel,
        grid=(DEPTH, 3 * D_MODEL // tn),
        in_specs=[
            pl.BlockSpec((nrow, D_MODEL), lambda l, j: (0, 0)),
            pl.BlockSpec((None, D_MODEL, tn), lambda l, j: (l, 0, j)),
            pl.BlockSpec((None, 1, tn), lambda l, j: (l, 0, j)),
        ],
        out_specs=pl.BlockSpec((None, nrow, tn), lambda l, j: (l, 0, j)),
        out_shape=jax.ShapeDtypeStruct((DEPTH, nrow, 3 * D_MODEL), f32),
        compiler_params=pltpu.CompilerParams(dimension_semantics=("arbitrary", "arbitrary")),
        name="modulation",
    )(rows, ada_w, ada_b.reshape(DEPTH, 1, 3 * D_MODEL))


def _normed(x, amp, shift):
    ms = jnp.mean(x * x, axis=-1, keepdims=True)
    return (x * lax.rsqrt(ms + EPS)) * amp + shift


def _inproj_even_kernel(x_ref, xp_ref, xn_ref, sh_ref, sc_ref, g_ref, w_ref, wgt_ref, gbt_ref,
                        cw_ref, q_ref, k_ref, v_ref, o_ref, za_ref, ub_ref, zb_ref, gt_ref, pbuf_ref, *, tm, nt):
    i = pl.program_id(1)
    amp = g_ref[...] * (1.0 + sc_ref[...])
    shift = sh_ref[...]
    hx = _normed(x_ref[...], amp, shift).astype(bf16)
    halo = jnp.concatenate([xp_ref[...], xn_ref[...]], axis=0)
    hh = _normed(halo, amp, shift).astype(bf16)
    has_prev = (i > 0).astype(f32)
    has_next = (i < nt - 1).astype(f32)
    row = lax.broadcasted_iota(jnp.int32, (tm, 1), 0)
    cn = 512
    ph = jnp.dot(hh, w_ref[:, :2 * D_A], preferred_element_type=f32)
    prev = ph[V7X_SUBLANES - 1:V7X_SUBLANES, :] * has_prev
    nxt = ph[V7X_SUBLANES:V7X_SUBLANES + 1, :] * has_next

    def conv_store(j, p):
        cols = slice(j * cn, (j + 1) * cn)
        up = jnp.where(row == 0, prev[:, cols], pltpu.roll(p, 1, 0))
        dn = jnp.where(row == tm - 1, nxt[:, cols], pltpu.roll(p, tm - 1, 0))
        cw = cw_ref[:, cols]
        y = _silu(cw[0:1, :] * up + cw[1:2, :] * p + cw[2:3, :] * dn)
        if j < D_A // cn:
            q_ref[:, cols] = (y * (HEAD_DIM_A ** -0.5)).astype(bf16)
        else:
            jj = j - D_A // cn
            k_ref[:, jj * cn:(jj + 1) * cn] = y.astype(bf16)

    def plain_store(ref, jj):
        def store(p):
            ref[:, jj * cn:(jj + 1) * cn] = p.astype(bf16)
        return store

    tasks = [(j * cn, functools.partial(conv_store, j)) for j in range(2 * D_A // cn)]
    for idx, ref in enumerate((v_ref, o_ref, za_ref, ub_ref, zb_ref)):
        for jj in range(D_A // cn):
            tasks.append((2 * D_A + idx * D_A + jj * cn, plain_store(ref, jj)))

    def matmul(t):
        c0 = tasks[t][0]
        return jnp.dot(hx, w_ref[:, c0:c0 + cn], preferred_element_type=f32)

    _pipelined_matmuls(len(tasks), matmul, lambda t, p: tasks[t][1](p), pbuf_ref)
    gt_ref[...] = lax.dot_general(wgt_ref[...], hx, (((1,), (1,)), ((), ())),
                                  preferred_element_type=f32) + gbt_ref[...]


def _inproj_even(x, shift, scale, norm_g, w_main, wgt, gbt, conv_w, tm):
    B, T, D = x.shape
    nt = T // tm
    hb = tm // V7X_SUBLANES
    nhb = T // V7X_SUBLANES
    row_spec = pl.BlockSpec((None, tm, D_A), lambda b, i: (b, i, 0))
    vec = lambda n: pl.BlockSpec((1, n), lambda b, i: (0, 0))
    est = (2 * w_main.size * 2 + 2 * tm * D * 4 + 7 * 2 * tm * D_A * 2 + 6 * tm * 512 * 4)
    outs = pl.pallas_call(
        functools.partial(_inproj_even_kernel, tm=tm, nt=nt),
        grid=(B, nt),
        in_specs=[
            pl.BlockSpec((None, tm, D), lambda b, i: (b, i, 0)),
            pl.BlockSpec((None, V7X_SUBLANES, D), lambda b, i: (b, jnp.maximum(i * hb - 1, 0), 0)),
            pl.BlockSpec((None, V7X_SUBLANES, D), lambda b, i: (b, jnp.minimum((i + 1) * hb, nhb - 1), 0)),
            pl.BlockSpec((None, 1, D), lambda b, i: (b, 0, 0)),
            pl.BlockSpec((None, 1, D), lambda b, i: (b, 0, 0)),
            vec(D),
            pl.BlockSpec(w_main.shape, lambda b, i: (0, 0)),
            pl.BlockSpec(wgt.shape, lambda b, i: (0, 0)),
            pl.BlockSpec((N_GATES, 1), lambda b, i: (0, 0)),
            pl.BlockSpec(conv_w.shape, lambda b, i: (0, 0)),
        ],
        out_specs=[row_spec] * 7 + [pl.BlockSpec((None, N_GATES, tm), lambda b, i: (b, 0, i))],
        out_shape=[jax.ShapeDtypeStruct((B, T, D_A), bf16)] * 7 + [jax.ShapeDtypeStruct((B, N_GATES, T), f32)],
        scratch_shapes=[pltpu.VMEM((3, tm, 512), f32)],
        compiler_params=pltpu.CompilerParams(dimension_semantics=("parallel", "arbitrary"),
                                             vmem_limit_bytes=_vmem_limit(est)),
        name="inproj_even",
    )(x, x, x, shift, scale, norm_g.reshape(1, D), w_main, wgt, gbt, conv_w)
    return outs


GATE_SLOTS = 2 * N_HEADS_A
GATE_PIECES = 3
GATE_QUANTS = 3
assert GATE_SLOTS & (GATE_SLOTS - 1) == 0 and GATE_QUANTS * GATE_PIECES * GATE_SLOTS <= V7X_LANES


def _scan_max_lanes(x, seg, reverse):
    n = x.shape[1]
    pos = lax.broadcasted_iota(jnp.int32, x.shape, 1) & (seg - 1)
    s = 1
    while s < seg:
        if reverse:
            x = jnp.where(pos < seg - s, jnp.maximum(x, pltpu.roll(x, n - s, 1)), x)
        else:
            x = jnp.where(pos >= s, jnp.maximum(x, pltpu.roll(x, s, 1)), x)
        s *= 2
    return x


def _split3(v):
    hi = v.astype(bf16)
    r1 = v - hi.astype(f32)
    mid = r1.astype(bf16)
    lo = (r1 - mid.astype(f32)).astype(bf16)
    return hi, mid, lo


def _gate_prep_kernel(gt_ref, cols_ref, arow_ref, *, L, nchunk):
    r = lax.broadcasted_iota(jnp.int32, (L, L), 0)
    c = lax.broadcasted_iota(jnp.int32, (L, L), 1)
    tri_l = (c <= r).astype(bf16)
    tri_u = (c >= r).astype(bf16)
    S = V7X_SUBLANES
    tb = nchunk * L
    fwd = (lax.broadcasted_iota(jnp.int32, (S, tb), 0) & 1) == 0
    ig = gt_ref[:S, :]
    lf = _log_sigmoid(gt_ref[S:, :])
    pieces = jnp.concatenate([p.astype(f32) for p in _split3(lf)] + [jnp.zeros((S, tb), f32)],
                             axis=0).astype(bf16)
    chunks = [slice(ci * L, (ci + 1) * L) for ci in range(nchunk)]
    pre = jnp.concatenate([jnp.dot(pieces[:, sl], tri_u, preferred_element_type=f32) for sl in chunks], axis=1)
    suf = jnp.concatenate([jnp.dot(pieces[:, sl], tri_l, preferred_element_type=f32) for sl in chunks], axis=1)
    pre = pre[:S] + pre[S:2 * S] + pre[2 * S:3 * S]
    suf = suf[:S] + suf[S:2 * S] + suf[2 * S:3 * S]
    b_row = jnp.where(fwd, pre, suf)
    a_row = ig - b_row
    cmax = jnp.where(fwd, _scan_max_lanes(a_row, L, False), _scan_max_lanes(a_row, L, True))
    arow_ref[...] = a_row
    parts = [p.astype(f32) for quant in (b_row, a_row, cmax) for p in _split3(quant)]
    fill = jnp.zeros((V7X_LANES - len(parts) * S, tb), f32)
    packed = jnp.concatenate(parts + [fill], axis=0)
    for sl in chunks:
        cols_ref[sl, :] = packed[:, sl].T.astype(bf16)


def _gate_prep(gt, L):
    B, _, T = gt.shape
    nchunk = min(4, T // L)
    tb = nchunk * L
    cols, arow = pl.pallas_call(
        functools.partial(_gate_prep_kernel, L=L, nchunk=nchunk),
        grid=(B, T // tb),
        in_specs=[pl.BlockSpec((None, N_GATES, tb), lambda b, i: (b, 0, i))],
        out_specs=[
            pl.BlockSpec((None, tb, V7X_LANES), lambda b, i: (b, i, 0)),
            pl.BlockSpec((None, V7X_SUBLANES, tb), lambda b, i: (b, 0, i)),
        ],
        out_shape=[
            jax.ShapeDtypeStruct((B, T, V7X_LANES), bf16),
            jax.ShapeDtypeStruct((B, V7X_SUBLANES, T), f32),
        ],
        compiler_params=pltpu.CompilerParams(dimension_semantics=("parallel", "parallel")),
        name="gate_prep",
    )(gt)
    return cols, arow.reshape(B, N_HEADS_A, 2, T)


def _mlstm_kernel(qx_ref, kx_ref, vx_ref, cx_ref, rx_ref, qc_ref, kc_ref, vc_ref, cc_ref, rc_ref,
                  hx_ref, hc_ref, c_ref, n_ref, *, L, nx, nc):
    Dh, LN, HP = HEAD_DIM_A, V7X_LANES, MLSTM_HEADS_PER_STEP
    head0 = pl.program_id(1) * HP
    chains = [(hh, d) for hh in range(HP) for d in range(2)]
    r_i = lax.broadcasted_iota(jnp.int32, (L, L), 0)
    c_i = lax.broadcasted_iota(jnp.int32, (L, L), 1)
    masks = (c_i <= r_i, c_i >= r_i)

    sr = lax.broadcasted_iota(jnp.int32, (LN, GATE_QUANTS * LN), 0)
    sc = lax.broadcasted_iota(jnp.int32, (LN, GATE_QUANTS * LN), 1)
    span = GATE_PIECES * GATE_SLOTS
    in_block = None
    for qi in range(GATE_QUANTS):
        blk = (sr >= qi * span) & (sr < (qi + 1) * span) & (sc >= qi * LN) & (sc < (qi + 1) * LN)
        in_block = blk if in_block is None else in_block | blk
    slot = sr & (GATE_SLOTS - 1)
    sels = {(hh, d): (in_block & (slot == 2 * (head0 + hh) + d)).astype(bf16) for hh, d in chains}

    def tile(v, width):
        return jnp.concatenate([v] * (width // LN), axis=1)

    def step(refs, r0s, ms):
        q_ref, k_ref, v_ref, col_ref, row_ref = refs
        st = []
        for ci, (hh, d) in enumerate(chains):
            rows = pl.ds(r0s[ci], L)
            hsl = slice(hh * Dh, (hh + 1) * Dh)
            q = q_ref[rows, hsl]
            k = k_ref[rows, hsl]
            rep = jnp.dot(col_ref[rows, :], sels[hh, d], preferred_element_type=f32)
            qk = lax.dot_general(q, k, (((1,), (1,)), ((), ())), preferred_element_type=f32)
            qc = jnp.dot(q, c_ref[ci].astype(bf16), preferred_element_type=f32)
            st.append(dict(q=q, k=k, v=v_ref[rows, hsl], rep=rep, qk=qk, qc=qc, a_row=row_ref[hh, d:d + 1, rows]))
        new_ms = []
        for ci, (hh, d) in enumerate(chains):
            c, m_prev = st[ci], ms[ci]
            b_rep, a_rep = c["rep"][:, :LN], c["rep"][:, LN:2 * LN]
            g_rep = jnp.maximum(c["rep"][:, 2 * LN:], m_prev)
            p = jnp.exp(jnp.where(masks[d], c["a_row"] - tile(g_rep, L), NEG_BIG))
            s = c["qk"] * p
            c["s"] = s.astype(bf16)
            inter = jnp.exp(m_prev - g_rep)
            qn = jnp.sum(c["q"].astype(f32) * n_ref[ci], axis=1, keepdims=True)
            den = jnp.sum(s, axis=1, keepdims=True) + inter[:, :1] * qn
            floor = jnp.exp(-(b_rep + g_rep))
            rcp = 1.0 / jnp.maximum(jnp.abs(den), floor[:, :1])
            c["rcp"] = jnp.broadcast_to(rcp, (L, LN))
            c["inter"] = inter
            b_end = b_rep[L - 1:L, :] if d == 0 else b_rep[0:1, :]
            w = b_end + a_rep
            m_new = jnp.maximum(b_end + m_prev, jnp.max(w, axis=0, keepdims=True))
            c["decay"] = jnp.exp(b_end + m_prev - m_new)
            kw = c["k"].astype(f32) * tile(jnp.exp(w - m_new), Dh)
            c["kw"] = kw.astype(bf16)
            c["ksum"] = jnp.sum(kw, axis=0, keepdims=True)
            new_ms.append(m_new)
        for c in st:
            c["sv"] = jnp.dot(c["s"], c["v"], preferred_element_type=f32)
            c["upd"] = lax.dot_general(c["kw"], c["v"], (((0,), (0,)), ((), ())), preferred_element_type=f32)
        hs = []
        for ci, c in enumerate(st):
            hs.append((c["sv"] + tile(c["inter"], Dh) * c["qc"]) * tile(c["rcp"], Dh))
            decay = tile(c["decay"], Dh)
            c_ref[ci] = decay * c_ref[ci] + c["upd"]
            n_ref[ci] = decay * n_ref[ci] + c["ksum"]
        return hs, new_ms

    ctx = (qc_ref, kc_ref, vc_ref, cc_ref, rc_ref)
    lat = (qx_ref, kx_ref, vx_ref, cx_ref, rx_ref)

    c_ref[...] = jnp.zeros_like(c_ref)
    n_ref[...] = jnp.zeros_like(n_ref)
    ms = [jnp.zeros((1, LN), f32) for _ in chains]
    written = set()
    for j in range(nc):
        cjs = [j if d == 0 else nc - 1 - j for _, d in chains]
        hs, ms = step(ctx, [cj * L for cj in cjs], ms)
        for (hh, d), cj, h in zip(chains, cjs, hs):
            dst = (slice(cj * L, (cj + 1) * L), slice(hh * Dh, (hh + 1) * Dh))
            if (cj, hh) in written:
                hc_ref[dst] = (hc_ref[dst].astype(f32) + h).astype(hc_ref.dtype)
            else:
                hc_ref[dst] = h.astype(hc_ref.dtype)
                written.add((cj, hh))

    def make_body(accumulate):
        def body(i, ms):
            r0s = [pl.multiple_of((i if d == 0 else nx - 1 - i) * L, L) for _, d in chains]
            hs, ms = step(lat, r0s, list(ms))
            for (hh, d), r0, h in zip(chains, r0s, hs):
                dst = (pl.ds(r0, L), slice(hh * Dh, (hh + 1) * Dh))
                if accumulate:
                    hx_ref[dst] = (hx_ref[dst].astype(f32) + h).astype(hx_ref.dtype)
                else:
                    hx_ref[dst] = h.astype(hx_ref.dtype)
            return tuple(ms)
        return body

    ms = lax.fori_loop(0, nx // 2, make_body(False), tuple(ms))
    lax.fori_loop(nx // 2, nx, make_body(True), ms)


def _mlstm(qx, kx, vx, colsx, rowsx, qc, kc, vc, colsc, rowsc, L):
    B, T, _ = qx.shape
    Tc = qc.shape[1]
    H, Dh, HP = N_HEADS_A, HEAD_DIM_A, MLSTM_HEADS_PER_STEP
    assert T % (2 * L) == 0 and Tc % L == 0 and H % HP == 0

    def seq_spec(t):
        return pl.BlockSpec((None, t, HP * Dh), lambda b, h: (b, 0, h))

    def col_spec(t):
        return pl.BlockSpec((None, t, V7X_LANES), lambda b, h: (b, 0, 0))

    def row_spec(t):
        return pl.BlockSpec((None, HP, 2, t), lambda b, h: (b, h, 0, 0))

    est = (2 * HP * (3 * (T + Tc) * Dh * 2 + 8 * (T + Tc) * 4 + (T + Tc) * Dh * 2)
           + 2 * (T + Tc) * V7X_LANES * 2 + 2 * HP * Dh * (Dh + V7X_LANES) * 4 + 16 * L * L * 4)
    return pl.pallas_call(
        functools.partial(_mlstm_kernel, L=L, nx=T // L, nc=Tc // L),
        grid=(B, H // HP),
        in_specs=[seq_spec(T), seq_spec(T), seq_spec(T), col_spec(T), row_spec(T),
                  seq_spec(Tc), seq_spec(Tc), seq_spec(Tc), col_spec(Tc), row_spec(Tc)],
        out_specs=[seq_spec(T), seq_spec(Tc)],
        out_shape=[jax.ShapeDtypeStruct((B, T, D_A), bf16), jax.ShapeDtypeStruct((B, Tc, D_A), bf16)],
        scratch_shapes=[pltpu.VMEM((2 * HP, Dh, Dh), f32), pltpu.VMEM((2 * HP, 1, Dh), f32)],
        compiler_params=pltpu.CompilerParams(dimension_semantics=("parallel", "parallel"),
                                             vmem_limit_bytes=_vmem_limit(est)),
        name="mlstm",
    )(qx, kx, vx, colsx, rowsx, qc, kc, vc, colsc, rowsc)


def _fourier_w_kernel(cs_ref, fw_ref, o_ref, *, scale):
    o_ref[...] = (jnp.dot(cs_ref[...], fw_ref[...], preferred_element_type=f32,
                          precision=lax.Precision.HIGHEST) * scale).astype(bf16)


def _fourier_weights(fw, T):
    n = GROUP_B
    kk = np.outer(np.arange(n), np.arange(n)) % n
    ang = 2.0 * np.pi * kk / n
    cs = jnp.asarray(np.concatenate([np.cos(ang), np.sin(ang)], axis=0), f32)
    return pl.pallas_call(
        functools.partial(_fourier_w_kernel, scale=float(1.0 / np.sqrt(T * n))),
        grid=(N_GROUPS_B,),
        in_specs=[pl.BlockSpec((2 * n, n), lambda g: (0, 0)),
                  pl.BlockSpec((None, n, n), lambda g: (g, 0, 0))],
        out_specs=pl.BlockSpec((None, 2 * n, n), lambda g: (g, 0, 0)),
        out_shape=jax.ShapeDtypeStruct((N_GROUPS_B, 2 * n, n), bf16),
        compiler_params=pltpu.CompilerParams(dimension_semantics=("arbitrary",)),
        name="fourier_weights",
    )(cs, fw)


def _fft_tables():
    n1, n2 = FFT_N1, FFT_N2
    n = n1 * n2
    t1 = np.arange(n1)
    k1 = np.arange(n1)
    t2 = np.arange(n2)
    idx = (k1[None, :, None] * (n2 * t1[None, None, :] + t2[:, None, None])) % n
    ang = 2.0 * np.pi * idx / n
    m1 = np.concatenate([np.cos(ang), -np.sin(ang)], axis=1)
    k2 = np.arange(n2)
    ph = 2.0 * np.pi * (np.outer(k2, t2) % n2) / n2
    c, s = np.cos(ph), np.sin(ph)
    m2 = np.block([[c, s], [-s, c]])
    return jnp.asarray(m1, f32).astype(bf16), jnp.asarray(m2, f32).astype(bf16)


def _fourier_kernel(u_ref, z_ref, m1_ref, m2_ref, w_ref, o_ref, up_ref, yp_ref, xp_ref, pbuf_ref):
    n1, n2, P = FFT_N1, FFT_N2, FFT_PITCH
    nl = GROUP_B // V7X_LANES

    def fill(t1, carry):
        src = pl.multiple_of(t1 * n2, n2)
        dst = pl.multiple_of(t1 * P, V7X_SUBLANES)
        blk = u_ref[pl.ds(src, n2), :].astype(f32)
        for s in range(nl):
            up_ref[s, pl.ds(dst, n2), :] = blk[:, s * V7X_LANES:(s + 1) * V7X_LANES]
        return carry

    lax.fori_loop(0, n1, fill, 0, unroll=4)

    G = FFT_UNROLL

    def stage1(grp, carry):
        t2s = [grp * G + j for j in range(G)]
        rhs = [jnp.concatenate([up_ref[s, pl.ds(t2, n1, stride=P), :] for s in range(nl)],
                               axis=1).astype(bf16) for t2 in t2s]
        ys = [jnp.dot(m1_ref[t2], r, preferred_element_type=f32) for t2, r in zip(t2s, rhs)]
        for t2, y in zip(t2s, ys):
            dst = pl.multiple_of(t2 * P, V7X_SUBLANES)
            for ri in range(2):
                for s in range(nl):
                    yp_ref[ri * nl + s, pl.ds(dst, n1), :] = y[ri * n1:(ri + 1) * n1,
                                                               s * V7X_LANES:(s + 1) * V7X_LANES]
        return carry

    lax.fori_loop(0, n2 // G, stage1, 0)

    m2 = m2_ref[...]

    def stage2(grp, carry):
        k1s = [grp * G + j for j in range(G)]
        rhs = []
        for k1 in k1s:
            parts = [jnp.concatenate([yp_ref[ri * nl + s, pl.ds(k1, n2, stride=P), :] for s in range(nl)], axis=1)
                     for ri in range(2)]
            rhs.append(jnp.concatenate(parts, axis=0).astype(bf16))
        xs = [jnp.dot(m2, r, preferred_element_type=f32) for r in rhs]
        for k1, x in zip(k1s, xs):
            for ri in range(2):
                for s in range(nl):
                    xp_ref[ri * nl + s, pl.ds(k1, n2, stride=P), :] = x[ri * n2:(ri + 1) * n2,
                                                                        s * V7X_LANES:(s + 1) * V7X_LANES]
        return carry

    lax.fori_loop(0, n1 // G, stage2, 0)

    tr = pbuf_ref.shape[1]

    def matmul(t):
        blocks = []
        for k2 in range(t * (tr // n1), (t + 1) * (tr // n1)):
            blocks.append(jnp.concatenate([xp_ref[sl, k2 * P:k2 * P + n1, :] for sl in range(2 * nl)],
                                          axis=1).astype(bf16))
        return jnp.dot(jnp.concatenate(blocks, axis=0), w_ref[...], preferred_element_type=f32)

    def gate_store(t, y):
        sl = slice(t * tr, (t + 1) * tr)
        o_ref[sl, :] = y.astype(bf16) * _silu(z_ref[sl, :])

    _pipelined_matmuls((n1 * n2) // tr, matmul, gate_store, pbuf_ref)


def _fourier_latent(ub, zb, wcs):
    B, T, _ = ub.shape
    assert T == FFT_N1 * FFT_N2
    m1, m2 = _fft_tables()
    G, n = N_GROUPS_B, GROUP_B
    nl = n // V7X_LANES
    rows_p = FFT_N1 * FFT_PITCH
    blk = pl.BlockSpec((None, T, n), lambda b, g: (b, 0, g))
    est = (3 * 2 * T * n * 2 + 5 * nl * rows_p * V7X_LANES * 4 + T * 2 * n * 2 + 2 * m1.size * 2 + 8 * 512 * n * 4)
    return pl.pallas_call(
        _fourier_kernel,
        grid=(B, G),
        in_specs=[blk, blk,
                  pl.BlockSpec(m1.shape, lambda b, g: (0, 0, 0)),
                  pl.BlockSpec(m2.shape, lambda b, g: (0, 0)),
                  pl.BlockSpec((None, 2 * n, n), lambda b, g: (g, 0, 0))],
        out_specs=blk,
        out_shape=jax.ShapeDtypeStruct((B, T, D_B), bf16),
        scratch_shapes=[pltpu.VMEM((nl, rows_p, V7X_LANES), f32),
                        pltpu.VMEM((2 * nl, rows_p, V7X_LANES), f32),
                        pltpu.VMEM((2 * nl, rows_p, V7X_LANES), f32),
                        pltpu.VMEM((3, 512, n), f32)],
        compiler_params=pltpu.CompilerParams(dimension_semantics=("parallel", "parallel"),
                                             vmem_limit_bytes=_vmem_limit(est)),
        name="fourier_latent",
    )(ub, zb, m1, m2, wcs)


def _fourier_ctx_kernel(u_ref, z_ref, cs_ref, w_ref, o_ref):
    u = u_ref[...]
    x = jnp.dot(cs_ref[...], u, preferred_element_type=f32).astype(bf16)
    t = u.shape[0]
    w = w_ref[...]
    n = GROUP_B
    y = (jnp.dot(x[:t], w[:n], preferred_element_type=f32) + jnp.dot(x[t:], w[n:], preferred_element_type=f32))
    o_ref[...] = (y * _silu(z_ref[...].astype(f32))).astype(bf16)


def _fourier_ctx(ub, zb, wcs):
    B, T, _ = ub.shape
    ang = 2.0 * np.pi * (np.outer(np.arange(T), np.arange(T)) % T) / T
    cs = jnp.asarray(np.concatenate([np.cos(ang), -np.sin(ang)], axis=0), f32).astype(bf16)
    n = GROUP_B
    blk = pl.BlockSpec((None, T, n), lambda b, g: (b, 0, g))
    return pl.pallas_call(
        _fourier_ctx_kernel,
        grid=(B, N_GROUPS_B),
        in_specs=[blk, blk, pl.BlockSpec(cs.shape, lambda b, g: (0, 0)),
                  pl.BlockSpec((None, 2 * n, n), lambda b, g: (g, 0, 0))],
        out_specs=blk,
        out_shape=jax.ShapeDtypeStruct((B, T, D_B), bf16),
        compiler_params=pltpu.CompilerParams(dimension_semantics=("parallel", "parallel")),
        name="fourier_ctx",
    )(ub, zb, cs, wcs)


def _outproj_even_kernel(x_ref, h_ref, o_ref, za_ref, yb_ref, gate_ref, hg_ref, w_ref, out_ref):
    parts = []
    for hd in range(N_HEADS_A):
        sl = slice(hd * HEAD_DIM_A, (hd + 1) * HEAD_DIM_A)
        hh = _sigmoid(o_ref[:, sl]).astype(f32) * h_ref[:, sl]
        hh = hh * lax.rsqrt(jnp.mean(hh * hh, axis=-1, keepdims=True) + EPS)
        parts.append(hh.astype(bf16) * (hg_ref[:, sl].astype(bf16) * _silu(za_ref[:, sl])))
    ya = jnp.concatenate(parts, axis=1)
    acc = jnp.dot(ya, w_ref[:D_A, :], preferred_element_type=f32)
    acc = acc + jnp.dot(yb_ref[...], w_ref[D_A:, :], preferred_element_type=f32)
    out_ref[...] = x_ref[...] + gate_ref[...] * acc


def _outproj_even(x, h, o, za, yb, gate, head_g, wout, tm):
    B, T, D = x.shape
    row = lambda n: pl.BlockSpec((None, tm, n), lambda b, i: (b, i, 0))
    est = 2 * wout.size * 2 + 2 * tm * (2 * D * 4 + D_A * 4 + 3 * D_A * 2) + 8 * tm * D * 4
    return pl.pallas_call(
        _outproj_even_kernel,
        grid=(B, T // tm),
        in_specs=[row(D), row(D_A), row(D_A), row(D_A), row(D_B),
                  pl.BlockSpec((None, 1, D), lambda b, i: (b, 0, 0)),
                  pl.BlockSpec((1, D_A), lambda b, i: (0, 0)),
                  pl.BlockSpec(wout.shape, lambda b, i: (0, 0))],
        out_specs=row(D),
        out_shape=jax.ShapeDtypeStruct((B, T, D), f32),
        compiler_params=pltpu.CompilerParams(dimension_semantics=("parallel", "parallel"),
                                             vmem_limit_bytes=_vmem_limit(est)),
        name="outproj_even",
    )(x, h, o, za, yb, gate, head_g.reshape(1, D_A), wout)


def _inproj_odd_kernel(x_ref, sh_ref, sc_ref, g_ref, w_ref, u_ref, z_ref):
    amp = g_ref[...] * (1.0 + sc_ref[...])
    hx = _normed(x_ref[...], amp, sh_ref[...]).astype(bf16)
    cn = 512
    for j in range(D_INNER // cn):
        u_ref[:, j * cn:(j + 1) * cn] = jnp.dot(hx, w_ref[:, j * cn:(j + 1) * cn],
                                                preferred_element_type=f32).astype(bf16)
        z_ref[:, j * cn:(j + 1) * cn] = jnp.dot(hx, w_ref[:, D_INNER + j * cn:D_INNER + (j + 1) * cn],
                                                preferred_element_type=f32).astype(bf16)


def _even_out_odd_in_kernel(x_ref, h_ref, o_ref, za_ref, yb_ref, gate_ref, hg_ref, wo_ref,
                            sh_ref, sc_ref, g_ref, wi_ref, x1_ref, u_ref, z_ref):
    _outproj_even_kernel(x_ref, h_ref, o_ref, za_ref, yb_ref, gate_ref, hg_ref, wo_ref, x1_ref)
    _inproj_odd_kernel(x1_ref, sh_ref, sc_ref, g_ref, wi_ref, u_ref, z_ref)


def _even_out_odd_in(x, h, o, za, yb, gate, head_g, wout, shift, scale, norm_g, win, tm):
    B, T, D = x.shape
    row = lambda n: pl.BlockSpec((None, tm, n), lambda b, i: (b, i, 0))
    bvec = pl.BlockSpec((None, 1, D), lambda b, i: (b, 0, 0))
    est = (2 * (wout.size + win.size) * 2 + 2 * tm * (2 * D * 4 + 4 * D_A * 2 + 2 * D_INNER * 2)
           + 8 * tm * D * 4 + 6 * tm * 512 * 4)
    return pl.pallas_call(
        _even_out_odd_in_kernel,
        grid=(B, T // tm),
        in_specs=[row(D), row(D_A), row(D_A), row(D_A), row(D_B), bvec,
                  pl.BlockSpec((1, D_A), lambda b, i: (0, 0)),
                  pl.BlockSpec(wout.shape, lambda b, i: (0, 0)),
                  bvec, bvec,
                  pl.BlockSpec((1, D), lambda b, i: (0, 0)),
                  pl.BlockSpec(win.shape, lambda b, i: (0, 0))],
        out_specs=[row(D), row(D_INNER), row(D_INNER)],
        out_shape=[jax.ShapeDtypeStruct((B, T, D), f32)] + [jax.ShapeDtypeStruct((B, T, D_INNER), bf16)] * 2,
        compiler_params=pltpu.CompilerParams(dimension_semantics=("parallel", "parallel"),
                                             vmem_limit_bytes=_vmem_limit(est)),
        name="even_out_odd_in",
    )(x, h, o, za, yb, gate, head_g.reshape(1, D_A), wout, shift, scale, norm_g.reshape(1, D), win)


POOL_UNROLL = 8


def _pool_tables():
    w_idx = np.arange(GRID_W)
    band = np.zeros((N_GROUPS_C, GRID_W, GRID_W), np.float32)
    inv_w = np.zeros((N_GROUPS_C, GRID_W, V7X_LANES), np.float32)
    for g, win in enumerate(POOL_WINDOWS):
        lo = np.clip(w_idx - win // 2, 0, GRID_W)
        hi = np.clip(w_idx + win - win // 2, 0, GRID_W)
        band[g] = (w_idx[None, :] >= lo[:, None]) & (w_idx[None, :] < hi[:, None])
        inv_w[g] = (1.0 / (hi - lo))[:, None]
    return jnp.asarray(band, bf16), jnp.asarray(inv_w, f32)


def _pool_kernel(u_ref, z_ref, band_ref, invw_ref, pw_ref, sc_ref, o_ref, ps_ref, pbuf_ref, *, rows):
    g = pl.program_id(1)
    W = GRID_W
    band = band_ref[...]
    lo_off = hi_off = 0
    for gi, win in enumerate(POOL_WINDOWS):
        lo_off = jnp.where(g == gi, win // 2, lo_off)
        hi_off = jnp.where(g == gi, win - win // 2, hi_off)

    ps_ref[0:W, :] = jnp.zeros((W, GROUP_C), f32)

    def width_sum(grp, carry):
        srcs = [pl.multiple_of((grp * POOL_UNROLL + j) * W, W) for j in range(POOL_UNROLL)]
        sums = [jnp.dot(band, u_ref[pl.ds(src, W), :], preferred_element_type=f32) for src in srcs]
        acc = ps_ref[pl.ds(srcs[0], W), :]
        for src, s in zip(srcs, sums):
            acc = acc + s
            ps_ref[pl.ds(src + W, W), :] = acc
        return carry

    lax.fori_loop(0, rows // POOL_UNROLL, width_sum, 0)

    inv_w = invw_ref[...]

    def pooled_minus_self(r):
        lo = jnp.maximum(r - lo_off, 0)
        hi = jnp.minimum(r + hi_off, rows)
        acc = (ps_ref[pl.ds(pl.multiple_of(hi * W, W), W), :]
               - ps_ref[pl.ds(pl.multiple_of(lo * W, W), W), :])
        inv = inv_w / (hi - lo).astype(f32)
        inv = jnp.concatenate([inv] * (GROUP_C // V7X_LANES), axis=1)
        ug = u_ref[r * W:(r + 1) * W, :].astype(f32)
        return (acc * inv - ug).astype(bf16)

    tr = pbuf_ref.shape[1]
    pws = (pw_ref[...] * sc_ref[...]).astype(bf16)

    def matmul(t):
        lhs = jnp.concatenate([pooled_minus_self(t * (tr // W) + j) for j in range(tr // W)], axis=0)
        return jnp.dot(lhs, pws, preferred_element_type=f32)

    def gate_store(t, y):
        sl = slice(t * tr, (t + 1) * tr)
        o_ref[sl, :] = y.astype(bf16) * _silu(z_ref[sl, :])

    _pipelined_matmuls((rows * W) // tr, matmul, gate_store, pbuf_ref)


def _pool_mix(u, z, pool_w, scale):
    B, T, _ = u.shape
    rows = T // GRID_W
    band, inv_w = _pool_tables()
    n = GROUP_C
    blk = pl.BlockSpec((None, T, n), lambda b, g: (b, 0, g))
    est = 3 * 2 * T * n * 2 + (rows + 1) * GRID_W * n * 4 + T * n * 2 + 2 * n * n * 4 + 8 * 512 * n * 4
    return pl.pallas_call(
        functools.partial(_pool_kernel, rows=rows),
        grid=(B, N_GROUPS_C),
        in_specs=[blk, blk,
                  pl.BlockSpec((None, GRID_W, GRID_W), lambda b, g: (g, 0, 0)),
                  pl.BlockSpec((None, GRID_W, V7X_LANES), lambda b, g: (g, 0, 0)),
                  pl.BlockSpec((None, n, n), lambda b, g: (g, 0, 0)),
                  pl.BlockSpec((None, 1, n), lambda b, g: (g, 0, 0))],
        out_specs=blk,
        out_shape=jax.ShapeDtypeStruct((B, T, D_INNER), bf16),
        scratch_shapes=[pltpu.VMEM(((rows + 1) * GRID_W, n), f32),
                        pltpu.VMEM((3, 512, n), f32)],
        compiler_params=pltpu.CompilerParams(dimension_semantics=("parallel", "arbitrary"),
                                             vmem_limit_bytes=_vmem_limit(est)),
        name="pool_mix",
    )(u, z, band, inv_w, pool_w, scale.reshape(N_GROUPS_C, 1, n))


def _outproj_odd_kernel(x_ref, y_ref, gate_ref, fg_ref, w_ref, out_ref):
    acc = jnp.dot(y_ref[...], w_ref[...], preferred_element_type=f32)
    x = x_ref[...] + gate_ref[...] * acc
    out_ref[...] = x * lax.rsqrt(jnp.mean(x * x, axis=-1, keepdims=True) + EPS) * fg_ref[...]


def _outproj_odd(x, y, gate, final_g, wout, tm):
    B, T, D = x.shape
    row = lambda n: pl.BlockSpec((None, tm, n), lambda b, i: (b, i, 0))
    est = 2 * wout.size * 2 + 2 * tm * (2 * D * 4 + D_INNER * 2) + 6 * tm * D * 4
    return pl.pallas_call(
        _outproj_odd_kernel,
        grid=(B, T // tm),
        in_specs=[row(D), row(D_INNER),
                  pl.BlockSpec((None, 1, D), lambda b, i: (b, 0, 0)),
                  pl.BlockSpec((1, D), lambda b, i: (0, 0)),
                  pl.BlockSpec(wout.shape, lambda b, i: (0, 0))],
        out_specs=row(D),
        out_shape=jax.ShapeDtypeStruct((B, T, D), f32),
        compiler_params=pltpu.CompilerParams(dimension_semantics=("parallel", "parallel"),
                                             vmem_limit_bytes=_vmem_limit(est)),
        name="outproj_odd",
    )(x, y, gate, final_g.reshape(1, D), wout)


def kernel(x, c, ctx, c_ctx, ada_w, ada_b, norm_g, win_even, gate_b_even, conv_qk_even, head_norm_even,
           fourier_w_even, wout_even, win_odd, pool_w_odd, pool_scale_odd, wout_odd, final_g):
    B, T, D = x.shape
    Tc = ctx.shape[1]
    H = N_HEADS_A
    L = MLSTM_CHUNK

    nrow = -(-(B + 1) // V7X_SUBLANES) * V7X_SUBLANES
    rows_in = jnp.zeros((nrow, D), f32).at[:B].set(c).at[B].set(c_ctx)
    mod = _modulation(rows_in, ada_w, ada_b)

    def mod_parts(l, r0, r1, n):
        m = mod[l, r0:r1]
        parts = [jnp.broadcast_to(m[:, None, i * D:(i + 1) * D], (n, 1, D)) for i in range(3)]
        return parts

    we = win_even[0]
    w_main = we.astype(bf16)
    gcols = we[:, W_MAIN_EVEN:].reshape(D, 4, H)
    gbias = gate_b_even[0].reshape(4, H)
    ig_w = gcols[:, 0::2, :].transpose(0, 2, 1).reshape(D, GATE_SLOTS)
    fg_w = gcols[:, 1::2, :].transpose(0, 2, 1).reshape(D, GATE_SLOTS)
    ig_b = gbias[0::2, :].T.reshape(GATE_SLOTS)
    fg_b = gbias[1::2, :].T.reshape(GATE_SLOTS)
    wgt =jnp.concatenate([ig_w, fg_w], axis=1).T.astype(bf16)
    gbt = jnp.concatenate([ig_b, fg_b]).reshape(N_GATES, 1)
    conv_w = conv_qk_even[0]

    shift_x, scale_x, gate_x = mod_parts(0, 0, B, B)
    shift_c, scale_c, gate_c = mod_parts(0, B, B + 1, B)

    qx, kx, vx, ox, zax, ubx, zbx, gtx = _inproj_even(
        x, shift_x, scale_x, norm_g[0], w_main, wgt, gbt, conv_w, tm=512)
    qc, kc, vc, oc, zac, ubc, zbc, gtc = _inproj_even(
        ctx, shift_c, scale_c, norm_g[0], w_main, wgt, gbt, conv_w, tm=Tc)

    colsx, rowsx = _gate_prep(gtx, L)
    colsc, rowsc = _gate_prep(gtc, L)
    h_x, h_c = _mlstm(qx, kx, vx, colsx, rowsx, qc, kc, vc, colsc, rowsc, L)

    wcs = _fourier_weights(fourier_w_even[0], T)
    yb_x = _fourier_latent(ubx, zbx, wcs)
    wout_e = wout_even[0].astype(bf16)
    shift_1, scale_1, gate_1 = mod_parts(1, 0, B, B)
    x1, u1, z1 = _even_out_odd_in(x, h_x, ox, zax, yb_x, gate_x, head_norm_even[0], wout_e,
                                  shift_1, scale_1, norm_g[1], win_odd[0].astype(bf16), tm=512)

    wcs_c = _fourier_weights(fourier_w_even[0], Tc)
    yb_c = _fourier_ctx(ubc, zbc, wcs_c)
    ctx1 = _outproj_even(ctx, h_c, oc, zac, yb_c, gate_c, head_norm_even[0], wout_e, tm=Tc)
    del ctx1

    y1 = _pool_mix(u1, z1, pool_w_odd[0], pool_scale_odd[0])
    return _outproj_odd(x1, y1, gate_1, final_g, wout_odd[0].astype(bf16), tm=1024)
```

```python
import functools

import numpy as np
import jax
import jax.numpy as jnp
from jax import lax
from jax.experimental import pallas as pl
from jax.experimental.pallas import tpu as pltpu

D_MODEL = 1024
DEPTH = 2
CTX_LEN = 256
GRID_W = 64
D_INNER = 2 * D_MODEL
D_A = D_INNER // 2
D_B = D_INNER - D_A
N_HEADS_A = 4
HEAD_DIM_A = D_A // N_HEADS_A
N_GROUPS_B = 4
GROUP_B = D_B // N_GROUPS_B
N_GROUPS_C = 4
GROUP_C = D_INNER // N_GROUPS_C
POOL_WINDOWS = (2, 4, 8, 16)
CONV_W = 3
N_GATES = 4 * N_HEADS_A
W_MAIN_EVEN = 5 * D_A + 2 * D_B
EPS = 1e-6

f32 = jnp.float32
bf16 = jnp.bfloat16

V7X_VMEM_BYTES = 64 * 1024 * 1024
V7X_LANES = 128
V7X_SUBLANES = 8

MLSTM_CHUNK = 256
MLSTM_HEADS_PER_STEP = 2
FFT_N1 = 64
FFT_N2 = 64
FFT_PITCH = 72
FFT_UNROLL = 16
NEG_BIG = -1e30


def _vmem_limit(nbytes):
    return int(min(max(nbytes * 5 // 4 + (4 << 20), 16 << 20), V7X_VMEM_BYTES - (6 << 20)))


def _sigmoid(v):
    return 0.5 * jnp.tanh(0.5 * v) + 0.5


def _silu(v):
    return v * _sigmoid(v)


def _pipelined_matmuls(n, matmul, epilogue, pbuf_ref):
    zero = jnp.minimum(pl.program_id(0), 0)
    nbuf = pbuf_ref.shape[0]
    pbuf_ref[zero] = matmul(0)
    for t in range(n):
        if t + 1 < n:
            pbuf_ref[zero + (t + 1) % nbuf] = matmul(t + 1)
        epilogue(t, pbuf_ref[zero + t % nbuf])


def _log_sigmoid(v):
    return jnp.minimum(v, 0.0) - jnp.log1p(jnp.exp(-jnp.abs(v)))


def _mod_kernel(r_ref, w_ref, b_ref, o_ref):
    s = _silu(r_ref[...])
    o_ref[...] = jnp.dot(s, w_ref[...], preferred_element_type=f32,
                         precision=lax.Precision.HIGHEST) + b_ref[...]


def _modulation(rows, ada_w, ada_b):
    nrow = rows.shape[0]
    tn = 1024
    return pl.pallas_call(
        _mod_kernel,
        grid=(DEPTH, 3 * D_MODEL // tn),
        in_specs=[
            pl.BlockSpec((nrow, D_MODEL), lambda l, j: (0, 0)),
            pl.BlockSpec((None, D_MODEL, tn), lambda l, j: (l, 0, j)),
            pl.BlockSpec((None, 1, tn), lambda l, j: (l, 0, j)),
        ],
        out_specs=pl.BlockSpec((None, nrow, tn), lambda l, j: (l, 0, j)),
        out_shape=jax.ShapeDtypeStruct((DEPTH, nrow, 3 * D_MODEL), f32),
        compiler_params=pltpu.CompilerParams(dimension_semantics=("arbitrary", "arbitrary")),
        name="modulation",
    )(rows, ada_w, ada_b.reshape(DEPTH, 1, 3 * D_MODEL))


def _normed(x, amp, shift):
    ms = jnp.mean(x * x, axis=-1, keepdims=True)
    return (x * lax.rsqrt(ms + EPS)) * amp + shift


def _inproj_even_kernel(x_ref, xp_ref, xn_ref, sh_ref, sc_ref, g_ref, w_ref, wgt_ref, gbt_ref,
                        cw_ref, q_ref, k_ref, v_ref, o_ref, za_ref, ub_ref, zb_ref, gt_ref, pbuf_ref, *, tm, nt):
    i = pl.program_id(1)
    amp = g_ref[...] * (1.0 + sc_ref[...])
    shift = sh_ref[...]
    hx = _normed(x_ref[...], amp, shift).astype(bf16)
    halo = jnp.concatenate([xp_ref[...], xn_ref[...]], axis=0)
    hh = _normed(halo, amp, shift).astype(bf16)
    has_prev = (i > 0).astype(f32)
    has_next = (i < nt - 1).astype(f32)
    row = lax.broadcasted_iota(jnp.int32, (tm, 1), 0)
    cn = 512
    ph = jnp.dot(hh, w_ref[:, :2 * D_A], preferred_element_type=f32)
    prev = ph[V7X_SUBLANES - 1:V7X_SUBLANES, :] * has_prev
    nxt = ph[V7X_SUBLANES:V7X_SUBLANES + 1, :] * has_next

    def conv_store(j, p):
        cols = slice(j * cn, (j + 1) * cn)
        up = jnp.where(row == 0, prev[:, cols], pltpu.roll(p, 1, 0))
        dn = jnp.where(row == tm - 1, nxt[:, cols], pltpu.roll(p, tm - 1, 0))
        cw = cw_ref[:, cols]
        y = _silu(cw[0:1, :] * up + cw[1:2, :] * p + cw[2:3, :] * dn)
        if j < D_A // cn:
            q_ref[:, cols] = (y * (HEAD_DIM_A ** -0.5)).astype(bf16)
        else:
            jj = j - D_A // cn
            k_ref[:, jj * cn:(jj + 1) * cn] = y.astype(bf16)

    def plain_store(ref, jj):
        def store(p):
            ref[:, jj * cn:(jj + 1) * cn] = p.astype(bf16)
        return store

    tasks = [(j * cn, functools.partial(conv_store, j)) for j in range(2 * D_A // cn)]
    for idx, ref in enumerate((v_ref, o_ref, za_ref, ub_ref, zb_ref)):
        for jj in range(D_A // cn):
            tasks.append((2 * D_A + idx * D_A + jj * cn, plain_store(ref, jj)))

    def matmul(t):
        c0 = tasks[t][0]
        return jnp.dot(hx, w_ref[:, c0:c0 + cn], preferred_element_type=f32)

    _pipelined_matmuls(len(tasks), matmul, lambda t, p: tasks[t][1](p), pbuf_ref)
    gt_ref[...] = lax.dot_general(wgt_ref[...], hx, (((1,), (1,)), ((), ())),
                                  preferred_element_type=f32) + gbt_ref[...]


def _inproj_even(x, shift, scale, norm_g, w_main, wgt, gbt, conv_w, tm):
    B, T, D = x.shape
    nt = T // tm
    hb = tm // V7X_SUBLANES
    nhb = T // V7X_SUBLANES
    row_spec = pl.BlockSpec((None, tm, D_A), lambda b, i: (b, i, 0))
    vec = lambda n: pl.BlockSpec((1, n), lambda b, i: (0, 0))
    est = (2 * w_main.size * 2 + 2 * tm * D * 4 + 7 * 2 * tm * D_A * 2 + 6 * tm * 512 * 4)
    outs = pl.pallas_call(
        functools.partial(_inproj_even_kernel, tm=tm, nt=nt),
        grid=(B, nt),
        in_specs=[
            pl.BlockSpec((None, tm, D), lambda b, i: (b, i, 0)),
            pl.BlockSpec((None, V7X_SUBLANES, D), lambda b, i: (b, jnp.maximum(i * hb - 1, 0), 0)),
            pl.BlockSpec((None, V7X_SUBLANES, D), lambda b, i: (b, jnp.minimum((i + 1) * hb, nhb - 1), 0)),
            pl.BlockSpec((None, 1, D), lambda b, i: (b, 0, 0)),
            pl.BlockSpec((None, 1, D), lambda b, i: (b, 0, 0)),
            vec(D),
            pl.BlockSpec(w_main.shape, lambda b, i: (0, 0)),
            pl.BlockSpec(wgt.shape, lambda b, i: (0, 0)),
            pl.BlockSpec((N_GATES, 1), lambda b, i: (0, 0)),
            pl.BlockSpec(conv_w.shape, lambda b, i: (0, 0)),
        ],
        out_specs=[row_spec] * 7 + [pl.BlockSpec((None, N_GATES, tm), lambda b, i: (b, 0, i))],
        out_shape=[jax.ShapeDtypeStruct((B, T, D_A), bf16)] * 7 + [jax.ShapeDtypeStruct((B, N_GATES, T), f32)],
        scratch_shapes=[pltpu.VMEM((3, tm, 512), f32)],
        compiler_params=pltpu.CompilerParams(dimension_semantics=("parallel", "arbitrary"),
                                             vmem_limit_bytes=_vmem_limit(est)),
        name="inproj_even",
    )(x, x, x, shift, scale, norm_g.reshape(1, D), w_main, wgt, gbt, conv_w)
    return outs


GATE_SLOTS = 2 * N_HEADS_A
GATE_PIECES = 3
GATE_QUANTS = 3
assert GATE_SLOTS & (GATE_SLOTS - 1) == 0 and GATE_QUANTS * GATE_PIECES * GATE_SLOTS <= V7X_LANES


def _scan_max_lanes(x, seg, reverse):
    n = x.shape[1]
    pos = lax.broadcasted_iota(jnp.int32, x.shape, 1) & (seg - 1)
    s = 1
    while s < seg:
        if reverse:
            x = jnp.where(pos < seg - s, jnp.maximum(x, pltpu.roll(x, n - s, 1)), x)
        else:
            x = jnp.where(pos >= s, jnp.maximum(x, pltpu.roll(x, s, 1)), x)
        s *= 2
    return x


def _split3(v):
    hi = v.astype(bf16)
    r1 = v - hi.astype(f32)
    mid = r1.astype(bf16)
    lo = (r1 - mid.astype(f32)).astype(bf16)
    return hi, mid, lo


def _gate_prep_kernel(gt_ref, cols_ref, arow_ref, *, L, nchunk):
    r = lax.broadcasted_iota(jnp.int32, (L, L), 0)
    c = lax.broadcasted_iota(jnp.int32, (L, L), 1)
    tri_l = (c <= r).astype(bf16)
    tri_u = (c >= r).astype(bf16)
    S = V7X_SUBLANES
    tb = nchunk * L
    fwd = (lax.broadcasted_iota(jnp.int32, (S, tb), 0) & 1) == 0
    ig = gt_ref[:S, :]
    lf = _log_sigmoid(gt_ref[S:, :])
    pieces = jnp.concatenate([p.astype(f32) for p in _split3(lf)] + [jnp.zeros((S, tb), f32)],
                             axis=0).astype(bf16)
    chunks = [slice(ci * L, (ci + 1) * L) for ci in range(nchunk)]
    pre = jnp.concatenate([jnp.dot(pieces[:, sl], tri_u, preferred_element_type=f32) for sl in chunks], axis=1)
    suf = jnp.concatenate([jnp.dot(pieces[:, sl], tri_l, preferred_element_type=f32) for sl in chunks], axis=1)
    pre = pre[:S] + pre[S:2 * S] + pre[2 * S:3 * S]
    suf = suf[:S] + suf[S:2 * S] + suf[2 * S:3 * S]
    b_row = jnp.where(fwd, pre, suf)
    a_row = ig - b_row
    cmax = jnp.where(fwd, _scan_max_lanes(a_row, L, False), _scan_max_lanes(a_row, L, True))
    arow_ref[...] = a_row
    parts = [p.astype(f32) for quant in (b_row, a_row, cmax) for p in _split3(quant)]
    fill = jnp.zeros((V7X_LANES - len(parts) * S, tb), f32)
    packed = jnp.concatenate(parts + [fill], axis=0)
    for sl in chunks:
        cols_ref[sl, :] = packed[:, sl].T.astype(bf16)


def _gate_prep(gt, L):
    B, _, T = gt.shape
    nchunk = min(4, T // L)
    tb = nchunk * L
    cols, arow = pl.pallas_call(
        functools.partial(_gate_prep_kernel, L=L, nchunk=nchunk),
        grid=(B, T // tb),
        in_specs=[pl.BlockSpec((None, N_GATES, tb), lambda b, i: (b, 0, i))],
        out_specs=[
            pl.BlockSpec((None, tb, V7X_LANES), lambda b, i: (b, i, 0)),
            pl.BlockSpec((None, V7X_SUBLANES, tb), lambda b, i: (b, 0, i)),
        ],
        out_shape=[
            jax.ShapeDtypeStruct((B, T, V7X_LANES), bf16),
            jax.ShapeDtypeStruct((B, V7X_SUBLANES, T), f32),
        ],
        compiler_params=pltpu.CompilerParams(dimension_semantics=("parallel", "parallel")),
        name="gate_prep",
    )(gt)
    return cols, arow.reshape(B, N_HEADS_A, 2, T)


def _mlstm_kernel(qx_ref, kx_ref, vx_ref, cx_ref, rx_ref, qc_ref, kc_ref, vc_ref, cc_ref, rc_ref,
                  hx_ref, hc_ref, c_ref, n_ref, *, L, nx, nc):
    Dh, LN, HP = HEAD_DIM_A, V7X_LANES, MLSTM_HEADS_PER_STEP
    head0 = pl.program_id(1) * HP
    chains = [(hh, d) for hh in range(HP) for d in range(2)]
    r_i = lax.broadcasted_iota(jnp.int32, (L, L), 0)
    c_i = lax.broadcasted_iota(jnp.int32, (L, L), 1)
    masks = (c_i <= r_i, c_i >= r_i)

    sr = lax.broadcasted_iota(jnp.int32, (LN, GATE_QUANTS * LN), 0)
    sc = lax.broadcasted_iota(jnp.int32, (LN, GATE_QUANTS * LN), 1)
    span = GATE_PIECES * GATE_SLOTS
    in_block = None
    for qi in range(GATE_QUANTS):
        blk = (sr >= qi * span) & (sr < (qi + 1) * span) & (sc >= qi * LN) & (sc < (qi + 1) * LN)
        in_block = blk if in_block is None else in_block | blk
    slot = sr & (GATE_SLOTS - 1)
    sels = {(hh, d): (in_block & (slot == 2 * (head0 + hh) + d)).astype(bf16) for hh, d in chains}

    def tile(v, width):
        return jnp.concatenate([v] * (width // LN), axis=1)

    def step(refs, r0s, ms):
        q_ref, k_ref, v_ref, col_ref, row_ref = refs
        st = []
        for ci, (hh, d) in enumerate(chains):
            rows = pl.ds(r0s[ci], L)
            hsl = slice(hh * Dh, (hh + 1) * Dh)
            q = q_ref[rows, hsl]
            k = k_ref[rows, hsl]
            rep = jnp.dot(col_ref[rows, :], sels[hh, d], preferred_element_type=f32)
            qk = lax.dot_general(q, k, (((1,), (1,)), ((), ())), preferred_element_type=f32)
            qc = jnp.dot(q, c_ref[ci].astype(bf16), preferred_element_type=f32)
            st.append(dict(q=q, k=k, v=v_ref[rows, hsl], rep=rep, qk=qk, qc=qc, a_row=row_ref[hh, d:d + 1, rows]))
        new_ms = []
        for ci, (hh, d) in enumerate(chains):
            c, m_prev = st[ci], ms[ci]
            b_rep, a_rep = c["rep"][:, :LN], c["rep"][:, LN:2 * LN]
            g_rep = jnp.maximum(c["rep"][:, 2 * LN:], m_prev)
            p = jnp.exp(jnp.where(masks[d], c["a_row"] - tile(g_rep, L), NEG_BIG))
            s = c["qk"] * p
            c["s"] = s.astype(bf16)
            inter = jnp.exp(m_prev - g_rep)
            qn = jnp.sum(c["q"].astype(f32) * n_ref[ci], axis=1, keepdims=True)
            den = jnp.sum(s, axis=1, keepdims=True) + inter[:, :1] * qn
            floor = jnp.exp(-(b_rep + g_rep))
            rcp = 1.0 / jnp.maximum(jnp.abs(den), floor[:, :1])
            c["rcp"] = jnp.broadcast_to(rcp, (L, LN))
            c["inter"] = inter
            b_end = b_rep[L - 1:L, :] if d == 0 else b_rep[0:1, :]
            w = b_end + a_rep
            m_new = jnp.maximum(b_end + m_prev, jnp.max(w, axis=0, keepdims=True))
            c["decay"] = jnp.exp(b_end + m_prev - m_new)
            kw = c["k"].astype(f32) * tile(jnp.exp(w - m_new), Dh)
            c["kw"] = kw.astype(bf16)
            c["ksum"] = jnp.sum(kw, axis=0, keepdims=True)
            new_ms.append(m_new)
        for c in st:
            c["sv"] = jnp.dot(c["s"], c["v"], preferred_element_type=f32)
            c["upd"] = lax.dot_general(c["kw"], c["v"], (((0,), (0,)), ((), ())), preferred_element_type=f32)
        hs = []
        for ci, c in enumerate(st):
            hs.append((c["sv"] + tile(c["inter"], Dh) * c["qc"]) * tile(c["rcp"], Dh))
            decay = tile(c["decay"], Dh)
            c_ref[ci] = decay * c_ref[ci] + c["upd"]
            n_ref[ci] = decay * n_ref[ci] + c["ksum"]
        return hs, new_ms

    ctx = (qc_ref, kc_ref, vc_ref, cc_ref, rc_ref)
    lat = (qx_ref, kx_ref, vx_ref, cx_ref, rx_ref)

    c_ref[...] = jnp.zeros_like(c_ref)
    n_ref[...] = jnp.zeros_like(n_ref)
    ms = [jnp.zeros((1, LN), f32) for _ in chains]
    written = set()
    for j in range(nc):
        cjs = [j if d == 0 else nc - 1 - j for _, d in chains]
        hs, ms = step(ctx, [cj * L for cj in cjs], ms)
        for (hh, d), cj, h in zip(chains, cjs, hs):
            dst = (slice(cj * L, (cj + 1) * L), slice(hh * Dh, (hh + 1) * Dh))
            if (cj, hh) in written:
                hc_ref[dst] = (hc_ref[dst].astype(f32) + h).astype(hc_ref.dtype)
            else:
                hc_ref[dst] = h.astype(hc_ref.dtype)
                written.add((cj, hh))

    def make_body(accumulate):
        def body(i, ms):
            r0s = [pl.multiple_of((i if d == 0 else nx - 1 - i) * L, L) for _, d in chains]
            hs, ms = step(lat, r0s, list(ms))
            for (hh, d), r0, h in zip(chains, r0s, hs):
                dst = (pl.ds(r0, L), slice(hh * Dh, (hh + 1) * Dh))
                if accumulate:
                    hx_ref[dst] = (hx_ref[dst].astype(f32) + h).astype(hx_ref.dtype)
                else:
                    hx_ref[dst] = h.astype(hx_ref.dtype)
            return tuple(ms)
        return body

    ms = lax.fori_loop(0, nx // 2, make_body(False), tuple(ms))
    lax.fori_loop(nx // 2, nx, make_body(True), ms)


def _mlstm(qx, kx, vx, colsx, rowsx, qc, kc, vc, colsc, rowsc, L):
    B, T, _ = qx.shape
    Tc = qc.shape[1]
    H, Dh, HP = N_HEADS_A, HEAD_DIM_A, MLSTM_HEADS_PER_STEP
    assert T % (2 * L) == 0 and Tc % L == 0 and H % HP == 0

    def seq_spec(t):
        return pl.BlockSpec((None, t, HP * Dh), lambda b, h: (b, 0, h))

    def col_spec(t):
        return pl.BlockSpec((None, t, V7X_LANES), lambda b, h: (b, 0, 0))

    def row_spec(t):
        return pl.BlockSpec((None, HP, 2, t), lambda b, h: (b, h, 0, 0))

    est = (2 * HP * (3 * (T + Tc) * Dh * 2 + 8 * (T + Tc) * 4 + (T + Tc) * Dh * 2)
           + 2 * (T + Tc) * V7X_LANES * 2 + 2 * HP * Dh * (Dh + V7X_LANES) * 4 + 16 * L * L * 4)
    return pl.pallas_call(
        functools.partial(_mlstm_kernel, L=L, nx=T // L, nc=Tc // L),
        grid=(B, H // HP),
        in_specs=[seq_spec(T), seq_spec(T), seq_spec(T), col_spec(T), row_spec(T),
                  seq_spec(Tc), seq_spec(Tc), seq_spec(Tc), col_spec(Tc), row_spec(Tc)],
        out_specs=[seq_spec(T), seq_spec(Tc)],
        out_shape=[jax.ShapeDtypeStruct((B, T, D_A), bf16), jax.ShapeDtypeStruct((B, Tc, D_A), bf16)],
        scratch_shapes=[pltpu.VMEM((2 * HP, Dh, Dh), f32), pltpu.VMEM((2 * HP, 1, Dh), f32)],
        compiler_params=pltpu.CompilerParams(dimension_semantics=("parallel", "parallel"),
                                             vmem_limit_bytes=_vmem_limit(est)),
        name="mlstm",
    )(qx, kx, vx, colsx, rowsx, qc, kc, vc, colsc, rowsc)


def _fourier_w_kernel(cs_ref, fw_ref, o_ref, *, scale):
    o_ref[...] = (jnp.dot(cs_ref[...], fw_ref[...], preferred_element_type=f32,
                          precision=lax.Precision.HIGHEST) * scale).astype(bf16)


def _fourier_weights(fw, T):
    n = GROUP_B
    kk = np.outer(np.arange(n), np.arange(n)) % n
    ang = 2.0 * np.pi * kk / n
    cs = jnp.asarray(np.concatenate([np.cos(ang), np.sin(ang)], axis=0), f32)
    return pl.pallas_call(
        functools.partial(_fourier_w_kernel, scale=float(1.0 / np.sqrt(T * n))),
        grid=(N_GROUPS_B,),
        in_specs=[pl.BlockSpec((2 * n, n), lambda g: (0, 0)),
                  pl.BlockSpec((None, n, n), lambda g: (g, 0, 0))],
        out_specs=pl.BlockSpec((None, 2 * n, n), lambda g: (g, 0, 0)),
        out_shape=jax.ShapeDtypeStruct((N_GROUPS_B, 2 * n, n), bf16),
        compiler_params=pltpu.CompilerParams(dimension_semantics=("arbitrary",)),
        name="fourier_weights",
    )(cs, fw)


def _fft_tables():
    n1, n2 = FFT_N1, FFT_N2
    n = n1 * n2
    t1 = np.arange(n1)
    k1 = np.arange(n1)
    t2 = np.arange(n2)
    idx = (k1[None, :, None] * (n2 * t1[None, None, :] + t2[:, None, None])) % n
    ang = 2.0 * np.pi * idx / n
    m1 = np.concatenate([np.cos(ang), -np.sin(ang)], axis=1)
    k2 = np.arange(n2)
    ph = 2.0 * np.pi * (np.outer(k2, t2) % n2) / n2
    c, s = np.cos(ph), np.sin(ph)
    m2 = np.block([[c, s], [-s, c]])
    return jnp.asarray(m1, f32).astype(bf16), jnp.asarray(m2, f32).astype(bf16)


def _fourier_kernel(u_ref, z_ref, m1_ref, m2_ref, w_ref, o_ref, up_ref, yp_ref, xp_ref, pbuf_ref):
    n1, n2, P = FFT_N1, FFT_N2, FFT_PITCH
    nl = GROUP_B // V7X_LANES

    def fill(t1, carry):
        src = pl.multiple_of(t1 * n2, n2)
        dst = pl.multiple_of(t1 * P, V7X_SUBLANES)
        blk = u_ref[pl.ds(src, n2), :].astype(f32)
        for s in range(nl):
            up_ref[s, pl.ds(dst, n2), :] = blk[:, s * V7X_LANES:(s + 1) * V7X_LANES]
        return carry

    lax.fori_loop(0, n1, fill, 0, unroll=4)

    G = FFT_UNROLL

    def stage1(grp, carry):
        t2s = [grp * G + j for j in range(G)]
        rhs = [jnp.concatenate([up_ref[s, pl.ds(t2, n1, stride=P), :] for s in range(nl)],
                               axis=1).astype(bf16) for t2 in t2s]
        ys = [jnp.dot(m1_ref[t2], r, preferred_element_type=f32) for t2, r in zip(t2s, rhs)]
        for t2, y in zip(t2s, ys):
            dst = pl.multiple_of(t2 * P, V7X_SUBLANES)
            for ri in range(2):
                for s in range(nl):
                    yp_ref[ri * nl + s, pl.ds(dst, n1), :] = y[ri * n1:(ri + 1) * n1,
                                                               s * V7X_LANES:(s + 1) * V7X_LANES]
        return carry

    lax.fori_loop(0, n2 // G, stage1, 0)

    m2 = m2_ref[...]

    def stage2(grp, carry):
        k1s = [grp * G + j for j in range(G)]
        rhs = []
        for k1 in k1s:
            parts = [jnp.concatenate([yp_ref[ri * nl + s, pl.ds(k1, n2, stride=P), :] for s in range(nl)], axis=1)
                     for ri in range(2)]
            rhs.append(jnp.concatenate(parts, axis=0).astype(bf16))
        xs = [jnp.dot(m2, r, preferred_element_type=f32) for r in rhs]
        for k1, x in zip(k1s, xs):
            for ri in range(2):
                for s in range(nl):
                    xp_ref[ri * nl + s, pl.ds(k1, n2, stride=P), :] = x[ri * n2:(ri + 1) * n2,
                                                                        s * V7X_LANES:(s + 1) * V7X_LANES]
        return carry

    lax.fori_loop(0, n1 // G, stage2, 0)

    tr = pbuf_ref.shape[1]

    def matmul(t):
        blocks = []
        for k2 in range(t * (tr // n1), (t + 1) * (tr // n1)):
            blocks.append(jnp.concatenate([xp_ref[sl, k2 * P:k2 * P + n1, :] for sl in range(2 * nl)],
                                          axis=1).astype(bf16))
        return jnp.dot(jnp.concatenate(blocks, axis=0), w_ref[...], preferred_element_type=f32)

    def gate_store(t, y):
        sl = slice(t * tr, (t + 1) * tr)
        o_ref[sl, :] = y.astype(bf16) * _silu(z_ref[sl, :])

    _pipelined_matmuls((n1 * n2) // tr, matmul, gate_store, pbuf_ref)


def _fourier_latent(ub, zb, wcs):
    B, T, _ = ub.shape
    assert T == FFT_N1 * FFT_N2
    m1, m2 = _fft_tables()
    G, n = N_GROUPS_B, GROUP_B
    nl = n // V7X_LANES
    rows_p = FFT_N1 * FFT_PITCH
    blk = pl.BlockSpec((None, T, n), lambda b, g: (b, 0, g))
    est = (3 * 2 * T * n * 2 + 5 * nl * rows_p * V7X_LANES * 4 + T * 2 * n * 2 + 2 * m1.size * 2 + 8 * 512 * n * 4)
    return pl.pallas_call(
        _fourier_kernel,
        grid=(B, G),
        in_specs=[blk, blk,
                  pl.BlockSpec(m1.shape, lambda b, g: (0, 0, 0)),
                  pl.BlockSpec(m2.shape, lambda b, g: (0, 0)),
                  pl.BlockSpec((None, 2 * n, n), lambda b, g: (g, 0, 0))],
        out_specs=blk,
        out_shape=jax.ShapeDtypeStruct((B, T, D_B), bf16),
        scratch_shapes=[pltpu.VMEM((nl, rows_p, V7X_LANES), f32),
                        pltpu.VMEM((2 * nl, rows_p, V7X_LANES), f32),
                        pltpu.VMEM((2 * nl, rows_p, V7X_LANES), f32),
                        pltpu.VMEM((3, 512, n), f32)],
        compiler_params=pltpu.CompilerParams(dimension_semantics=("parallel", "parallel"),
                                             vmem_limit_bytes=_vmem_limit(est)),
        name="fourier_latent",
    )(ub, zb, m1, m2, wcs)


def _fourier_ctx_kernel(u_ref, z_ref, cs_ref, w_ref, o_ref):
    u = u_ref[...]
    x = jnp.dot(cs_ref[...], u, preferred_element_type=f32).astype(bf16)
    t = u.shape[0]
    w = w_ref[...]
    n = GROUP_B
    y = (jnp.dot(x[:t], w[:n], preferred_element_type=f32) + jnp.dot(x[t:], w[n:], preferred_element_type=f32))
    o_ref[...] = (y * _silu(z_ref[...].astype(f32))).astype(bf16)


def _fourier_ctx(ub, zb, wcs):
    B, T, _ = ub.shape
    ang = 2.0 * np.pi * (np.outer(np.arange(T), np.arange(T)) % T) / T
    cs = jnp.asarray(np.concatenate([np.cos(ang), -np.sin(ang)], axis=0), f32).astype(bf16)
    n = GROUP_B
    blk = pl.BlockSpec((None, T, n), lambda b, g: (b, 0, g))
    return pl.pallas_call(
        _fourier_ctx_kernel,
        grid=(B, N_GROUPS_B),
        in_specs=[blk, blk, pl.BlockSpec(cs.shape, lambda b, g: (0, 0)),
                  pl.BlockSpec((None, 2 * n, n), lambda b, g: (g, 0, 0))],
        out_specs=blk,
        out_shape=jax.ShapeDtypeStruct((B, T, D_B), bf16),
        compiler_params=pltpu.CompilerParams(dimension_semantics=("parallel", "parallel")),
        name="fourier_ctx",
    )(ub, zb, cs, wcs)


def _outproj_even_kernel(x_ref, h_ref, o_ref, za_ref, yb_ref, gate_ref, hg_ref, w_ref, out_ref):
    parts = []
    for hd in range(N_HEADS_A):
        sl = slice(hd * HEAD_DIM_A, (hd + 1) * HEAD_DIM_A)
        hh = _sigmoid(o_ref[:, sl]).astype(f32) * h_ref[:, sl]
        hh = hh * lax.rsqrt(jnp.mean(hh * hh, axis=-1, keepdims=True) + EPS)
        parts.append(hh.astype(bf16) * (hg_ref[:, sl].astype(bf16) * _silu(za_ref[:, sl])))
    ya = jnp.concatenate(parts, axis=1)
    acc = jnp.dot(ya, w_ref[:D_A, :], preferred_element_type=f32)
    acc = acc + jnp.dot(yb_ref[...], w_ref[D_A:, :], preferred_element_type=f32)
    out_ref[...] = x_ref[...] + gate_ref[...] * acc


def _outproj_even(x, h, o, za, yb, gate, head_g, wout, tm):
    B, T, D = x.shape
    row = lambda n: pl.BlockSpec((None, tm, n), lambda b, i: (b, i, 0))
    est = 2 * wout.size * 2 + 2 * tm * (2 * D * 4 + D_A * 4 + 3 * D_A * 2) + 8 * tm * D * 4
    return pl.pallas_call(
        _outproj_even_kernel,
        grid=(B, T // tm),
        in_specs=[row(D), row(D_A), row(D_A), row(D_A), row(D_B),
                  pl.BlockSpec((None, 1, D), lambda b, i: (b, 0, 0)),
                  pl.BlockSpec((1, D_A), lambda b, i: (0, 0)),
                  pl.BlockSpec(wout.shape, lambda b, i: (0, 0))],
        out_specs=row(D),
        out_shape=jax.ShapeDtypeStruct((B, T, D), f32),
        compiler_params=pltpu.CompilerParams(dimension_semantics=("parallel", "parallel"),
                                             vmem_limit_bytes=_vmem_limit(est)),
        name="outproj_even",
    )(x, h, o, za, yb, gate, head_g.reshape(1, D_A), wout)


def _inproj_odd_kernel(x_ref, sh_ref, sc_ref, g_ref, w_ref, u_ref, z_ref):
    amp = g_ref[...] * (1.0 + sc_ref[...])
    hx = _normed(x_ref[...], amp, sh_ref[...]).astype(bf16)
    cn = 512
    for j in range(D_INNER // cn):
        u_ref[:, j * cn:(j + 1) * cn] = jnp.dot(hx, w_ref[:, j * cn:(j + 1) * cn],
                                                preferred_element_type=f32).astype(bf16)
        z_ref[:, j * cn:(j + 1) * cn] = jnp.dot(hx, w_ref[:, D_INNER + j * cn:D_INNER + (j + 1) * cn],
                                                preferred_element_type=f32).astype(bf16)


def _even_out_odd_in_kernel(x_ref, h_ref, o_ref, za_ref, yb_ref, gate_ref, hg_ref, wo_ref,
                            sh_ref, sc_ref, g_ref, wi_ref, x1_ref, u_ref, z_ref):
    _outproj_even_kernel(x_ref, h_ref, o_ref, za_ref, yb_ref, gate_ref, hg_ref, wo_ref, x1_ref)
    _inproj_odd_kernel(x1_ref, sh_ref, sc_ref, g_ref, wi_ref, u_ref, z_ref)


def _even_out_odd_in(x, h, o, za, yb, gate, head_g, wout, shift, scale, norm_g, win, tm):
    B, T, D = x.shape
    row = lambda n: pl.BlockSpec((None, tm, n), lambda b, i: (b, i, 0))
    bvec = pl.BlockSpec((None, 1, D), lambda b, i: (b, 0, 0))
    est = (2 * (wout.size + win.size) * 2 + 2 * tm * (2 * D * 4 + 4 * D_A * 2 + 2 * D_INNER * 2)
           + 8 * tm * D * 4 + 6 * tm * 512 * 4)
    return pl.pallas_call(
        _even_out_odd_in_kernel,
        grid=(B, T // tm),
        in_specs=[row(D), row(D_A), row(D_A), row(D_A), row(D_B), bvec,
                  pl.BlockSpec((1, D_A), lambda b, i: (0, 0)),
                  pl.BlockSpec(wout.shape, lambda b, i: (0, 0)),
                  bvec, bvec,
                  pl.BlockSpec((1, D), lambda b, i: (0, 0)),
                  pl.BlockSpec(win.shape, lambda b, i: (0, 0))],
        out_specs=[row(D), row(D_INNER), row(D_INNER)],
        out_shape=[jax.ShapeDtypeStruct((B, T, D), f32)] + [jax.ShapeDtypeStruct((B, T, D_INNER), bf16)] * 2,
        compiler_params=pltpu.CompilerParams(dimension_semantics=("parallel", "parallel"),
                                             vmem_limit_bytes=_vmem_limit(est)),
        name="even_out_odd_in",
    )(x, h, o, za, yb, gate, head_g.reshape(1, D_A), wout, shift, scale, norm_g.reshape(1, D), win)


POOL_UNROLL = 16


def _pool_tables():
    w_idx = np.arange(GRID_W)
    band = np.zeros((N_GROUPS_C, GRID_W, GRID_W), np.float32)
    inv_w = np.zeros((N_GROUPS_C, GRID_W, V7X_LANES), np.float32)
    for g, win in enumerate(POOL_WINDOWS):
        lo = np.clip(w_idx - win // 2, 0, GRID_W)
        hi = np.clip(w_idx + win - win // 2, 0, GRID_W)
        band[g] = (w_idx[None, :] >= lo[:, None]) & (w_idx[None, :] < hi[:, None])
        inv_w[g] = (1.0 / (hi - lo))[:, None]
    return jnp.asarray(band, bf16), jnp.asarray(inv_w, f32)


def _pool_kernel(u_ref, z_ref, band_ref, invw_ref, pw_ref, sc_ref, o_ref, ps_ref, pbuf_ref, *, rows):
    g = pl.program_id(1)
    W = GRID_W
    band = band_ref[...]
    lo_off = hi_off = 0
    for gi, win in enumerate(POOL_WINDOWS):
        lo_off = jnp.where(g == gi, win // 2, lo_off)
        hi_off = jnp.where(g == gi, win - win // 2, hi_off)

    ps_ref[0:W, :] = jnp.zeros((W, GROUP_C), f32)

    def width_sum(grp, carry):
        srcs = [pl.multiple_of((grp * POOL_UNROLL + j) * W, W) for j in range(POOL_UNROLL)]
        sums = [jnp.dot(band, u_ref[pl.ds(src, W), :], preferred_element_type=f32) for src in srcs]
        acc = ps_ref[pl.ds(srcs[0], W), :]
        for src, s in zip(srcs, sums):
            acc = acc + s
            ps_ref[pl.ds(src + W, W), :] = acc
        return carry

    lax.fori_loop(0, rows // POOL_UNROLL, width_sum, 0)

    inv_w = invw_ref[...]

    def pooled_minus_self(r):
        lo = jnp.maximum(r - lo_off, 0)
        hi = jnp.minimum(r + hi_off, rows)
        acc = (ps_ref[pl.ds(pl.multiple_of(hi * W, W), W), :]
               - ps_ref[pl.ds(pl.multiple_of(lo * W, W), W), :])
        inv = inv_w / (hi - lo).astype(f32)
        inv = jnp.concatenate([inv] * (GROUP_C // V7X_LANES), axis=1)
        ug = u_ref[r * W:(r + 1) * W, :].astype(f32)
        return (acc * inv - ug).astype(bf16)

    tr = pbuf_ref.shape[1]
    pws = (pw_ref[...] * sc_ref[...]).astype(bf16)

    def matmul(t):
        lhs = jnp.concatenate([pooled_minus_self(t * (tr // W) + j) for j in range(tr // W)], axis=0)
        return jnp.dot(lhs, pws, preferred_element_type=f32)

    def gate_store(t, y):
        sl = slice(t * tr, (t + 1) * tr)
        o_ref[sl, :] = y.astype(bf16) * _silu(z_ref[sl, :])

    _pipelined_matmuls((rows * W) // tr, matmul, gate_store, pbuf_ref)


def _pool_mix(u, z, pool_w, scale):
    B, T, _ = u.shape
    rows = T // GRID_W
    band, inv_w = _pool_tables()
    n = GROUP_C
    blk = pl.BlockSpec((None, T, n), lambda b, g: (b, 0, g))
    est = 3 * 2 * T * n * 2 + (rows + 1) * GRID_W * n * 4 + T * n * 2 + 2 * n * n * 4 + 8 * 512 * n * 4
    return pl.pallas_call(
        functools.partial(_pool_kernel, rows=rows),
        grid=(B, N_GROUPS_C),
        in_specs=[blk, blk,
                  pl.BlockSpec((None, GRID_W, GRID_W), lambda b, g: (g, 0, 0)),
                  pl.BlockSpec((None, GRID_W, V7X_LANES), lambda b, g: (g, 0, 0)),
                  pl.BlockSpec((None, n, n), lambda b, g: (g, 0, 0)),
                  pl.BlockSpec((None, 1, n), lambda b, g: (g, 0, 0))],
        out_specs=blk,
        out_shape=jax.ShapeDtypeStruct((B, T, D_INNER), bf16),
        scratch_shapes=[pltpu.VMEM(((rows + 1) * GRID_W, n), f32),
                        pltpu.VMEM((3, 512, n), f32)],
        compiler_params=pltpu.CompilerParams(dimension_semantics=("parallel", "arbitrary"),
                                             vmem_limit_bytes=_vmem_limit(est)),
        name="pool_mix",
    )(u, z, band, inv_w, pool_w, scale.reshape(N_GROUPS_C, 1, n))


def _outproj_odd_kernel(x_ref, y_ref, gate_ref, fg_ref, w_ref, out_ref):
    acc = jnp.dot(y_ref[...], w_ref[...], preferred_element_type=f32)
    x = x_ref[...] + gate_ref[...] * acc
    out_ref[...] = x * lax.rsqrt(jnp.mean(x * x, axis=-1, keepdims=True) + EPS) * fg_ref[...]


def _outproj_odd(x, y, gate, final_g, wout, tm):
    B, T, D = x.shape
    row = lambda n: pl.BlockSpec((None, tm, n), lambda b, i: (b, i, 0))
    est = 2 * wout.size * 2 + 2 * tm * (2 * D * 4 + D_INNER * 2) + 6 * tm * D * 4
    return pl.pallas_call(
        _outproj_odd_kernel,
        grid=(B, T // tm),
        in_specs=[row(D), row(D_INNER),
                  pl.BlockSpec((None, 1, D), lambda b, i: (b, 0, 0)),
                  pl.BlockSpec((1, D), lambda b, i: (0, 0)),
                  pl.BlockSpec(wout.shape, lambda b, i: (0, 0))],
        out_specs=row(D),
        out_shape=jax.ShapeDtypeStruct((B, T, D), f32),
        compiler_params=pltpu.CompilerParams(dimension_semantics=("parallel", "parallel"),
                                             vmem_limit_bytes=_vmem_limit(est)),
        name="outproj_odd",
    )(x, y, gate, final_g.reshape(1, D), wout)


def kernel(x, c, ctx, c_ctx, ada_w, ada_b, norm_g, win_even, gate_b_even, conv_qk_even, head_norm_even,
           fourier_w_even, wout_even, win_odd, pool_w_odd, pool_scale_odd, wout_odd, final_g):
    B, T, D = x.shape
    Tc = ctx.shape[1]
    H = N_HEADS_A
    L = MLSTM_CHUNK

    nrow = -(-(B + 1) // V7X_SUBLANES) * V7X_SUBLANES
    rows_in = jnp.zeros((nrow, D), f32).at[:B].set(c).at[B].set(c_ctx)
    mod = _modulation(rows_in, ada_w, ada_b)

    def mod_parts(l, r0, r1, n):
        m = mod[l, r0:r1]
        parts = [jnp.broadcast_to(m[:, None, i * D:(i + 1) * D], (n, 1, D)) for i in range(3)]
        return parts

    we = win_even[0]
    w_main = we.astype(bf16)
    gcols = we[:, W_MAIN_EVEN:].reshape(D, 4, H)
    gbias = gate_b_even[0].reshape(4, H)
    ig_w = gcols[:, 0::2, :].transpose(0, 2, 1).reshape(D, GATE_SLOTS)
    fg_w = gcols[:, 1::2, :].transpose(0, 2, 1).reshape(D, GATE_SLOTS)
    ig_b = gbias[0::2, :].T.reshape(GATE_SLOTS)
    fg_b = gbias[1::2, :].T.reshape(GATE_SLOTS)
    wgt =jnp.concatenate([ig_w, fg_w], axis=1).T.astype(bf16)
    gbt = jnp.concatenate([ig_b, fg_b]).reshape(N_GATES, 1)
    conv_w = conv_qk_even[0]

    shift_x, scale_x, gate_x = mod_parts(0, 0, B, B)
    shift_c, scale_c, gate_c = mod_parts(0, B, B + 1, B)

    qx, kx, vx, ox, zax, ubx, zbx, gtx = _inproj_even(
        x, shift_x, scale_x, norm_g[0], w_main, wgt, gbt, conv_w, tm=512)
    qc, kc, vc, oc, zac, ubc, zbc, gtc = _inproj_even(
        ctx, shift_c, scale_c, norm_g[0], w_main, wgt, gbt, conv_w, tm=Tc)

    colsx, rowsx = _gate_prep(gtx, L)
    colsc, rowsc = _gate_prep(gtc, L)
    h_x, h_c = _mlstm(qx, kx, vx, colsx, rowsx, qc, kc, vc, colsc, rowsc, L)

    wcs = _fourier_weights(fourier_w_even[0], T)
    yb_x = _fourier_latent(ubx, zbx, wcs)
    wout_e = wout_even[0].astype(bf16)
    shift_1, scale_1, gate_1 = mod_parts(1, 0, B, B)
    x1, u1, z1 = _even_out_odd_in(x, h_x, ox, zax, yb_x, gate_x, head_norm_even[0], wout_e,
                                  shift_1, scale_1, norm_g[1], win_odd[0].astype(bf16), tm=512)

    wcs_c = _fourier_weights(fourier_w_even[0], Tc)
    yb_c = _fourier_ctx(ubc, zbc, wcs_c)
    ctx1 = _outproj_even(ctx, h_c, oc, zac, yb_c, gate_c, head_norm_even[0], wout_e, tm=Tc)
    del ctx1

    y1 = _pool_mix(u1, z1, pool_w_odd[0], pool_scale_odd[0])
    return _outproj_odd(x1, y1, gate_1, final_g, wout_odd[0].astype(bf16), tm=1024)
```

```python
import functools

import numpy as np
import jax
import jax.numpy as jnp
from jax import lax
from jax.experimental import pallas as pl
from jax.experimental.pallas import tpu as pltpu

D_MODEL = 1024
DEPTH = 2
CTX_LEN = 256
GRID_W = 64
D_INNER = 2 * D_MODEL
D_A = D_INNER // 2
D_B = D_INNER - D_A
N_HEADS_A = 4
HEAD_DIM_A = D_A // N_HEADS_A
N_GROUPS_B = 4
GROUP_B = D_B // N_GROUPS_B
N_GROUPS_C = 4
GROUP_C = D_INNER // N_GROUPS_C
POOL_WINDOWS = (2, 4, 8, 16)
CONV_W = 3
N_GATES = 4 * N_HEADS_A
W_MAIN_EVEN = 5 * D_A + 2 * D_B
EPS = 1e-6

f32 = jnp.float32
bf16 = jnp.bfloat16

V7X_VMEM_BYTES = 64 * 1024 * 1024
V7X_LANES = 128
V7X_SUBLANES = 8

MLSTM_CHUNK = 256
MLSTM_HEADS_PER_STEP = 2
FFT_N1 = 64
FFT_N2 = 64
FFT_PITCH = 72
FFT_UNROLL = 16
NEG_BIG = -1e30


def _vmem_limit(nbytes):
    return int(min(max(nbytes * 5 // 4 + (4 << 20), 16 << 20), V7X_VMEM_BYTES - (6 << 20)))


def _sigmoid(v):
    return 0.5 * jnp.tanh(0.5 * v) + 0.5


def _silu(v):
    return v * _sigmoid(v)


def _pipelined_matmuls(n, matmul, epilogue, pbuf_ref):
    zero = jnp.minimum(pl.program_id(0), 0)
    nbuf = pbuf_ref.shape[0]
    pbuf_ref[zero] = matmul(0)
    for t in range(n):
        if t + 1 < n:
            pbuf_ref[zero + (t + 1) % nbuf] = matmul(t + 1)
        epilogue(t, pbuf_ref[zero + t % nbuf])


def _log_sigmoid(v):
    return jnp.minimum(v, 0.0) - jnp.log1p(jnp.exp(-jnp.abs(v)))


def _mod_kernel(r_ref, w_ref, b_ref, o_ref):
    s = _silu(r_ref[...])
    o_ref[...] = jnp.dot(s, w_ref[...], preferred_element_type=f32,
                         precision=lax.Precision.HIGHEST) + b_ref[...]


def _modulation(rows, ada_w, ada_b):
    nrow = rows.shape[0]
    tn = 1024
    return pl.pallas_call(
        _mod_kernel,
        grid=(DEPTH, 3 * D_MODEL // tn),
        in_specs=[
            pl.BlockSpec((nrow, D_MODEL), lambda l, j: (0, 0)),
            pl.BlockSpec((None, D_MODEL, tn), lambda l, j: (l, 0, j)),
            pl.BlockSpec((None, 1, tn), lambda l, j: (l, 0, j)),
        ],
        out_specs=pl.BlockSpec((None, nrow, tn), lambda l, j: (l, 0, j)),
        out_shape=jax.ShapeDtypeStruct((DEPTH, nrow, 3 * D_MODEL), f32),
        compiler_params=pltpu.CompilerParams(dimension_semantics=("arbitrary", "arbitrary")),
        name="modulation",
    )(rows, ada_w, ada_b.reshape(DEPTH, 1, 3 * D_MODEL))


def _normed(x, amp, shift):
    ms = jnp.mean(x * x, axis=-1, keepdims=True)
    return (x * lax.rsqrt(ms + EPS)) * amp + shift


INPROJ_EVEN_OUTPUTS = {
    "all": ("q", "k", "v", "o", "za", "ub", "zb", "gt"),
    "scan": ("q", "k", "v", "gt"),
    "mix": ("o", "za", "ub", "zb"),
}
INPROJ_EVEN_PLAIN = ("v", "o", "za", "ub", "zb")


def _inproj_even_kernel(x_ref, xp_ref, xn_ref, sh_ref, sc_ref, g_ref, w_ref, wgt_ref, gbt_ref, cw_ref, *refs,
                        tm, nt, part):
    out = dict(zip(INPROJ_EVEN_OUTPUTS[part], refs[:-1]))
    pbuf_ref = refs[-1]
    i = pl.program_id(1)
    amp = g_ref[...] * (1.0 + sc_ref[...])
    shift = sh_ref[...]
    hx = _normed(x_ref[...], amp, shift).astype(bf16)
    cn = 512
    tasks = []
    if "q" in out:
        halo = jnp.concatenate([xp_ref[...], xn_ref[...]], axis=0)
        hh = _normed(halo, amp, shift).astype(bf16)
        has_prev = (i > 0).astype(f32)
        has_next = (i < nt - 1).astype(f32)
        row = lax.broadcasted_iota(jnp.int32, (tm, 1), 0)
        ph = jnp.dot(hh, w_ref[:, :2 * D_A], preferred_element_type=f32)
        prev = ph[V7X_SUBLANES - 1:V7X_SUBLANES, :] * has_prev
        nxt = ph[V7X_SUBLANES:V7X_SUBLANES + 1, :] * has_next

        def conv_store(j, p):
            cols = slice(j * cn, (j + 1) * cn)
            up = jnp.where(row == 0, prev[:, cols], pltpu.roll(p, 1, 0))
            dn = jnp.where(row == tm - 1, nxt[:, cols], pltpu.roll(p, tm - 1, 0))
            cw = cw_ref[:, cols]
            y = _silu(cw[0:1, :] * up + cw[1:2, :] * p + cw[2:3, :] * dn)
            if j < D_A // cn:
                out["q"][:, cols] = (y * (HEAD_DIM_A ** -0.5)).astype(bf16)
            else:
                jj = j - D_A // cn
                out["k"][:, jj * cn:(jj + 1) * cn] = y.astype(bf16)

        tasks += [(j * cn, functools.partial(conv_store, j)) for j in range(2 * D_A // cn)]

    def plain_store(ref, jj):
        def store(p):
            ref[:, jj * cn:(jj + 1) * cn] = p.astype(bf16)
        return store

    for idx, name in enumerate(INPROJ_EVEN_PLAIN):
        if name in out:
            for jj in range(D_A // cn):
                tasks.append((2 * D_A + idx * D_A + jj * cn, plain_store(out[name], jj)))

    def matmul(t):
        c0 = tasks[t][0]
        return jnp.dot(hx, w_ref[:, c0:c0 + cn], preferred_element_type=f32)

    _pipelined_matmuls(len(tasks), matmul, lambda t, p: tasks[t][1](p), pbuf_ref)
    if "gt" in out:
        out["gt"][...] = lax.dot_general(wgt_ref[...], hx, (((1,), (1,)), ((), ())),
                                         preferred_element_type=f32) + gbt_ref[...]


def _inproj_even(x, shift, scale, norm_g, w_main, wgt, gbt, conv_w, tm, part="all"):
    B, T, D = x.shape
    nt = T // tm
    hb = tm // V7X_SUBLANES
    nhb = T // V7X_SUBLANES
    names = INPROJ_EVEN_OUTPUTS[part]
    row_spec = pl.BlockSpec((None, tm, D_A), lambda b, i: (b, i, 0))
    gt_spec = pl.BlockSpec((None, N_GATES, tm), lambda b, i: (b, 0, i))
    vec = lambda n: pl.BlockSpec((1, n), lambda b, i: (0, 0))
    est = (2 * w_main.size * 2 + 2 * tm * D * 4 + len(names) * 2 * tm * D_A * 2 + 6 * tm * 512 * 4)
    outs = pl.pallas_call(
        functools.partial(_inproj_even_kernel, tm=tm, nt=nt, part=part),
        grid=(B, nt),
        in_specs=[
            pl.BlockSpec((None, tm, D), lambda b, i: (b, i, 0)),
            pl.BlockSpec((None, V7X_SUBLANES, D), lambda b, i: (b, jnp.maximum(i * hb - 1, 0), 0)),
            pl.BlockSpec((None, V7X_SUBLANES, D), lambda b, i: (b, jnp.minimum((i + 1) * hb, nhb - 1), 0)),
            pl.BlockSpec((None, 1, D), lambda b, i: (b, 0, 0)),
            pl.BlockSpec((None, 1, D), lambda b, i: (b, 0, 0)),
            vec(D),
            pl.BlockSpec(w_main.shape, lambda b, i: (0, 0)),
            pl.BlockSpec(wgt.shape, lambda b, i: (0, 0)),
            pl.BlockSpec((N_GATES, 1), lambda b, i: (0, 0)),
            pl.BlockSpec(conv_w.shape, lambda b, i: (0, 0)),
        ],
        out_specs=[gt_spec if n == "gt" else row_spec for n in names],
        out_shape=[jax.ShapeDtypeStruct((B, N_GATES, T), f32) if n == "gt"
                   else jax.ShapeDtypeStruct((B, T, D_A), bf16) for n in names],
        scratch_shapes=[pltpu.VMEM((3, tm, 512), f32)],
        compiler_params=pltpu.CompilerParams(dimension_semantics=("parallel", "arbitrary"),
                                             vmem_limit_bytes=_vmem_limit(est)),
        name="inproj_even_" + part,
    )(x, x, x, shift, scale, norm_g.reshape(1, D), w_main, wgt, gbt, conv_w)
    return outs


GATE_SLOTS = 2 * N_HEADS_A
GATE_PIECES = 3
GATE_QUANTS = 3
assert GATE_SLOTS & (GATE_SLOTS - 1) == 0 and GATE_QUANTS * GATE_PIECES * GATE_SLOTS <= V7X_LANES


def _scan_max_lanes(x, seg, reverse):
    n = x.shape[1]
    pos = lax.broadcasted_iota(jnp.int32, x.shape, 1) & (seg - 1)
    s = 1
    while s < seg:
        if reverse:
            x = jnp.where(pos < seg - s, jnp.maximum(x, pltpu.roll(x, n - s, 1)), x)
        else:
            x = jnp.where(pos >= s, jnp.maximum(x, pltpu.roll(x, s, 1)), x)
        s *= 2
    return x


def _split3(v):
    hi = v.astype(bf16)
    r1 = v - hi.astype(f32)
    mid = r1.astype(bf16)
    lo = (r1 - mid.astype(f32)).astype(bf16)
    return hi, mid, lo


def _gate_prep_kernel(gt_ref, cols_ref, arow_ref, *, L, nchunk):
    r = lax.broadcasted_iota(jnp.int32, (L, L), 0)
    c = lax.broadcasted_iota(jnp.int32, (L, L), 1)
    tri_l = (c <= r).astype(bf16)
    tri_u = (c >= r).astype(bf16)
    S = V7X_SUBLANES
    tb = nchunk * L
    fwd = (lax.broadcasted_iota(jnp.int32, (S, tb), 0) & 1) == 0
    ig = gt_ref[:S, :]
    lf = _log_sigmoid(gt_ref[S:, :])
    pieces = jnp.concatenate([p.astype(f32) for p in _split3(lf)] + [jnp.zeros((S, tb), f32)],
                             axis=0).astype(bf16)
    chunks = [slice(ci * L, (ci + 1) * L) for ci in range(nchunk)]
    pre = jnp.concatenate([jnp.dot(pieces[:, sl], tri_u, preferred_element_type=f32) for sl in chunks], axis=1)
    suf = jnp.concatenate([jnp.dot(pieces[:, sl], tri_l, preferred_element_type=f32) for sl in chunks], axis=1)
    pre = pre[:S] + pre[S:2 * S] + pre[2 * S:3 * S]
    suf = suf[:S] + suf[S:2 * S] + suf[2 * S:3 * S]
    b_row = jnp.where(fwd, pre, suf)
    a_row = ig - b_row
    cmax = jnp.where(fwd, _scan_max_lanes(a_row, L, False), _scan_max_lanes(a_row, L, True))
    arow_ref[...] = a_row
    parts = [p.astype(f32) for quant in (b_row, a_row, cmax) for p in _split3(quant)]
    fill = jnp.zeros((V7X_LANES - len(parts) * S, tb), f32)
    packed = jnp.concatenate(parts + [fill], axis=0)
    for sl in chunks:
        cols_ref[sl, :] = packed[:, sl].T.astype(bf16)


def _gate_prep(gt, L):
    B, _, T = gt.shape
    nchunk = min(4, T // L)
    tb = nchunk * L
    cols, arow = pl.pallas_call(
        functools.partial(_gate_prep_kernel, L=L, nchunk=nchunk),
        grid=(B, T // tb),
        in_specs=[pl.BlockSpec((None, N_GATES, tb), lambda b, i: (b, 0, i))],
        out_specs=[
            pl.BlockSpec((None, tb, V7X_LANES), lambda b, i: (b, i, 0)),
            pl.BlockSpec((None, V7X_SUBLANES, tb), lambda b, i: (b, 0, i)),
        ],
        out_shape=[
            jax.ShapeDtypeStruct((B, T, V7X_LANES), bf16),
            jax.ShapeDtypeStruct((B, V7X_SUBLANES, T), f32),
        ],
        compiler_params=pltpu.CompilerParams(dimension_semantics=("parallel", "parallel")),
        name="gate_prep",
    )(gt)
    return cols, arow.reshape(B, N_HEADS_A, 2, T)


def _mlstm_kernel(qx_ref, kx_ref, vx_ref, cx_ref, rx_ref, qc_ref, kc_ref, vc_ref, cc_ref, rc_ref,
                  hx_ref, hc_ref, c_ref, n_ref, *, L, nx, nc):
    Dh, LN, HP = HEAD_DIM_A, V7X_LANES, MLSTM_HEADS_PER_STEP
    head0 = pl.program_id(1) * HP
    chains = [(hh, d) for hh in range(HP) for d in range(2)]
    r_i = lax.broadcasted_iota(jnp.int32, (L, L), 0)
    c_i = lax.broadcasted_iota(jnp.int32, (L, L), 1)
    masks = (c_i <= r_i, c_i >= r_i)

    sr = lax.broadcasted_iota(jnp.int32, (LN, GATE_QUANTS * LN), 0)
    sc = lax.broadcasted_iota(jnp.int32, (LN, GATE_QUANTS * LN), 1)
    span = GATE_PIECES * GATE_SLOTS
    in_block = None
    for qi in range(GATE_QUANTS):
        blk = (sr >= qi * span) & (sr < (qi + 1) * span) & (sc >= qi * LN) & (sc < (qi + 1) * LN)
        in_block = blk if in_block is None else in_block | blk
    slot = sr & (GATE_SLOTS - 1)
    sels = {(hh, d): (in_block & (slot == 2 * (head0 + hh) + d)).astype(bf16) for hh, d in chains}

    def tile(v, width):
        return jnp.concatenate([v] * (width // LN), axis=1)

    def step(refs, r0s, ms):
        q_ref, k_ref, v_ref, col_ref, row_ref = refs
        st = []
        for ci, (hh, d) in enumerate(chains):
            rows = pl.ds(r0s[ci], L)
            hsl = slice(hh * Dh, (hh + 1) * Dh)
            q = q_ref[rows, hsl]
            k = k_ref[rows, hsl]
            rep = jnp.dot(col_ref[rows, :], sels[hh, d], preferred_element_type=f32)
            qk = lax.dot_general(q, k, (((1,), (1,)), ((), ())), preferred_element_type=f32)
            qc = jnp.dot(q, c_ref[ci].astype(bf16), preferred_element_type=f32)
            st.append(dict(q=q, k=k, v=v_ref[rows, hsl], rep=rep, qk=qk, qc=qc, a_row=row_ref[hh, d:d + 1, rows]))
        new_ms = []
        for ci, (hh, d) in enumerate(chains):
            c, m_prev = st[ci], ms[ci]
            b_rep, a_rep = c["rep"][:, :LN], c["rep"][:, LN:2 * LN]
            g_rep = jnp.maximum(c["rep"][:, 2 * LN:], m_prev)
            p = jnp.exp(jnp.where(masks[d], c["a_row"] - tile(g_rep, L), NEG_BIG))
            s = c["qk"] * p
            c["s"] = s.astype(bf16)
            inter = jnp.exp(m_prev - g_rep)
            qn = jnp.sum(c["q"].astype(f32) * n_ref[ci], axis=1, keepdims=True)
            den = jnp.sum(s, axis=1, keepdims=True) + inter[:, :1] * qn
            floor = jnp.exp(-(b_rep + g_rep))
            rcp = 1.0 / jnp.maximum(jnp.abs(den), floor[:, :1])
            c["rcp"] = jnp.broadcast_to(rcp, (L, LN))
            c["inter"] = inter
            b_end = b_rep[L - 1:L, :] if d == 0 else b_rep[0:1, :]
            w = b_end + a_rep
            m_new = jnp.maximum(b_end + m_prev, jnp.max(w, axis=0, keepdims=True))
            c["decay"] = jnp.exp(b_end + m_prev - m_new)
            kw = c["k"].astype(f32) * tile(jnp.exp(w - m_new), Dh)
            c["kw"] = kw.astype(bf16)
            c["ksum"] = jnp.sum(kw, axis=0, keepdims=True)
            new_ms.append(m_new)
        for c in st:
            c["sv"] = jnp.dot(c["s"], c["v"], preferred_element_type=f32)
            c["upd"] = lax.dot_general(c["kw"], c["v"], (((0,), (0,)), ((), ())), preferred_element_type=f32)
        hs = []
        for ci, c in enumerate(st):
            hs.append((c["sv"] + tile(c["inter"], Dh) * c["qc"]) * tile(c["rcp"], Dh))
            decay = tile(c["decay"], Dh)
            c_ref[ci] = decay * c_ref[ci] + c["upd"]
            n_ref[ci] = decay * n_ref[ci] + c["ksum"]
        return hs, new_ms

    ctx = (qc_ref, kc_ref, vc_ref, cc_ref, rc_ref)
    lat = (qx_ref, kx_ref, vx_ref, cx_ref, rx_ref)

    c_ref[...] = jnp.zeros_like(c_ref)
    n_ref[...] = jnp.zeros_like(n_ref)
    ms = [jnp.zeros((1, LN), f32) for _ in chains]
    written = set()
    for j in range(nc):
        cjs = [j if d == 0 else nc - 1 - j for _, d in chains]
        hs, ms = step(ctx, [cj * L for cj in cjs], ms)
        for (hh, d), cj, h in zip(chains, cjs, hs):
            dst = (slice(cj * L, (cj + 1) * L), slice(hh * Dh, (hh + 1) * Dh))
            if (cj, hh) in written:
                hc_ref[dst] = (hc_ref[dst].astype(f32) + h).astype(hc_ref.dtype)
            else:
                hc_ref[dst] = h.astype(hc_ref.dtype)
                written.add((cj, hh))

    def make_body(accumulate):
        def body(i, ms):
            r0s = [pl.multiple_of((i if d == 0 else nx - 1 - i) * L, L) for _, d in chains]
            hs, ms = step(lat, r0s, list(ms))
            for (hh, d), r0, h in zip(chains, r0s, hs):
                dst = (pl.ds(r0, L), slice(hh * Dh, (hh + 1) * Dh))
                if accumulate:
                    hx_ref[dst] = (hx_ref[dst].astype(f32) + h).astype(hx_ref.dtype)
                else:
                    hx_ref[dst] = h.astype(hx_ref.dtype)
            return tuple(ms)
        return body

    ms = lax.fori_loop(0, nx // 2, make_body(False), tuple(ms))
    lax.fori_loop(nx // 2, nx, make_body(True), ms)


def _mlstm(qx, kx, vx, colsx, rowsx, qc, kc, vc, colsc, rowsc, L):
    B, T, _ = qx.shape
    Tc = qc.shape[1]
    H, Dh, HP = N_HEADS_A, HEAD_DIM_A, MLSTM_HEADS_PER_STEP
    assert T % (2 * L) == 0 and Tc % L == 0 and H % HP == 0

    def seq_spec(t):
        return pl.BlockSpec((None, t, HP * Dh), lambda b, h: (b, 0, h))

    def col_spec(t):
        return pl.BlockSpec((None, t, V7X_LANES), lambda b, h: (b, 0, 0))

    def row_spec(t):
        return pl.BlockSpec((None, HP, 2, t), lambda b, h: (b, h, 0, 0))

    est = (2 * HP * (3 * (T + Tc) * Dh * 2 + 8 * (T + Tc) * 4 + (T + Tc) * Dh * 2)
           + 2 * (T + Tc) * V7X_LANES * 2 + 2 * HP * Dh * (Dh + V7X_LANES) * 4 + 16 * L * L * 4)
    return pl.pallas_call(
        functools.partial(_mlstm_kernel, L=L, nx=T // L, nc=Tc // L),
        grid=(B, H // HP),
        in_specs=[seq_spec(T), seq_spec(T), seq_spec(T), col_spec(T), row_spec(T),
                  seq_spec(Tc), seq_spec(Tc), seq_spec(Tc), col_spec(Tc), row_spec(Tc)],
        out_specs=[seq_spec(T), seq_spec(Tc)],
        out_shape=[jax.ShapeDtypeStruct((B, T, D_A), bf16), jax.ShapeDtypeStruct((B, Tc, D_A), bf16)],
        scratch_shapes=[pltpu.VMEM((2 * HP, Dh, Dh), f32), pltpu.VMEM((2 * HP, 1, Dh), f32)],
        compiler_params=pltpu.CompilerParams(dimension_semantics=("parallel", "parallel"),
                                             vmem_limit_bytes=_vmem_limit(est)),
        name="mlstm",
    )(qx, kx, vx, colsx, rowsx, qc, kc, vc, colsc, rowsc)


def _fourier_w_kernel(cs_ref, fw_ref, o_ref, *, scale):
    o_ref[...] = (jnp.dot(cs_ref[...], fw_ref[...], preferred_element_type=f32,
                          precision=lax.Precision.HIGHEST) * scale).astype(bf16)


def _fourier_weights(fw, T):
    n = GROUP_B
    kk = np.outer(np.arange(n), np.arange(n)) % n
    ang = 2.0 * np.pi * kk / n
    cs = jnp.asarray(np.concatenate([np.cos(ang), np.sin(ang)], axis=0), f32)
    return pl.pallas_call(
        functools.partial(_fourier_w_kernel, scale=float(1.0 / np.sqrt(T * n))),
        grid=(N_GROUPS_B,),
        in_specs=[pl.BlockSpec((2 * n, n), lambda g: (0, 0)),
                  pl.BlockSpec((None, n, n), lambda g: (g, 0, 0))],
        out_specs=pl.BlockSpec((None, 2 * n, n), lambda g: (g, 0, 0)),
        out_shape=jax.ShapeDtypeStruct((N_GROUPS_B, 2 * n, n), bf16),
        compiler_params=pltpu.CompilerParams(dimension_semantics=("arbitrary",)),
        name="fourier_weights",
    )(cs, fw)


def _fft_tables():
    n1, n2 = FFT_N1, FFT_N2
    n = n1 * n2
    t1 = np.arange(n1)
    k1 = np.arange(n1)
    t2 = np.arange(n2)
    idx = (k1[None, :, None] * (n2 * t1[None, None, :] + t2[:, None, None])) % n
    ang = 2.0 * np.pi * idx / n
    m1 = np.concatenate([np.cos(ang), -np.sin(ang)], axis=1)
    k2 = np.arange(n2)
    ph = 2.0 * np.pi * (np.outer(k2, t2) % n2) / n2
    c, s = np.cos(ph), np.sin(ph)
    m2 = np.block([[c, s], [-s, c]])
    return jnp.asarray(m1, f32).astype(bf16), jnp.asarray(m2, f32).astype(bf16)


def _fourier_kernel(u_ref, z_ref, m1_ref, m2_ref, w_ref, o_ref, up_ref, yp_ref, xp_ref, pbuf_ref):
    n1, n2, P = FFT_N1, FFT_N2, FFT_PITCH
    nl = GROUP_B // V7X_LANES

    def fill(t1, carry):
        src = pl.multiple_of(t1 * n2, n2)
        dst = pl.multiple_of(t1 * P, V7X_SUBLANES)
        blk = u_ref[pl.ds(src, n2), :].astype(f32)
        for s in range(nl):
            up_ref[s, pl.ds(dst, n2), :] = blk[:, s * V7X_LANES:(s + 1) * V7X_LANES]
        return carry

    lax.fori_loop(0, n1, fill, 0, unroll=4)

    G = FFT_UNROLL

    def stage1(grp, carry):
        t2s = [grp * G + j for j in range(G)]
        rhs = [jnp.concatenate([up_ref[s, pl.ds(t2, n1, stride=P), :] for s in range(nl)],
                               axis=1).astype(bf16) for t2 in t2s]
        ys = [jnp.dot(m1_ref[t2], r, preferred_element_type=f32) for t2, r in zip(t2s, rhs)]
        for t2, y in zip(t2s, ys):
            dst = pl.multiple_of(t2 * P, V7X_SUBLANES)
            for ri in range(2):
                for s in range(nl):
                    yp_ref[ri * nl + s, pl.ds(dst, n1), :] = y[ri * n1:(ri + 1) * n1,
                                                               s * V7X_LANES:(s + 1) * V7X_LANES]
        return carry

    lax.fori_loop(0, n2 // G, stage1, 0)

    m2 = m2_ref[...]

    def stage2(grp, carry):
        k1s = [grp * G + j for j in range(G)]
        rhs = []
        for k1 in k1s:
            parts = [jnp.concatenate([yp_ref[ri * nl + s, pl.ds(k1, n2, stride=P), :] for s in range(nl)], axis=1)
                     for ri in range(2)]
            rhs.append(jnp.concatenate(parts, axis=0).astype(bf16))
        xs = [jnp.dot(m2, r, preferred_element_type=f32) for r in rhs]
        for k1, x in zip(k1s, xs):
            for ri in range(2):
                for s in range(nl):
                    xp_ref[ri * nl + s, pl.ds(k1, n2, stride=P), :] = x[ri * n2:(ri + 1) * n2,
                                                                        s * V7X_LANES:(s + 1) * V7X_LANES]
        return carry

    lax.fori_loop(0, n1 // G, stage2, 0)

    tr = pbuf_ref.shape[1]

    def matmul(t):
        blocks = []
        for k2 in range(t * (tr // n1), (t + 1) * (tr // n1)):
            blocks.append(jnp.concatenate([xp_ref[sl, k2 * P:k2 * P + n1, :] for sl in range(2 * nl)],
                                          axis=1).astype(bf16))
        return jnp.dot(jnp.concatenate(blocks, axis=0), w_ref[...], preferred_element_type=f32)

    def gate_store(t, y):
        sl = slice(t * tr, (t + 1) * tr)
        o_ref[sl, :] = y.astype(bf16) * _silu(z_ref[sl, :])

    _pipelined_matmuls((n1 * n2) // tr, matmul, gate_store, pbuf_ref)


def _fourier_latent(ub, zb, wcs):
    B, T, _ = ub.shape
    assert T == FFT_N1 * FFT_N2
    m1, m2 = _fft_tables()
    G, n = N_GROUPS_B, GROUP_B
    nl = n // V7X_LANES
    rows_p = FFT_N1 * FFT_PITCH
    blk = pl.BlockSpec((None, T, n), lambda b, g: (b, 0, g))
    est = (3 * 2 * T * n * 2 + 5 * nl * rows_p * V7X_LANES * 4 + T * 2 * n * 2 + 2 * m1.size * 2 + 8 * 512 * n * 4)
    return pl.pallas_call(
        _fourier_kernel,
        grid=(B, G),
        in_specs=[blk, blk,
                  pl.BlockSpec(m1.shape, lambda b, g: (0, 0, 0)),
                  pl.BlockSpec(m2.shape, lambda b, g: (0, 0)),
                  pl.BlockSpec((None, 2 * n, n), lambda b, g: (g, 0, 0))],
        out_specs=blk,
        out_shape=jax.ShapeDtypeStruct((B, T, D_B), bf16),
        scratch_shapes=[pltpu.VMEM((nl, rows_p, V7X_LANES), f32),
                        pltpu.VMEM((2 * nl, rows_p, V7X_LANES), f32),
                        pltpu.VMEM((2 * nl, rows_p, V7X_LANES), f32),
                        pltpu.VMEM((3, 512, n), f32)],
        compiler_params=pltpu.CompilerParams(dimension_semantics=("parallel", "parallel"),
                                             vmem_limit_bytes=_vmem_limit(est)),
        name="fourier_latent",
    )(ub, zb, m1, m2, wcs)


def _fourier_ctx_kernel(u_ref, z_ref, cs_ref, w_ref, o_ref):
    u = u_ref[...]
    x = jnp.dot(cs_ref[...], u, preferred_element_type=f32).astype(bf16)
    t = u.shape[0]
    w = w_ref[...]
    n = GROUP_B
    y = (jnp.dot(x[:t], w[:n], preferred_element_type=f32) + jnp.dot(x[t:], w[n:], preferred_element_type=f32))
    o_ref[...] = (y * _silu(z_ref[...].astype(f32))).astype(bf16)


def _fourier_ctx(ub, zb, wcs):
    B, T, _ = ub.shape
    ang = 2.0 * np.pi * (np.outer(np.arange(T), np.arange(T)) % T) / T
    cs = jnp.asarray(np.concatenate([np.cos(ang), -np.sin(ang)], axis=0), f32).astype(bf16)
    n = GROUP_B
    blk = pl.BlockSpec((None, T, n), lambda b, g: (b, 0, g))
    return pl.pallas_call(
        _fourier_ctx_kernel,
        grid=(B, N_GROUPS_B),
        in_specs=[blk, blk, pl.BlockSpec(cs.shape, lambda b, g: (0, 0)),
                  pl.BlockSpec((None, 2 * n, n), lambda b, g: (g, 0, 0))],
        out_specs=blk,
        out_shape=jax.ShapeDtypeStruct((B, T, D_B), bf16),
        compiler_params=pltpu.CompilerParams(dimension_semantics=("parallel", "parallel")),
        name="fourier_ctx",
    )(ub, zb, cs, wcs)


def _outproj_even_kernel(x_ref, h_ref, o_ref, za_ref, yb_ref, gate_ref, hg_ref, w_ref, out_ref):
    parts = []
    for hd in range(N_HEADS_A):
        sl = slice(hd * HEAD_DIM_A, (hd + 1) * HEAD_DIM_A)
        hh = _sigmoid(o_ref[:, sl]).astype(f32) * h_ref[:, sl]
        hh = hh * lax.rsqrt(jnp.mean(hh * hh, axis=-1, keepdims=True) + EPS)
        parts.append(hh.astype(bf16) * (hg_ref[:, sl].astype(bf16) * _silu(za_ref[:, sl])))
    ya = jnp.concatenate(parts, axis=1)
    acc = jnp.dot(ya, w_ref[:D_A, :], preferred_element_type=f32)
    acc = acc + jnp.dot(yb_ref[...], w_ref[D_A:, :], preferred_element_type=f32)
    out_ref[...] = x_ref[...] + gate_ref[...] * acc


def _outproj_even(x, h, o, za, yb, gate, head_g, wout, tm):
    B, T, D = x.shape
    row = lambda n: pl.BlockSpec((None, tm, n), lambda b, i: (b, i, 0))
    est = 2 * wout.size * 2 + 2 * tm * (2 * D * 4 + D_A * 4 + 3 * D_A * 2) + 8 * tm * D * 4
    return pl.pallas_call(
        _outproj_even_kernel,
        grid=(B, T // tm),
        in_specs=[row(D), row(D_A), row(D_A), row(D_A), row(D_B),
                  pl.BlockSpec((None, 1, D), lambda b, i: (b, 0, 0)),
                  pl.BlockSpec((1, D_A), lambda b, i: (0, 0)),
                  pl.BlockSpec(wout.shape, lambda b, i: (0, 0))],
        out_specs=row(D),
        out_shape=jax.ShapeDtypeStruct((B, T, D), f32),
        compiler_params=pltpu.CompilerParams(dimension_semantics=("parallel", "parallel"),
                                             vmem_limit_bytes=_vmem_limit(est)),
        name="outproj_even",
    )(x, h, o, za, yb, gate, head_g.reshape(1, D_A), wout)


def _inproj_odd_kernel(x_ref, sh_ref, sc_ref, g_ref, w_ref, u_ref, z_ref):
    amp = g_ref[...] * (1.0 + sc_ref[...])
    hx = _normed(x_ref[...], amp, sh_ref[...]).astype(bf16)
    cn = 512
    for j in range(D_INNER // cn):
        u_ref[:, j * cn:(j + 1) * cn] = jnp.dot(hx, w_ref[:, j * cn:(j + 1) * cn],
                                                preferred_element_type=f32).astype(bf16)
        z_ref[:, j * cn:(j + 1) * cn] = jnp.dot(hx, w_ref[:, D_INNER + j * cn:D_INNER + (j + 1) * cn],
                                                preferred_element_type=f32).astype(bf16)


def _even_out_odd_in_kernel(x_ref, h_ref, o_ref, za_ref, yb_ref, gate_ref, hg_ref, wo_ref,
                            sh_ref, sc_ref, g_ref, wi_ref, x1_ref, u_ref, z_ref):
    _outproj_even_kernel(x_ref, h_ref, o_ref, za_ref, yb_ref, gate_ref, hg_ref, wo_ref, x1_ref)
    _inproj_odd_kernel(x1_ref, sh_ref, sc_ref, g_ref, wi_ref, u_ref, z_ref)


def _even_out_odd_in(x, h, o, za, yb, gate, head_g, wout, shift, scale, norm_g, win, tm):
    B, T, D = x.shape
    row = lambda n: pl.BlockSpec((None, tm, n), lambda b, i: (b, i, 0))
    bvec = pl.BlockSpec((None, 1, D), lambda b, i: (b, 0, 0))
    est = (2 * (wout.size + win.size) * 2 + 2 * tm * (2 * D * 4 + 4 * D_A * 2 + 2 * D_INNER * 2)
           + 8 * tm * D * 4 + 6 * tm * 512 * 4)
    return pl.pallas_call(
        _even_out_odd_in_kernel,
        grid=(B, T // tm),
        in_specs=[row(D), row(D_A), row(D_A), row(D_A), row(D_B), bvec,
                  pl.BlockSpec((1, D_A), lambda b, i: (0, 0)),
                  pl.BlockSpec(wout.shape, lambda b, i: (0, 0)),
                  bvec, bvec,
                  pl.BlockSpec((1, D), lambda b, i: (0, 0)),
                  pl.BlockSpec(win.shape, lambda b, i: (0, 0))],
        out_specs=[row(D), row(D_INNER), row(D_INNER)],
        out_shape=[jax.ShapeDtypeStruct((B, T, D), f32)] + [jax.ShapeDtypeStruct((B, T, D_INNER), bf16)] * 2,
        compiler_params=pltpu.CompilerParams(dimension_semantics=("parallel", "parallel"),
                                             vmem_limit_bytes=_vmem_limit(est)),
        name="even_out_odd_in",
    )(x, h, o, za, yb, gate, head_g.reshape(1, D_A), wout, shift, scale, norm_g.reshape(1, D), win)


POOL_UNROLL = 16


def _pool_tables():
    w_idx = np.arange(GRID_W)
    band = np.zeros((N_GROUPS_C, GRID_W, GRID_W), np.float32)
    inv_w = np.zeros((N_GROUPS_C, GRID_W, V7X_LANES), np.float32)
    for g, win in enumerate(POOL_WINDOWS):
        lo = np.clip(w_idx - win // 2, 0, GRID_W)
        hi = np.clip(w_idx + win - win // 2, 0, GRID_W)
        band[g] = (w_idx[None, :] >= lo[:, None]) & (w_idx[None, :] < hi[:, None])
        inv_w[g] = (1.0 / (hi - lo))[:, None]
    return jnp.asarray(band, bf16), jnp.asarray(inv_w, f32)


def _pool_kernel(u_ref, z_ref, band_ref, invw_ref, pw_ref, sc_ref, o_ref, ps_ref, pbuf_ref, *, rows):
    g = pl.program_id(1)
    W = GRID_W
    band = band_ref[...]
    lo_off = hi_off = 0
    for gi, win in enumerate(POOL_WINDOWS):
        lo_off = jnp.where(g == gi, win // 2, lo_off)
        hi_off = jnp.where(g == gi, win - win // 2, hi_off)

    ps_ref[0:W, :] = jnp.zeros((W, GROUP_C), f32)

    def width_sum(grp, carry):
        srcs = [pl.multiple_of((grp * POOL_UNROLL + j) * W, W) for j in range(POOL_UNROLL)]
        sums = [jnp.dot(band, u_ref[pl.ds(src, W), :], preferred_element_type=f32) for src in srcs]
        acc = ps_ref[pl.ds(srcs[0], W), :]
        for src, s in zip(srcs, sums):
            acc = acc + s
            ps_ref[pl.ds(src + W, W), :] = acc
        return carry

    lax.fori_loop(0, rows // POOL_UNROLL, width_sum, 0)

    inv_w = invw_ref[...]

    def pooled_minus_self(r):
        lo = jnp.maximum(r - lo_off, 0)
        hi = jnp.minimum(r + hi_off, rows)
        acc = (ps_ref[pl.ds(pl.multiple_of(hi * W, W), W), :]
               - ps_ref[pl.ds(pl.multiple_of(lo * W, W), W), :])
        inv = inv_w / (hi - lo).astype(f32)
        inv = jnp.concatenate([inv] * (GROUP_C // V7X_LANES), axis=1)
        ug = u_ref[r * W:(r + 1) * W, :].astype(f32)
        return (acc * inv - ug).astype(bf16)

    tr = pbuf_ref.shape[1]
    pws = (pw_ref[...] * sc_ref[...]).astype(bf16)

    def matmul(t):
        lhs = jnp.concatenate([pooled_minus_self(t * (tr // W) + j) for j in range(tr // W)], axis=0)
        return jnp.dot(lhs, pws, preferred_element_type=f32)

    def gate_store(t, y):
        sl = slice(t * tr, (t + 1) * tr)
        o_ref[sl, :] = y.astype(bf16) * _silu(z_ref[sl, :])

    _pipelined_matmuls((rows * W) // tr, matmul, gate_store, pbuf_ref)


def _pool_mix(u, z, pool_w, scale):
    B, T, _ = u.shape
    rows = T // GRID_W
    band, inv_w = _pool_tables()
    n = GROUP_C
    blk = pl.BlockSpec((None, T, n), lambda b, g: (b, 0, g))
    est = 3 * 2 * T * n * 2 + (rows + 1) * GRID_W * n * 4 + T * n * 2 + 2 * n * n * 4 + 8 * 512 * n * 4
    return pl.pallas_call(
        functools.partial(_pool_kernel, rows=rows),
        grid=(B, N_GROUPS_C),
        in_specs=[blk, blk,
                  pl.BlockSpec((None, GRID_W, GRID_W), lambda b, g: (g, 0, 0)),
                  pl.BlockSpec((None, GRID_W, V7X_LANES), lambda b, g: (g, 0, 0)),
                  pl.BlockSpec((None, n, n), lambda b, g: (g, 0, 0)),
                  pl.BlockSpec((None, 1, n), lambda b, g: (g, 0, 0))],
        out_specs=blk,
        out_shape=jax.ShapeDtypeStruct((B, T, D_INNER), bf16),
        scratch_shapes=[pltpu.VMEM(((rows + 1) * GRID_W, n), f32),
                        pltpu.VMEM((3, 512, n), f32)],
        compiler_params=pltpu.CompilerParams(dimension_semantics=("parallel", "arbitrary"),
                                             vmem_limit_bytes=_vmem_limit(est)),
        name="pool_mix",
    )(u, z, band, inv_w, pool_w, scale.reshape(N_GROUPS_C, 1, n))


def _outproj_odd_kernel(x_ref, y_ref, gate_ref, fg_ref, w_ref, out_ref):
    acc = jnp.dot(y_ref[...], w_ref[...], preferred_element_type=f32)
    x = x_ref[...] + gate_ref[...] * acc
    out_ref[...] = x * lax.rsqrt(jnp.mean(x * x, axis=-1, keepdims=True) + EPS) * fg_ref[...]


def _outproj_odd(x, y, gate, final_g, wout, tm):
    B, T, D = x.shape
    row = lambda n: pl.BlockSpec((None, tm, n), lambda b, i: (b, i, 0))
    est = 2 * wout.size * 2 + 2 * tm * (2 * D * 4 + D_INNER * 2) + 6 * tm * D * 4
    return pl.pallas_call(
        _outproj_odd_kernel,
        grid=(B, T // tm),
        in_specs=[row(D), row(D_INNER),
                  pl.BlockSpec((None, 1, D), lambda b, i: (b, 0, 0)),
                  pl.BlockSpec((1, D), lambda b, i: (0, 0)),
                  pl.BlockSpec(wout.shape, lambda b, i: (0, 0))],
        out_specs=row(D),
        out_shape=jax.ShapeDtypeStruct((B, T, D), f32),
        compiler_params=pltpu.CompilerParams(dimension_semantics=("parallel", "parallel"),
                                             vmem_limit_bytes=_vmem_limit(est)),
        name="outproj_odd",
    )(x, y, gate, final_g.reshape(1, D), wout)


def kernel(x, c, ctx, c_ctx, ada_w, ada_b, norm_g, win_even, gate_b_even, conv_qk_even, head_norm_even,
           fourier_w_even, wout_even, win_odd, pool_w_odd, pool_scale_odd, wout_odd, final_g):
    B, T, D = x.shape
    Tc = ctx.shape[1]
    H = N_HEADS_A
    L = MLSTM_CHUNK

    nrow = -(-(B + 1) // V7X_SUBLANES) * V7X_SUBLANES
    rows_in = jnp.zeros((nrow, D), f32).at[:B].set(c).at[B].set(c_ctx)
    mod = _modulation(rows_in, ada_w, ada_b)

    def mod_parts(l, r0, r1, n):
        m = mod[l, r0:r1]
        parts = [jnp.broadcast_to(m[:, None, i * D:(i + 1) * D], (n, 1, D)) for i in range(3)]
        return parts

    we = win_even[0]
    w_main = we.astype(bf16)
    gcols = we[:, W_MAIN_EVEN:].reshape(D, 4, H)
    gbias = gate_b_even[0].reshape(4, H)
    ig_w = gcols[:, 0::2, :].transpose(0, 2, 1).reshape(D, GATE_SLOTS)
    fg_w = gcols[:, 1::2, :].transpose(0, 2, 1).reshape(D, GATE_SLOTS)
    ig_b = gbias[0::2, :].T.reshape(GATE_SLOTS)
    fg_b = gbias[1::2, :].T.reshape(GATE_SLOTS)
    wgt =jnp.concatenate([ig_w, fg_w], axis=1).T.astype(bf16)
    gbt = jnp.concatenate([ig_b, fg_b]).reshape(N_GATES, 1)
    conv_w = conv_qk_even[0]

    shift_x, scale_x, gate_x = mod_parts(0, 0, B, B)
    shift_c, scale_c, gate_c = mod_parts(0, B, B + 1, B)

    qx, kx, vx, ox, zax, ubx, zbx, gtx = _inproj_even(
        x, shift_x, scale_x, norm_g[0], w_main, wgt, gbt, conv_w, tm=512)
    qc, kc, vc, gtc = _inproj_even(ctx, shift_c, scale_c, norm_g[0], w_main, wgt, gbt, conv_w, tm=Tc, part="scan")
    oc, zac, ubc, zbc = _inproj_even(ctx, shift_c, scale_c, norm_g[0], w_main, wgt, gbt, conv_w, tm=Tc, part="mix")

    colsx, rowsx = _gate_prep(gtx, L)
    colsc, rowsc = _gate_prep(gtc, L)
    h_x, h_c = _mlstm(qx, kx, vx, colsx, rowsx, qc, kc, vc, colsc, rowsc, L)

    wcs = _fourier_weights(fourier_w_even[0], T)
    yb_x = _fourier_latent(ubx, zbx, wcs)
    wout_e = wout_even[0].astype(bf16)
    shift_1, scale_1, gate_1 = mod_parts(1, 0, B, B)
    x1, u1, z1 = _even_out_odd_in(x, h_x, ox, zax, yb_x, gate_x, head_norm_even[0], wout_e,
                                  shift_1, scale_1, norm_g[1], win_odd[0].astype(bf16), tm=512)

    wcs_c = _fourier_weights(fourier_w_even[0], Tc)
    yb_c = _fourier_ctx(ubc, zbc, wcs_c)
    ctx1 = _outproj_even(ctx, h_c, oc, zac, yb_c, gate_c, head_norm_even[0], wout_e, tm=Tc)
    del ctx1

    y1 = _pool_mix(u1, z1, pool_w_odd[0], pool_scale_odd[0])
    return _outproj_odd(x1, y1, gate_1, final_g, wout_odd[0].astype(bf16), tm=1024)
```

```python
import functools

import numpy as np
import jax
import jax.numpy as jnp
from jax import lax
from jax.experimental import pallas as pl
from jax.experimental.pallas import tpu as pltpu

D_MODEL = 1024
DEPTH = 2
CTX_LEN = 256
GRID_W = 64
D_INNER = 2 * D_MODEL
D_A = D_INNER // 2
D_B = D_INNER - D_A
N_HEADS_A = 4
HEAD_DIM_A = D_A // N_HEADS_A
N_GROUPS_B = 4
GROUP_B = D_B // N_GROUPS_B
N_GROUPS_C = 4
GROUP_C = D_INNER // N_GROUPS_C
POOL_WINDOWS = (2, 4, 8, 16)
CONV_W = 3
N_GATES = 4 * N_HEADS_A
W_MAIN_EVEN = 5 * D_A + 2 * D_B
EPS = 1e-6

f32 = jnp.float32
bf16 = jnp.bfloat16

V7X_VMEM_BYTES = 64 * 1024 * 1024
V7X_LANES = 128
V7X_SUBLANES = 8

MLSTM_CHUNK = 256
MLSTM_HEADS_PER_STEP = 2
FFT_N1 = 64
FFT_N2 = 64
FFT_PITCH = 72
FFT_UNROLL = 32
NEG_BIG = -1e30


def _vmem_limit(nbytes):
    return int(min(max(nbytes * 5 // 4 + (4 << 20), 16 << 20), V7X_VMEM_BYTES - (6 << 20)))


def _sigmoid(v):
    return 0.5 * jnp.tanh(0.5 * v) + 0.5


def _silu(v):
    return v * _sigmoid(v)


def _pipelined_matmuls(n, matmul, epilogue, pbuf_ref):
    zero = jnp.minimum(pl.program_id(0), 0)
    nbuf = pbuf_ref.shape[0]
    pbuf_ref[zero] = matmul(0)
    for t in range(n):
        if t + 1 < n:
            pbuf_ref[zero + (t + 1) % nbuf] = matmul(t + 1)
        epilogue(t, pbuf_ref[zero + t % nbuf])


def _log_sigmoid(v):
    return jnp.minimum(v, 0.0) - jnp.log1p(jnp.exp(-jnp.abs(v)))


def _mod_kernel(r_ref, w_ref, b_ref, o_ref):
    s = _silu(r_ref[...])
    o_ref[...] = jnp.dot(s, w_ref[...], preferred_element_type=f32,
                         precision=lax.Precision.HIGHEST) + b_ref[...]


def _modulation(rows, ada_w, ada_b):
    nrow = rows.shape[0]
    tn = 1024
    return pl.pallas_call(
        _mod_kernel,
        grid=(DEPTH, 3 * D_MODEL // tn),
        in_specs=[
            pl.BlockSpec((nrow, D_MODEL), lambda l, j: (0, 0)),
            pl.BlockSpec((None, D_MODEL, tn), lambda l, j: (l, 0, j)),
            pl.BlockSpec((None, 1, tn), lambda l, j: (l, 0, j)),
        ],
        out_specs=pl.BlockSpec((None, nrow, tn), lambda l, j: (l, 0, j)),
        out_shape=jax.ShapeDtypeStruct((DEPTH, nrow, 3 * D_MODEL), f32),
        compiler_params=pltpu.CompilerParams(dimension_semantics=("arbitrary", "arbitrary")),
        name="modulation",
    )(rows, ada_w, ada_b.reshape(DEPTH, 1, 3 * D_MODEL))


def _normed(x, amp, shift):
    ms = jnp.mean(x * x, axis=-1, keepdims=True)
    return (x * lax.rsqrt(ms + EPS)) * amp + shift


INPROJ_EVEN_OUTPUTS = {
    "all": ("q", "k", "v", "o", "za", "ub", "zb", "gt"),
    "scan": ("q", "k", "v", "gt"),
    "mix": ("o", "za", "ub", "zb"),
}
INPROJ_EVEN_PLAIN = ("v", "o", "za", "ub", "zb")


def _inproj_even_kernel(x_ref, xp_ref, xn_ref, sh_ref, sc_ref, g_ref, w_ref, wgt_ref, gbt_ref, cw_ref, *refs,
                        tm, nt, part):
    out = dict(zip(INPROJ_EVEN_OUTPUTS[part], refs[:-1]))
    pbuf_ref = refs[-1]
    i = pl.program_id(1)
    amp = g_ref[...] * (1.0 + sc_ref[...])
    shift = sh_ref[...]
    hx = _normed(x_ref[...], amp, shift).astype(bf16)
    cn = 512
    tasks = []
    if "q" in out:
        halo = jnp.concatenate([xp_ref[...], xn_ref[...]], axis=0)
        hh = _normed(halo, amp, shift).astype(bf16)
        has_prev = (i > 0).astype(f32)
        has_next = (i < nt - 1).astype(f32)
        row = lax.broadcasted_iota(jnp.int32, (tm, 1), 0)
        ph = jnp.dot(hh, w_ref[:, :2 * D_A], preferred_element_type=f32)
        prev = ph[V7X_SUBLANES - 1:V7X_SUBLANES, :] * has_prev
        nxt = ph[V7X_SUBLANES:V7X_SUBLANES + 1, :] * has_next

        def conv_store(j, p):
            cols = slice(j * cn, (j + 1) * cn)
            up = jnp.where(row == 0, prev[:, cols], pltpu.roll(p, 1, 0))
            dn = jnp.where(row == tm - 1, nxt[:, cols], pltpu.roll(p, tm - 1, 0))
            cw = cw_ref[:, cols]
            y = _silu(cw[0:1, :] * up + cw[1:2, :] * p + cw[2:3, :] * dn)
            if j < D_A // cn:
                out["q"][:, cols] = (y * (HEAD_DIM_A ** -0.5)).astype(bf16)
            else:
                jj = j - D_A // cn
                out["k"][:, jj * cn:(jj + 1) * cn] = y.astype(bf16)

        tasks += [(j * cn, functools.partial(conv_store, j)) for j in range(2 * D_A // cn)]

    def plain_store(ref, jj):
        def store(p):
            ref[:, jj * cn:(jj + 1) * cn] = p.astype(bf16)
        return store

    for idx, name in enumerate(INPROJ_EVEN_PLAIN):
        if name in out:
            for jj in range(D_A // cn):
                tasks.append((2 * D_A + idx * D_A + jj * cn, plain_store(out[name], jj)))

    def matmul(t):
        c0 = tasks[t][0]
        return jnp.dot(hx, w_ref[:, c0:c0 + cn], preferred_element_type=f32)

    _pipelined_matmuls(len(tasks), matmul, lambda t, p: tasks[t][1](p), pbuf_ref)
    if "gt" in out:
        out["gt"][...] = lax.dot_general(wgt_ref[...], hx, (((1,), (1,)), ((), ())),
                                         preferred_element_type=f32) + gbt_ref[...]


def _inproj_even(x, shift, scale, norm_g, w_main, wgt, gbt, conv_w, tm, part="all"):
    B, T, D = x.shape
    nt = T // tm
    hb = tm // V7X_SUBLANES
    nhb = T // V7X_SUBLANES
    names = INPROJ_EVEN_OUTPUTS[part]
    row_spec = pl.BlockSpec((None, tm, D_A), lambda b, i: (b, i, 0))
    gt_spec = pl.BlockSpec((None, N_GATES, tm), lambda b, i: (b, 0, i))
    vec = lambda n: pl.BlockSpec((1, n), lambda b, i: (0, 0))
    est = (2 * w_main.size * 2 + 2 * tm * D * 4 + len(names) * 2 * tm * D_A * 2 + 6 * tm * 512 * 4)
    outs = pl.pallas_call(
        functools.partial(_inproj_even_kernel, tm=tm, nt=nt, part=part),
        grid=(B, nt),
        in_specs=[
            pl.BlockSpec((None, tm, D), lambda b, i: (b, i, 0)),
            pl.BlockSpec((None, V7X_SUBLANES, D), lambda b, i: (b, jnp.maximum(i * hb - 1, 0), 0)),
            pl.BlockSpec((None, V7X_SUBLANES, D), lambda b, i: (b, jnp.minimum((i + 1) * hb, nhb - 1), 0)),
            pl.BlockSpec((None, 1, D), lambda b, i: (b, 0, 0)),
            pl.BlockSpec((None, 1, D), lambda b, i: (b, 0, 0)),
            vec(D),
            pl.BlockSpec(w_main.shape, lambda b, i: (0, 0)),
            pl.BlockSpec(wgt.shape, lambda b, i: (0, 0)),
            pl.BlockSpec((N_GATES, 1), lambda b, i: (0, 0)),
            pl.BlockSpec(conv_w.shape, lambda b, i: (0, 0)),
        ],
        out_specs=[gt_spec if n == "gt" else row_spec for n in names],
        out_shape=[jax.ShapeDtypeStruct((B, N_GATES, T), f32) if n == "gt"
                   else jax.ShapeDtypeStruct((B, T, D_A), bf16) for n in names],
        scratch_shapes=[pltpu.VMEM((3, tm, 512), f32)],
        compiler_params=pltpu.CompilerParams(dimension_semantics=("parallel", "arbitrary"),
                                             vmem_limit_bytes=_vmem_limit(est)),
        name="inproj_even_" + part,
    )(x, x, x, shift, scale, norm_g.reshape(1, D), w_main, wgt, gbt, conv_w)
    return outs


GATE_SLOTS = 2 * N_HEADS_A
GATE_PIECES = 3
GATE_QUANTS = 3
assert GATE_SLOTS & (GATE_SLOTS - 1) == 0 and GATE_QUANTS * GATE_PIECES * GATE_SLOTS <= V7X_LANES


def _scan_max_lanes(x, seg, reverse):
    n = x.shape[1]
    pos = lax.broadcasted_iota(jnp.int32, x.shape, 1) & (seg - 1)
    s = 1
    while s < seg:
        if reverse:
            x = jnp.where(pos < seg - s, jnp.maximum(x, pltpu.roll(x, n - s, 1)), x)
        else:
            x = jnp.where(pos >= s, jnp.maximum(x, pltpu.roll(x, s, 1)), x)
        s *= 2
    return x


def _split3(v):
    hi = v.astype(bf16)
    r1 = v - hi.astype(f32)
    mid = r1.astype(bf16)
    lo = (r1 - mid.astype(f32)).astype(bf16)
    return hi, mid, lo


def _gate_prep_kernel(gt_ref, cols_ref, arow_ref, *, L, nchunk):
    r = lax.broadcasted_iota(jnp.int32, (L, L), 0)
    c = lax.broadcasted_iota(jnp.int32, (L, L), 1)
    tri_l = (c <= r).astype(bf16)
    tri_u = (c >= r).astype(bf16)
    S = V7X_SUBLANES
    tb = nchunk * L
    fwd = (lax.broadcasted_iota(jnp.int32, (S, tb), 0) & 1) == 0
    ig = gt_ref[:S, :]
    lf = _log_sigmoid(gt_ref[S:, :])
    pieces = jnp.concatenate([p.astype(f32) for p in _split3(lf)] + [jnp.zeros((S, tb), f32)],
                             axis=0).astype(bf16)
    chunks = [slice(ci * L, (ci + 1) * L) for ci in range(nchunk)]
    pre = jnp.concatenate([jnp.dot(pieces[:, sl], tri_u, preferred_element_type=f32) for sl in chunks], axis=1)
    suf = jnp.concatenate([jnp.dot(pieces[:, sl], tri_l, preferred_element_type=f32) for sl in chunks], axis=1)
    pre = pre[:S] + pre[S:2 * S] + pre[2 * S:3 * S]
    suf = suf[:S] + suf[S:2 * S] + suf[2 * S:3 * S]
    b_row = jnp.where(fwd, pre, suf)
    a_row = ig - b_row
    cmax = jnp.where(fwd, _scan_max_lanes(a_row, L, False), _scan_max_lanes(a_row, L, True))
    arow_ref[...] = a_row
    parts = [p.astype(f32) for quant in (b_row, a_row, cmax) for p in _split3(quant)]
    fill = jnp.zeros((V7X_LANES - len(parts) * S, tb), f32)
    packed = jnp.concatenate(parts + [fill], axis=0)
    for sl in chunks:
        cols_ref[sl, :] = packed[:, sl].T.astype(bf16)


def _gate_prep(gt, L):
    B, _, T = gt.shape
    nchunk = min(4, T // L)
    tb = nchunk * L
    cols, arow = pl.pallas_call(
        functools.partial(_gate_prep_kernel, L=L, nchunk=nchunk),
        grid=(B, T // tb),
        in_specs=[pl.BlockSpec((None, N_GATES, tb), lambda b, i: (b, 0, i))],
        out_specs=[
            pl.BlockSpec((None, tb, V7X_LANES), lambda b, i: (b, i, 0)),
            pl.BlockSpec((None, V7X_SUBLANES, tb), lambda b, i: (b, 0, i)),
        ],
        out_shape=[
            jax.ShapeDtypeStruct((B, T, V7X_LANES), bf16),
            jax.ShapeDtypeStruct((B, V7X_SUBLANES, T), f32),
        ],
        compiler_params=pltpu.CompilerParams(dimension_semantics=("parallel", "parallel")),
        name="gate_prep",
    )(gt)
    return cols, arow.reshape(B, N_HEADS_A, 2, T)


def _mlstm_kernel(qx_ref, kx_ref, vx_ref, cx_ref, rx_ref, qc_ref, kc_ref, vc_ref, cc_ref, rc_ref,
                  hx_ref, hc_ref, c_ref, n_ref, *, L, nx, nc):
    Dh, LN, HP = HEAD_DIM_A, V7X_LANES, MLSTM_HEADS_PER_STEP
    head0 = pl.program_id(1) * HP
    chains = [(hh, d) for hh in range(HP) for d in range(2)]
    r_i = lax.broadcasted_iota(jnp.int32, (L, L), 0)
    c_i = lax.broadcasted_iota(jnp.int32, (L, L), 1)
    masks = (c_i <= r_i, c_i >= r_i)

    sr = lax.broadcasted_iota(jnp.int32, (LN, GATE_QUANTS * LN), 0)
    sc = lax.broadcasted_iota(jnp.int32, (LN, GATE_QUANTS * LN), 1)
    span = GATE_PIECES * GATE_SLOTS
    in_block = None
    for qi in range(GATE_QUANTS):
        blk = (sr >= qi * span) & (sr < (qi + 1) * span) & (sc >= qi * LN) & (sc < (qi + 1) * LN)
        in_block = blk if in_block is None else in_block | blk
    slot = sr & (GATE_SLOTS - 1)
    sels = {(hh, d): (in_block & (slot == 2 * (head0 + hh) + d)).astype(bf16) for hh, d in chains}

    def tile(v, width):
        return jnp.concatenate([v] * (width // LN), axis=1)

    def step(refs, r0s, ms):
        q_ref, k_ref, v_ref, col_ref, row_ref = refs
        st = []
        for ci, (hh, d) in enumerate(chains):
            rows = pl.ds(r0s[ci], L)
            hsl = slice(hh * Dh, (hh + 1) * Dh)
            q = q_ref[rows, hsl]
            k = k_ref[rows, hsl]
            rep = jnp.dot(col_ref[rows, :], sels[hh, d], preferred_element_type=f32)
            qk = lax.dot_general(q, k, (((1,), (1,)), ((), ())), preferred_element_type=f32)
            qc = jnp.dot(q, c_ref[ci].astype(bf16), preferred_element_type=f32)
            st.append(dict(q=q, k=k, v=v_ref[rows, hsl], rep=rep, qk=qk, qc=qc, a_row=row_ref[hh, d:d + 1, rows]))
        new_ms = []
        for ci, (hh, d) in enumerate(chains):
            c, m_prev = st[ci], ms[ci]
            b_rep, a_rep = c["rep"][:, :LN], c["rep"][:, LN:2 * LN]
            g_rep = jnp.maximum(c["rep"][:, 2 * LN:], m_prev)
            p = jnp.exp(jnp.where(masks[d], c["a_row"] - tile(g_rep, L), NEG_BIG))
            s = c["qk"] * p
            c["s"] = s.astype(bf16)
            inter = jnp.exp(m_prev - g_rep)
            qn = jnp.sum(c["q"].astype(f32) * n_ref[ci], axis=1, keepdims=True)
            den = jnp.sum(s, axis=1, keepdims=True) + inter[:, :1] * qn
            floor = jnp.exp(-(b_rep + g_rep))
            rcp = 1.0 / jnp.maximum(jnp.abs(den), floor[:, :1])
            c["rcp"] = jnp.broadcast_to(rcp, (L, LN))
            c["inter"] = inter
            b_end = b_rep[L - 1:L, :] if d == 0 else b_rep[0:1, :]
            w = b_end + a_rep
            m_new = jnp.maximum(b_end + m_prev, jnp.max(w, axis=0, keepdims=True))
            c["decay"] = jnp.exp(b_end + m_prev - m_new)
            kw = c["k"].astype(f32) * tile(jnp.exp(w - m_new), Dh)
            c["kw"] = kw.astype(bf16)
            c["ksum"] = jnp.sum(kw, axis=0, keepdims=True)
            new_ms.append(m_new)
        for c in st:
            c["sv"] = jnp.dot(c["s"], c["v"], preferred_element_type=f32)
            c["upd"] = lax.dot_general(c["kw"], c["v"], (((0,), (0,)), ((), ())), preferred_element_type=f32)
        hs = []
        for ci, c in enumerate(st):
            hs.append((c["sv"] + tile(c["inter"], Dh) * c["qc"]) * tile(c["rcp"], Dh))
            decay = tile(c["decay"], Dh)
            c_ref[ci] = decay * c_ref[ci] + c["upd"]
            n_ref[ci] = decay * n_ref[ci] + c["ksum"]
        return hs, new_ms

    ctx = (qc_ref, kc_ref, vc_ref, cc_ref, rc_ref)
    lat = (qx_ref, kx_ref, vx_ref, cx_ref, rx_ref)

    c_ref[...] = jnp.zeros_like(c_ref)
    n_ref[...] = jnp.zeros_like(n_ref)
    ms = [jnp.zeros((1, LN), f32) for _ in chains]
    written = set()
    for j in range(nc):
        cjs = [j if d == 0 else nc - 1 - j for _, d in chains]
        hs, ms = step(ctx, [cj * L for cj in cjs], ms)
        for (hh, d), cj, h in zip(chains, cjs, hs):
            dst = (slice(cj * L, (cj + 1) * L), slice(hh * Dh, (hh + 1) * Dh))
            if (cj, hh) in written:
                hc_ref[dst] = (hc_ref[dst].astype(f32) + h).astype(hc_ref.dtype)
            else:
                hc_ref[dst] = h.astype(hc_ref.dtype)
                written.add((cj, hh))

    def make_body(accumulate):
        def body(i, ms):
            r0s = [pl.multiple_of((i if d == 0 else nx - 1 - i) * L, L) for _, d in chains]
            hs, ms = step(lat, r0s, list(ms))
            for (hh, d), r0, h in zip(chains, r0s, hs):
                dst = (pl.ds(r0, L), slice(hh * Dh, (hh + 1) * Dh))
                if accumulate:
                    hx_ref[dst] = (hx_ref[dst].astype(f32) + h).astype(hx_ref.dtype)
                else:
                    hx_ref[dst] = h.astype(hx_ref.dtype)
            return tuple(ms)
        return body

    ms = lax.fori_loop(0, nx // 2, make_body(False), tuple(ms))
    lax.fori_loop(nx // 2, nx, make_body(True), ms)


def _mlstm(qx, kx, vx, colsx, rowsx, qc, kc, vc, colsc, rowsc, L):
    B, T, _ = qx.shape
    Tc = qc.shape[1]
    H, Dh, HP = N_HEADS_A, HEAD_DIM_A, MLSTM_HEADS_PER_STEP
    assert T % (2 * L) == 0 and Tc % L == 0 and H % HP == 0

    def seq_spec(t):
        return pl.BlockSpec((None, t, HP * Dh), lambda b, h: (b, 0, h))

    def col_spec(t):
        return pl.BlockSpec((None, t, V7X_LANES), lambda b, h: (b, 0, 0))

    def row_spec(t):
        return pl.BlockSpec((None, HP, 2, t), lambda b, h: (b, h, 0, 0))

    est = (2 * HP * (3 * (T + Tc) * Dh * 2 + 8 * (T + Tc) * 4 + (T + Tc) * Dh * 2)
           + 2 * (T + Tc) * V7X_LANES * 2 + 2 * HP * Dh * (Dh + V7X_LANES) * 4 + 16 * L * L * 4)
    return pl.pallas_call(
        functools.partial(_mlstm_kernel, L=L, nx=T // L, nc=Tc // L),
        grid=(B, H // HP),
        in_specs=[seq_spec(T), seq_spec(T), seq_spec(T), col_spec(T), row_spec(T),
                  seq_spec(Tc), seq_spec(Tc), seq_spec(Tc), col_spec(Tc), row_spec(Tc)],
        out_specs=[seq_spec(T), seq_spec(Tc)],
        out_shape=[jax.ShapeDtypeStruct((B, T, D_A), bf16), jax.ShapeDtypeStruct((B, Tc, D_A), bf16)],
        scratch_shapes=[pltpu.VMEM((2 * HP, Dh, Dh), f32), pltpu.VMEM((2 * HP, 1, Dh), f32)],
        compiler_params=pltpu.CompilerParams(dimension_semantics=("parallel", "parallel"),
                                             vmem_limit_bytes=_vmem_limit(est)),
        name="mlstm",
    )(qx, kx, vx, colsx, rowsx, qc, kc, vc, colsc, rowsc)


def _fourier_w_kernel(cs_ref, fw_ref, o_ref, *, scale):
    o_ref[...] = (jnp.dot(cs_ref[...], fw_ref[...], preferred_element_type=f32,
                          precision=lax.Precision.HIGHEST) * scale).astype(bf16)


def _fourier_weights(fw, T):
    n = GROUP_B
    kk = np.outer(np.arange(n), np.arange(n)) % n
    ang = 2.0 * np.pi * kk / n
    cs = jnp.asarray(np.concatenate([np.cos(ang), np.sin(ang)], axis=0), f32)
    return pl.pallas_call(
        functools.partial(_fourier_w_kernel, scale=float(1.0 / np.sqrt(T * n))),
        grid=(N_GROUPS_B,),
        in_specs=[pl.BlockSpec((2 * n, n), lambda g: (0, 0)),
                  pl.BlockSpec((None, n, n), lambda g: (g, 0, 0))],
        out_specs=pl.BlockSpec((None, 2 * n, n), lambda g: (g, 0, 0)),
        out_shape=jax.ShapeDtypeStruct((N_GROUPS_B, 2 * n, n), bf16),
        compiler_params=pltpu.CompilerParams(dimension_semantics=("arbitrary",)),
        name="fourier_weights",
    )(cs, fw)


def _fft_tables():
    n1, n2 = FFT_N1, FFT_N2
    n = n1 * n2
    t1 = np.arange(n1)
    k1 = np.arange(n1)
    t2 = np.arange(n2)
    idx = (k1[None, :, None] * (n2 * t1[None, None, :] + t2[:, None, None])) % n
    ang = 2.0 * np.pi * idx / n
    m1 = np.concatenate([np.cos(ang), -np.sin(ang)], axis=1)
    k2 = np.arange(n2)
    ph = 2.0 * np.pi * (np.outer(k2, t2) % n2) / n2
    c, s = np.cos(ph), np.sin(ph)
    m2 = np.block([[c, s], [-s, c]])
    return jnp.asarray(m1, f32).astype(bf16), jnp.asarray(m2, f32).astype(bf16)


def _fourier_kernel(u_ref, z_ref, m1_ref, m2_ref, w_ref, o_ref, up_ref, yp_ref, xp_ref, pbuf_ref):
    n1, n2, P = FFT_N1, FFT_N2, FFT_PITCH
    nl = GROUP_B // V7X_LANES

    def fill(t1, carry):
        src = pl.multiple_of(t1 * n2, n2)
        dst = pl.multiple_of(t1 * P, V7X_SUBLANES)
        blk = u_ref[pl.ds(src, n2), :].astype(f32)
        for s in range(nl):
            up_ref[s, pl.ds(dst, n2), :] = blk[:, s * V7X_LANES:(s + 1) * V7X_LANES]
        return carry

    lax.fori_loop(0, n1, fill, 0, unroll=4)

    G = FFT_UNROLL

    def stage1(grp, carry):
        t2s = [grp * G + j for j in range(G)]
        rhs = [jnp.concatenate([up_ref[s, pl.ds(t2, n1, stride=P), :] for s in range(nl)],
                               axis=1).astype(bf16) for t2 in t2s]
        ys = [jnp.dot(m1_ref[t2], r, preferred_element_type=f32) for t2, r in zip(t2s, rhs)]
        for t2, y in zip(t2s, ys):
            dst = pl.multiple_of(t2 * P, V7X_SUBLANES)
            for ri in range(2):
                for s in range(nl):
                    yp_ref[ri * nl + s, pl.ds(dst, n1), :] = y[ri * n1:(ri + 1) * n1,
                                                               s * V7X_LANES:(s + 1) * V7X_LANES]
        return carry

    lax.fori_loop(0, n2 // G, stage1, 0)

    m2 = m2_ref[...]

    def stage2(grp, carry):
        k1s = [grp * G + j for j in range(G)]
        rhs = []
        for k1 in k1s:
            parts = [jnp.concatenate([yp_ref[ri * nl + s, pl.ds(k1, n2, stride=P), :] for s in range(nl)], axis=1)
                     for ri in range(2)]
            rhs.append(jnp.concatenate(parts, axis=0).astype(bf16))
        xs = [jnp.dot(m2, r, preferred_element_type=f32) for r in rhs]
        for k1, x in zip(k1s, xs):
            for ri in range(2):
                for s in range(nl):
                    xp_ref[ri * nl + s, pl.ds(k1, n2, stride=P), :] = x[ri * n2:(ri + 1) * n2,
                                                                        s * V7X_LANES:(s + 1) * V7X_LANES]
        return carry

    lax.fori_loop(0, n1 // G, stage2, 0)

    tr = pbuf_ref.shape[1]

    def matmul(t):
        blocks = []
        for k2 in range(t * (tr // n1), (t + 1) * (tr // n1)):
            blocks.append(jnp.concatenate([xp_ref[sl, k2 * P:k2 * P + n1, :] for sl in range(2 * nl)],
                                          axis=1).astype(bf16))
        return jnp.dot(jnp.concatenate(blocks, axis=0), w_ref[...], preferred_element_type=f32)

    def gate_store(t, y):
        sl = slice(t * tr, (t + 1) * tr)
        o_ref[sl, :] = y.astype(bf16) * _silu(z_ref[sl, :])

    _pipelined_matmuls((n1 * n2) // tr, matmul, gate_store, pbuf_ref)


def _fourier_latent(ub, zb, wcs):
    B, T, _ = ub.shape
    assert T == FFT_N1 * FFT_N2
    m1, m2 = _fft_tables()
    G, n = N_GROUPS_B, GROUP_B
    nl = n // V7X_LANES
    rows_p = FFT_N1 * FFT_PITCH
    blk = pl.BlockSpec((None, T, n), lambda b, g: (b, 0, g))
    est = (3 * 2 * T * n * 2 + 5 * nl * rows_p * V7X_LANES * 4 + T * 2 * n * 2 + 2 * m1.size * 2 + 8 * 512 * n * 4)
    return pl.pallas_call(
        _fourier_kernel,
        grid=(B, G),
        in_specs=[blk, blk,
                  pl.BlockSpec(m1.shape, lambda b, g: (0, 0, 0)),
                  pl.BlockSpec(m2.shape, lambda b, g: (0, 0)),
                  pl.BlockSpec((None, 2 * n, n), lambda b, g: (g, 0, 0))],
        out_specs=blk,
        out_shape=jax.ShapeDtypeStruct((B, T, D_B), bf16),
        scratch_shapes=[pltpu.VMEM((nl, rows_p, V7X_LANES), f32),
                        pltpu.VMEM((2 * nl, rows_p, V7X_LANES), f32),
                        pltpu.VMEM((2 * nl, rows_p, V7X_LANES), f32),
                        pltpu.VMEM((3, 512, n), f32)],
        compiler_params=pltpu.CompilerParams(dimension_semantics=("parallel", "parallel"),
                                             vmem_limit_bytes=_vmem_limit(est)),
        name="fourier_latent",
    )(ub, zb, m1, m2, wcs)


def _fourier_ctx_kernel(u_ref, z_ref, cs_ref, w_ref, o_ref):
    u = u_ref[...]
    x = jnp.dot(cs_ref[...], u, preferred_element_type=f32).astype(bf16)
    t = u.shape[0]
    w = w_ref[...]
    n = GROUP_B
    y = (jnp.dot(x[:t], w[:n], preferred_element_type=f32) + jnp.dot(x[t:], w[n:], preferred_element_type=f32))
    o_ref[...] = (y * _silu(z_ref[...].astype(f32))).astype(bf16)


def _fourier_ctx(ub, zb, wcs):
    B, T, _ = ub.shape
    ang = 2.0 * np.pi * (np.outer(np.arange(T), np.arange(T)) % T) / T
    cs = jnp.asarray(np.concatenate([np.cos(ang), -np.sin(ang)], axis=0), f32).astype(bf16)
    n = GROUP_B
    blk = pl.BlockSpec((None, T, n), lambda b, g: (b, 0, g))
    return pl.pallas_call(
        _fourier_ctx_kernel,
        grid=(B, N_GROUPS_B),
        in_specs=[blk, blk, pl.BlockSpec(cs.shape, lambda b, g: (0, 0)),
                  pl.BlockSpec((None, 2 * n, n), lambda b, g: (g, 0, 0))],
        out_specs=blk,
        out_shape=jax.ShapeDtypeStruct((B, T, D_B), bf16),
        compiler_params=pltpu.CompilerParams(dimension_semantics=("parallel", "parallel")),
        name="fourier_ctx",
    )(ub, zb, cs, wcs)


def _outproj_even_kernel(x_ref, h_ref, o_ref, za_ref, yb_ref, gate_ref, hg_ref, w_ref, out_ref):
    parts = []
    for hd in range(N_HEADS_A):
        sl = slice(hd * HEAD_DIM_A, (hd + 1) * HEAD_DIM_A)
        hh = _sigmoid(o_ref[:, sl]).astype(f32) * h_ref[:, sl]
        hh = hh * lax.rsqrt(jnp.mean(hh * hh, axis=-1, keepdims=True) + EPS)
        parts.append(hh.astype(bf16) * (hg_ref[:, sl].astype(bf16) * _silu(za_ref[:, sl])))
    ya = jnp.concatenate(parts, axis=1)
    acc = jnp.dot(ya, w_ref[:D_A, :], preferred_element_type=f32)
    acc = acc + jnp.dot(yb_ref[...], w_ref[D_A:, :], preferred_element_type=f32)
    out_ref[...] = x_ref[...] + gate_ref[...] * acc


def _outproj_even(x, h, o, za, yb, gate, head_g, wout, tm):
    B, T, D = x.shape
    row = lambda n: pl.BlockSpec((None, tm, n), lambda b, i: (b, i, 0))
    est = 2 * wout.size * 2 + 2 * tm * (2 * D * 4 + D_A * 4 + 3 * D_A * 2) + 8 * tm * D * 4
    return pl.pallas_call(
        _outproj_even_kernel,
        grid=(B, T // tm),
        in_specs=[row(D), row(D_A), row(D_A), row(D_A), row(D_B),
                  pl.BlockSpec((None, 1, D), lambda b, i: (b, 0, 0)),
                  pl.BlockSpec((1, D_A), lambda b, i: (0, 0)),
                  pl.BlockSpec(wout.shape, lambda b, i: (0, 0))],
        out_specs=row(D),
        out_shape=jax.ShapeDtypeStruct((B, T, D), f32),
        compiler_params=pltpu.CompilerParams(dimension_semantics=("parallel", "parallel"),
                                             vmem_limit_bytes=_vmem_limit(est)),
        name="outproj_even",
    )(x, h, o, za, yb, gate, head_g.reshape(1, D_A), wout)


def _inproj_odd_kernel(x_ref, sh_ref, sc_ref, g_ref, w_ref, u_ref, z_ref):
    amp = g_ref[...] * (1.0 + sc_ref[...])
    hx = _normed(x_ref[...], amp, sh_ref[...]).astype(bf16)
    cn = 512
    for j in range(D_INNER // cn):
        u_ref[:, j * cn:(j + 1) * cn] = jnp.dot(hx, w_ref[:, j * cn:(j + 1) * cn],
                                                preferred_element_type=f32).astype(bf16)
        z_ref[:, j * cn:(j + 1) * cn] = jnp.dot(hx, w_ref[:, D_INNER + j * cn:D_INNER + (j + 1) * cn],
                                                preferred_element_type=f32).astype(bf16)


def _even_out_odd_in_kernel(x_ref, h_ref, o_ref, za_ref, yb_ref, gate_ref, hg_ref, wo_ref,
                            sh_ref, sc_ref, g_ref, wi_ref, x1_ref, u_ref, z_ref):
    _outproj_even_kernel(x_ref, h_ref, o_ref, za_ref, yb_ref, gate_ref, hg_ref, wo_ref, x1_ref)
    _inproj_odd_kernel(x1_ref, sh_ref, sc_ref, g_ref, wi_ref, u_ref, z_ref)


def _even_out_odd_in(x, h, o, za, yb, gate, head_g, wout, shift, scale, norm_g, win, tm):
    B, T, D = x.shape
    row = lambda n: pl.BlockSpec((None, tm, n), lambda b, i: (b, i, 0))
    bvec = pl.BlockSpec((None, 1, D), lambda b, i: (b, 0, 0))
    est = (2 * (wout.size + win.size) * 2 + 2 * tm * (2 * D * 4 + 4 * D_A * 2 + 2 * D_INNER * 2)
           + 8 * tm * D * 4 + 6 * tm * 512 * 4)
    return pl.pallas_call(
        _even_out_odd_in_kernel,
        grid=(B, T // tm),
        in_specs=[row(D), row(D_A), row(D_A), row(D_A), row(D_B), bvec,
                  pl.BlockSpec((1, D_A), lambda b, i: (0, 0)),
                  pl.BlockSpec(wout.shape, lambda b, i: (0, 0)),
                  bvec, bvec,
                  pl.BlockSpec((1, D), lambda b, i: (0, 0)),
                  pl.BlockSpec(win.shape, lambda b, i: (0, 0))],
        out_specs=[row(D), row(D_INNER), row(D_INNER)],
        out_shape=[jax.ShapeDtypeStruct((B, T, D), f32)] + [jax.ShapeDtypeStruct((B, T, D_INNER), bf16)] * 2,
        compiler_params=pltpu.CompilerParams(dimension_semantics=("parallel", "parallel"),
                                             vmem_limit_bytes=_vmem_limit(est)),
        name="even_out_odd_in",
    )(x, h, o, za, yb, gate, head_g.reshape(1, D_A), wout, shift, scale, norm_g.reshape(1, D), win)


POOL_UNROLL = 32


def _pool_tables():
    w_idx = np.arange(GRID_W)
    band = np.zeros((N_GROUPS_C, GRID_W, GRID_W), np.float32)
    inv_w = np.zeros((N_GROUPS_C, GRID_W, V7X_LANES), np.float32)
    for g, win in enumerate(POOL_WINDOWS):
        lo = np.clip(w_idx - win // 2, 0, GRID_W)
        hi = np.clip(w_idx + win - win // 2, 0, GRID_W)
        band[g] = (w_idx[None, :] >= lo[:, None]) & (w_idx[None, :] < hi[:, None])
        inv_w[g] = (1.0 / (hi - lo))[:, None]
    return jnp.asarray(band, bf16), jnp.asarray(inv_w, f32)


def _pool_kernel(u_ref, z_ref, band_ref, invw_ref, pw_ref, sc_ref, o_ref, ps_ref, pbuf_ref, *, rows):
    g = pl.program_id(1)
    W = GRID_W
    band = band_ref[...]
    lo_off = hi_off = 0
    for gi, win in enumerate(POOL_WINDOWS):
        lo_off = jnp.where(g == gi, win // 2, lo_off)
        hi_off = jnp.where(g == gi, win - win // 2, hi_off)

    ps_ref[0:W, :] = jnp.zeros((W, GROUP_C), f32)

    def width_sum(grp, carry):
        srcs = [pl.multiple_of((grp * POOL_UNROLL + j) * W, W) for j in range(POOL_UNROLL)]
        sums = [jnp.dot(band, u_ref[pl.ds(src, W), :], preferred_element_type=f32) for src in srcs]
        acc = ps_ref[pl.ds(srcs[0], W), :]
        for src, s in zip(srcs, sums):
            acc = acc + s
            ps_ref[pl.ds(src + W, W), :] = acc
        return carry

    lax.fori_loop(0, rows // POOL_UNROLL, width_sum, 0)

    inv_w = invw_ref[...]

    def pooled_minus_self(r):
        lo = jnp.maximum(r - lo_off, 0)
        hi = jnp.minimum(r + hi_off, rows)
        acc = (ps_ref[pl.ds(pl.multiple_of(hi * W, W), W), :]
               - ps_ref[pl.ds(pl.multiple_of(lo * W, W), W), :])
        inv = inv_w / (hi - lo).astype(f32)
        inv = jnp.concatenate([inv] * (GROUP_C // V7X_LANES), axis=1)
        ug = u_ref[r * W:(r + 1) * W, :].astype(f32)
        return (acc * inv - ug).astype(bf16)

    tr = pbuf_ref.shape[1]
    pws = (pw_ref[...] * sc_ref[...]).astype(bf16)

    def matmul(t):
        lhs = jnp.concatenate([pooled_minus_self(t * (tr // W) + j) for j in range(tr // W)], axis=0)
        return jnp.dot(lhs, pws, preferred_element_type=f32)

    def gate_store(t, y):
        sl = slice(t * tr, (t + 1) * tr)
        o_ref[sl, :] = y.astype(bf16) * _silu(z_ref[sl, :])

    _pipelined_matmuls((rows * W) // tr, matmul, gate_store, pbuf_ref)


def _pool_mix(u, z, pool_w, scale):
    B, T, _ = u.shape
    rows = T // GRID_W
    band, inv_w = _pool_tables()
    n = GROUP_C
    blk = pl.BlockSpec((None, T, n), lambda b, g: (b, 0, g))
    est = 3 * 2 * T * n * 2 + (rows + 1) * GRID_W * n * 4 + T * n * 2 + 2 * n * n * 4 + 8 * 512 * n * 4
    return pl.pallas_call(
        functools.partial(_pool_kernel, rows=rows),
        grid=(B, N_GROUPS_C),
        in_specs=[blk, blk,
                  pl.BlockSpec((None, GRID_W, GRID_W), lambda b, g: (g, 0, 0)),
                  pl.BlockSpec((None, GRID_W, V7X_LANES), lambda b, g: (g, 0, 0)),
                  pl.BlockSpec((None, n, n), lambda b, g: (g, 0, 0)),
                  pl.BlockSpec((None, 1, n), lambda b, g: (g, 0, 0))],
        out_specs=blk,
        out_shape=jax.ShapeDtypeStruct((B, T, D_INNER), bf16),
        scratch_shapes=[pltpu.VMEM(((rows + 1) * GRID_W, n), f32),
                        pltpu.VMEM((3, 512, n), f32)],
        compiler_params=pltpu.CompilerParams(dimension_semantics=("parallel", "arbitrary"),
                                             vmem_limit_bytes=_vmem_limit(est)),
        name="pool_mix",
    )(u, z, band, inv_w, pool_w, scale.reshape(N_GROUPS_C, 1, n))


def _outproj_odd_kernel(x_ref, y_ref, gate_ref, fg_ref, w_ref, out_ref):
    acc = jnp.dot(y_ref[...], w_ref[...], preferred_element_type=f32)
    x = x_ref[...] + gate_ref[...] * acc
    out_ref[...] = x * lax.rsqrt(jnp.mean(x * x, axis=-1, keepdims=True) + EPS) * fg_ref[...]


def _outproj_odd(x, y, gate, final_g, wout, tm):
    B, T, D = x.shape
    row = lambda n: pl.BlockSpec((None, tm, n), lambda b, i: (b, i, 0))
    est = 2 * wout.size * 2 + 2 * tm * (2 * D * 4 + D_INNER * 2) + 6 * tm * D * 4
    return pl.pallas_call(
        _outproj_odd_kernel,
        grid=(B, T // tm),
        in_specs=[row(D), row(D_INNER),
                  pl.BlockSpec((None, 1, D), lambda b, i: (b, 0, 0)),
                  pl.BlockSpec((1, D), lambda b, i: (0, 0)),
                  pl.BlockSpec(wout.shape, lambda b, i: (0, 0))],
        out_specs=row(D),
        out_shape=jax.ShapeDtypeStruct((B, T, D), f32),
        compiler_params=pltpu.CompilerParams(dimension_semantics=("parallel", "parallel"),
                                             vmem_limit_bytes=_vmem_limit(est)),
        name="outproj_odd",
    )(x, y, gate, final_g.reshape(1, D), wout)


def kernel(x, c, ctx, c_ctx, ada_w, ada_b, norm_g, win_even, gate_b_even, conv_qk_even, head_norm_even,
           fourier_w_even, wout_even, win_odd, pool_w_odd, pool_scale_odd, wout_odd, final_g):
    B, T, D = x.shape
    Tc = ctx.shape[1]
    H = N_HEADS_A
    L = MLSTM_CHUNK

    nrow = -(-(B + 1) // V7X_SUBLANES) * V7X_SUBLANES
    rows_in = jnp.zeros((nrow, D), f32).at[:B].set(c).at[B].set(c_ctx)
    mod = _modulation(rows_in, ada_w, ada_b)

    def mod_parts(l, r0, r1, n):
        m = mod[l, r0:r1]
        parts = [jnp.broadcast_to(m[:, None, i * D:(i + 1) * D], (n, 1, D)) for i in range(3)]
        return parts

    we = win_even[0]
    w_main = we.astype(bf16)
    gcols = we[:, W_MAIN_EVEN:].reshape(D, 4, H)
    gbias = gate_b_even[0].reshape(4, H)
    ig_w = gcols[:, 0::2, :].transpose(0, 2, 1).reshape(D, GATE_SLOTS)
    fg_w = gcols[:, 1::2, :].transpose(0, 2, 1).reshape(D, GATE_SLOTS)
    ig_b = gbias[0::2, :].T.reshape(GATE_SLOTS)
    fg_b = gbias[1::2, :].T.reshape(GATE_SLOTS)
    wgt =jnp.concatenate([ig_w, fg_w], axis=1).T.astype(bf16)
    gbt = jnp.concatenate([ig_b, fg_b]).reshape(N_GATES, 1)
    conv_w = conv_qk_even[0]

    shift_x, scale_x, gate_x = mod_parts(0, 0, B, B)
    shift_c, scale_c, gate_c = mod_parts(0, B, B + 1, B)

    qx, kx, vx, ox, zax, ubx, zbx, gtx = _inproj_even(
        x, shift_x, scale_x, norm_g[0], w_main, wgt, gbt, conv_w, tm=512)
    qc, kc, vc, gtc = _inproj_even(ctx, shift_c, scale_c, norm_g[0], w_main, wgt, gbt, conv_w, tm=Tc, part="scan")
    oc, zac, ubc, zbc = _inproj_even(ctx, shift_c, scale_c, norm_g[0], w_main, wgt, gbt, conv_w, tm=Tc, part="mix")

    colsx, rowsx = _gate_prep(gtx, L)
    colsc, rowsc = _gate_prep(gtc, L)
    h_x, h_c = _mlstm(qx, kx, vx, colsx, rowsx, qc, kc, vc, colsc, rowsc, L)

    wcs = _fourier_weights(fourier_w_even[0], T)
    yb_x = _fourier_latent(ubx, zbx, wcs)
    wout_e = wout_even[0].astype(bf16)
    shift_1, scale_1, gate_1 = mod_parts(1, 0, B, B)
    x1, u1, z1 = _even_out_odd_in(x, h_x, ox, zax, yb_x, gate_x, head_norm_even[0], wout_e,
                                  shift_1, scale_1, norm_g[1], win_odd[0].astype(bf16), tm=512)

    wcs_c = _fourier_weights(fourier_w_even[0], Tc)
    yb_c = _fourier_ctx(ubc, zbc, wcs_c)
    ctx1 = _outproj_even(ctx, h_c, oc, zac, yb_c, gate_c, head_norm_even[0], wout_e, tm=Tc)
    del ctx1

    y1 = _pool_mix(u1, z1, pool_w_odd[0], pool_scale_odd[0])
    return _outproj_odd(x1, y1, gate_1, final_g, wout_odd[0].astype(bf16), tm=1024)
```

```python
import functools

import numpy as np
import jax
import jax.numpy as jnp
from jax import lax
from jax.experimental import pallas as pl
from jax.experimental.pallas import tpu as pltpu

D_MODEL = 1024
DEPTH = 2
CTX_LEN = 256
GRID_W = 64
D_INNER = 2 * D_MODEL
D_A = D_INNER // 2
D_B = D_INNER - D_A
N_HEADS_A = 4
HEAD_DIM_A = D_A // N_HEADS_A
N_GROUPS_B = 4
GROUP_B = D_B // N_GROUPS_B
N_GROUPS_C = 4
GROUP_C = D_INNER // N_GROUPS_C
POOL_WINDOWS = (2, 4, 8, 16)
CONV_W = 3
N_GATES = 4 * N_HEADS_A
W_MAIN_EVEN = 5 * D_A + 2 * D_B
EPS = 1e-6

f32 = jnp.float32
bf16 = jnp.bfloat16

V7X_VMEM_BYTES = 64 * 1024 * 1024
V7X_LANES = 128
V7X_SUBLANES = 8

MLSTM_CHUNK = 256
MLSTM_HEADS_PER_STEP = 2
FFT_N1 = 64
FFT_N2 = 64
FFT_PITCH = 72
FFT_UNROLL = 64
NEG_BIG = -1e30


def _vmem_limit(nbytes):
    return int(min(max(nbytes * 5 // 4 + (4 << 20), 16 << 20), V7X_VMEM_BYTES - (6 << 20)))


def _sigmoid(v):
    return 0.5 * jnp.tanh(0.5 * v) + 0.5


def _silu(v):
    return v * _sigmoid(v)


def _pipelined_matmuls(n, matmul, epilogue, pbuf_ref):
    zero = jnp.minimum(pl.program_id(0), 0)
    nbuf = pbuf_ref.shape[0]
    pbuf_ref[zero] = matmul(0)
    for t in range(n):
        if t + 1 < n:
            pbuf_ref[zero + (t + 1) % nbuf] = matmul(t + 1)
        epilogue(t, pbuf_ref[zero + t % nbuf])


def _log_sigmoid(v):
    return jnp.minimum(v, 0.0) - jnp.log1p(jnp.exp(-jnp.abs(v)))


def _mod_kernel(r_ref, w_ref, b_ref, o_ref):
    s = _silu(r_ref[...])
    o_ref[...] = jnp.dot(s, w_ref[...], preferred_element_type=f32,
                         precision=lax.Precision.HIGHEST) + b_ref[...]


def _modulation(rows, ada_w, ada_b):
    nrow = rows.shape[0]
    tn = 1024
    return pl.pallas_call(
        _mod_kernel,
        grid=(DEPTH, 3 * D_MODEL // tn),
        in_specs=[
            pl.BlockSpec((nrow, D_MODEL), lambda l, j: (0, 0)),
            pl.BlockSpec((None, D_MODEL, tn), lambda l, j: (l, 0, j)),
            pl.BlockSpec((None, 1, tn), lambda l, j: (l, 0, j)),
        ],
        out_specs=pl.BlockSpec((None, nrow, tn), lambda l, j: (l, 0, j)),
        out_shape=jax.ShapeDtypeStruct((DEPTH, nrow, 3 * D_MODEL), f32),
        compiler_params=pltpu.CompilerParams(dimension_semantics=("arbitrary", "arbitrary")),
        name="modulation",
    )(rows, ada_w, ada_b.reshape(DEPTH, 1, 3 * D_MODEL))


def _normed(x, amp, shift):
    ms = jnp.mean(x * x, axis=-1, keepdims=True)
    return (x * lax.rsqrt(ms + EPS)) * amp + shift


INPROJ_EVEN_OUTPUTS = {
    "all": ("q", "k", "v", "o", "za", "ub", "zb", "gt"),
    "scan": ("q", "k", "v", "gt"),
    "mix": ("o", "za", "ub", "zb"),
}
INPROJ_EVEN_PLAIN = ("v", "o", "za", "ub", "zb")


def _inproj_even_kernel(x_ref, xp_ref, xn_ref, sh_ref, sc_ref, g_ref, w_ref, wgt_ref, gbt_ref, cw_ref, *refs,
                        tm, nt, part):
    out = dict(zip(INPROJ_EVEN_OUTPUTS[part], refs[:-1]))
    pbuf_ref = refs[-1]
    i = pl.program_id(1)
    amp = g_ref[...] * (1.0 + sc_ref[...])
    shift = sh_ref[...]
    hx = _normed(x_ref[...], amp, shift).astype(bf16)
    cn = 512
    tasks = []
    if "q" in out:
        halo = jnp.concatenate([xp_ref[...], xn_ref[...]], axis=0)
        hh = _normed(halo, amp, shift).astype(bf16)
        has_prev = (i > 0).astype(f32)
        has_next = (i < nt - 1).astype(f32)
        row = lax.broadcasted_iota(jnp.int32, (tm, 1), 0)
        ph = jnp.dot(hh, w_ref[:, :2 * D_A], preferred_element_type=f32)
        prev = ph[V7X_SUBLANES - 1:V7X_SUBLANES, :] * has_prev
        nxt = ph[V7X_SUBLANES:V7X_SUBLANES + 1, :] * has_next

        def conv_store(j, p):
            cols = slice(j * cn, (j + 1) * cn)
            up = jnp.where(row == 0, prev[:, cols], pltpu.roll(p, 1, 0))
            dn = jnp.where(row == tm - 1, nxt[:, cols], pltpu.roll(p, tm - 1, 0))
            cw = cw_ref[:, cols]
            y = _silu(cw[0:1, :] * up + cw[1:2, :] * p + cw[2:3, :] * dn)
            if j < D_A // cn:
                out["q"][:, cols] = (y * (HEAD_DIM_A ** -0.5)).astype(bf16)
            else:
                jj = j - D_A // cn
                out["k"][:, jj * cn:(jj + 1) * cn] = y.astype(bf16)

        tasks += [(j * cn, functools.partial(conv_store, j)) for j in range(2 * D_A // cn)]

    def plain_store(ref, jj):
        def store(p):
            ref[:, jj * cn:(jj + 1) * cn] = p.astype(bf16)
        return store

    for idx, name in enumerate(INPROJ_EVEN_PLAIN):
        if name in out:
            for jj in range(D_A // cn):
                tasks.append((2 * D_A + idx * D_A + jj * cn, plain_store(out[name], jj)))

    def matmul(t):
        c0 = tasks[t][0]
        return jnp.dot(hx, w_ref[:, c0:c0 + cn], preferred_element_type=f32)

    _pipelined_matmuls(len(tasks), matmul, lambda t, p: tasks[t][1](p), pbuf_ref)
    if "gt" in out:
        out["gt"][...] = lax.dot_general(wgt_ref[...], hx, (((1,), (1,)), ((), ())),
                                         preferred_element_type=f32) + gbt_ref[...]


def _inproj_even(x, shift, scale, norm_g, w_main, wgt, gbt, conv_w, tm, part="all"):
    B, T, D = x.shape
    nt = T // tm
    hb = tm // V7X_SUBLANES
    nhb = T // V7X_SUBLANES
    names = INPROJ_EVEN_OUTPUTS[part]
    row_spec = pl.BlockSpec((None, tm, D_A), lambda b, i: (b, i, 0))
    gt_spec = pl.BlockSpec((None, N_GATES, tm), lambda b, i: (b, 0, i))
    vec = lambda n: pl.BlockSpec((1, n), lambda b, i: (0, 0))
    est = (2 * w_main.size * 2 + 2 * tm * D * 4 + len(names) * 2 * tm * D_A * 2 + 6 * tm * 512 * 4)
    outs = pl.pallas_call(
        functools.partial(_inproj_even_kernel, tm=tm, nt=nt, part=part),
        grid=(B, nt),
        in_specs=[
            pl.BlockSpec((None, tm, D), lambda b, i: (b, i, 0)),
            pl.BlockSpec((None, V7X_SUBLANES, D), lambda b, i: (b, jnp.maximum(i * hb - 1, 0), 0)),
            pl.BlockSpec((None, V7X_SUBLANES, D), lambda b, i: (b, jnp.minimum((i + 1) * hb, nhb - 1), 0)),
            pl.BlockSpec((None, 1, D), lambda b, i: (b, 0, 0)),
            pl.BlockSpec((None, 1, D), lambda b, i: (b, 0, 0)),
            vec(D),
            pl.BlockSpec(w_main.shape, lambda b, i: (0, 0)),
            pl.BlockSpec(wgt.shape, lambda b, i: (0, 0)),
            pl.BlockSpec((N_GATES, 1), lambda b, i: (0, 0)),
            pl.BlockSpec(conv_w.shape, lambda b, i: (0, 0)),
        ],
        out_specs=[gt_spec if n == "gt" else row_spec for n in names],
        out_shape=[jax.ShapeDtypeStruct((B, N_GATES, T), f32) if n == "gt"
                   else jax.ShapeDtypeStruct((B, T, D_A), bf16) for n in names],
        scratch_shapes=[pltpu.VMEM((3, tm, 512), f32)],
        compiler_params=pltpu.CompilerParams(dimension_semantics=("parallel", "arbitrary"),
                                             vmem_limit_bytes=_vmem_limit(est)),
        name="inproj_even_" + part,
    )(x, x, x, shift, scale, norm_g.reshape(1, D), w_main, wgt, gbt, conv_w)
    return outs


GATE_SLOTS = 2 * N_HEADS_A
GATE_PIECES = 3
GATE_QUANTS = 3
assert GATE_SLOTS & (GATE_SLOTS - 1) == 0 and GATE_QUANTS * GATE_PIECES * GATE_SLOTS <= V7X_LANES


def _scan_max_lanes(x, seg, reverse):
    n = x.shape[1]
    pos = lax.broadcasted_iota(jnp.int32, x.shape, 1) & (seg - 1)
    s = 1
    while s < seg:
        if reverse:
            x = jnp.where(pos < seg - s, jnp.maximum(x, pltpu.roll(x, n - s, 1)), x)
        else:
            x = jnp.where(pos >= s, jnp.maximum(x, pltpu.roll(x, s, 1)), x)
        s *= 2
    return x


def _split3(v):
    hi = v.astype(bf16)
    r1 = v - hi.astype(f32)
    mid = r1.astype(bf16)
    lo = (r1 - mid.astype(f32)).astype(bf16)
    return hi, mid, lo


def _gate_prep_kernel(gt_ref, cols_ref, arow_ref, *, L, nchunk):
    r = lax.broadcasted_iota(jnp.int32, (L, L), 0)
    c = lax.broadcasted_iota(jnp.int32, (L, L), 1)
    tri_l = (c <= r).astype(bf16)
    tri_u = (c >= r).astype(bf16)
    S = V7X_SUBLANES
    tb = nchunk * L
    fwd = (lax.broadcasted_iota(jnp.int32, (S, tb), 0) & 1) == 0
    ig = gt_ref[:S, :]
    lf = _log_sigmoid(gt_ref[S:, :])
    pieces = jnp.concatenate([p.astype(f32) for p in _split3(lf)] + [jnp.zeros((S, tb), f32)],
                             axis=0).astype(bf16)
    chunks = [slice(ci * L, (ci + 1) * L) for ci in range(nchunk)]
    pre = jnp.concatenate([jnp.dot(pieces[:, sl], tri_u, preferred_element_type=f32) for sl in chunks], axis=1)
    suf = jnp.concatenate([jnp.dot(pieces[:, sl], tri_l, preferred_element_type=f32) for sl in chunks], axis=1)
    pre = pre[:S] + pre[S:2 * S] + pre[2 * S:3 * S]
    suf = suf[:S] + suf[S:2 * S] + suf[2 * S:3 * S]
    b_row = jnp.where(fwd, pre, suf)
    a_row = ig - b_row
    cmax = jnp.where(fwd, _scan_max_lanes(a_row, L, False), _scan_max_lanes(a_row, L, True))
    arow_ref[...] = a_row
    parts = [p.astype(f32) for quant in (b_row, a_row, cmax) for p in _split3(quant)]
    fill = jnp.zeros((V7X_LANES - len(parts) * S, tb), f32)
    packed = jnp.concatenate(parts + [fill], axis=0)
    for sl in chunks:
        cols_ref[sl, :] = packed[:, sl].T.astype(bf16)


def _gate_prep(gt, L):
    B, _, T = gt.shape
    nchunk = min(4, T // L)
    tb = nchunk * L
    cols, arow = pl.pallas_call(
        functools.partial(_gate_prep_kernel, L=L, nchunk=nchunk),
        grid=(B, T // tb),
        in_specs=[pl.BlockSpec((None, N_GATES, tb), lambda b, i: (b, 0, i))],
        out_specs=[
            pl.BlockSpec((None, tb, V7X_LANES), lambda b, i: (b, i, 0)),
            pl.BlockSpec((None, V7X_SUBLANES, tb), lambda b, i: (b, 0, i)),
        ],
        out_shape=[
            jax.ShapeDtypeStruct((B, T, V7X_LANES), bf16),
            jax.ShapeDtypeStruct((B, V7X_SUBLANES, T), f32),
        ],
        compiler_params=pltpu.CompilerParams(dimension_semantics=("parallel", "parallel")),
        name="gate_prep",
    )(gt)
    return cols, arow.reshape(B, N_HEADS_A, 2, T)


def _mlstm_kernel(qx_ref, kx_ref, vx_ref, cx_ref, rx_ref, qc_ref, kc_ref, vc_ref, cc_ref, rc_ref,
                  hx_ref, hc_ref, c_ref, n_ref, *, L, nx, nc):
    Dh, LN, HP = HEAD_DIM_A, V7X_LANES, MLSTM_HEADS_PER_STEP
    head0 = pl.program_id(1) * HP
    chains = [(hh, d) for hh in range(HP) for d in range(2)]
    r_i = lax.broadcasted_iota(jnp.int32, (L, L), 0)
    c_i = lax.broadcasted_iota(jnp.int32, (L, L), 1)
    masks = (c_i <= r_i, c_i >= r_i)

    sr = lax.broadcasted_iota(jnp.int32, (LN, GATE_QUANTS * LN), 0)
    sc = lax.broadcasted_iota(jnp.int32, (LN, GATE_QUANTS * LN), 1)
    span = GATE_PIECES * GATE_SLOTS
    in_block = None
    for qi in range(GATE_QUANTS):
        blk = (sr >= qi * span) & (sr < (qi + 1) * span) & (sc >= qi * LN) & (sc < (qi + 1) * LN)
        in_block = blk if in_block is None else in_block | blk
    slot = sr & (GATE_SLOTS - 1)
    sels = {(hh, d): (in_block & (slot == 2 * (head0 + hh) + d)).astype(bf16) for hh, d in chains}

    def tile(v, width):
        return jnp.concatenate([v] * (width // LN), axis=1)

    def step(refs, r0s, ms):
        q_ref, k_ref, v_ref, col_ref, row_ref = refs
        st = []
        for ci, (hh, d) in enumerate(chains):
            rows = pl.ds(r0s[ci], L)
            hsl = slice(hh * Dh, (hh + 1) * Dh)
            q = q_ref[rows, hsl]
            k = k_ref[rows, hsl]
            rep = jnp.dot(col_ref[rows, :], sels[hh, d], preferred_element_type=f32)
            qk = lax.dot_general(q, k, (((1,), (1,)), ((), ())), preferred_element_type=f32)
            qc = jnp.dot(q, c_ref[ci].astype(bf16), preferred_element_type=f32)
            st.append(dict(q=q, k=k, v=v_ref[rows, hsl], rep=rep, qk=qk, qc=qc, a_row=row_ref[hh, d:d + 1, rows]))
        new_ms = []
        for ci, (hh, d) in enumerate(chains):
            c, m_prev = st[ci], ms[ci]
            b_rep, a_rep = c["rep"][:, :LN], c["rep"][:, LN:2 * LN]
            g_rep = jnp.maximum(c["rep"][:, 2 * LN:], m_prev)
            p = jnp.exp(jnp.where(masks[d], c["a_row"] - tile(g_rep, L), NEG_BIG))
            s = c["qk"] * p
            c["s"] = s.astype(bf16)
            inter = jnp.exp(m_prev - g_rep)
            qn = jnp.sum(c["q"].astype(f32) * n_ref[ci], axis=1, keepdims=True)
            den = jnp.sum(s, axis=1, keepdims=True) + inter[:, :1] * qn
            floor = jnp.exp(-(b_rep + g_rep))
            rcp = 1.0 / jnp.maximum(jnp.abs(den), floor[:, :1])
            c["rcp"] = jnp.broadcast_to(rcp, (L, LN))
            c["inter"] = inter
            b_end = b_rep[L - 1:L, :] if d == 0 else b_rep[0:1, :]
            w = b_end + a_rep
            m_new = jnp.maximum(b_end + m_prev, jnp.max(w, axis=0, keepdims=True))
            c["decay"] = jnp.exp(b_end + m_prev - m_new)
            kw = c["k"].astype(f32) * tile(jnp.exp(w - m_new), Dh)
            c["kw"] = kw.astype(bf16)
            c["ksum"] = jnp.sum(kw, axis=0, keepdims=True)
            new_ms.append(m_new)
        for c in st:
            c["sv"] = jnp.dot(c["s"], c["v"], preferred_element_type=f32)
            c["upd"] = lax.dot_general(c["kw"], c["v"], (((0,), (0,)), ((), ())), preferred_element_type=f32)
        hs = []
        for ci, c in enumerate(st):
            hs.append((c["sv"] + tile(c["inter"], Dh) * c["qc"]) * tile(c["rcp"], Dh))
            decay = tile(c["decay"], Dh)
            c_ref[ci] = decay * c_ref[ci] + c["upd"]
            n_ref[ci] = decay * n_ref[ci] + c["ksum"]
        return hs, new_ms

    ctx = (qc_ref, kc_ref, vc_ref, cc_ref, rc_ref)
    lat = (qx_ref, kx_ref, vx_ref, cx_ref, rx_ref)

    c_ref[...] = jnp.zeros_like(c_ref)
    n_ref[...] = jnp.zeros_like(n_ref)
    ms = [jnp.zeros((1, LN), f32) for _ in chains]
    written = set()
    for j in range(nc):
        cjs = [j if d == 0 else nc - 1 - j for _, d in chains]
        hs, ms = step(ctx, [cj * L for cj in cjs], ms)
        for (hh, d), cj, h in zip(chains, cjs, hs):
            dst = (slice(cj * L, (cj + 1) * L), slice(hh * Dh, (hh + 1) * Dh))
            if (cj, hh) in written:
                hc_ref[dst] = (hc_ref[dst].astype(f32) + h).astype(hc_ref.dtype)
            else:
                hc_ref[dst] = h.astype(hc_ref.dtype)
                written.add((cj, hh))

    def make_body(accumulate):
        def body(i, ms):
            r0s = [pl.multiple_of((i if d == 0 else nx - 1 - i) * L, L) for _, d in chains]
            hs, ms = step(lat, r0s, list(ms))
            for (hh, d), r0, h in zip(chains, r0s, hs):
                dst = (pl.ds(r0, L), slice(hh * Dh, (hh + 1) * Dh))
                if accumulate:
                    hx_ref[dst] = (hx_ref[dst].astype(f32) + h).astype(hx_ref.dtype)
                else:
                    hx_ref[dst] = h.astype(hx_ref.dtype)
            return tuple(ms)
        return body

    ms = lax.fori_loop(0, nx // 2, make_body(False), tuple(ms))
    lax.fori_loop(nx // 2, nx, make_body(True), ms)


def _mlstm(qx, kx, vx, colsx, rowsx, qc, kc, vc, colsc, rowsc, L):
    B, T, _ = qx.shape
    Tc = qc.shape[1]
    H, Dh, HP = N_HEADS_A, HEAD_DIM_A, MLSTM_HEADS_PER_STEP
    assert T % (2 * L) == 0 and Tc % L == 0 and H % HP == 0

    def seq_spec(t):
        return pl.BlockSpec((None, t, HP * Dh), lambda b, h: (b, 0, h))

    def col_spec(t):
        return pl.BlockSpec((None, t, V7X_LANES), lambda b, h: (b, 0, 0))

    def row_spec(t):
        return pl.BlockSpec((None, HP, 2, t), lambda b, h: (b, h, 0, 0))

    est = (2 * HP * (3 * (T + Tc) * Dh * 2 + 8 * (T + Tc) * 4 + (T + Tc) * Dh * 2)
           + 2 * (T + Tc) * V7X_LANES * 2 + 2 * HP * Dh * (Dh + V7X_LANES) * 4 + 16 * L * L * 4)
    return pl.pallas_call(
        functools.partial(_mlstm_kernel, L=L, nx=T // L, nc=Tc // L),
        grid=(B, H // HP),
        in_specs=[seq_spec(T), seq_spec(T), seq_spec(T), col_spec(T), row_spec(T),
                  seq_spec(Tc), seq_spec(Tc), seq_spec(Tc), col_spec(Tc), row_spec(Tc)],
        out_specs=[seq_spec(T), seq_spec(Tc)],
        out_shape=[jax.ShapeDtypeStruct((B, T, D_A), bf16), jax.ShapeDtypeStruct((B, Tc, D_A), bf16)],
        scratch_shapes=[pltpu.VMEM((2 * HP, Dh, Dh), f32), pltpu.VMEM((2 * HP, 1, Dh), f32)],
        compiler_params=pltpu.CompilerParams(dimension_semantics=("parallel", "parallel"),
                                             vmem_limit_bytes=_vmem_limit(est)),
        name="mlstm",
    )(qx, kx, vx, colsx, rowsx, qc, kc, vc, colsc, rowsc)


def _fourier_w_kernel(cs_ref, fw_ref, o_ref, *, scale):
    o_ref[...] = (jnp.dot(cs_ref[...], fw_ref[...], preferred_element_type=f32,
                          precision=lax.Precision.HIGHEST) * scale).astype(bf16)


def _fourier_weights(fw, T):
    n = GROUP_B
    kk = np.outer(np.arange(n), np.arange(n)) % n
    ang = 2.0 * np.pi * kk / n
    cs = jnp.asarray(np.concatenate([np.cos(ang), np.sin(ang)], axis=0), f32)
    return pl.pallas_call(
        functools.partial(_fourier_w_kernel, scale=float(1.0 / np.sqrt(T * n))),
        grid=(N_GROUPS_B,),
        in_specs=[pl.BlockSpec((2 * n, n), lambda g: (0, 0)),
                  pl.BlockSpec((None, n, n), lambda g: (g, 0, 0))],
        out_specs=pl.BlockSpec((None, 2 * n, n), lambda g: (g, 0, 0)),
        out_shape=jax.ShapeDtypeStruct((N_GROUPS_B, 2 * n, n), bf16),
        compiler_params=pltpu.CompilerParams(dimension_semantics=("arbitrary",)),
        name="fourier_weights",
    )(cs, fw)


def _fft_tables():
    n1, n2 = FFT_N1, FFT_N2
    n = n1 * n2
    t1 = np.arange(n1)
    k1 = np.arange(n1)
    t2 = np.arange(n2)
    idx = (k1[None, :, None] * (n2 * t1[None, None, :] + t2[:, None, None])) % n
    ang = 2.0 * np.pi * idx / n
    m1 = np.concatenate([np.cos(ang), -np.sin(ang)], axis=1)
    k2 = np.arange(n2)
    ph = 2.0 * np.pi * (np.outer(k2, t2) % n2) / n2
    c, s = np.cos(ph), np.sin(ph)
    m2 = np.block([[c, s], [-s, c]])
    return jnp.asarray(m1, f32).astype(bf16), jnp.asarray(m2, f32).astype(bf16)


def _fourier_kernel(u_ref, z_ref, m1_ref, m2_ref, w_ref, o_ref, up_ref, yp_ref, xp_ref, pbuf_ref):
    n1, n2, P = FFT_N1, FFT_N2, FFT_PITCH
    nl = GROUP_B // V7X_LANES

    def fill(t1, carry):
        src = pl.multiple_of(t1 * n2, n2)
        dst = pl.multiple_of(t1 * P, V7X_SUBLANES)
        blk = u_ref[pl.ds(src, n2), :].astype(f32)
        for s in range(nl):
            up_ref[s, pl.ds(dst, n2), :] = blk[:, s * V7X_LANES:(s + 1) * V7X_LANES]
        return carry

    lax.fori_loop(0, n1, fill, 0, unroll=4)

    G = FFT_UNROLL

    def stage1(grp, carry):
        t2s = [grp * G + j for j in range(G)]
        rhs = [jnp.concatenate([up_ref[s, pl.ds(t2, n1, stride=P), :] for s in range(nl)],
                               axis=1).astype(bf16) for t2 in t2s]
        ys = [jnp.dot(m1_ref[t2], r, preferred_element_type=f32) for t2, r in zip(t2s, rhs)]
        for t2, y in zip(t2s, ys):
            dst = pl.multiple_of(t2 * P, V7X_SUBLANES)
            for ri in range(2):
                for s in range(nl):
                    yp_ref[ri * nl + s, pl.ds(dst, n1), :] = y[ri * n1:(ri + 1) * n1,
                                                               s * V7X_LANES:(s + 1) * V7X_LANES]
        return carry

    lax.fori_loop(0, n2 // G, stage1, 0)

    m2 = m2_ref[...]

    def stage2(grp, carry):
        k1s = [grp * G + j for j in range(G)]
        rhs = []
        for k1 in k1s:
            parts = [jnp.concatenate([yp_ref[ri * nl + s, pl.ds(k1, n2, stride=P), :] for s in range(nl)], axis=1)
                     for ri in range(2)]
            rhs.append(jnp.concatenate(parts, axis=0).astype(bf16))
        xs = [jnp.dot(m2, r, preferred_element_type=f32) for r in rhs]
        for k1, x in zip(k1s, xs):
            for ri in range(2):
                for s in range(nl):
                    xp_ref[ri * nl + s, pl.ds(k1, n2, stride=P), :] = x[ri * n2:(ri + 1) * n2,
                                                                        s * V7X_LANES:(s + 1) * V7X_LANES]
        return carry

    lax.fori_loop(0, n1 // G, stage2, 0)

    tr = pbuf_ref.shape[1]

    def matmul(t):
        blocks = []
        for k2 in range(t * (tr // n1), (t + 1) * (tr // n1)):
            blocks.append(jnp.concatenate([xp_ref[sl, k2 * P:k2 * P + n1, :] for sl in range(2 * nl)],
                                          axis=1).astype(bf16))
        return jnp.dot(jnp.concatenate(blocks, axis=0), w_ref[...], preferred_element_type=f32)

    def gate_store(t, y):
        sl = slice(t * tr, (t + 1) * tr)
        o_ref[sl, :] = y.astype(bf16) * _silu(z_ref[sl, :])

    _pipelined_matmuls((n1 * n2) // tr, matmul, gate_store, pbuf_ref)


def _fourier_latent(ub, zb, wcs):
    B, T, _ = ub.shape
    assert T == FFT_N1 * FFT_N2
    m1, m2 = _fft_tables()
    G, n = N_GROUPS_B, GROUP_B
    nl = n // V7X_LANES
    rows_p = FFT_N1 * FFT_PITCH
    blk = pl.BlockSpec((None, T, n), lambda b, g: (b, 0, g))
    est = (3 * 2 * T * n * 2 + 5 * nl * rows_p * V7X_LANES * 4 + T * 2 * n * 2 + 2 * m1.size * 2 + 8 * 512 * n * 4)
    return pl.pallas_call(
        _fourier_kernel,
        grid=(B, G),
        in_specs=[blk, blk,
                  pl.BlockSpec(m1.shape, lambda b, g: (0, 0, 0)),
                  pl.BlockSpec(m2.shape, lambda b, g: (0, 0)),
                  pl.BlockSpec((None, 2 * n, n), lambda b, g: (g, 0, 0))],
        out_specs=blk,
        out_shape=jax.ShapeDtypeStruct((B, T, D_B), bf16),
        scratch_shapes=[pltpu.VMEM((nl, rows_p, V7X_LANES), f32),
                        pltpu.VMEM((2 * nl, rows_p, V7X_LANES), f32),
                        pltpu.VMEM((2 * nl, rows_p, V7X_LANES), f32),
                        pltpu.VMEM((3, 512, n), f32)],
        compiler_params=pltpu.CompilerParams(dimension_semantics=("parallel", "parallel"),
                                             vmem_limit_bytes=_vmem_limit(est)),
        name="fourier_latent",
    )(ub, zb, m1, m2, wcs)


def _fourier_ctx_kernel(u_ref, z_ref, cs_ref, w_ref, o_ref):
    u = u_ref[...]
    x = jnp.dot(cs_ref[...], u, preferred_element_type=f32).astype(bf16)
    t = u.shape[0]
    w = w_ref[...]
    n = GROUP_B
    y = (jnp.dot(x[:t], w[:n], preferred_element_type=f32) + jnp.dot(x[t:], w[n:], preferred_element_type=f32))
    o_ref[...] = (y * _silu(z_ref[...].astype(f32))).astype(bf16)


def _fourier_ctx(ub, zb, wcs):
    B, T, _ = ub.shape
    ang = 2.0 * np.pi * (np.outer(np.arange(T), np.arange(T)) % T) / T
    cs = jnp.asarray(np.concatenate([np.cos(ang), -np.sin(ang)], axis=0), f32).astype(bf16)
    n = GROUP_B
    blk = pl.BlockSpec((None, T, n), lambda b, g: (b, 0, g))
    return pl.pallas_call(
        _fourier_ctx_kernel,
        grid=(B, N_GROUPS_B),
        in_specs=[blk, blk, pl.BlockSpec(cs.shape, lambda b, g: (0, 0)),
                  pl.BlockSpec((None, 2 * n, n), lambda b, g: (g, 0, 0))],
        out_specs=blk,
        out_shape=jax.ShapeDtypeStruct((B, T, D_B), bf16),
        compiler_params=pltpu.CompilerParams(dimension_semantics=("parallel", "parallel")),
        name="fourier_ctx",
    )(ub, zb, cs, wcs)


def _outproj_even_kernel(x_ref, h_ref, o_ref, za_ref, yb_ref, gate_ref, hg_ref, w_ref, out_ref):
    parts = []
    for hd in range(N_HEADS_A):
        sl = slice(hd * HEAD_DIM_A, (hd + 1) * HEAD_DIM_A)
        hh = _sigmoid(o_ref[:, sl]).astype(f32) * h_ref[:, sl]
        hh = hh * lax.rsqrt(jnp.mean(hh * hh, axis=-1, keepdims=True) + EPS)
        parts.append(hh.astype(bf16) * (hg_ref[:, sl].astype(bf16) * _silu(za_ref[:, sl])))
    ya = jnp.concatenate(parts, axis=1)
    acc = jnp.dot(ya, w_ref[:D_A, :], preferred_element_type=f32)
    acc = acc + jnp.dot(yb_ref[...], w_ref[D_A:, :], preferred_element_type=f32)
    out_ref[...] = x_ref[...] + gate_ref[...] * acc


def _outproj_even(x, h, o, za, yb, gate, head_g, wout, tm):
    B, T, D = x.shape
    row = lambda n: pl.BlockSpec((None, tm, n), lambda b, i: (b, i, 0))
    est = 2 * wout.size * 2 + 2 * tm * (2 * D * 4 + D_A * 4 + 3 * D_A * 2) + 8 * tm * D * 4
    return pl.pallas_call(
        _outproj_even_kernel,
        grid=(B, T // tm),
        in_specs=[row(D), row(D_A), row(D_A), row(D_A), row(D_B),
                  pl.BlockSpec((None, 1, D), lambda b, i: (b, 0, 0)),
                  pl.BlockSpec((1, D_A), lambda b, i: (0, 0)),
                  pl.BlockSpec(wout.shape, lambda b, i: (0, 0))],
        out_specs=row(D),
        out_shape=jax.ShapeDtypeStruct((B, T, D), f32),
        compiler_params=pltpu.CompilerParams(dimension_semantics=("parallel", "parallel"),
                                             vmem_limit_bytes=_vmem_limit(est)),
        name="outproj_even",
    )(x, h, o, za, yb, gate, head_g.reshape(1, D_A), wout)


def _inproj_odd_kernel(x_ref, sh_ref, sc_ref, g_ref, w_ref, u_ref, z_ref):
    amp = g_ref[...] * (1.0 + sc_ref[...])
    hx = _normed(x_ref[...], amp, sh_ref[...]).astype(bf16)
    cn = 512
    for j in range(D_INNER // cn):
        u_ref[:, j * cn:(j + 1) * cn] = jnp.dot(hx, w_ref[:, j * cn:(j + 1) * cn],
                                                preferred_element_type=f32).astype(bf16)
        z_ref[:, j * cn:(j + 1) * cn] = jnp.dot(hx, w_ref[:, D_INNER + j * cn:D_INNER + (j + 1) * cn],
                                                preferred_element_type=f32).astype(bf16)


def _even_out_odd_in_kernel(x_ref, h_ref, o_ref, za_ref, yb_ref, gate_ref, hg_ref, wo_ref,
                            sh_ref, sc_ref, g_ref, wi_ref, x1_ref, u_ref, z_ref):
    _outproj_even_kernel(x_ref, h_ref, o_ref, za_ref, yb_ref, gate_ref, hg_ref, wo_ref, x1_ref)
    _inproj_odd_kernel(x1_ref, sh_ref, sc_ref, g_ref, wi_ref, u_ref, z_ref)


def _even_out_odd_in(x, h, o, za, yb, gate, head_g, wout, shift, scale, norm_g, win, tm):
    B, T, D = x.shape
    row = lambda n: pl.BlockSpec((None, tm, n), lambda b, i: (b, i, 0))
    bvec = pl.BlockSpec((None, 1, D), lambda b, i: (b, 0, 0))
    est = (2 * (wout.size + win.size) * 2 + 2 * tm * (2 * D * 4 + 4 * D_A * 2 + 2 * D_INNER * 2)
           + 8 * tm * D * 4 + 6 * tm * 512 * 4)
    return pl.pallas_call(
        _even_out_odd_in_kernel,
        grid=(B, T // tm),
        in_specs=[row(D), row(D_A), row(D_A), row(D_A), row(D_B), bvec,
                  pl.BlockSpec((1, D_A), lambda b, i: (0, 0)),
                  pl.BlockSpec(wout.shape, lambda b, i: (0, 0)),
                  bvec, bvec,
                  pl.BlockSpec((1, D), lambda b, i: (0, 0)),
                  pl.BlockSpec(win.shape, lambda b, i: (0, 0))],
        out_specs=[row(D), row(D_INNER), row(D_INNER)],
        out_shape=[jax.ShapeDtypeStruct((B, T, D), f32)] + [jax.ShapeDtypeStruct((B, T, D_INNER), bf16)] * 2,
        compiler_params=pltpu.CompilerParams(dimension_semantics=("parallel", "parallel"),
                                             vmem_limit_bytes=_vmem_limit(est)),
        name="even_out_odd_in",
    )(x, h, o, za, yb, gate, head_g.reshape(1, D_A), wout, shift, scale, norm_g.reshape(1, D), win)


POOL_UNROLL = 32


def _pool_tables():
    w_idx = np.arange(GRID_W)
    band = np.zeros((N_GROUPS_C, GRID_W, GRID_W), np.float32)
    inv_w = np.zeros((N_GROUPS_C, GRID_W, V7X_LANES), np.float32)
    for g, win in enumerate(POOL_WINDOWS):
        lo = np.clip(w_idx - win // 2, 0, GRID_W)
        hi = np.clip(w_idx + win - win // 2, 0, GRID_W)
        band[g] = (w_idx[None, :] >= lo[:, None]) & (w_idx[None, :] < hi[:, None])
        inv_w[g] = (1.0 / (hi - lo))[:, None]
    return jnp.asarray(band, bf16), jnp.asarray(inv_w, f32)


def _pool_kernel(u_ref, z_ref, band_ref, invw_ref, pw_ref, sc_ref, o_ref, ps_ref, pbuf_ref, *, rows):
    g = pl.program_id(1)
    W = GRID_W
    band = band_ref[...]
    lo_off = hi_off = 0
    for gi, win in enumerate(POOL_WINDOWS):
        lo_off = jnp.where(g == gi, win // 2, lo_off)
        hi_off = jnp.where(g == gi, win - win // 2, hi_off)

    ps_ref[0:W, :] = jnp.zeros((W, GROUP_C), f32)

    def width_sum(grp, carry):
        srcs = [pl.multiple_of((grp * POOL_UNROLL + j) * W, W) for j in range(POOL_UNROLL)]
        sums = [jnp.dot(band, u_ref[pl.ds(src, W), :], preferred_element_type=f32) for src in srcs]
        acc = ps_ref[pl.ds(srcs[0], W), :]
        for src, s in zip(srcs, sums):
            acc = acc + s
            ps_ref[pl.ds(src + W, W), :] = acc
        return carry

    lax.fori_loop(0, rows // POOL_UNROLL, width_sum, 0)

    inv_w = invw_ref[...]

    def pooled_minus_self(r):
        lo = jnp.maximum(r - lo_off, 0)
        hi = jnp.minimum(r + hi_off, rows)
        acc = (ps_ref[pl.ds(pl.multiple_of(hi * W, W), W), :]
               - ps_ref[pl.ds(pl.multiple_of(lo * W, W), W), :])
        inv = inv_w / (hi - lo).astype(f32)
        inv = jnp.concatenate([inv] * (GROUP_C // V7X_LANES), axis=1)
        ug = u_ref[r * W:(r + 1) * W, :].astype(f32)
        return (acc * inv - ug).astype(bf16)

    tr = pbuf_ref.shape[1]
    pws = (pw_ref[...] * sc_ref[...]).astype(bf16)

    def matmul(t):
        lhs = jnp.concatenate([pooled_minus_self(t * (tr // W) + j) for j in range(tr // W)], axis=0)
        return jnp.dot(lhs, pws, preferred_element_type=f32)

    def gate_store(t, y):
        sl = slice(t * tr, (t + 1) * tr)
        o_ref[sl, :] = y.astype(bf16) * _silu(z_ref[sl, :])

    _pipelined_matmuls((rows * W) // tr, matmul, gate_store, pbuf_ref)


def _pool_mix(u, z, pool_w, scale):
    B, T, _ = u.shape
    rows = T // GRID_W
    band, inv_w = _pool_tables()
    n = GROUP_C
    blk = pl.BlockSpec((None, T, n), lambda b, g: (b, 0, g))
    est = 3 * 2 * T * n * 2 + (rows + 1) * GRID_W * n * 4 + T * n * 2 + 2 * n * n * 4 + 8 * 512 * n * 4
    return pl.pallas_call(
        functools.partial(_pool_kernel, rows=rows),
        grid=(B, N_GROUPS_C),
        in_specs=[blk, blk,
                  pl.BlockSpec((None, GRID_W, GRID_W), lambda b, g: (g, 0, 0)),
                  pl.BlockSpec((None, GRID_W, V7X_LANES), lambda b, g: (g, 0, 0)),
                  pl.BlockSpec((None, n, n), lambda b, g: (g, 0, 0)),
                  pl.BlockSpec((None, 1, n), lambda b, g: (g, 0, 0))],
        out_specs=blk,
        out_shape=jax.ShapeDtypeStruct((B, T, D_INNER), bf16),
        scratch_shapes=[pltpu.VMEM(((rows + 1) * GRID_W, n), f32),
                        pltpu.VMEM((3, 512, n), f32)],
        compiler_params=pltpu.CompilerParams(dimension_semantics=("parallel", "arbitrary"),
                                             vmem_limit_bytes=_vmem_limit(est)),
        name="pool_mix",
    )(u, z, band, inv_w, pool_w, scale.reshape(N_GROUPS_C, 1, n))


def _outproj_odd_kernel(x_ref, y_ref, gate_ref, fg_ref, w_ref, out_ref):
    acc = jnp.dot(y_ref[...], w_ref[...], preferred_element_type=f32)
    x = x_ref[...] + gate_ref[...] * acc
    out_ref[...] = x * lax.rsqrt(jnp.mean(x * x, axis=-1, keepdims=True) + EPS) * fg_ref[...]


def _outproj_odd(x, y, gate, final_g, wout, tm):
    B, T, D = x.shape
    row = lambda n: pl.BlockSpec((None, tm, n), lambda b, i: (b, i, 0))
    est = 2 * wout.size * 2 + 2 * tm * (2 * D * 4 + D_INNER * 2) + 6 * tm * D * 4
    return pl.pallas_call(
        _outproj_odd_kernel,
        grid=(B, T // tm),
        in_specs=[row(D), row(D_INNER),
                  pl.BlockSpec((None, 1, D), lambda b, i: (b, 0, 0)),
                  pl.BlockSpec((1, D), lambda b, i: (0, 0)),
                  pl.BlockSpec(wout.shape, lambda b, i: (0, 0))],
        out_specs=row(D),
        out_shape=jax.ShapeDtypeStruct((B, T, D), f32),
        compiler_params=pltpu.CompilerParams(dimension_semantics=("parallel", "parallel"),
                                             vmem_limit_bytes=_vmem_limit(est)),
        name="outproj_odd",
    )(x, y, gate, final_g.reshape(1, D), wout)


def kernel(x, c, ctx, c_ctx, ada_w, ada_b, norm_g, win_even, gate_b_even, conv_qk_even, head_norm_even,
           fourier_w_even, wout_even, win_odd, pool_w_odd, pool_scale_odd, wout_odd, final_g):
    B, T, D = x.shape
    Tc = ctx.shape[1]
    H = N_HEADS_A
    L = MLSTM_CHUNK

    nrow = -(-(B + 1) // V7X_SUBLANES) * V7X_SUBLANES
    rows_in = jnp.zeros((nrow, D), f32).at[:B].set(c).at[B].set(c_ctx)
    mod = _modulation(rows_in, ada_w, ada_b)

    def mod_parts(l, r0, r1, n):
        m = mod[l, r0:r1]
        parts = [jnp.broadcast_to(m[:, None, i * D:(i + 1) * D], (n, 1, D)) for i in range(3)]
        return parts

    we = win_even[0]
    w_main = we.astype(bf16)
    gcols = we[:, W_MAIN_EVEN:].reshape(D, 4, H)
    gbias = gate_b_even[0].reshape(4, H)
    ig_w = gcols[:, 0::2, :].transpose(0, 2, 1).reshape(D, GATE_SLOTS)
    fg_w = gcols[:, 1::2, :].transpose(0, 2, 1).reshape(D, GATE_SLOTS)
    ig_b = gbias[0::2, :].T.reshape(GATE_SLOTS)
    fg_b = gbias[1::2, :].T.reshape(GATE_SLOTS)
    wgt =jnp.concatenate([ig_w, fg_w], axis=1).T.astype(bf16)
    gbt = jnp.concatenate([ig_b, fg_b]).reshape(N_GATES, 1)
    conv_w = conv_qk_even[0]

    shift_x, scale_x, gate_x = mod_parts(0, 0, B, B)
    shift_c, scale_c, gate_c = mod_parts(0, B, B + 1, B)

    qx, kx, vx, ox, zax, ubx, zbx, gtx = _inproj_even(
        x, shift_x, scale_x, norm_g[0], w_main, wgt, gbt, conv_w, tm=512)
    qc, kc, vc, gtc = _inproj_even(ctx, shift_c, scale_c, norm_g[0], w_main, wgt, gbt, conv_w, tm=Tc, part="scan")
    oc, zac, ubc, zbc = _inproj_even(ctx, shift_c, scale_c, norm_g[0], w_main, wgt, gbt, conv_w, tm=Tc, part="mix")

    colsx, rowsx = _gate_prep(gtx, L)
    colsc, rowsc = _gate_prep(gtc, L)
    h_x, h_c = _mlstm(qx, kx, vx, colsx, rowsx, qc, kc, vc, colsc, rowsc, L)

    wcs = _fourier_weights(fourier_w_even[0], T)
    yb_x = _fourier_latent(ubx, zbx, wcs)
    wout_e = wout_even[0].astype(bf16)
    shift_1, scale_1, gate_1 = mod_parts(1, 0, B, B)
    x1, u1, z1 = _even_out_odd_in(x, h_x, ox, zax, yb_x, gate_x, head_norm_even[0], wout_e,
                                  shift_1, scale_1, norm_g[1], win_odd[0].astype(bf16), tm=512)

    wcs_c = _fourier_weights(fourier_w_even[0], Tc)
    yb_c = _fourier_ctx(ubc, zbc, wcs_c)
    ctx1 = _outproj_even(ctx, h_c, oc, zac, yb_c, gate_c, head_norm_even[0], wout_e, tm=Tc)
    del ctx1

    y1 = _pool_mix(u1, z1, pool_w_odd[0], pool_scale_odd[0])
    return _outproj_odd(x1, y1, gate_1, final_g, wout_odd[0].astype(bf16), tm=1024)
```

```python
import functools

import numpy as np
import jax
import jax.numpy as jnp
from jax import lax
from jax.experimental import pallas as pl
from jax.experimental.pallas import tpu as pltpu

D_MODEL = 1024
DEPTH = 2
CTX_LEN = 256
GRID_W = 64
D_INNER = 2 * D_MODEL
D_A = D_INNER // 2
D_B = D_INNER - D_A
N_HEADS_A = 4
HEAD_DIM_A = D_A // N_HEADS_A
N_GROUPS_B = 4
GROUP_B = D_B // N_GROUPS_B
N_GROUPS_C = 4
GROUP_C = D_INNER // N_GROUPS_C
POOL_WINDOWS = (2, 4, 8, 16)
CONV_W = 3
N_GATES = 4 * N_HEADS_A
W_MAIN_EVEN = 5 * D_A + 2 * D_B
EPS = 1e-6

f32 = jnp.float32
bf16 = jnp.bfloat16

V7X_VMEM_BYTES = 64 * 1024 * 1024
V7X_LANES = 128
V7X_SUBLANES = 8

MLSTM_CHUNK = 256
MLSTM_HEADS_PER_STEP = 2
FFT_N1 = 64
FFT_N2 = 64
FFT_PITCH = 72
FFT_UNROLL = 64
NEG_BIG = -1e30


def _vmem_limit(nbytes):
    return int(min(max(nbytes * 5 // 4 + (4 << 20), 16 << 20), V7X_VMEM_BYTES - (6 << 20)))


def _sigmoid(v):
    return 0.5 * jnp.tanh(0.5 * v) + 0.5


def _silu(v):
    return v * _sigmoid(v)


def _pipelined_matmuls(n, matmul, epilogue, pbuf_ref):
    zero = jnp.minimum(pl.program_id(0), 0)
    nbuf = pbuf_ref.shape[0]
    pbuf_ref[zero] = matmul(0)
    for t in range(n):
        if t + 1 < n:
            pbuf_ref[zero + (t + 1) % nbuf] = matmul(t + 1)
        epilogue(t, pbuf_ref[zero + t % nbuf])


def _log_sigmoid(v):
    return jnp.minimum(v, 0.0) - jnp.log1p(jnp.exp(-jnp.abs(v)))


def _mod_kernel(r_ref, w_ref, b_ref, o_ref):
    s = _silu(r_ref[...])
    o_ref[...] = jnp.dot(s, w_ref[...], preferred_element_type=f32,
                         precision=lax.Precision.HIGHEST) + b_ref[...]


def _modulation(rows, ada_w, ada_b):
    nrow = rows.shape[0]
    tn = 1024
    return pl.pallas_call(
        _mod_kernel,
        grid=(DEPTH, 3 * D_MODEL // tn),
        in_specs=[
            pl.BlockSpec((nrow, D_MODEL), lambda l, j: (0, 0)),
            pl.BlockSpec((None, D_MODEL, tn), lambda l, j: (l, 0, j)),
            pl.BlockSpec((None, 1, tn), lambda l, j: (l, 0, j)),
        ],
        out_specs=pl.BlockSpec((None, nrow, tn), lambda l, j: (l, 0, j)),
        out_shape=jax.ShapeDtypeStruct((DEPTH, nrow, 3 * D_MODEL), f32),
        compiler_params=pltpu.CompilerParams(dimension_semantics=("arbitrary", "arbitrary")),
        name="modulation",
    )(rows, ada_w, ada_b.reshape(DEPTH, 1, 3 * D_MODEL))


def _normed(x, amp, shift):
    ms = jnp.mean(x * x, axis=-1, keepdims=True)
    return (x * lax.rsqrt(ms + EPS)) * amp + shift


INPROJ_EVEN_OUTPUTS = {
    "all": ("q", "k", "v", "o", "za", "ub", "zb", "gt"),
    "scan": ("q", "k", "v", "gt"),
    "mix": ("o", "za", "ub", "zb"),
}
INPROJ_EVEN_PLAIN = ("v", "o", "za", "ub", "zb")


def _inproj_even_kernel(x_ref, xp_ref, xn_ref, sh_ref, sc_ref, g_ref, w_ref, wgt_ref, gbt_ref, cw_ref, *refs,
                        tm, nt, part):
    out = dict(zip(INPROJ_EVEN_OUTPUTS[part], refs[:-1]))
    pbuf_ref = refs[-1]
    i = pl.program_id(1)
    amp = g_ref[...] * (1.0 + sc_ref[...])
    shift = sh_ref[...]
    hx = _normed(x_ref[...], amp, shift).astype(bf16)
    cn = 512
    tasks = []
    if "q" in out:
        halo = jnp.concatenate([xp_ref[...], xn_ref[...]], axis=0)
        hh = _normed(halo, amp, shift).astype(bf16)
        has_prev = (i > 0).astype(f32)
        has_next = (i < nt - 1).astype(f32)
        row = lax.broadcasted_iota(jnp.int32, (tm, 1), 0)
        ph = jnp.dot(hh, w_ref[:, :2 * D_A], preferred_element_type=f32)
        prev = ph[V7X_SUBLANES - 1:V7X_SUBLANES, :] * has_prev
        nxt = ph[V7X_SUBLANES:V7X_SUBLANES + 1, :] * has_next

        def conv_store(j, p):
            cols = slice(j * cn, (j + 1) * cn)
            up = jnp.where(row == 0, prev[:, cols], pltpu.roll(p, 1, 0))
            dn = jnp.where(row == tm - 1, nxt[:, cols], pltpu.roll(p, tm - 1, 0))
            cw = cw_ref[:, cols]
            y = _silu(cw[0:1, :] * up + cw[1:2, :] * p + cw[2:3, :] * dn)
            if j < D_A // cn:
                out["q"][:, cols] = (y * (HEAD_DIM_A ** -0.5)).astype(bf16)
            else:
                jj = j - D_A // cn
                out["k"][:, jj * cn:(jj + 1) * cn] = y.astype(bf16)

        tasks += [(j * cn, functools.partial(conv_store, j)) for j in range(2 * D_A // cn)]

    def plain_store(ref, jj):
        def store(p):
            ref[:, jj * cn:(jj + 1) * cn] = p.astype(bf16)
        return store

    for idx, name in enumerate(INPROJ_EVEN_PLAIN):
        if name in out:
            for jj in range(D_A // cn):
                tasks.append((2 * D_A + idx * D_A + jj * cn, plain_store(out[name], jj)))

    def matmul(t):
        c0 = tasks[t][0]
        return jnp.dot(hx, w_ref[:, c0:c0 + cn], preferred_element_type=f32)

    _pipelined_matmuls(len(tasks), matmul, lambda t, p: tasks[t][1](p), pbuf_ref)
    if "gt" in out:
        out["gt"][...] = lax.dot_general(wgt_ref[...], hx, (((1,), (1,)), ((), ())),
                                         preferred_element_type=f32) + gbt_ref[...]


def _inproj_even(x, shift, scale, norm_g, w_main, wgt, gbt, conv_w, tm, part="all"):
    B, T, D = x.shape
    nt = T // tm
    hb = tm // V7X_SUBLANES
    nhb = T // V7X_SUBLANES
    names = INPROJ_EVEN_OUTPUTS[part]
    row_spec = pl.BlockSpec((None, tm, D_A), lambda b, i: (b, i, 0))
    gt_spec = pl.BlockSpec((None, N_GATES, tm), lambda b, i: (b, 0, i))
    vec = lambda n: pl.BlockSpec((1, n), lambda b, i: (0, 0))
    est = (2 * w_main.size * 2 + 2 * tm * D * 4 + len(names) * 2 * tm * D_A * 2 + 6 * tm * 512 * 4)
    outs = pl.pallas_call(
        functools.partial(_inproj_even_kernel, tm=tm, nt=nt, part=part),
        grid=(B, nt),
        in_specs=[
            pl.BlockSpec((None, tm, D), lambda b, i: (b, i, 0)),
            pl.BlockSpec((None, V7X_SUBLANES, D), lambda b, i: (b, jnp.maximum(i * hb - 1, 0), 0)),
            pl.BlockSpec((None, V7X_SUBLANES, D), lambda b, i: (b, jnp.minimum((i + 1) * hb, nhb - 1), 0)),
            pl.BlockSpec((None, 1, D), lambda b, i: (b, 0, 0)),
            pl.BlockSpec((None, 1, D), lambda b, i: (b, 0, 0)),
            vec(D),
            pl.BlockSpec(w_main.shape, lambda b, i: (0, 0)),
            pl.BlockSpec(wgt.shape, lambda b, i: (0, 0)),
            pl.BlockSpec((N_GATES, 1), lambda b, i: (0, 0)),
            pl.BlockSpec(conv_w.shape, lambda b, i: (0, 0)),
        ],
        out_specs=[gt_spec if n == "gt" else row_spec for n in names],
        out_shape=[jax.ShapeDtypeStruct((B, N_GATES, T), f32) if n == "gt"
                   else jax.ShapeDtypeStruct((B, T, D_A), bf16) for n in names],
        scratch_shapes=[pltpu.VMEM((3, tm, 512), f32)],
        compiler_params=pltpu.CompilerParams(dimension_semantics=("parallel", "arbitrary"),
                                             vmem_limit_bytes=_vmem_limit(est)),
        name="inproj_even_" + part,
    )(x, x, x, shift, scale, norm_g.reshape(1, D), w_main, wgt, gbt, conv_w)
    return outs


GATE_SLOTS = 2 * N_HEADS_A
GATE_PIECES = 3
GATE_QUANTS = 3
assert GATE_SLOTS & (GATE_SLOTS - 1) == 0 and GATE_QUANTS * GATE_PIECES * GATE_SLOTS <= V7X_LANES


def _scan_max_lanes(x, seg, reverse):
    n = x.shape[1]
    pos = lax.broadcasted_iota(jnp.int32, x.shape, 1) & (seg - 1)
    s = 1
    while s < seg:
        if reverse:
            x = jnp.where(pos < seg - s, jnp.maximum(x, pltpu.roll(x, n - s, 1)), x)
        else:
            x = jnp.where(pos >= s, jnp.maximum(x, pltpu.roll(x, s, 1)), x)
        s *= 2
    return x


def _split3(v):
    hi = v.astype(bf16)
    r1 = v - hi.astype(f32)
    mid = r1.astype(bf16)
    lo = (r1 - mid.astype(f32)).astype(bf16)
    return hi, mid, lo


def _gate_prep_kernel(gt_ref, cols_ref, arow_ref, *, L, nchunk):
    r = lax.broadcasted_iota(jnp.int32, (L, L), 0)
    c = lax.broadcasted_iota(jnp.int32, (L, L), 1)
    tri_l = (c <= r).astype(bf16)
    tri_u = (c >= r).astype(bf16)
    S = V7X_SUBLANES
    tb = nchunk * L
    fwd = (lax.broadcasted_iota(jnp.int32, (S, tb), 0) & 1) == 0
    ig = gt_ref[:S, :]
    lf = _log_sigmoid(gt_ref[S:, :])
    pieces = jnp.concatenate([p.astype(f32) for p in _split3(lf)] + [jnp.zeros((S, tb), f32)],
                             axis=0).astype(bf16)
    chunks = [slice(ci * L, (ci + 1) * L) for ci in range(nchunk)]
    pre = jnp.concatenate([jnp.dot(pieces[:, sl], tri_u, preferred_element_type=f32) for sl in chunks], axis=1)
    suf = jnp.concatenate([jnp.dot(pieces[:, sl], tri_l, preferred_element_type=f32) for sl in chunks], axis=1)
    pre = pre[:S] + pre[S:2 * S] + pre[2 * S:3 * S]
    suf = suf[:S] + suf[S:2 * S] + suf[2 * S:3 * S]
    b_row = jnp.where(fwd, pre, suf)
    a_row = ig - b_row
    cmax = jnp.where(fwd, _scan_max_lanes(a_row, L, False), _scan_max_lanes(a_row, L, True))
    arow_ref[...] = a_row
    parts = [p.astype(f32) for quant in (b_row, a_row, cmax) for p in _split3(quant)]
    fill = jnp.zeros((V7X_LANES - len(parts) * S, tb), f32)
    packed = jnp.concatenate(parts + [fill], axis=0)
    for sl in chunks:
        cols_ref[sl, :] = packed[:, sl].T.astype(bf16)


def _gate_prep(gt, L):
    B, _, T = gt.shape
    nchunk = min(4, T // L)
    tb = nchunk * L
    cols, arow = pl.pallas_call(
        functools.partial(_gate_prep_kernel, L=L, nchunk=nchunk),
        grid=(B, T // tb),
        in_specs=[pl.BlockSpec((None, N_GATES, tb), lambda b, i: (b, 0, i))],
        out_specs=[
            pl.BlockSpec((None, tb, V7X_LANES), lambda b, i: (b, i, 0)),
            pl.BlockSpec((None, V7X_SUBLANES, tb), lambda b, i: (b, 0, i)),
        ],
        out_shape=[
            jax.ShapeDtypeStruct((B, T, V7X_LANES), bf16),
            jax.ShapeDtypeStruct((B, V7X_SUBLANES, T), f32),
        ],
        compiler_params=pltpu.CompilerParams(dimension_semantics=("parallel", "parallel")),
        name="gate_prep",
    )(gt)
    return cols, arow.reshape(B, N_HEADS_A, 2, T)


def _mlstm_kernel(qx_ref, kx_ref, vx_ref, cx_ref, rx_ref, qc_ref, kc_ref, vc_ref, cc_ref, rc_ref,
                  hx_ref, hc_ref, c_ref, n_ref, *, L, nx, nc):
    Dh, LN, HP = HEAD_DIM_A, V7X_LANES, MLSTM_HEADS_PER_STEP
    head0 = pl.program_id(1) * HP
    chains = [(hh, d) for hh in range(HP) for d in range(2)]
    r_i = lax.broadcasted_iota(jnp.int32, (L, L), 0)
    c_i = lax.broadcasted_iota(jnp.int32, (L, L), 1)
    masks = (c_i <= r_i, c_i >= r_i)

    sr = lax.broadcasted_iota(jnp.int32, (LN, GATE_QUANTS * LN), 0)
    sc = lax.broadcasted_iota(jnp.int32, (LN, GATE_QUANTS * LN), 1)
    span = GATE_PIECES * GATE_SLOTS
    in_block = None
    for qi in range(GATE_QUANTS):
        blk = (sr >= qi * span) & (sr < (qi + 1) * span) & (sc >= qi * LN) & (sc < (qi + 1) * LN)
        in_block = blk if in_block is None else in_block | blk
    slot = sr & (GATE_SLOTS - 1)
    sels = {(hh, d): (in_block & (slot == 2 * (head0 + hh) + d)).astype(bf16) for hh, d in chains}

    def tile(v, width):
        return jnp.concatenate([v] * (width // LN), axis=1)

    def step(refs, r0s, ms):
        q_ref, k_ref, v_ref, col_ref, row_ref = refs
        st = []
        for ci, (hh, d) in enumerate(chains):
            rows = pl.ds(r0s[ci], L)
            hsl = slice(hh * Dh, (hh + 1) * Dh)
            q = q_ref[rows, hsl]
            k = k_ref[rows, hsl]
            rep = jnp.dot(col_ref[rows, :], sels[hh, d], preferred_element_type=f32)
            qk = lax.dot_general(q, k, (((1,), (1,)), ((), ())), preferred_element_type=f32)
            qc = jnp.dot(q, c_ref[ci].astype(bf16), preferred_element_type=f32)
            st.append(dict(q=q, k=k, v=v_ref[rows, hsl], rep=rep, qk=qk, qc=qc, a_row=row_ref[hh, d:d + 1, rows]))
        new_ms = []
        for ci, (hh, d) in enumerate(chains):
            c, m_prev = st[ci], ms[ci]
            b_rep, a_rep = c["rep"][:, :LN], c["rep"][:, LN:2 * LN]
            g_rep = jnp.maximum(c["rep"][:, 2 * LN:], m_prev)
            p = jnp.exp(jnp.where(masks[d], c["a_row"] - tile(g_rep, L), NEG_BIG))
            s = c["qk"] * p
            c["s"] = s.astype(bf16)
            inter = jnp.exp(m_prev - g_rep)
            qn = jnp.sum(c["q"].astype(f32) * n_ref[ci], axis=1, keepdims=True)
            den = jnp.sum(s, axis=1, keepdims=True) + inter[:, :1] * qn
            floor = jnp.exp(-(b_rep + g_rep))
            rcp = 1.0 / jnp.maximum(jnp.abs(den), floor[:, :1])
            c["rcp"] = jnp.broadcast_to(rcp, (L, LN))
            c["inter"] = inter
            b_end = b_rep[L - 1:L, :] if d == 0 else b_rep[0:1, :]
            w = b_end + a_rep
            m_new = jnp.maximum(b_end + m_prev, jnp.max(w, axis=0, keepdims=True))
            c["decay"] = jnp.exp(b_end + m_prev - m_new)
            kw = c["k"].astype(f32) * tile(jnp.exp(w - m_new), Dh)
            c["kw"] = kw.astype(bf16)
            c["ksum"] = jnp.sum(kw, axis=0, keepdims=True)
            new_ms.append(m_new)
        for c in st:
            c["sv"] = jnp.dot(c["s"], c["v"], preferred_element_type=f32)
            c["upd"] = lax.dot_general(c["kw"], c["v"], (((0,), (0,)), ((), ())), preferred_element_type=f32)
        hs = []
        for ci, c in enumerate(st):
            hs.append((c["sv"] + tile(c["inter"], Dh) * c["qc"]) * tile(c["rcp"], Dh))
            decay = tile(c["decay"], Dh)
            c_ref[ci] = decay * c_ref[ci] + c["upd"]
            n_ref[ci] = decay * n_ref[ci] + c["ksum"]
        return hs, new_ms

    ctx = (qc_ref, kc_ref, vc_ref, cc_ref, rc_ref)
    lat = (qx_ref, kx_ref, vx_ref, cx_ref, rx_ref)

    c_ref[...] = jnp.zeros_like(c_ref)
    n_ref[...] = jnp.zeros_like(n_ref)
    ms = [jnp.zeros((1, LN), f32) for _ in chains]
    written = set()
    for j in range(nc):
        cjs = [j if d == 0 else nc - 1 - j for _, d in chains]
        hs, ms = step(ctx, [cj * L for cj in cjs], ms)
        for (hh, d), cj, h in zip(chains, cjs, hs):
            dst = (slice(cj * L, (cj + 1) * L), slice(hh * Dh, (hh + 1) * Dh))
            if (cj, hh) in written:
                hc_ref[dst] = (hc_ref[dst].astype(f32) + h).astype(hc_ref.dtype)
            else:
                hc_ref[dst] = h.astype(hc_ref.dtype)
                written.add((cj, hh))

    def make_body(accumulate):
        def body(i, ms):
            r0s = [pl.multiple_of((i if d == 0 else nx - 1 - i) * L, L) for _, d in chains]
            hs, ms = step(lat, r0s, list(ms))
            for (hh, d), r0, h in zip(chains, r0s, hs):
                dst = (pl.ds(r0, L), slice(hh * Dh, (hh + 1) * Dh))
                if accumulate:
                    hx_ref[dst] = (hx_ref[dst].astype(f32) + h).astype(hx_ref.dtype)
                else:
                    hx_ref[dst] = h.astype(hx_ref.dtype)
            return tuple(ms)
        return body

    ms = lax.fori_loop(0, nx // 2, make_body(False), tuple(ms))
    lax.fori_loop(nx // 2, nx, make_body(True), ms)


def _mlstm(qx, kx, vx, colsx, rowsx, qc, kc, vc, colsc, rowsc, L):
    B, T, _ = qx.shape
    Tc = qc.shape[1]
    H, Dh, HP = N_HEADS_A, HEAD_DIM_A, MLSTM_HEADS_PER_STEP
    assert T % (2 * L) == 0 and Tc % L == 0 and H % HP == 0

    def seq_spec(t):
        return pl.BlockSpec((None, t, HP * Dh), lambda b, h: (b, 0, h))

    def col_spec(t):
        return pl.BlockSpec((None, t, V7X_LANES), lambda b, h: (b, 0, 0))

    def row_spec(t):
        return pl.BlockSpec((None, HP, 2, t), lambda b, h: (b, h, 0, 0))

    est = (2 * HP * (3 * (T + Tc) * Dh * 2 + 8 * (T + Tc) * 4 + (T + Tc) * Dh * 2)
           + 2 * (T + Tc) * V7X_LANES * 2 + 2 * HP * Dh * (Dh + V7X_LANES) * 4 + 16 * L * L * 4)
    return pl.pallas_call(
        functools.partial(_mlstm_kernel, L=L, nx=T // L, nc=Tc // L),
        grid=(B, H // HP),
        in_specs=[seq_spec(T), seq_spec(T), seq_spec(T), col_spec(T), row_spec(T),
                  seq_spec(Tc), seq_spec(Tc), seq_spec(Tc), col_spec(Tc), row_spec(Tc)],
        out_specs=[seq_spec(T), seq_spec(Tc)],
        out_shape=[jax.ShapeDtypeStruct((B, T, D_A), bf16), jax.ShapeDtypeStruct((B, Tc, D_A), bf16)],
        scratch_shapes=[pltpu.VMEM((2 * HP, Dh, Dh), f32), pltpu.VMEM((2 * HP, 1, Dh), f32)],
        compiler_params=pltpu.CompilerParams(dimension_semantics=("parallel", "parallel"),
                                             vmem_limit_bytes=_vmem_limit(est)),
        name="mlstm",
    )(qx, kx, vx, colsx, rowsx, qc, kc, vc, colsc, rowsc)


def _fourier_w_kernel(cs_ref, fw_ref, o_ref, *, scale):
    o_ref[...] = (jnp.dot(cs_ref[...], fw_ref[...], preferred_element_type=f32,
                          precision=lax.Precision.HIGHEST) * scale).astype(bf16)


def _fourier_weights(fw, T):
    n = GROUP_B
    kk = np.outer(np.arange(n), np.arange(n)) % n
    ang = 2.0 * np.pi * kk / n
    cs = jnp.asarray(np.concatenate([np.cos(ang), np.sin(ang)], axis=0), f32)
    return pl.pallas_call(
        functools.partial(_fourier_w_kernel, scale=float(1.0 / np.sqrt(T * n))),
        grid=(N_GROUPS_B,),
        in_specs=[pl.BlockSpec((2 * n, n), lambda g: (0, 0)),
                  pl.BlockSpec((None, n, n), lambda g: (g, 0, 0))],
        out_specs=pl.BlockSpec((None, 2 * n, n), lambda g: (g, 0, 0)),
        out_shape=jax.ShapeDtypeStruct((N_GROUPS_B, 2 * n, n), bf16),
        compiler_params=pltpu.CompilerParams(dimension_semantics=("arbitrary",)),
        name="fourier_weights",
    )(cs, fw)


def _fft_tables():
    n1, n2 = FFT_N1, FFT_N2
    n = n1 * n2
    t1 = np.arange(n1)
    k1 = np.arange(n1)
    t2 = np.arange(n2)
    idx = (k1[None, :, None] * (n2 * t1[None, None, :] + t2[:, None, None])) % n
    ang = 2.0 * np.pi * idx / n
    m1 = np.concatenate([np.cos(ang), -np.sin(ang)], axis=1)
    k2 = np.arange(n2)
    ph = 2.0 * np.pi * (np.outer(k2, t2) % n2) / n2
    c, s = np.cos(ph), np.sin(ph)
    m2 = np.block([[c, s], [-s, c]])
    return jnp.asarray(m1, f32).astype(bf16), jnp.asarray(m2, f32).astype(bf16)


def _fourier_kernel(u_ref, z_ref, m1_ref, m2_ref, w_ref, o_ref, up_ref, yp_ref, xp_ref, pbuf_ref):
    n1, n2, P = FFT_N1, FFT_N2, FFT_PITCH
    nl = GROUP_B // V7X_LANES

    def fill(t1, carry):
        src = pl.multiple_of(t1 * n2, n2)
        dst = pl.multiple_of(t1 * P, V7X_SUBLANES)
        blk = u_ref[pl.ds(src, n2), :].astype(f32)
        for s in range(nl):
            up_ref[s, pl.ds(dst, n2), :] = blk[:, s * V7X_LANES:(s + 1) * V7X_LANES]
        return carry

    lax.fori_loop(0, n1, fill, 0, unroll=4)

    G = FFT_UNROLL

    def stage1(grp, carry):
        t2s = [grp * G + j for j in range(G)]
        rhs = [jnp.concatenate([up_ref[s, pl.ds(t2, n1, stride=P), :] for s in range(nl)],
                               axis=1).astype(bf16) for t2 in t2s]
        ys = [jnp.dot(m1_ref[t2], r, preferred_element_type=f32) for t2, r in zip(t2s, rhs)]
        for t2, y in zip(t2s, ys):
            dst = pl.multiple_of(t2 * P, V7X_SUBLANES)
            for ri in range(2):
                for s in range(nl):
                    yp_ref[ri * nl + s, pl.ds(dst, n1), :] = y[ri * n1:(ri + 1) * n1,
                                                               s * V7X_LANES:(s + 1) * V7X_LANES]
        return carry

    lax.fori_loop(0, n2 // G, stage1, 0)

    m2 = m2_ref[...]

    def stage2(grp, carry):
        k1s = [grp * G + j for j in range(G)]
        rhs = []
        for k1 in k1s:
            parts = [jnp.concatenate([yp_ref[ri * nl + s, pl.ds(k1, n2, stride=P), :] for s in range(nl)], axis=1)
                     for ri in range(2)]
            rhs.append(jnp.concatenate(parts, axis=0).astype(bf16))
        xs = [jnp.dot(m2, r, preferred_element_type=f32) for r in rhs]
        for k1, x in zip(k1s, xs):
            for ri in range(2):
                for s in range(nl):
                    xp_ref[ri * nl + s, pl.ds(k1, n2, stride=P), :] = x[ri * n2:(ri + 1) * n2,
                                                                        s * V7X_LANES:(s + 1) * V7X_LANES]
        return carry

    lax.fori_loop(0, n1 // G, stage2, 0)

    tr = pbuf_ref.shape[1]

    def matmul(t):
        blocks = []
        for k2 in range(t * (tr // n1), (t + 1) * (tr // n1)):
            blocks.append(jnp.concatenate([xp_ref[sl, k2 * P:k2 * P + n1, :] for sl in range(2 * nl)],
                                          axis=1).astype(bf16))
        return jnp.dot(jnp.concatenate(blocks, axis=0), w_ref[...], preferred_element_type=f32)

    def gate_store(t, y):
        sl = slice(t * tr, (t + 1) * tr)
        o_ref[sl, :] = y.astype(bf16) * _silu(z_ref[sl, :])

    _pipelined_matmuls((n1 * n2) // tr, matmul, gate_store, pbuf_ref)


def _fourier_latent(ub, zb, wcs):
    B, T, _ = ub.shape
    assert T == FFT_N1 * FFT_N2
    m1, m2 = _fft_tables()
    G, n = N_GROUPS_B, GROUP_B
    nl = n // V7X_LANES
    rows_p = FFT_N1 * FFT_PITCH
    blk = pl.BlockSpec((None, T, n), lambda b, g: (b, 0, g))
    est = (3 * 2 * T * n * 2 + 5 * nl * rows_p * V7X_LANES * 4 + T * 2 * n * 2 + 2 * m1.size * 2 + 8 * 512 * n * 4)
    return pl.pallas_call(
        _fourier_kernel,
        grid=(B, G),
        in_specs=[blk, blk,
                  pl.BlockSpec(m1.shape, lambda b, g: (0, 0, 0)),
                  pl.BlockSpec(m2.shape, lambda b, g: (0, 0)),
                  pl.BlockSpec((None, 2 * n, n), lambda b, g: (g, 0, 0))],
        out_specs=blk,
        out_shape=jax.ShapeDtypeStruct((B, T, D_B), bf16),
        scratch_shapes=[pltpu.VMEM((nl, rows_p, V7X_LANES), f32),
                        pltpu.VMEM((2 * nl, rows_p, V7X_LANES), f32),
                        pltpu.VMEM((2 * nl, rows_p, V7X_LANES), f32),
                        pltpu.VMEM((3, 512, n), f32)],
        compiler_params=pltpu.CompilerParams(dimension_semantics=("parallel", "parallel"),
                                             vmem_limit_bytes=_vmem_limit(est)),
        name="fourier_latent",
    )(ub, zb, m1, m2, wcs)


def _fourier_ctx_kernel(u_ref, z_ref, cs_ref, w_ref, o_ref):
    u = u_ref[...]
    x = jnp.dot(cs_ref[...], u, preferred_element_type=f32).astype(bf16)
    t = u.shape[0]
    w = w_ref[...]
    n = GROUP_B
    y = (jnp.dot(x[:t], w[:n], preferred_element_type=f32) + jnp.dot(x[t:], w[n:], preferred_element_type=f32))
    o_ref[...] = (y * _silu(z_ref[...].astype(f32))).astype(bf16)


def _fourier_ctx(ub, zb, wcs):
    B, T, _ = ub.shape
    ang = 2.0 * np.pi * (np.outer(np.arange(T), np.arange(T)) % T) / T
    cs = jnp.asarray(np.concatenate([np.cos(ang), -np.sin(ang)], axis=0), f32).astype(bf16)
    n = GROUP_B
    blk = pl.BlockSpec((None, T, n), lambda b, g: (b, 0, g))
    return pl.pallas_call(
        _fourier_ctx_kernel,
        grid=(B, N_GROUPS_B),
        in_specs=[blk, blk, pl.BlockSpec(cs.shape, lambda b, g: (0, 0)),
                  pl.BlockSpec((None, 2 * n, n), lambda b, g: (g, 0, 0))],
        out_specs=blk,
        out_shape=jax.ShapeDtypeStruct((B, T, D_B), bf16),
        compiler_params=pltpu.CompilerParams(dimension_semantics=("parallel", "parallel")),
        name="fourier_ctx",
    )(ub, zb, cs, wcs)


def _outproj_even_kernel(x_ref, h_ref, o_ref, za_ref, yb_ref, gate_ref, hg_ref, w_ref, out_ref):
    parts = []
    for hd in range(N_HEADS_A):
        sl = slice(hd * HEAD_DIM_A, (hd + 1) * HEAD_DIM_A)
        hh = _sigmoid(o_ref[:, sl]).astype(f32) * h_ref[:, sl]
        hh = hh * lax.rsqrt(jnp.mean(hh * hh, axis=-1, keepdims=True) + EPS)
        parts.append(hh.astype(bf16) * (hg_ref[:, sl].astype(bf16) * _silu(za_ref[:, sl])))
    ya = jnp.concatenate(parts, axis=1)
    acc = jnp.dot(ya, w_ref[:D_A, :], preferred_element_type=f32)
    acc = acc + jnp.dot(yb_ref[...], w_ref[D_A:, :], preferred_element_type=f32)
    out_ref[...] = x_ref[...] + gate_ref[...] * acc


def _outproj_even(x, h, o, za, yb, gate, head_g, wout, tm):
    B, T, D = x.shape
    row = lambda n: pl.BlockSpec((None, tm, n), lambda b, i: (b, i, 0))
    est = 2 * wout.size * 2 + 2 * tm * (2 * D * 4 + D_A * 4 + 3 * D_A * 2) + 8 * tm * D * 4
    return pl.pallas_call(
        _outproj_even_kernel,
        grid=(B, T // tm),
        in_specs=[row(D), row(D_A), row(D_A), row(D_A), row(D_B),
                  pl.BlockSpec((None, 1, D), lambda b, i: (b, 0, 0)),
                  pl.BlockSpec((1, D_A), lambda b, i: (0, 0)),
                  pl.BlockSpec(wout.shape, lambda b, i: (0, 0))],
        out_specs=row(D),
        out_shape=jax.ShapeDtypeStruct((B, T, D), f32),
        compiler_params=pltpu.CompilerParams(dimension_semantics=("parallel", "parallel"),
                                             vmem_limit_bytes=_vmem_limit(est)),
        name="outproj_even",
    )(x, h, o, za, yb, gate, head_g.reshape(1, D_A), wout)


def _inproj_odd_kernel(x_ref, sh_ref, sc_ref, g_ref, w_ref, u_ref, z_ref):
    amp = g_ref[...] * (1.0 + sc_ref[...])
    hx = _normed(x_ref[...], amp, sh_ref[...]).astype(bf16)
    cn = 512
    for j in range(D_INNER // cn):
        u_ref[:, j * cn:(j + 1) * cn] = jnp.dot(hx, w_ref[:, j * cn:(j + 1) * cn],
                                                preferred_element_type=f32).astype(bf16)
        z_ref[:, j * cn:(j + 1) * cn] = jnp.dot(hx, w_ref[:, D_INNER + j * cn:D_INNER + (j + 1) * cn],
                                                preferred_element_type=f32).astype(bf16)


def _even_out_odd_in_kernel(x_ref, h_ref, o_ref, za_ref, yb_ref, gate_ref, hg_ref, wo_ref,
                            sh_ref, sc_ref, g_ref, wi_ref, x1_ref, u_ref, z_ref):
    _outproj_even_kernel(x_ref, h_ref, o_ref, za_ref, yb_ref, gate_ref, hg_ref, wo_ref, x1_ref)
    _inproj_odd_kernel(x1_ref, sh_ref, sc_ref, g_ref, wi_ref, u_ref, z_ref)


def _even_out_odd_in(x, h, o, za, yb, gate, head_g, wout, shift, scale, norm_g, win, tm):
    B, T, D = x.shape
    row = lambda n: pl.BlockSpec((None, tm, n), lambda b, i: (b, i, 0))
    bvec = pl.BlockSpec((None, 1, D), lambda b, i: (b, 0, 0))
    est = (2 * (wout.size + win.size) * 2 + 2 * tm * (2 * D * 4 + 4 * D_A * 2 + 2 * D_INNER * 2)
           + 8 * tm * D * 4 + 6 * tm * 512 * 4)
    return pl.pallas_call(
        _even_out_odd_in_kernel,
        grid=(B, T // tm),
        in_specs=[row(D), row(D_A), row(D_A), row(D_A), row(D_B), bvec,
                  pl.BlockSpec((1, D_A), lambda b, i: (0, 0)),
                  pl.BlockSpec(wout.shape, lambda b, i: (0, 0)),
                  bvec, bvec,
                  pl.BlockSpec((1, D), lambda b, i: (0, 0)),
                  pl.BlockSpec(win.shape, lambda b, i: (0, 0))],
        out_specs=[row(D), row(D_INNER), row(D_INNER)],
        out_shape=[jax.ShapeDtypeStruct((B, T, D), f32)] + [jax.ShapeDtypeStruct((B, T, D_INNER), bf16)] * 2,
        compiler_params=pltpu.CompilerParams(dimension_semantics=("parallel", "parallel"),
                                             vmem_limit_bytes=_vmem_limit(est)),
        name="even_out_odd_in",
    )(x, h, o, za, yb, gate, head_g.reshape(1, D_A), wout, shift, scale, norm_g.reshape(1, D), win)


POOL_UNROLL = 32


def _pool_tables():
    w_idx = np.arange(GRID_W)
    band = np.zeros((N_GROUPS_C, GRID_W, GRID_W), np.float32)
    inv_w = np.zeros((N_GROUPS_C, GRID_W, V7X_LANES), np.float32)
    for g, win in enumerate(POOL_WINDOWS):
        lo = np.clip(w_idx - win // 2, 0, GRID_W)
        hi = np.clip(w_idx + win - win // 2, 0, GRID_W)
        band[g] = (w_idx[None, :] >= lo[:, None]) & (w_idx[None, :] < hi[:, None])
        inv_w[g] = (1.0 / (hi - lo))[:, None]
    return jnp.asarray(band, bf16), jnp.asarray(inv_w, f32)


def _pool_kernel(u_ref, z_ref, band_ref, invw_ref, pw_ref, sc_ref, o_ref, ps_ref, pbuf_ref, *, rows):
    g = pl.program_id(1)
    W = GRID_W
    band = band_ref[...]
    lo_off = hi_off = 0
    for gi, win in enumerate(POOL_WINDOWS):
        lo_off = jnp.where(g == gi, win // 2, lo_off)
        hi_off = jnp.where(g == gi, win - win // 2, hi_off)

    ps_ref[0:W, :] = jnp.zeros((W, GROUP_C), f32)

    def width_sum(grp, carry):
        srcs = [pl.multiple_of((grp * POOL_UNROLL + j) * W, W) for j in range(POOL_UNROLL)]
        sums = [jnp.dot(band, u_ref[pl.ds(src, W), :], preferred_element_type=f32) for src in srcs]
        acc = ps_ref[pl.ds(srcs[0], W), :]
        for src, s in zip(srcs, sums):
            acc = acc + s
            ps_ref[pl.ds(src + W, W), :] = acc
        return carry

    lax.fori_loop(0, rows // POOL_UNROLL, width_sum, 0)

    inv_w = invw_ref[...]

    def pooled_minus_self(r):
        lo = jnp.maximum(r - lo_off, 0)
        hi = jnp.minimum(r + hi_off, rows)
        acc = (ps_ref[pl.ds(pl.multiple_of(hi * W, W), W), :]
               - ps_ref[pl.ds(pl.multiple_of(lo * W, W), W), :])
        inv = inv_w / (hi - lo).astype(f32)
        inv = jnp.concatenate([inv] * (GROUP_C // V7X_LANES), axis=1)
        return (acc * inv).astype(bf16) - u_ref[r * W:(r + 1) * W, :]

    tr = pbuf_ref.shape[1]
    pws = (pw_ref[...] * sc_ref[...]).astype(bf16)

    def matmul(t):
        lhs = jnp.concatenate([pooled_minus_self(t * (tr // W) + j) for j in range(tr // W)], axis=0)
        return jnp.dot(lhs, pws, preferred_element_type=f32)

    def gate_store(t, y):
        sl = slice(t * tr, (t + 1) * tr)
        o_ref[sl, :] = y.astype(bf16) * _silu(z_ref[sl, :])

    _pipelined_matmuls((rows * W) // tr, matmul, gate_store, pbuf_ref)


def _pool_mix(u, z, pool_w, scale):
    B, T, _ = u.shape
    rows = T // GRID_W
    band, inv_w = _pool_tables()
    n = GROUP_C
    blk = pl.BlockSpec((None, T, n), lambda b, g: (b, 0, g))
    est = 3 * 2 * T * n * 2 + (rows + 1) * GRID_W * n * 4 + T * n * 2 + 2 * n * n * 4 + 8 * 512 * n * 4
    return pl.pallas_call(
        functools.partial(_pool_kernel, rows=rows),
        grid=(B, N_GROUPS_C),
        in_specs=[blk, blk,
                  pl.BlockSpec((None, GRID_W, GRID_W), lambda b, g: (g, 0, 0)),
                  pl.BlockSpec((None, GRID_W, V7X_LANES), lambda b, g: (g, 0, 0)),
                  pl.BlockSpec((None, n, n), lambda b, g: (g, 0, 0)),
                  pl.BlockSpec((None, 1, n), lambda b, g: (g, 0, 0))],
        out_specs=blk,
        out_shape=jax.ShapeDtypeStruct((B, T, D_INNER), bf16),
        scratch_shapes=[pltpu.VMEM(((rows + 1) * GRID_W, n), f32),
                        pltpu.VMEM((3, 512, n), f32)],
        compiler_params=pltpu.CompilerParams(dimension_semantics=("parallel", "arbitrary"),
                                             vmem_limit_bytes=_vmem_limit(est)),
        name="pool_mix",
    )(u, z, band, inv_w, pool_w, scale.reshape(N_GROUPS_C, 1, n))


def _outproj_odd_kernel(x_ref, y_ref, gate_ref, fg_ref, w_ref, out_ref):
    acc = jnp.dot(y_ref[...], w_ref[...], preferred_element_type=f32)
    x = x_ref[...] + gate_ref[...] * acc
    out_ref[...] = x * lax.rsqrt(jnp.mean(x * x, axis=-1, keepdims=True) + EPS) * fg_ref[...]


def _outproj_odd(x, y, gate, final_g, wout, tm):
    B, T, D = x.shape
    row = lambda n: pl.BlockSpec((None, tm, n), lambda b, i: (b, i, 0))
    est = 2 * wout.size * 2 + 2 * tm * (2 * D * 4 + D_INNER * 2) + 6 * tm * D * 4
    return pl.pallas_call(
        _outproj_odd_kernel,
        grid=(B, T // tm),
        in_specs=[row(D), row(D_INNER),
                  pl.BlockSpec((None, 1, D), lambda b, i: (b, 0, 0)),
                  pl.BlockSpec((1, D), lambda b, i: (0, 0)),
                  pl.BlockSpec(wout.shape, lambda b, i: (0, 0))],
        out_specs=row(D),
        out_shape=jax.ShapeDtypeStruct((B, T, D), f32),
        compiler_params=pltpu.CompilerParams(dimension_semantics=("parallel", "parallel"),
                                             vmem_limit_bytes=_vmem_limit(est)),
        name="outproj_odd",
    )(x, y, gate, final_g.reshape(1, D), wout)


def kernel(x, c, ctx, c_ctx, ada_w, ada_b, norm_g, win_even, gate_b_even, conv_qk_even, head_norm_even,
           fourier_w_even, wout_even, win_odd, pool_w_odd, pool_scale_odd, wout_odd, final_g):
    B, T, D = x.shape
    Tc = ctx.shape[1]
    H = N_HEADS_A
    L = MLSTM_CHUNK

    nrow = -(-(B + 1) // V7X_SUBLANES) * V7X_SUBLANES
    rows_in = jnp.zeros((nrow, D), f32).at[:B].set(c).at[B].set(c_ctx)
    mod = _modulation(rows_in, ada_w, ada_b)

    def mod_parts(l, r0, r1, n):
        m = mod[l, r0:r1]
        parts = [jnp.broadcast_to(m[:, None, i * D:(i + 1) * D], (n, 1, D)) for i in range(3)]
        return parts

    we = win_even[0]
    w_main = we.astype(bf16)
    gcols = we[:, W_MAIN_EVEN:].reshape(D, 4, H)
    gbias = gate_b_even[0].reshape(4, H)
    ig_w = gcols[:, 0::2, :].transpose(0, 2, 1).reshape(D, GATE_SLOTS)
    fg_w = gcols[:, 1::2, :].transpose(0, 2, 1).reshape(D, GATE_SLOTS)
    ig_b = gbias[0::2, :].T.reshape(GATE_SLOTS)
    fg_b = gbias[1::2, :].T.reshape(GATE_SLOTS)
    wgt =jnp.concatenate([ig_w, fg_w], axis=1).T.astype(bf16)
    gbt = jnp.concatenate([ig_b, fg_b]).reshape(N_GATES, 1)
    conv_w = conv_qk_even[0]

    shift_x, scale_x, gate_x = mod_parts(0, 0, B, B)
    shift_c, scale_c, gate_c = mod_parts(0, B, B + 1, B)

    qx, kx, vx, ox, zax, ubx, zbx, gtx = _inproj_even(
        x, shift_x, scale_x, norm_g[0], w_main, wgt, gbt, conv_w, tm=512)
    qc, kc, vc, gtc = _inproj_even(ctx, shift_c, scale_c, norm_g[0], w_main, wgt, gbt, conv_w, tm=Tc, part="scan")
    oc, zac, ubc, zbc = _inproj_even(ctx, shift_c, scale_c, norm_g[0], w_main, wgt, gbt, conv_w, tm=Tc, part="mix")

    colsx, rowsx = _gate_prep(gtx, L)
    colsc, rowsc = _gate_prep(gtc, L)
    h_x, h_c = _mlstm(qx, kx, vx, colsx, rowsx, qc, kc, vc, colsc, rowsc, L)

    wcs = _fourier_weights(fourier_w_even[0], T)
    yb_x = _fourier_latent(ubx, zbx, wcs)
    wout_e = wout_even[0].astype(bf16)
    shift_1, scale_1, gate_1 = mod_parts(1, 0, B, B)
    x1, u1, z1 = _even_out_odd_in(x, h_x, ox, zax, yb_x, gate_x, head_norm_even[0], wout_e,
                                  shift_1, scale_1, norm_g[1], win_odd[0].astype(bf16), tm=512)

    wcs_c = _fourier_weights(fourier_w_even[0], Tc)
    yb_c = _fourier_ctx(ubc, zbc, wcs_c)
    ctx1 = _outproj_even(ctx, h_c, oc, zac, yb_c, gate_c, head_norm_even[0], wout_e, tm=Tc)
    del ctx1

    y1 = _pool_mix(u1, z1, pool_w_odd[0], pool_scale_odd[0])
    return _outproj_odd(x1, y1, gate_1, final_g, wout_odd[0].astype(bf16), tm=1024)
```
